```python
import math
import jax
import jax.numpy as jnp
from jax import lax
import numpy as np

D_MODEL = 1024
BATCH = 8
SEQ = 8192
DEPTH = 1
DEC_BATCH = 32
DEC_SEQ = 16
PAST_LEN = 2048

CHUNK = 64
N_MEM = 256
D_MIX = D_MODEL
D_RWKV = D_MIX // 2
RWKV_HEAD = 64
RWKV_HEADS = D_RWKV // RWKV_HEAD
W_LORA = 64
A_LORA = 64
G_LORA = 128
C_RWKV = 3 * D_RWKV + W_LORA + A_LORA + G_LORA
D_DIFF = D_MIX - D_RWKV
DIFF_HEADS = 4
DIFF_V = D_DIFF // DIFF_HEADS
DIFF_QK = DIFF_V // 2
C_DIFF_QK = DIFF_HEADS * 2 * DIFF_QK
C_DIFF = 2 * C_DIFF_QK + D_DIFF
C_IN = C_RWKV + C_DIFF
MEM_HEADS = 4
MEM_HEAD = D_MODEL // MEM_HEADS
N_GROUPS = 4
EXPERTS_PER_GROUP = 8
N_EXPERTS = N_GROUPS * EXPERTS_PER_GROUP
TOP_K = 2
D_EXPERT = D_MODEL // 2
MOE_BLOCK = 128
Q_BLOCK = 128
DEEPNORM_ALPHA = (2.0 * DEPTH) ** 0.25
DEEPNORM_BETA = (8.0 * DEPTH) ** -0.25
LN_EPS = 1e-5
RMS_EPS = 1e-5
RWKV_GN_EPS = 64e-5
NEG_INF = -1e30

kernel_name = 'hybrid_rwkv7_diffattn_hmoe_stream'


def layer_norm(x, g, b):
    xf = x.astype(jnp.float32)
    mu = jnp.mean(xf, -1, keepdims=True)
    var = jnp.mean(jnp.square(xf - mu), -1, keepdims=True)
    y = (xf - mu) * lax.rsqrt(var + LN_EPS) * g.astype(jnp.float32) + b.astype(jnp.float32)
    return y.astype(x.dtype)


def alibi_slopes():
    return jnp.asarray(np.array([2.0 ** (-8.0 * (h + 1) / DIFF_HEADS) for h in range(DIFF_HEADS)], np.float32))


def rwkv_scan(state0, r, w, k, v, kk, a):
    def step(s, inp):
        r_t, w_t, k_t, v_t, kk_t, a_t = inp
        sa = jnp.einsum('bhvk,bhk->bhv', s, -kk_t)
        s = (s * w_t[:, :, None, :] + sa[..., None] * (kk_t * a_t)[:, :, None, :]
             + v_t[..., None] * k_t[:, :, None, :])
        return s, jnp.einsum('bhvk,bhk->bhv', s, r_t)
    xs = tuple(jnp.swapaxes(t, 0, 1) for t in (r, w, k, v, kk, a))
    s_final, o = lax.scan(step, state0, xs)
    return s_final, jnp.swapaxes(o, 0, 1)


def rwkv_branch(p_a, shift_prev, state0, lp):
    B, S, _ = p_a.shape
    f32 = jnp.float32
    p_prev = jnp.concatenate([shift_prev.astype(p_a.dtype), p_a[:, :-1]], axis=1)
    ps = p_a + (p_prev - p_a) * lp['rwkv_mu']
    r = ps[..., :D_RWKV]
    k = ps[..., D_RWKV:2 * D_RWKV]
    v = ps[..., 2 * D_RWKV:3 * D_RWKV]
    o0 = 3 * D_RWKV
    w_lo = ps[..., o0:o0 + W_LORA]
    a_lo = ps[..., o0 + W_LORA:o0 + W_LORA + A_LORA]
    g_lo = ps[..., o0 + W_LORA + A_LORA:]
    w_log = -jax.nn.softplus(-(lp['rwkv_w0'] + jnp.tanh(w_lo) @ lp['rwkv_w2'])) - 0.5
    decay = jnp.exp(-jnp.exp(w_log.astype(f32)))
    a = jax.nn.sigmoid(lp['rwkv_a0'] + a_lo @ lp['rwkv_a2'])
    g = jax.nn.sigmoid(g_lo) @ lp['rwkv_g2']
    kk = (k * lp['rwkv_k_k']).reshape(B, S, RWKV_HEADS, RWKV_HEAD).astype(f32)
    kk = kk / jnp.maximum(jnp.sqrt(jnp.sum(jnp.square(kk), -1, keepdims=True)), 1e-12)
    k = k * (1.0 + (a - 1.0) * lp['rwkv_k_a'])
    hd = lambda t: t.reshape(B, S, RWKV_HEADS, RWKV_HEAD).astype(f32)
    rh, kh, vh, ah, wh = hd(r), hd(k), hd(v), hd(a), hd(decay)
    state, o = rwkv_scan(state0.astype(f32), rh, wh, kh, vh, kk, ah)
    mu = jnp.mean(o, -1, keepdims=True)
    var = jnp.mean(jnp.square(o - mu), -1, keepdims=True)
    o = ((o - mu) * lax.rsqrt(var + RWKV_GN_EPS)).reshape(B, S, D_RWKV)
    o = o * lp['rwkv_ln_w'].astype(f32) + lp['rwkv_ln_b'].astype(f32)
    bonus = jnp.sum(rh * kh * lp['rwkv_r_k'].astype(f32), -1, keepdims=True) * vh
    o = (o + bonus.reshape(B, S, D_RWKV)) * g.astype(f32)
    return o.astype(p_a.dtype), state, p_a[:, -1:]


def split_diff(p_b):
    B, S, _ = p_b.shape
    q = p_b[..., :C_DIFF_QK].reshape(B, S, DIFF_HEADS, 2 * DIFF_QK)
    k = p_b[..., C_DIFF_QK:2 * C_DIFF_QK].reshape(B, S, DIFF_HEADS, 2 * DIFF_QK)
    v = p_b[..., 2 * C_DIFF_QK:].reshape(B, S, DIFF_HEADS, DIFF_V)
    t = lambda z: jnp.transpose(z, (0, 2, 1, 3))
    return t(q), t(k), t(v)


def diff_lambda(lq1, lk1, lq2, lk2, lam_init):
    f = lambda z: z.astype(jnp.float32)
    return jnp.exp(jnp.sum(f(lq1) * f(lk1))) - jnp.exp(jnp.sum(f(lq2) * f(lk2))) + lam_init


def diff_attend(q, k, v, q_pos, k_pos, lam):
    B, H, Q, _ = q.shape
    K = k.shape[2]
    qh = q.reshape(B, H, Q, 2, DIFF_QK)
    kh = k.reshape(B, H, K, 2, DIFF_QK)
    s = jnp.einsum('bhqcd,bhkcd->bhcqk', qh, kh).astype(jnp.float32) * (DIFF_QK ** -0.5)
    dist = jnp.abs(q_pos[:, None] - k_pos[None, :]).astype(jnp.float32)
    s = s - alibi_slopes()[None, :, None, None, None] * dist
    visible = (k_pos[None, :] // CHUNK) <= (q_pos[:, None] // CHUNK)
    s = jnp.where(visible, s, NEG_INF)
    pr = jax.nn.softmax(s, axis=-1)
    attn = pr[:, :, 0] - lam * pr[:, :, 1]
    return jnp.einsum('bhqk,bhkd->bhqd', attn.astype(v.dtype), v)


def diff_attend_prompt(q, k, v, lam):
    B, H, S, C = q.shape
    nb = S // Q_BLOCK
    pos = jnp.arange(S, dtype=jnp.int32)
    qb = jnp.moveaxis(q.reshape(B, H, nb, Q_BLOCK, C), 2, 0)
    pb = pos.reshape(nb, Q_BLOCK)
    out = lax.map(lambda z: diff_attend(z[0], k, v, z[1], pos, lam), (qb, pb))
    return jnp.moveaxis(out, 0, 2).reshape(B, H, S, DIFF_V)


def diff_output(o, subln, lam_init):
    B, H, S, _ = o.shape
    of = o.astype(jnp.float32)
    of = of * lax.rsqrt(jnp.mean(jnp.square(of), -1, keepdims=True) + RMS_EPS)
    of = of * subln.astype(jnp.float32) * (1.0 - lam_init)
    return jnp.transpose(of, (0, 2, 1, 3)).reshape(B, S, D_DIFF).astype(o.dtype)


def mem_attend(h, mk, mv, wq, wo):
    B, S, _ = h.shape
    q = (h @ wq).reshape(B, S, MEM_HEADS, MEM_HEAD)
    k = mk.reshape(B, -1, MEM_HEADS, MEM_HEAD).astype(q.dtype)
    v = mv.reshape(B, -1, MEM_HEADS, MEM_HEAD).astype(q.dtype)
    s = jnp.einsum('bqhd,bkhd->bhqk', q, k).astype(jnp.float32) * (MEM_HEAD ** -0.5)
    p = jax.nn.softmax(s, axis=-1).astype(v.dtype)
    o = jnp.einsum('bhqk,bkhd->bqhd', p, v).reshape(B, S, D_MODEL)
    return o @ wo


def moe_dispatch(xf, expert, gates, w1, w3, w2):
    T, D = xf.shape
    A = T * TOP_K
    e = expert.reshape(A)
    tok = jnp.arange(A, dtype=jnp.int32) // TOP_K
    order = jnp.argsort(e)
    e_s = e[order]
    counts = jnp.bincount(e, length=N_EXPERTS)
    padded = (counts + MOE_BLOCK - 1) // MOE_BLOCK * MOE_BLOCK
    starts = jnp.cumsum(counts) - counts
    pends = jnp.cumsum(padded)
    pstarts = pends - padded
    dest = pstarts[e_s] + (jnp.arange(A, dtype=jnp.int32) - starts[e_s])
    nblk = -(-A // MOE_BLOCK) + N_EXPERTS
    P = nblk * MOE_BLOCK
    slot_tok = jnp.full((P,), T, jnp.int32).at[dest].set(tok[order])
    slot_gate = jnp.zeros((P,), gates.dtype).at[dest].set(gates.reshape(A)[order])
    blk_start = jnp.arange(nblk, dtype=jnp.int32) * MOE_BLOCK
    blk_exp = jnp.minimum(jnp.searchsorted(pends, blk_start, side='right'), N_EXPERTS - 1)
    x_pad = jnp.concatenate([xf, jnp.zeros((1, D), xf.dtype)], axis=0)

    def run(z):
        tb, gb, eb = z
        xb = x_pad[tb]
        hb = jax.nn.silu(xb @ w1[eb]) * (xb @ w3[eb])
        return (hb @ w2[eb]) * gb[:, None]

    yb = lax.map(run, (slot_tok.reshape(nblk, MOE_BLOCK), slot_gate.reshape(nblk, MOE_BLOCK), blk_exp))
    y = jnp.zeros((T + 1, D), yb.dtype).at[slot_tok].add(yb.reshape(P, D))
    return y[:T]


def hier_moe(h, lp):
    B, S, D = h.shape
    T = B * S
    f32 = jnp.float32
    xf = h.reshape(T, D)
    g_logit = (xf @ lp['moe_w_group']).astype(f32) + lp['moe_b_group'].astype(f32)
    grp = jnp.argmax(g_logit, axis=-1).astype(jnp.int32)
    g_prob = jnp.take_along_axis(jax.nn.softmax(g_logit, -1), grp[:, None], axis=-1)
    e_logit = ((xf @ lp['moe_w_expert']).astype(f32) + lp['moe_b_expert'].astype(f32))
    e_logit = e_logit.reshape(T, N_GROUPS, EXPERTS_PER_GROUP)
    e_logit = jnp.take_along_axis(e_logit, grp[:, None, None], axis=1)[:, 0]
    top_val, top_idx = lax.top_k(e_logit, TOP_K)
    gates = g_prob * jax.nn.softmax(top_val, axis=-1)
    expert = grp[:, None] * EXPERTS_PER_GROUP + top_idx.astype(jnp.int32)
    y = moe_dispatch(xf, expert, gates.astype(h.dtype), lp['moe_w1'], lp['moe_w3'], lp['moe_w2'])
    return y.reshape(B, S, D)


def trunk_layer(x, shift_prev, rwkv_state0, past_k, past_v, mk, mv, lp, lam_init):
    B, S, _ = x.shape
    p = x @ lp['w_in']
    p_a, p_b = p[..., :C_RWKV], p[..., C_RWKV:]
    o_a, rwkv_state, shift_new = rwkv_branch(p_a, shift_prev, rwkv_state0, lp)
    q, k, v = split_diff(p_b)
    lam = diff_lambda(lp['diff_lq1'], lp['diff_lk1'], lp['diff_lq2'], lp['diff_lk2'], lam_init)
    if past_k is None:
        o_b = diff_attend_prompt(q, k, v, lam)
    else:
        P = past_k.shape[2]
        k_all = jnp.concatenate([past_k.astype(k.dtype), k], axis=2)
        v_all = jnp.concatenate([past_v.astype(v.dtype), v], axis=2)
        q_pos = P + jnp.arange(S, dtype=jnp.int32)
        k_pos = jnp.arange(P + S, dtype=jnp.int32)
        o_b = diff_attend(q, k_all, v_all, q_pos, k_pos, lam)
    mix = jnp.concatenate([o_a, diff_output(o_b, lp['diff_subln'], lam_init)], axis=-1) @ lp['w_out']
    h1 = layer_norm(DEEPNORM_ALPHA * x + mix, lp['ln1_g'], lp['ln1_b'])
    h2 = layer_norm(DEEPNORM_ALPHA * h1 + mem_attend(h1, mk, mv, lp['mem_wq'], lp['mem_wo']),
                    lp['ln2_g'], lp['ln2_b'])
    y = layer_norm(DEEPNORM_ALPHA * h2 + hier_moe(h2, lp), lp['ln3_g'], lp['ln3_b'])
    return y, k, v, rwkv_state, shift_new


def setup_inputs(seed: int = 0) -> dict:
    key = jax.random.key(seed)
    ks = jax.random.split(key, 44)
    f32 = jnp.float32
    L = DEPTH
    inv = D_MODEL ** -0.5

    def nrm(i, shape, scale=1.0):
        return jax.random.normal(ks[i], shape, f32) * scale

    def gain(i, shape, base=1.0):
        return base + 0.05 * jax.random.normal(ks[i], shape, f32)

    return {
        'x_prompt': nrm(0, (BATCH, SEQ, D_MODEL)),
        'x_sample': nrm(1, (DEC_BATCH, DEC_SEQ, D_MODEL)),
        'mem_prompt': nrm(2, (BATCH, N_MEM, D_MODEL)),
        'cache_diff_k': nrm(3, (L, DEC_BATCH, DIFF_HEADS, PAST_LEN, 2 * DIFF_QK)),
        'cache_diff_v': nrm(4, (L, DEC_BATCH, DIFF_HEADS, PAST_LEN, DIFF_V)),
        'cache_mem_k': nrm(5, (L, DEC_BATCH, N_MEM, D_MODEL)),
        'cache_mem_v': nrm(6, (L, DEC_BATCH, N_MEM, D_MODEL)),
        'state_rwkv': nrm(7, (L, DEC_BATCH, RWKV_HEADS, RWKV_HEAD, RWKV_HEAD), 0.5),
        'state_shift': nrm(8, (L, DEC_BATCH, 1, C_RWKV)),
        'w_in': nrm(9, (L, D_MODEL, C_IN), inv),
        'rwkv_mu': jax.random.uniform(ks[10], (L, C_RWKV), f32),
        'rwkv_w0': nrm(11, (L, D_RWKV), 0.5),
        'rwkv_w2': nrm(12, (L, W_LORA, D_RWKV), 0.5 * W_LORA ** -0.5),
        'rwkv_a0': nrm(13, (L, D_RWKV), 0.1),
        'rwkv_a2': nrm(14, (L, A_LORA, D_RWKV), A_LORA ** -0.5),
        'rwkv_g2': nrm(15, (L, G_LORA, D_RWKV), G_LORA ** -0.5),
        'rwkv_k_k': gain(16, (L, D_RWKV), 0.85),
        'rwkv_k_a': gain(17, (L, D_RWKV)),
        'rwkv_r_k': nrm(18, (L, RWKV_HEADS, RWKV_HEAD), 0.1),
        'rwkv_ln_w': gain(19, (L, D_RWKV)),
        'rwkv_ln_b': nrm(20, (L, D_RWKV), 0.02),
        'diff_lq1': nrm(21, (L, DIFF_QK), 0.1),
        'diff_lk1': nrm(22, (L, DIFF_QK), 0.1),
        'diff_lq2': nrm(23, (L, DIFF_QK), 0.1),
        'diff_lk2': nrm(24, (L, DIFF_QK), 0.1),
        'diff_subln': gain(25, (L, DIFF_V)),
        'w_out': nrm(26, (L, D_MIX, D_MODEL), D_MIX ** -0.5 * DEEPNORM_BETA),
        'ln1_g': gain(27, (L, D_MODEL)),
        'ln1_b': nrm(28, (L, D_MODEL), 0.02),
        'mem_wq': nrm(29, (L, D_MODEL, D_MODEL), inv),
        'mem_wk': nrm(30, (L, D_MODEL, D_MODEL), inv),
        'mem_wv': nrm(31, (L, D_MODEL, D_MODEL), inv),
        'mem_wo': nrm(32, (L, D_MODEL, D_MODEL), inv * DEEPNORM_BETA),
        'ln2_g': gain(33, (L, D_MODEL)),
        'ln2_b': nrm(34, (L, D_MODEL), 0.02),
        'moe_w_group': nrm(35, (L, D_MODEL, N_GROUPS), inv),
        'moe_b_group': nrm(36, (L, N_GROUPS), 0.01),
        'moe_w_expert': nrm(37, (L, D_MODEL, N_EXPERTS), inv),
        'moe_b_expert': nrm(38, (L, N_EXPERTS), 0.01),
        'moe_w1': nrm(39, (L, N_EXPERTS, D_MODEL, D_EXPERT), inv),
        'moe_w3': nrm(40, (L, N_EXPERTS, D_MODEL, D_EXPERT), inv),
        'moe_w2': nrm(41, (L, N_EXPERTS, D_EXPERT, D_MODEL), D_EXPERT ** -0.5 * DEEPNORM_BETA),
        'ln3_g': gain(42, (L, D_MODEL)),
        'ln3_b': nrm(43, (L, D_MODEL), 0.02),
    }


def reference(x_prompt, x_sample, mem_prompt, cache_diff_k, cache_diff_v, cache_mem_k, cache_mem_v,
              state_rwkv, state_shift, w_in, rwkv_mu, rwkv_w0, rwkv_w2, rwkv_a0, rwkv_a2, rwkv_g2,
              rwkv_k_k, rwkv_k_a, rwkv_r_k, rwkv_ln_w, rwkv_ln_b, diff_lq1, diff_lk1, diff_lq2, diff_lk2,
              diff_subln, w_out, ln1_g, ln1_b, mem_wq, mem_wk, mem_wv, mem_wo, ln2_g, ln2_b,
              moe_w_group, moe_b_group, moe_w_expert, moe_b_expert, moe_w1, moe_w3, moe_w2, ln3_g, ln3_b):
    B = x_prompt.shape[0]
    yp, ys = x_prompt, x_sample
    pk, pv, pmk, pmv, prw, psh = [], [], [], [], [], []
    sk, sv, srw, ssh = [], [], [], []
    for l in range(DEPTH):
        lp = {
            'w_in': w_in[l], 'rwkv_mu': rwkv_mu[l], 'rwkv_w0': rwkv_w0[l], 'rwkv_w2': rwkv_w2[l],
            'rwkv_a0': rwkv_a0[l], 'rwkv_a2': rwkv_a2[l], 'rwkv_g2': rwkv_g2[l], 'rwkv_k_k': rwkv_k_k[l],
            'rwkv_k_a': rwkv_k_a[l], 'rwkv_r_k': rwkv_r_k[l], 'rwkv_ln_w': rwkv_ln_w[l],
            'rwkv_ln_b': rwkv_ln_b[l], 'diff_lq1': diff_lq1[l], 'diff_lk1': diff_lk1[l],
            'diff_lq2': diff_lq2[l], 'diff_lk2': diff_lk2[l], 'diff_subln': diff_subln[l],
            'w_out': w_out[l], 'ln1_g': ln1_g[l], 'ln1_b': ln1_b[l], 'mem_wq': mem_wq[l],
            'mem_wo': mem_wo[l], 'ln2_g': ln2_g[l], 'ln2_b': ln2_b[l],
            'moe_w_group': moe_w_group[l], 'moe_b_group': moe_b_group[l],
            'moe_w_expert': moe_w_expert[l], 'moe_b_expert': moe_b_expert[l],
            'moe_w1': moe_w1[l], 'moe_w3': moe_w3[l], 'moe_w2': moe_w2[l],
            'ln3_g': ln3_g[l], 'ln3_b': ln3_b[l],
        }
        lam_init = 0.8 - 0.6 * math.exp(-0.3 * l)
        mk_p = mem_prompt @ mem_wk[l]
        mv_p = mem_prompt @ mem_wv[l]
        yp, k_p, v_p, st_p, sh_p = trunk_layer(
            yp, jnp.zeros((B, 1, C_RWKV), yp.dtype),
            jnp.zeros((B, RWKV_HEADS, RWKV_HEAD, RWKV_HEAD), jnp.float32),
            None, None, mk_p, mv_p, lp, lam_init)
        ys, k_s, v_s, st_s, sh_s = trunk_layer(
            ys, state_shift[l], state_rwkv[l], cache_diff_k[l], cache_diff_v[l],
            cache_mem_k[l], cache_mem_v[l], lp, lam_init)
        pk.append(k_p)
        pv.append(v_p)
        pmk.append(mk_p)
        pmv.append(mv_p)
        prw.append(st_p)
        psh.append(sh_p)
        sk.append(k_s)
        sv.append(v_s)
        srw.append(st_s)
        ssh.append(sh_s)
    return (yp, ys, jnp.stack(pk), jnp.stack(pv), jnp.stack(pmk), jnp.stack(pmv), jnp.stack(prw),
            jnp.stack(psh), jnp.stack(sk), jnp.stack(sv), jnp.stack(srw), jnp.stack(ssh))
```

```python
import functools
import math

import jax
import jax.numpy as jnp
from jax import lax
from jax.experimental import pallas as pl
from jax.experimental.pallas import tpu as pltpu

F32 = jnp.float32
BF16 = jnp.bfloat16

D_MODEL = 1024
CHUNK = 64
D_RWKV = 512
RWKV_HEAD = 64
RWKV_PAIRS = D_RWKV // (2 * RWKV_HEAD)
W_LORA = 64
A_LORA = 64
G_LORA = 128
C_RWKV = 3 * D_RWKV + W_LORA + A_LORA + G_LORA
D_DIFF = 512
DIFF_HEADS = 4
DIFF_V = 128
DIFF_QK = 64
C_IN = C_RWKV + 3 * D_DIFF
MEM_HEADS = 4
MEM_HEAD = D_MODEL // MEM_HEADS
N_GROUPS = 4
EXPERTS_PER_GROUP = 8
N_EXPERTS = 32
D_EXPERT = 512
DEPTH = 1
DEEPNORM_ALPHA = (2.0 * DEPTH) ** 0.25
LN_EPS = 1e-5
RMS_EPS = 1e-5
RWKV_GN_EPS = 64e-5
NEG_INF = -1e30
LANES = 128
VMEM_LIMIT = 48 * 1024 * 1024


def _params(sem):
    return pltpu.CompilerParams(dimension_semantics=sem, vmem_limit_bytes=VMEM_LIMIT)


def _dot(a, b):
    return jnp.dot(a, b, preferred_element_type=F32)


def _dot_nt(a, b):
    return lax.dot_general(a, b, (((1,), (1,)), ((), ())), preferred_element_type=F32)


def _dot_tn(a, b):
    return lax.dot_general(a, b, (((0,), (0,)), ((), ())), preferred_element_type=F32)


def _split2(x):
    hi = x.astype(BF16)
    lo = (x - hi.astype(F32)).astype(BF16)
    return hi, lo


def _split3(x):
    hi = x.astype(BF16)
    r = x - hi.astype(F32)
    mid = r.astype(BF16)
    lo = (r - mid.astype(F32)).astype(BF16)
    return hi, mid, lo


def _layer_norm(x, g, b):
    mu = jnp.mean(x, axis=-1, keepdims=True)
    xc = x - mu
    var = jnp.mean(xc * xc, axis=-1, keepdims=True)
    return xc * lax.rsqrt(var + LN_EPS) * g + b


def _memproj_kernel(m_ref, wk_ref, wv_ref, k_ref, v_ref, kb_ref, vb_ref):
    m = m_ref[0].astype(BF16)
    k = _dot(m, wk_ref[...])
    v = _dot(m, wv_ref[...])
    k_ref[0] = k
    v_ref[0] = v
    kb_ref[0] = k.astype(BF16)
    vb_ref[0] = v.astype(BF16)


def _memproj(mem, wk, wv):
    b, n, d = mem.shape
    blk = pl.BlockSpec((1, n, d), lambda i: (i, 0, 0))
    wspec = pl.BlockSpec((d, d), lambda i: (0, 0))
    return pl.pallas_call(
        _memproj_kernel,
        grid=(b,),
        in_specs=[blk, wspec, wspec],
        out_specs=[blk, blk, blk, blk],
        out_shape=[jax.ShapeDtypeStruct((b, n, d), F32), jax.ShapeDtypeStruct((b, n, d), F32),
                   jax.ShapeDtypeStruct((b, n, d), BF16), jax.ShapeDtypeStruct((b, n, d), BF16)],
        compiler_params=_params(("arbitrary",)),
        name="memproj",
    )(mem, wk, wv)


def _inproj_kernel(x_ref, w_ref, pa_ref, q_ref, k_ref, v_ref, kb_ref, vb_ref):
    bb, ts, d = x_ref.shape
    x = x_ref[...].reshape(bb * ts, d).astype(BF16)
    p = _dot(x, w_ref[...])
    pa_ref[...] = p[:, :C_RWKV].reshape(bb, ts, C_RWKV)
    for h in range(DIFF_HEADS):
        o = C_RWKV + h * 2 * DIFF_QK
        q_ref[:, h] = (p[:, o:o + LANES] * (DIFF_QK ** -0.5)).reshape(bb, ts, LANES).astype(BF16)
        o = C_RWKV + D_DIFF + h * 2 * DIFF_QK
        kh = p[:, o:o + LANES].reshape(bb, ts, LANES)
        k_ref[:, h] = kh
        kb_ref[:, h] = kh.astype(BF16)
        o = C_RWKV + 2 * D_DIFF + h * DIFF_V
        vh = p[:, o:o + LANES].reshape(bb, ts, LANES)
        v_ref[:, h] = vh
        vb_ref[:, h] = vh.astype(BF16)


def _inproj(x, w_in_bf, bb, ts):
    b, s, d = x.shape
    hm = pl.BlockSpec((bb, DIFF_HEADS, ts, LANES), lambda i, j: (i, 0, j, 0))
    hshape = (b, DIFF_HEADS, s, LANES)
    return pl.pallas_call(
        _inproj_kernel,
        grid=(b // bb, s // ts),
        in_specs=[pl.BlockSpec((bb, ts, d), lambda i, j: (i, j, 0)),
                  pl.BlockSpec((d, C_IN), lambda i, j: (0, 0))],
        out_specs=[pl.BlockSpec((bb, ts, C_RWKV), lambda i, j: (i, j, 0)), hm, hm, hm, hm, hm],
        out_shape=[jax.ShapeDtypeStruct((b, s, C_RWKV), F32),
                   jax.ShapeDtypeStruct(hshape, BF16),
                   jax.ShapeDtypeStruct(hshape, F32), jax.ShapeDtypeStruct(hshape, F32),
                   jax.ShapeDtypeStruct(hshape, BF16), jax.ShapeDtypeStruct(hshape, BF16)],
        compiler_params=_params(("arbitrary", "arbitrary")),
        name="inproj",
    )(x, w_in_bf)


def _rwkv_kernel(pa_ref, shift_ref, s0_ref, mu_ref, w0_ref, w2_ref, a0_ref, a2_ref, g2_ref,
                 kk_ref, ka_ref, rk_ref, lnw_ref, lnb_ref, hsum_ref,
                 o_ref, sfin_ref,
                 carry_ref, r_s, k_s, v_s, kk_s, a_s, lw_s, o_s, *, chunk):
    L = chunk
    tb = pa_ref.shape[1]
    t = pl.program_id(1)

    @pl.when(t == 0)
    def _():
        sfin_ref[...] = s0_ref[...]
        carry_ref[...] = shift_ref[0]

    p = pa_ref[0]
    prev = pltpu.roll(p, 1, axis=0)
    row = lax.broadcasted_iota(jnp.int32, (tb, 1), 0)
    prev = jnp.where(row == 0, carry_ref[...], prev)
    carry_ref[...] = p[tb - 1:tb, :]
    ps = p + (prev - p) * mu_ref[...]

    r = ps[:, :D_RWKV]
    k = ps[:, D_RWKV:2 * D_RWKV]
    v = ps[:, 2 * D_RWKV:3 * D_RWKV]
    o0 = 3 * D_RWKV
    w_lo = ps[:, o0:o0 + W_LORA]
    a_lo = ps[:, o0 + W_LORA:o0 + W_LORA + A_LORA]
    g_lo = ps[:, o0 + W_LORA + A_LORA:]

    hsum = hsum_ref[...]

    def headsum(x):
        hi, lo = _split2(x)
        return _dot(hi, hsum) + _dot(lo, hsum)

    z = w0_ref[...] + _dot(jnp.tanh(w_lo).astype(BF16), w2_ref[...])
    lw = -math.exp(-0.5) / (1.0 + jnp.exp(-z))
    a = 1.0 / (1.0 + jnp.exp(-(a0_ref[...] + _dot(a_lo.astype(BF16), a2_ref[...]))))
    g = _dot((1.0 / (1.0 + jnp.exp(-g_lo))).astype(BF16), g2_ref[...])
    kk = k * kk_ref[...]
    kk = kk / jnp.maximum(jnp.sqrt(headsum(kk * kk)), 1e-12)
    kmod = k * (1.0 + (a - 1.0) * ka_ref[...])

    r_s[...] = r
    k_s[...] = kmod
    v_s[...] = v
    kk_s[...] = kk
    a_s[...] = a
    lw_s[...] = lw

    l2 = 2 * L
    ri = lax.broadcasted_iota(jnp.int32, (l2, l2), 0)
    ci = lax.broadcasted_iota(jnp.int32, (l2, l2), 1)
    same = (ri // L) == (ci // L)
    strict = same & (ri > ci)
    incl = same & (ri >= ci)
    eye = (ri == ci).astype(F32)
    ti = lax.broadcasted_iota(jnp.int32, (L, L), 0)
    tj = lax.broadcasted_iota(jnp.int32, (L, L), 1)
    tri = (ti >= tj).astype(BF16)
    lane = lax.broadcasted_iota(jnp.int32, (1, LANES), 1)
    first = lane < RWKV_HEAD

    def stack(x):
        return jnp.concatenate([jnp.where(first, x, 0.0), jnp.where(first, 0.0, x)], axis=0)

    def chunk_body(c, carry):
        base = pl.multiple_of(c * L, L)
        sl = pl.ds(base, L)
        lwc = lw_s[sl, :]
        h3 = _split3(lwc)
        cum = _dot(tri, h3[0]) + _dot(tri, h3[1]) + _dot(tri, h3[2])
        gam = jnp.exp(cum)
        gam_prev = jnp.exp(cum - lwc)
        igam = jnp.exp(-cum)
        kkc = kk_s[sl, :]
        al = kkc * gam_prev
        be = kkc * a_s[sl, :] * igam
        kb = k_s[sl, :] * igam
        rb = r_s[sl, :] * gam
        vc = v_s[sl, :]
        outs = []
        for j in range(RWKV_PAIRS):
            ls = slice(j * LANES, (j + 1) * LANES)
            x = jnp.concatenate([stack(al[:, ls]), stack(rb[:, ls])], axis=0).astype(BF16)
            y = jnp.concatenate([stack(be[:, ls]), stack(kb[:, ls])], axis=0).astype(BF16)
            v2 = stack(vc[:, ls])
            gmat = _dot_nt(x, y)
            ab = jnp.where(strict, gmat[:l2, :l2], 0.0)
            ak = jnp.where(strict, gmat[:l2, l2:], 0.0)
            rbm = jnp.where(incl, gmat[l2:, :l2], 0.0)
            rkm = jnp.where(incl, gmat[l2:, l2:], 0.0)
            tm = eye - ab
            pw = ab
            n = 2
            while n < L:
                pb = pw.astype(BF16)
                pw = _dot(pb, pb)
                tm = _dot(tm.astype(BF16), (eye + pw).astype(BF16))
                n *= 2
            sb = sfin_ref[0, j]
            xs = _dot_nt(x, sb.astype(BF16))
            rhs = xs[:l2] + _dot(ak.astype(BF16), v2.astype(BF16))
            u = -_dot(tm.astype(BF16), rhs.astype(BF16))
            uv = jnp.concatenate([u, v2], axis=0).astype(BF16)
            o2 = xs[l2:] + _dot(jnp.concatenate([rbm, rkm], axis=1).astype(BF16), uv)
            outs.append(o2[:L] + o2[L:])
            sfin_ref[0, j] = (sb + _dot_tn(uv, y)) * gam[L - 1:L, ls]
        o_s[sl, :] = jnp.concatenate(outs, axis=1)
        return carry

    lax.fori_loop(0, tb // L, chunk_body, 0)

    o = o_s[...]
    inv_n = 1.0 / RWKV_HEAD
    mu_h = headsum(o) * inv_n
    oc = o - mu_h
    var_h = headsum(oc * oc) * inv_n
    on = oc * lax.rsqrt(var_h + RWKV_GN_EPS) * lnw_ref[...] + lnb_ref[...]
    kmod = k_s[...]
    bonus = headsum(r * kmod * rk_ref[...]) * v
    o_ref[0] = ((on + bonus) * g).astype(o_ref.dtype)


def _rwkv(pa, shift_prev, s0_pairs, prm, tb, chunk):
    b, s, _ = pa.shape
    row = lambda n: pl.BlockSpec((1, n), lambda i, j: (0, 0))
    mat = lambda m, n: pl.BlockSpec((m, n), lambda i, j: (0, 0))
    sspec = pl.BlockSpec((1, RWKV_PAIRS, LANES, LANES), lambda i, j: (i, 0, 0, 0))
    return pl.pallas_call(
        functools.partial(_rwkv_kernel, chunk=chunk),
        grid=(b, s // tb),
        in_specs=[pl.BlockSpec((1, tb, C_RWKV), lambda i, j: (i, j, 0)),
                  pl.BlockSpec((1, 1, C_RWKV), lambda i, j: (i, 0, 0)),
                  sspec,
                  row(C_RWKV), row(D_RWKV), mat(W_LORA, D_RWKV), row(D_RWKV), mat(A_LORA, D_RWKV),
                  mat(G_LORA, D_RWKV), row(D_RWKV), row(D_RWKV), row(D_RWKV), row(D_RWKV), row(D_RWKV),
                  mat(D_RWKV, D_RWKV)],
        out_specs=[pl.BlockSpec((1, tb, D_RWKV), lambda i, j: (i, j, 0)), sspec],
        out_shape=[jax.ShapeDtypeStruct((b, s, D_RWKV), BF16),
                   jax.ShapeDtypeStruct((b, RWKV_PAIRS, LANES, LANES), F32)],
        scratch_shapes=[pltpu.VMEM((1, C_RWKV), F32)] + [pltpu.VMEM((tb, D_RWKV), F32)] * 7,
        compiler_params=_params(("arbitrary", "arbitrary")),
        name="rwkv",
    )(pa, shift_prev, s0_pairs, prm["mu"], prm["w0"], prm["w2"], prm["a0"], prm["a2"], prm["g2"],
      prm["k_k"], prm["k_a"], prm["r_k"], prm["ln_w"], prm["ln_b"], prm["hsum"])


def _state_to_pairs(s):
    b = s.shape[0]
    s = s.reshape(b, RWKV_PAIRS, 2, RWKV_HEAD, RWKV_HEAD)
    z = jnp.zeros_like(s[:, :, 0])
    top = jnp.concatenate([s[:, :, 0], z], axis=-1)
    bot = jnp.concatenate([z, s[:, :, 1]], axis=-1)
    return jnp.concatenate([top, bot], axis=-2)


def _pairs_to_state(sp):
    b = sp.shape[0]
    h = RWKV_HEAD
    s = jnp.stack([sp[:, :, :h, :h], sp[:, :, h:, h:]], axis=2)
    return s.reshape(b, 2 * RWKV_PAIRS, h, h)


def _diff_finish(accs, ls, lam, sub, lam_init):
    o = accs[0] / ls[0] - lam * (accs[1] / ls[1])
    o = o * lax.rsqrt(jnp.mean(o * o, axis=-1, keepdims=True) + RMS_EPS)
    return o * sub * (1.0 - lam_init)


def _attn_kernel(lam_ref, q_ref, k_ref, v_ref, cb_ref, bd_ref, sub_ref, o_ref, *, tq, lam_init):
    h = pl.program_id(1)
    qi = pl.program_id(2)
    slope = lam_ref[1 + h]
    q = q_ref[0, 0]
    qs = (q[:, :DIFF_QK], q[:, DIFF_QK:])
    cb = cb_ref[0]

    def tile(j, carry, bias):
        kt = k_ref[0, 0, pl.ds(pl.multiple_of(j * tq, tq), tq), :]
        vt = v_ref[0, 0, pl.ds(pl.multiple_of(j * tq, tq), tq), :]
        new = []
        for c in range(2):
            m, l, acc = carry[c]
            s = _dot_nt(qs[c], kt[:, c * DIFF_QK:(c + 1) * DIFF_QK]) + bias
            m_new = jnp.maximum(m, jnp.max(s, axis=-1, keepdims=True))
            alpha = jnp.exp(m - m_new)
            p = jnp.exp(s - m_new)
            l = alpha * l + jnp.sum(p, axis=-1, keepdims=True)
            acc = alpha * acc + _dot(p.astype(BF16), vt)
            new.append((m_new, l, acc))
        return tuple(new)

    def off_diag(j, carry):
        off = slope * ((j - qi) * tq).astype(F32)
        return tile(j, carry, cb + off)

    init = tuple((jnp.full((tq, 1), NEG_INF, F32), jnp.zeros((tq, 1), F32), jnp.zeros((tq, DIFF_V), F32))
                 for _ in range(2))
    carry = lax.fori_loop(0, qi, off_diag, init)
    carry = tile(qi, carry, bd_ref[0])
    o = _diff_finish([carry[0][2], carry[1][2]], [carry[0][1], carry[1][1]], lam_ref[0], sub_ref[...], lam_init)
    o_ref[0] = o.astype(o_ref.dtype)


def _attn_bias_tables(tq):
    slopes = jnp.asarray([2.0 ** (-8.0 * (h + 1) / DIFF_HEADS) for h in range(DIFF_HEADS)], F32)
    pos = jnp.arange(tq, dtype=jnp.int32)
    cb = slopes[:, None, None] * pos.astype(F32)[None, None, :]
    qp, kp = pos[:, None], pos[None, :]
    rel = jnp.where(kp <= qp, kp, 2 * qp - kp).astype(F32)
    vis = (kp // CHUNK) <= (qp // CHUNK)
    bd = jnp.where(vis[None], slopes[:, None, None] * rel[None], NEG_INF)
    return cb, bd


def _attn_prompt(lam, q, kb, vb, subln, tq, lam_init):
    b, h, s, _ = q.shape
    cb, bd = _attn_bias_tables(tq)
    full = pl.BlockSpec((1, 1, s, LANES), lambda i, j, t: (i, j, 0, 0))
    return pl.pallas_call(
        functools.partial(_attn_kernel, tq=tq, lam_init=lam_init),
        grid=(b, h, s // tq),
        in_specs=[pl.BlockSpec(memory_space=pltpu.SMEM),
                  pl.BlockSpec((1, 1, tq, LANES), lambda i, j, t: (i, j, t, 0)),
                  full, full,
                  pl.BlockSpec((1, 1, tq), lambda i, j, t: (j, 0, 0)),
                  pl.BlockSpec((1, tq, tq), lambda i, j, t: (j, 0, 0)),
                  pl.BlockSpec((1, DIFF_V), lambda i, j, t: (0, 0))],
        out_specs=pl.BlockSpec((1, tq, LANES), lambda i, j, t: (i, t, j)),
        out_shape=jax.ShapeDtypeStruct((b, s, D_DIFF), BF16),
        compiler_params=_params(("arbitrary", "arbitrary", "arbitrary")),
        name="attn_prompt",
    )(lam, q, kb, vb, cb, bd, subln)


def _attn_cached_kernel(lam_ref, q_ref, kn_ref, vn_ref, kp_ref, vp_ref, sub_ref, o_ref, *, lam_init):
    h = pl.program_id(1)
    slope = lam_ref[1 + h]
    s_new = q_ref.shape[2]
    past = kp_ref.shape[2]
    q = q_ref[0, 0]
    kp = kp_ref[0, 0].astype(BF16)
    vp = vp_ref[0, 0].astype(BF16)
    kn = kn_ref[0, 0]
    vn = vn_ref[0, 0]
    q_pos = past + lax.broadcasted_iota(jnp.int32, (s_new, 1), 0)

    def bias(k_pos):
        dist = jnp.abs(q_pos - k_pos).astype(F32)
        vis = (k_pos // CHUNK) <= (q_pos // CHUNK)
        return jnp.where(vis, -slope * dist, NEG_INF), vis

    b_past, vis_past = bias(lax.broadcasted_iota(jnp.int32, (1, past), 1))
    b_new, vis_new = bias(past + lax.broadcasted_iota(jnp.int32, (1, s_new), 1))
    accs, ls = [], []
    for c in range(2):
        qc = q[:, c * DIFF_QK:(c + 1) * DIFF_QK]
        sp = jnp.where(vis_past, _dot_nt(qc, kp[:, c * DIFF_QK:(c + 1) * DIFF_QK]) + b_past, NEG_INF)
        sn = jnp.where(vis_new, _dot_nt(qc, kn[:, c * DIFF_QK:(c + 1) * DIFF_QK]) + b_new, NEG_INF)
        m = jnp.maximum(jnp.max(sp, axis=-1, keepdims=True), jnp.max(sn, axis=-1, keepdims=True))
        pp = jnp.exp(sp - m)
        pn = jnp.exp(sn - m)
        ls.append(jnp.sum(pp, axis=-1, keepdims=True) + jnp.sum(pn, axis=-1, keepdims=True))
        accs.append(_dot(pp.astype(BF16), vp) + _dot(pn.astype(BF16), vn))
    o = _diff_finish(accs, ls, lam_ref[0], sub_ref[...], lam_init)
    o_ref[0] = o.astype(o_ref.dtype)


def _attn_cached(lam, q, kb, vb, past_k, past_v, subln, lam_init):
    b, h, s, _ = q.shape
    past = past_k.shape[2]
    new = pl.BlockSpec((1, 1, s, LANES), lambda i, j: (i, j, 0, 0))
    old = pl.BlockSpec((1, 1, past, LANES), lambda i, j: (i, j, 0, 0))
    return pl.pallas_call(
        functools.partial(_attn_cached_kernel, lam_init=lam_init),
        grid=(b, h),
        in_specs=[pl.BlockSpec(memory_space=pltpu.SMEM), new, new, new, old, old,
                  pl.BlockSpec((1, DIFF_V), lambda i, j: (0, 0))],
        out_specs=pl.BlockSpec((1, s, LANES), lambda i, j: (i, 0, j)),
        out_shape=jax.ShapeDtypeStruct((b, s, D_DIFF), BF16),
        compiler_params=_params(("arbitrary", "arbitrary")),
        name="attn_cached",
    )(lam, q, kb, vb, past_k, past_v, subln)


def _route(lg):
    tm = lg.shape[0]
    lane = lax.broadcasted_iota(jnp.int32, (tm, LANES), 1)
    lane_f = lane.astype(F32)
    big = 1e9
    low = -3e38
    is_g = (lane >= N_EXPERTS) & (lane < N_EXPERTS + N_GROUPS)
    gl = jnp.where(is_g, lg, low)
    gmax = jnp.max(gl, axis=-1, keepdims=True)
    grp_lane = jnp.min(jnp.where(gl == gmax, lane_f, big), axis=-1, keepdims=True)
    gsum = jnp.sum(jnp.where(is_g, jnp.exp(gl - gmax), 0.0), axis=-1, keepdims=True)
    g_prob = 1.0 / gsum
    grp = grp_lane - float(N_EXPERTS)
    lane_grp = (lane // EXPERTS_PER_GROUP).astype(F32)
    el = jnp.where(lane_grp == grp, lg, low)
    v1 = jnp.max(el, axis=-1, keepdims=True)
    i1 = jnp.min(jnp.where(el == v1, lane_f, big), axis=-1, keepdims=True)
    el2 = jnp.where(lane_f == i1, low, el)
    v2 = jnp.max(el2, axis=-1, keepdims=True)
    i2 = jnp.min(jnp.where(el2 == v2, lane_f, big), axis=-1, keepdims=True)
    e21 = jnp.exp(v2 - v1)
    den = 1.0 + e21
    gate1 = g_prob / den
    gate2 = g_prob * e21 / den
    out = jnp.where(lane == 0, i1, jnp.where(lane == 1, i2, jnp.where(lane == 2, gate1,
                    jnp.where(lane == 3, gate2, 0.0))))
    return out


def _post_kernel(x_ref, oa_ref, ob_ref, mk_ref, mv_ref, wout_ref, wq_ref, wo_ref,
                 g1_ref, b1_ref, g2_ref, b2_ref, wrh_ref, wrl_ref, br_ref, h2_ref, route_ref):
    x = x_ref[0]
    mix = _dot(oa_ref[0], wout_ref[:D_RWKV, :]) + _dot(ob_ref[0], wout_ref[D_RWKV:, :])
    h1 = _layer_norm(DEEPNORM_ALPHA * x + mix, g1_ref[...], b1_ref[...])
    q = (_dot(h1.astype(BF16), wq_ref[...]) * (MEM_HEAD ** -0.5)).astype(BF16)
    heads = []
    for h in range(MEM_HEADS):
        hs = slice(h * MEM_HEAD, (h + 1) * MEM_HEAD)
        s = _dot_nt(q[:, hs], mk_ref[0, :, hs])
        m = jnp.max(s, axis=-1, keepdims=True)
        p = jnp.exp(s - m)
        p = p / jnp.sum(p, axis=-1, keepdims=True)
        heads.append(_dot(p.astype(BF16), mv_ref[0, :, hs]).astype(BF16))
    att = _dot(jnp.concatenate(heads, axis=-1), wo_ref[...])
    h2 = _layer_norm(DEEPNORM_ALPHA * h1 + att, g2_ref[...], b2_ref[...])
    h2_ref[0] = h2
    hi, lo = _split2(h2)
    lg = _dot(hi, wrh_ref[...]) + _dot(hi, wrl_ref[...]) + _dot(lo, wrh_ref[...]) + br_ref[...]
    route_ref[0] = _route(lg)


def _post(x, oa, ob, mk, mv, prm, tm):
    b, s, d = x.shape
    n = mk.shape[1]
    tok = lambda w: pl.BlockSpec((1, tm, w), lambda i, j: (i, j, 0))
    mem = pl.BlockSpec((1, n, d), lambda i, j: (i, 0, 0))
    mat = lambda m_, n_: pl.BlockSpec((m_, n_), lambda i, j: (0, 0))
    return pl.pallas_call(
        _post_kernel,
        grid=(b, s // tm),
        in_specs=[tok(d), tok(D_RWKV), tok(D_DIFF), mem, mem, mat(d, d), mat(d, d), mat(d, d),
                  mat(1, d), mat(1, d), mat(1, d), mat(1, d), mat(d, LANES), mat(d, LANES), mat(1, LANES)],
        out_specs=[tok(d), tok(LANES)],
        out_shape=[jax.ShapeDtypeStruct((b, s, d), F32), jax.ShapeDtypeStruct((b, s, LANES), F32)],
        compiler_params=_params(("arbitrary", "arbitrary")),
        name="post",
    )(x, oa, ob, mk, mv, prm["w_out"], prm["wq"], prm["wo"], prm["ln1_g"], prm["ln1_b"],
      prm["ln2_g"], prm["ln2_b"], prm["wr_hi"], prm["wr_lo"], prm["br"])


def _moe_kernel(bexp_ref, nused_ref, tok_ref, h2_hbm, w1_ref, w3_ref, w2_ref, y_ref, xbuf, sem):
    i = pl.program_id(0)
    tb = xbuf.shape[0]

    def row_copy(r, slot):
        return pltpu.make_async_copy(h2_hbm.at[pl.ds(r, 1)], xbuf.at[pl.ds(slot, 1)], sem)

    @pl.when(i < nused_ref[0])
    def _():
        def issue(s, c):
            row_copy(tok_ref[0, 0, s], s).start()
            return c
        lax.fori_loop(0, tb, issue, 0, unroll=8)

        def drain(s, c):
            row_copy(0, s).wait()
            return c
        lax.fori_loop(0, tb, drain, 0, unroll=8)
        x = xbuf[...].astype(BF16)
        a = _dot(x, w1_ref[0])
        g = _dot(x, w3_ref[0])
        hmid = (a / (1.0 + jnp.exp(-a))) * g
        y_ref[...] = _dot(hmid.astype(BF16), w2_ref[0])

    @pl.when(i >= nused_ref[0])
    def _():
        y_ref[...] = jnp.zeros_like(y_ref)


def _moe_ffn(h2f, blk_exp, nused, tok, w1, w3, w2, tb):
    nblk = tok.shape[0]
    d = h2f.shape[1]
    grid_spec = pltpu.PrefetchScalarGridSpec(
        num_scalar_prefetch=2,
        grid=(nblk,),
        in_specs=[pl.BlockSpec((1, 1, tb), lambda i, be, nu: (i, 0, 0), memory_space=pltpu.SMEM),
                  pl.BlockSpec(memory_space=pl.ANY),
                  pl.BlockSpec((1, d, D_EXPERT), lambda i, be, nu: (be[i], 0, 0)),
                  pl.BlockSpec((1, d, D_EXPERT), lambda i, be, nu: (be[i], 0, 0)),
                  pl.BlockSpec((1, D_EXPERT, d), lambda i, be, nu: (be[i], 0, 0))],
        out_specs=pl.BlockSpec((tb, d), lambda i, be, nu: (i, 0)),
        scratch_shapes=[pltpu.VMEM((tb, d), F32), pltpu.SemaphoreType.DMA(())],
    )
    return pl.pallas_call(
        _moe_kernel,
        grid_spec=grid_spec,
        out_shape=jax.ShapeDtypeStruct((nblk * tb, d), F32),
        compiler_params=_params(("arbitrary",)),
        name="moe_ffn",
    )(blk_exp, nused, tok, h2f, w1, w3, w2)


def _combine_kernel(pos_ref, ys_hbm, h2_ref, route_ref, g_ref, b_ref, y_ref, gbuf, sem):
    tm = h2_ref.shape[0]

    def row_copy(r, slot):
        return pltpu.make_async_copy(ys_hbm.at[pl.ds(r, 1)], gbuf.at[pl.ds(slot, 1)], sem)

    def issue(s, c):
        row_copy(pos_ref[0, 0, s], s).start()
        return c
    lax.fori_loop(0, 2 * tm, issue, 0, unroll=8)

    def drain(s, c):
        row_copy(0, s).wait()
        return c
    lax.fori_loop(0, 2 * tm, drain, 0, unroll=8)
    rt = route_ref[...]
    moe = rt[:, 2:3] * gbuf[:tm, :] + rt[:, 3:4] * gbuf[tm:, :]
    y_ref[...] = _layer_norm(DEEPNORM_ALPHA * h2_ref[...] + moe, g_ref[...], b_ref[...])


def _combine(pos, ysort, h2f, route, g, b, tm):
    t, d = h2f.shape
    return pl.pallas_call(
        _combine_kernel,
        grid=(t // tm,),
        in_specs=[pl.BlockSpec((1, 1, 2 * tm), lambda i: (i, 0, 0), memory_space=pltpu.SMEM),
                  pl.BlockSpec(memory_space=pl.ANY),
                  pl.BlockSpec((tm, d), lambda i: (i, 0)),
                  pl.BlockSpec((tm, LANES), lambda i: (i, 0)),
                  pl.BlockSpec((1, d), lambda i: (0, 0)),
                  pl.BlockSpec((1, d), lambda i: (0, 0))],
        out_specs=pl.BlockSpec((tm, d), lambda i: (i, 0)),
        out_shape=jax.ShapeDtypeStruct((t, d), F32),
        scratch_shapes=[pltpu.VMEM((2 * tm, d), F32), pltpu.SemaphoreType.DMA(())],
        compiler_params=_params(("arbitrary",)),
        name="combine",
    )(pos, ysort, h2f, route, g, b)


def _dispatch(expert, tb, tm):
    t = expert.shape[0]
    a = 2 * t
    ef = expert.reshape(a)
    order = jnp.argsort(ef, stable=True).astype(jnp.int32)
    es = ef[order]
    counts = jnp.sum((ef[:, None] == jnp.arange(N_EXPERTS, dtype=jnp.int32)[None, :]).astype(jnp.int32), axis=0)
    starts = jnp.cumsum(counts) - counts
    nb = (counts + tb - 1) // tb
    bend = jnp.cumsum(nb)
    bstart = bend - nb
    nblk = -(-a // tb) + N_EXPERTS
    blk = jnp.arange(nblk, dtype=jnp.int32)
    bexp = jnp.minimum(jnp.searchsorted(bend, blk, side="right"), N_EXPERTS - 1).astype(jnp.int32)
    row0 = starts[bexp] + (blk - bstart[bexp]) * tb
    idx = jnp.clip(row0[:, None] + jnp.arange(tb, dtype=jnp.int32)[None, :], 0, a - 1)
    tok = (order[idx] // 2).astype(jnp.int32).reshape(nblk, 1, tb)
    dest_sorted = bstart[es] * tb + (jnp.arange(a, dtype=jnp.int32) - starts[es])
    pos = jnp.zeros((a,), jnp.int32).at[order].set(dest_sorted.astype(jnp.int32)).reshape(t, 2)
    pos = pos.reshape(t // tm, tm, 2).transpose(0, 2, 1).reshape(t // tm, 1, 2 * tm)
    return bexp, bend[-1:].astype(jnp.int32), tok, pos


def _moe(h2, route, prm, tb, tm):
    b, s, d = h2.shape
    t = b * s
    h2f = h2.reshape(t, d)
    rf = route.reshape(t, LANES)
    expert = rf[:, :2].astype(jnp.int32)
    bexp, nused, tok, pos = _dispatch(expert, tb, tm)
    ysort = _moe_ffn(h2f, bexp, nused, tok, prm["moe_w1"], prm["moe_w3"], prm["moe_w2"], tb)
    y = _combine(pos, ysort, h2f, rf, prm["ln3_g"], prm["ln3_b"], tm)
    return y.reshape(b, s, d)


def _trunk(x, shift_prev, state0, past_k, past_v, mk, mv, prm, lam, lam_init, cfg):
    b, s, _ = x.shape
    pa, q, k, v, kb, vb = _inproj(x, prm["w_in"], cfg["in_bb"], cfg["in_ts"])
    oa, sfin = _rwkv(pa, shift_prev, _state_to_pairs(state0), prm, cfg["rwkv_tb"], cfg["rwkv_chunk"])
    if past_k is None:
        ob = _attn_prompt(lam, q, kb, vb, prm["subln"], cfg["attn_tq"], lam_init)
    else:
        ob = _attn_cached(lam, q, kb, vb, past_k, past_v, prm["subln"], lam_init)
    h2, route = _post(x, oa, ob, mk, mv, prm, cfg["post_tm"])
    y = _moe(h2, route, prm, cfg["moe_tb"], cfg["comb_tm"])
    return y, k, v, _pairs_to_state(sfin), pa[:, s - 1:s, :]


def _prep(w, l, lam_scalar_inputs):
    row = lambda a: a[l].reshape(1, -1).astype(F32)
    idx = jnp.arange(D_RWKV, dtype=jnp.int32) // RWKV_HEAD
    wr = jnp.zeros((D_MODEL, LANES), F32)
    wr = wr.at[:, :N_EXPERTS].set(w["moe_w_expert"][l]).at[:, N_EXPERTS:N_EXPERTS + N_GROUPS].set(w["moe_w_group"][l])
    wr_hi = wr.astype(BF16)
    br = jnp.zeros((1, LANES), F32)
    br = br.at[0, :N_EXPERTS].set(w["moe_b_expert"][l]).at[0, N_EXPERTS:N_EXPERTS + N_GROUPS].set(w["moe_b_group"][l])
    return {
        "w_in": w["w_in"][l].astype(BF16),
        "mu": row(w["rwkv_mu"]), "w0": row(w["rwkv_w0"]), "w2": w["rwkv_w2"][l].astype(BF16),
        "a0": row(w["rwkv_a0"]), "a2": w["rwkv_a2"][l].astype(BF16), "g2": w["rwkv_g2"][l].astype(BF16),
        "k_k": row(w["rwkv_k_k"]), "k_a": row(w["rwkv_k_a"]), "r_k": row(w["rwkv_r_k"]),
        "ln_w": row(w["rwkv_ln_w"]), "ln_b": row(w["rwkv_ln_b"]),
        "hsum": (idx[:, None] == idx[None, :]).astype(BF16),
        "subln": row(w["diff_subln"]),
        "w_out": w["w_out"][l].astype(BF16), "wq": w["mem_wq"][l].astype(BF16), "wo": w["mem_wo"][l].astype(BF16),
        "ln1_g": row(w["ln1_g"]), "ln1_b": row(w["ln1_b"]), "ln2_g": row(w["ln2_g"]), "ln2_b": row(w["ln2_b"]),
        "ln3_g": row(w["ln3_g"]), "ln3_b": row(w["ln3_b"]),
        "wr_hi": wr_hi, "wr_lo": (wr - wr_hi.astype(F32)).astype(BF16), "br": br,
        "moe_w1": w["moe_w1"][l].astype(BF16), "moe_w3": w["moe_w3"][l].astype(BF16),
        "moe_w2": w["moe_w2"][l].astype(BF16),
    }


def _tile(n, pref):
    return pref if n % pref == 0 else n


def kernel(x_prompt, x_sample, mem_prompt, cache_diff_k, cache_diff_v, cache_mem_k, cache_mem_v, state_rwkv, state_shift, w_in, rwkv_mu, rwkv_w0, rwkv_w2, rwkv_a0, rwkv_a2, rwkv_g2, rwkv_k_k, rwkv_k_a, rwkv_r_k, rwkv_ln_w, rwkv_ln_b, diff_lq1, diff_lk1, diff_lq2, diff_lk2, diff_subln, w_out, ln1_g, ln1_b, mem_wq, mem_wk, mem_wv, mem_wo, ln2_g, ln2_b, moe_w_group, moe_b_group, moe_w_expert, moe_b_expert, moe_w1, moe_w3, moe_w2, ln3_g, ln3_b):
    w = dict(w_in=w_in, rwkv_mu=rwkv_mu, rwkv_w0=rwkv_w0, rwkv_w2=rwkv_w2, rwkv_a0=rwkv_a0, rwkv_a2=rwkv_a2,
             rwkv_g2=rwkv_g2, rwkv_k_k=rwkv_k_k, rwkv_k_a=rwkv_k_a, rwkv_r_k=rwkv_r_k, rwkv_ln_w=rwkv_ln_w,
             rwkv_ln_b=rwkv_ln_b, diff_subln=diff_subln, w_out=w_out, ln1_g=ln1_g, ln1_b=ln1_b, mem_wq=mem_wq,
             mem_wo=mem_wo, ln2_g=ln2_g, ln2_b=ln2_b, moe_w_group=moe_w_group, moe_b_group=moe_b_group,
             moe_w_expert=moe_w_expert, moe_b_expert=moe_b_expert, moe_w1=moe_w1, moe_w3=moe_w3, moe_w2=moe_w2,
             ln3_g=ln3_g, ln3_b=ln3_b)
    bp, sp, _ = x_prompt.shape
    bs, ss, _ = x_sample.shape
    depth = w_in.shape[0]
    cfg_p = dict(in_bb=1, in_ts=_tile(sp, 512), rwkv_tb=_tile(sp, 256), rwkv_chunk=CHUNK,
                 attn_tq=_tile(sp, 256), post_tm=_tile(sp, 256), moe_tb=256, comb_tm=_tile(bp * sp, 256))
    cfg_s = dict(in_bb=bs, in_ts=ss, rwkv_tb=ss, rwkv_chunk=ss, attn_tq=ss, post_tm=ss, moe_tb=64,
                 comb_tm=_tile(bs * ss, 256))
    yp, ys = x_prompt, x_sample
    outs = [[] for _ in range(10)]
    for l in range(depth):
        prm = _prep(w, l, None)
        lam_init = 0.8 - 0.6 * math.exp(-0.3 * l)
        f = lambda z: z[l].astype(F32)
        lam = (jnp.exp(jnp.sum(f(diff_lq1) * f(diff_lk1))) - jnp.exp(jnp.sum(f(diff_lq2) * f(diff_lk2)))
               + lam_init).reshape(1)
        slopes = jnp.asarray([2.0 ** (-8.0 * (h + 1) / DIFF_HEADS) for h in range(DIFF_HEADS)], F32)
        lam = jnp.concatenate([lam, slopes, jnp.zeros((3,), F32)])
        mk_p, mv_p, mkb, mvb = _memproj(mem_prompt, mem_wk[l].astype(BF16), mem_wv[l].astype(BF16))
        yp, k_p, v_p, st_p, sh_p = _trunk(
            yp, jnp.zeros((bp, 1, C_RWKV), F32), jnp.zeros((bp, 2 * RWKV_PAIRS, RWKV_HEAD, RWKV_HEAD), F32),
            None, None, mkb, mvb, prm, lam, lam_init, cfg_p)
        ys, k_s, v_s, st_s, sh_s = _trunk(
            ys, state_shift[l], state_rwkv[l], cache_diff_k[l], cache_diff_v[l],
            cache_mem_k[l].astype(BF16), cache_mem_v[l].astype(BF16), prm, lam, lam_init, cfg_s)
        for lst, val in zip(outs, (k_p, v_p, mk_p, mv_p, st_p, sh_p, k_s, v_s, st_s, sh_s)):
            lst.append(val)
    return (yp, ys) + tuple(jnp.stack(o) for o in outs)
```

```python
import functools
import math

import jax
import jax.numpy as jnp
from jax import lax
from jax.experimental import pallas as pl
from jax.experimental.pallas import tpu as pltpu

F32 = jnp.float32
BF16 = jnp.bfloat16

D_MODEL = 1024
CHUNK = 64
D_RWKV = 512
RWKV_HEAD = 64
RWKV_PAIRS = D_RWKV // (2 * RWKV_HEAD)
W_LORA = 64
A_LORA = 64
G_LORA = 128
C_RWKV = 3 * D_RWKV + W_LORA + A_LORA + G_LORA
D_DIFF = 512
DIFF_HEADS = 4
DIFF_V = 128
DIFF_QK = 64
C_IN = C_RWKV + 3 * D_DIFF
MEM_HEADS = 4
MEM_HEAD = D_MODEL // MEM_HEADS
N_GROUPS = 4
EXPERTS_PER_GROUP = 8
N_EXPERTS = 32
D_EXPERT = 512
DEPTH = 1
DEEPNORM_ALPHA = (2.0 * DEPTH) ** 0.25
LN_EPS = 1e-5
RMS_EPS = 1e-5
RWKV_GN_EPS = 64e-5
NEG_INF = -1e30
LOG2E = math.log2(math.e)
LANES = 128
VMEM_LIMIT = 48 * 1024 * 1024


def _params(sem):
    return pltpu.CompilerParams(dimension_semantics=sem, vmem_limit_bytes=VMEM_LIMIT)


def _dot(a, b):
    return jnp.dot(a, b, preferred_element_type=F32)


def _dot_nt(a, b):
    return lax.dot_general(a, b, (((1,), (1,)), ((), ())), preferred_element_type=F32)


def _dot_tn(a, b):
    return lax.dot_general(a, b, (((0,), (0,)), ((), ())), preferred_element_type=F32)


def _split2(x):
    hi = x.astype(BF16)
    lo = (x - hi.astype(F32)).astype(BF16)
    return hi, lo


def _split3(x):
    hi = x.astype(BF16)
    r = x - hi.astype(F32)
    mid = r.astype(BF16)
    lo = (r - mid.astype(F32)).astype(BF16)
    return hi, mid, lo


def _layer_norm(x, g, b):
    mu = jnp.mean(x, axis=-1, keepdims=True)
    xc = x - mu
    var = jnp.mean(xc * xc, axis=-1, keepdims=True)
    return xc * lax.rsqrt(var + LN_EPS) * g + b


def _memproj_kernel(m_ref, wk_ref, wv_ref, k_ref, v_ref, kb_ref, vb_ref):
    m = m_ref[0].astype(BF16)
    k = _dot(m, wk_ref[...])
    v = _dot(m, wv_ref[...])
    k_ref[0] = k
    v_ref[0] = v
    kb_ref[0] = k.astype(BF16)
    vb_ref[0] = v.astype(BF16)


def _memproj(mem, wk, wv):
    b, n, d = mem.shape
    blk = pl.BlockSpec((1, n, d), lambda i: (i, 0, 0))
    wspec = pl.BlockSpec((d, d), lambda i: (0, 0))
    return pl.pallas_call(
        _memproj_kernel,
        grid=(b,),
        in_specs=[blk, wspec, wspec],
        out_specs=[blk, blk, blk, blk],
        out_shape=[jax.ShapeDtypeStruct((b, n, d), F32), jax.ShapeDtypeStruct((b, n, d), F32),
                   jax.ShapeDtypeStruct((b, n, d), BF16), jax.ShapeDtypeStruct((b, n, d), BF16)],
        compiler_params=_params(("arbitrary",)),
        name="memproj",
    )(mem, wk, wv)


def _inproj_kernel(x_ref, w_ref, pa_ref, q_ref, k_ref, v_ref, kb_ref, vb_ref, *, transposed):
    bb, ts, d = x_ref.shape
    x = x_ref[...].reshape(bb * ts, d).astype(BF16)
    p = _dot(x, w_ref[...])
    pa_ref[...] = p[:, :C_RWKV].reshape(bb, ts, C_RWKV)
    for h in range(DIFF_HEADS):
        o = C_RWKV + h * 2 * DIFF_QK
        qh = p[:, o:o + LANES]
        o = C_RWKV + D_DIFF + h * 2 * DIFF_QK
        kh = p[:, o:o + LANES].reshape(bb, ts, LANES)
        k_ref[:, h] = kh
        kb_ref[:, h] = kh.astype(BF16)
        o = C_RWKV + 2 * D_DIFF + h * DIFF_V
        vh = p[:, o:o + LANES]
        v_ref[:, h] = vh.reshape(bb, ts, LANES)
        if transposed:
            q_ref[0, h] = (qh * (DIFF_QK ** -0.5 * LOG2E)).T.astype(BF16)
            vb_ref[0, h] = vh.T.astype(BF16)
        else:
            q_ref[:, h] = (qh * (DIFF_QK ** -0.5)).reshape(bb, ts, LANES).astype(BF16)
            vb_ref[:, h] = vh.reshape(bb, ts, LANES).astype(BF16)


def _inproj(x, w_in_bf, bb, ts, transposed):
    b, s, d = x.shape
    hm = pl.BlockSpec((bb, DIFF_HEADS, ts, LANES), lambda i, j: (i, 0, j, 0))
    hshape = (b, DIFF_HEADS, s, LANES)
    if transposed:
        assert bb == 1
        fm = pl.BlockSpec((1, DIFF_HEADS, LANES, ts), lambda i, j: (i, 0, 0, j))
        fshape = (b, DIFF_HEADS, LANES, s)
    else:
        fm, fshape = hm, hshape
    return pl.pallas_call(
        functools.partial(_inproj_kernel, transposed=transposed),
        grid=(b // bb, s // ts),
        in_specs=[pl.BlockSpec((bb, ts, d), lambda i, j: (i, j, 0)),
                  pl.BlockSpec((d, C_IN), lambda i, j: (0, 0))],
        out_specs=[pl.BlockSpec((bb, ts, C_RWKV), lambda i, j: (i, j, 0)), fm, hm, hm, hm, fm],
        out_shape=[jax.ShapeDtypeStruct((b, s, C_RWKV), F32),
                   jax.ShapeDtypeStruct(fshape, BF16),
                   jax.ShapeDtypeStruct(hshape, F32), jax.ShapeDtypeStruct(hshape, F32),
                   jax.ShapeDtypeStruct(hshape, BF16), jax.ShapeDtypeStruct(fshape, BF16)],
        compiler_params=_params(("arbitrary", "arbitrary")),
        name="inproj",
    )(x, w_in_bf)


def _rwkv_kernel(pa_ref, shift_ref, s0_ref, mu_ref, w0_ref, w2_ref, a0_ref, a2_ref, g2_ref,
                 kk_ref, ka_ref, rk_ref, lnw_ref, lnb_ref, hsum_ref,
                 o_ref, sfin_ref,
                 carry_ref, r_s, k_s, v_s, kk_s, a_s, lw_s, o_s, *, chunk):
    L = chunk
    tb = pa_ref.shape[1]
    t = pl.program_id(1)

    @pl.when(t == 0)
    def _():
        sfin_ref[...] = s0_ref[...]
        carry_ref[...] = shift_ref[0]

    p = pa_ref[0]
    prev = pltpu.roll(p, 1, axis=0)
    row = lax.broadcasted_iota(jnp.int32, (tb, 1), 0)
    prev = jnp.where(row == 0, carry_ref[...], prev)
    carry_ref[...] = p[tb - 1:tb, :]
    ps = p + (prev - p) * mu_ref[...]

    r = ps[:, :D_RWKV]
    k = ps[:, D_RWKV:2 * D_RWKV]
    v = ps[:, 2 * D_RWKV:3 * D_RWKV]
    o0 = 3 * D_RWKV
    w_lo = ps[:, o0:o0 + W_LORA]
    a_lo = ps[:, o0 + W_LORA:o0 + W_LORA + A_LORA]
    g_lo = ps[:, o0 + W_LORA + A_LORA:]

    hsum = hsum_ref[...]

    def headsum(x):
        hi, lo = _split2(x)
        return _dot(hi, hsum) + _dot(lo, hsum)

    z = w0_ref[...] + _dot(jnp.tanh(w_lo).astype(BF16), w2_ref[...])
    lw = -math.exp(-0.5) / (1.0 + jnp.exp(-z))
    a = 1.0 / (1.0 + jnp.exp(-(a0_ref[...] + _dot(a_lo.astype(BF16), a2_ref[...]))))
    g = _dot((1.0 / (1.0 + jnp.exp(-g_lo))).astype(BF16), g2_ref[...])
    kk = k * kk_ref[...]
    kk = kk / jnp.maximum(jnp.sqrt(headsum(kk * kk)), 1e-12)
    kmod = k * (1.0 + (a - 1.0) * ka_ref[...])

    r_s[...] = r
    k_s[...] = kmod
    v_s[...] = v
    kk_s[...] = kk
    a_s[...] = a
    lw_s[...] = lw

    l2 = 2 * L
    ri = lax.broadcasted_iota(jnp.int32, (l2, l2), 0)
    ci = lax.broadcasted_iota(jnp.int32, (l2, l2), 1)
    same = (ri // L) == (ci // L)
    strict = same & (ri > ci)
    incl = same & (ri >= ci)
    eye = (ri == ci).astype(F32)
    ti = lax.broadcasted_iota(jnp.int32, (L, L), 0)
    tj = lax.broadcasted_iota(jnp.int32, (L, L), 1)
    tri = (ti >= tj).astype(BF16)
    lane = lax.broadcasted_iota(jnp.int32, (1, LANES), 1)
    first = lane < RWKV_HEAD

    def stack(x):
        return jnp.concatenate([jnp.where(first, x, 0.0), jnp.where(first, 0.0, x)], axis=0)

    def chunk_body(c, carry):
        base = pl.multiple_of(c * L, L)
        sl = pl.ds(base, L)
        lwc = lw_s[sl, :]
        h3 = _split3(lwc)
        cum = _dot(tri, h3[0]) + _dot(tri, h3[1]) + _dot(tri, h3[2])
        gam = jnp.exp(cum)
        gam_prev = jnp.exp(cum - lwc)
        igam = jnp.exp(-cum)
        kkc = kk_s[sl, :]
        al = kkc * gam_prev
        be = kkc * a_s[sl, :] * igam
        kb = k_s[sl, :] * igam
        rb = r_s[sl, :] * gam
        vc = v_s[sl, :]
        outs = []
        for j in range(RWKV_PAIRS):
            ls = slice(j * LANES, (j + 1) * LANES)
            x = jnp.concatenate([stack(al[:, ls]), stack(rb[:, ls])], axis=0).astype(BF16)
            y = jnp.concatenate([stack(be[:, ls]), stack(kb[:, ls])], axis=0).astype(BF16)
            v2 = stack(vc[:, ls])
            gmat = _dot_nt(x, y)
            ab = jnp.where(strict, gmat[:l2, :l2], 0.0)
            ak = jnp.where(strict, gmat[:l2, l2:], 0.0)
            rbm = jnp.where(incl, gmat[l2:, :l2], 0.0)
            rkm = jnp.where(incl, gmat[l2:, l2:], 0.0)
            tm = eye - ab
            pw = ab
            n = 2
            while n < L:
                pb = pw.astype(BF16)
                pw = _dot(pb, pb)
                tm = _dot(tm.astype(BF16), (eye + pw).astype(BF16))
                n *= 2
            sb = sfin_ref[0, j]
            xs = _dot_nt(x, sb.astype(BF16))
            rhs = xs[:l2] + _dot(ak.astype(BF16), v2.astype(BF16))
            u = -_dot(tm.astype(BF16), rhs.astype(BF16))
            uv = jnp.concatenate([u, v2], axis=0).astype(BF16)
            o2 = xs[l2:] + _dot(jnp.concatenate([rbm, rkm], axis=1).astype(BF16), uv)
            outs.append(o2[:L] + o2[L:])
            sfin_ref[0, j] = (sb + _dot_tn(uv, y)) * gam[L - 1:L, ls]
        o_s[sl, :] = jnp.concatenate(outs, axis=1)
        return carry

    lax.fori_loop(0, tb // L, chunk_body, 0)

    o = o_s[...]
    inv_n = 1.0 / RWKV_HEAD
    mu_h = headsum(o) * inv_n
    oc = o - mu_h
    var_h = headsum(oc * oc) * inv_n
    on = oc * lax.rsqrt(var_h + RWKV_GN_EPS) * lnw_ref[...] + lnb_ref[...]
    kmod = k_s[...]
    bonus = headsum(r * kmod * rk_ref[...]) * v
    o_ref[0] = ((on + bonus) * g).astype(o_ref.dtype)


def _rwkv(pa, shift_prev, s0_pairs, prm, tb, chunk):
    b, s, _ = pa.shape
    row = lambda n: pl.BlockSpec((1, n), lambda i, j: (0, 0))
    mat = lambda m, n: pl.BlockSpec((m, n), lambda i, j: (0, 0))
    sspec = pl.BlockSpec((1, RWKV_PAIRS, LANES, LANES), lambda i, j: (i, 0, 0, 0))
    return pl.pallas_call(
        functools.partial(_rwkv_kernel, chunk=chunk),
        grid=(b, s // tb),
        in_specs=[pl.BlockSpec((1, tb, C_RWKV), lambda i, j: (i, j, 0)),
                  pl.BlockSpec((1, 1, C_RWKV), lambda i, j: (i, 0, 0)),
                  sspec,
                  row(C_RWKV), row(D_RWKV), mat(W_LORA, D_RWKV), row(D_RWKV), mat(A_LORA, D_RWKV),
                  mat(G_LORA, D_RWKV), row(D_RWKV), row(D_RWKV), row(D_RWKV), row(D_RWKV), row(D_RWKV),
                  mat(D_RWKV, D_RWKV)],
        out_specs=[pl.BlockSpec((1, tb, D_RWKV), lambda i, j: (i, j, 0)), sspec],
        out_shape=[jax.ShapeDtypeStruct((b, s, D_RWKV), BF16),
                   jax.ShapeDtypeStruct((b, RWKV_PAIRS, LANES, LANES), F32)],
        scratch_shapes=[pltpu.VMEM((1, C_RWKV), F32)] + [pltpu.VMEM((tb, D_RWKV), F32)] * 7,
        compiler_params=_params(("arbitrary", "arbitrary")),
        name="rwkv",
    )(pa, shift_prev, s0_pairs, prm["mu"], prm["w0"], prm["w2"], prm["a0"], prm["a2"], prm["g2"],
      prm["k_k"], prm["k_a"], prm["r_k"], prm["ln_w"], prm["ln_b"], prm["hsum"])


def _state_to_pairs(s):
    b = s.shape[0]
    s = s.reshape(b, RWKV_PAIRS, 2, RWKV_HEAD, RWKV_HEAD)
    z = jnp.zeros_like(s[:, :, 0])
    top = jnp.concatenate([s[:, :, 0], z], axis=-1)
    bot = jnp.concatenate([z, s[:, :, 1]], axis=-1)
    return jnp.concatenate([top, bot], axis=-2)


def _pairs_to_state(sp):
    b = sp.shape[0]
    h = RWKV_HEAD
    s = jnp.stack([sp[:, :, :h, :h], sp[:, :, h:, h:]], axis=2)
    return s.reshape(b, 2 * RWKV_PAIRS, h, h)


def _diff_finish(accs, ls, lam, sub, lam_init):
    o = accs[0] / ls[0] - lam * (accs[1] / ls[1])
    o = o * lax.rsqrt(jnp.mean(o * o, axis=-1, keepdims=True) + RMS_EPS)
    return o * sub * (1.0 - lam_init)


def _attn_kernel(lam_ref, qt_ref, k_ref, vt_ref, tab_ref, sub_ref, o_ref, acc_ref, s_ref, m_ref, l_ref, *, tq,
                 lam_init):
    h = pl.program_id(1)
    qi = pl.program_id(2)
    slope = lam_ref[1 + h] * LOG2E
    qt = qt_ref[0, 0]
    half = lax.broadcasted_iota(jnp.int32, (LANES, 1), 0) < DIFF_QK
    zero = jnp.zeros_like(qt)
    qts = (jnp.where(half, qt, zero), jnp.where(half, zero, qt))
    acc_ref[...] = jnp.zeros_like(acc_ref)
    m_ref[...] = jnp.full_like(m_ref, NEG_INF)
    l_ref[...] = jnp.zeros_like(l_ref)

    def scores(j, slot):
        kt = k_ref[0, 0, pl.ds(pl.multiple_of(j * tq, tq), tq), :]
        for c in range(2):
            s_ref[slot, c] = _dot(kt, qts[c])

    def softmax_pv(j, slot):
        vt = vt_ref[0, 0, :, pl.ds(pl.multiple_of(j * tq, tq), tq)]
        off = slope * ((j - qi) * tq).astype(F32)
        bias = tab_ref[0, (j == qi).astype(jnp.int32)]
        for c in range(2):
            s = s_ref[slot, c] + bias
            m = m_ref[c]
            m_new = jnp.maximum(m, jnp.max(s, axis=0, keepdims=True) + off)
            alpha = jnp.exp2(m - m_new)
            p = jnp.exp2(s - (m_new - off))
            m_ref[c] = m_new
            l_ref[c] = alpha * l_ref[c] + jnp.sum(p, axis=0, keepdims=True)
            acc_ref[c] = alpha * acc_ref[c] + _dot(vt, p.astype(BF16))

    def pair(jj, carry):
        j = 2 * jj
        scores(j + 1, 1)
        softmax_pv(j, 0)
        scores(jnp.minimum(j + 2, qi), 0)
        softmax_pv(j + 1, 1)
        return carry

    n = qi + 1
    scores(0, 0)
    lax.fori_loop(0, n // 2, pair, 0)

    @pl.when(lax.rem(n, 2) == 1)
    def _():
        softmax_pv(qi, 0)
    o = acc_ref[0] / l_ref[0] - lam_ref[0] * (acc_ref[1] / l_ref[1])
    o = o * lax.rsqrt(jnp.mean(o * o, axis=0, keepdims=True) + RMS_EPS)
    o = o * (sub_ref[...] * (1.0 - lam_init))
    o_ref[0] = o.T.astype(o_ref.dtype)


def _attn_bias_tables(tq):
    slopes = jnp.asarray([2.0 ** (-8.0 * (h + 1) / DIFF_HEADS) for h in range(DIFF_HEADS)], F32) * LOG2E
    pos = jnp.arange(tq, dtype=jnp.int32)
    kp, qp = pos[:, None], pos[None, :]
    cb = slopes[:, None, None] * jnp.broadcast_to(kp.astype(F32), (tq, tq))[None]
    rel = jnp.where(kp <= qp, kp, 2 * qp - kp).astype(F32)
    vis = (kp // CHUNK) <= (qp // CHUNK)
    bd = jnp.where(vis[None], slopes[:, None, None] * rel[None], NEG_INF)
    return jnp.stack([cb, bd], axis=1)


def _attn_prompt(lam, qt, kb, vt, subln_col, tq, lam_init):
    b, h, _, s = qt.shape
    return pl.pallas_call(
        functools.partial(_attn_kernel, tq=tq, lam_init=lam_init),
        grid=(b, h, s // tq),
        in_specs=[pl.BlockSpec(memory_space=pltpu.SMEM),
                  pl.BlockSpec((1, 1, LANES, tq), lambda i, j, t: (i, j, 0, t)),
                  pl.BlockSpec((1, 1, s, LANES), lambda i, j, t: (i, j, 0, 0)),
                  pl.BlockSpec((1, 1, LANES, s), lambda i, j, t: (i, j, 0, 0)),
                  pl.BlockSpec((1, 2, tq, tq), lambda i, j, t: (j, 0, 0, 0)),
                  pl.BlockSpec((DIFF_V, 1), lambda i, j, t: (0, 0))],
        out_specs=pl.BlockSpec((1, tq, LANES), lambda i, j, t: (i, t, j)),
        out_shape=jax.ShapeDtypeStruct((b, s, D_DIFF), BF16),
        scratch_shapes=[pltpu.VMEM((2, DIFF_V, tq), F32), pltpu.VMEM((2, 2, tq, tq), F32),
                        pltpu.VMEM((2, 1, tq), F32), pltpu.VMEM((2, 1, tq), F32)],
        compiler_params=_params(("arbitrary", "arbitrary", "arbitrary")),
        name="attn_prompt",
    )(lam, qt, kb, vt, _attn_bias_tables(tq), subln_col)


def _attn_cached_kernel(lam_ref, q_ref, kn_ref, vn_ref, kp_ref, vp_ref, sub_ref, o_ref, *, lam_init):
    h = pl.program_id(1)
    slope = lam_ref[1 + h]
    s_new = q_ref.shape[2]
    past = kp_ref.shape[2]
    q = q_ref[0, 0]
    kp = kp_ref[0, 0].astype(BF16)
    vp = vp_ref[0, 0].astype(BF16)
    kn = kn_ref[0, 0]
    vn = vn_ref[0, 0]
    q_pos = past + lax.broadcasted_iota(jnp.int32, (s_new, 1), 0)

    def bias(k_pos):
        dist = jnp.abs(q_pos - k_pos).astype(F32)
        vis = (k_pos // CHUNK) <= (q_pos // CHUNK)
        return jnp.where(vis, -slope * dist, NEG_INF), vis

    b_past, vis_past = bias(lax.broadcasted_iota(jnp.int32, (1, past), 1))
    b_new, vis_new = bias(past + lax.broadcasted_iota(jnp.int32, (1, s_new), 1))
    accs, ls = [], []
    for c in range(2):
        qc = q[:, c * DIFF_QK:(c + 1) * DIFF_QK]
        sp = jnp.where(vis_past, _dot_nt(qc, kp[:, c * DIFF_QK:(c + 1) * DIFF_QK]) + b_past, NEG_INF)
        sn = jnp.where(vis_new, _dot_nt(qc, kn[:, c * DIFF_QK:(c + 1) * DIFF_QK]) + b_new, NEG_INF)
        m = jnp.maximum(jnp.max(sp, axis=-1, keepdims=True), jnp.max(sn, axis=-1, keepdims=True))
        pp = jnp.exp(sp - m)
        pn = jnp.exp(sn - m)
        ls.append(jnp.sum(pp, axis=-1, keepdims=True) + jnp.sum(pn, axis=-1, keepdims=True))
        accs.append(_dot(pp.astype(BF16), vp) + _dot(pn.astype(BF16), vn))
    o = _diff_finish(accs, ls, lam_ref[0], sub_ref[...], lam_init)
    o_ref[0] = o.astype(o_ref.dtype)


def _attn_cached(lam, q, kb, vb, past_k, past_v, subln, lam_init):
    b, h, s, _ = q.shape
    past = past_k.shape[2]
    new = pl.BlockSpec((1, 1, s, LANES), lambda i, j: (i, j, 0, 0))
    old = pl.BlockSpec((1, 1, past, LANES), lambda i, j: (i, j, 0, 0))
    return pl.pallas_call(
        functools.partial(_attn_cached_kernel, lam_init=lam_init),
        grid=(b, h),
        in_specs=[pl.BlockSpec(memory_space=pltpu.SMEM), new, new, new, old, old,
                  pl.BlockSpec((1, DIFF_V), lambda i, j: (0, 0))],
        out_specs=pl.BlockSpec((1, s, LANES), lambda i, j: (i, 0, j)),
        out_shape=jax.ShapeDtypeStruct((b, s, D_DIFF), BF16),
        compiler_params=_params(("arbitrary", "arbitrary")),
        name="attn_cached",
    )(lam, q, kb, vb, past_k, past_v, subln)


def _route(lg):
    tm = lg.shape[0]
    lane = lax.broadcasted_iota(jnp.int32, (tm, LANES), 1)
    lane_f = lane.astype(F32)
    big = 1e9
    low = -3e38
    is_g = (lane >= N_EXPERTS) & (lane < N_EXPERTS + N_GROUPS)
    gl = jnp.where(is_g, lg, low)
    gmax = jnp.max(gl, axis=-1, keepdims=True)
    grp_lane = jnp.min(jnp.where(gl == gmax, lane_f, big), axis=-1, keepdims=True)
    gsum = jnp.sum(jnp.where(is_g, jnp.exp(gl - gmax), 0.0), axis=-1, keepdims=True)
    g_prob = 1.0 / gsum
    grp = grp_lane - float(N_EXPERTS)
    lane_grp = (lane // EXPERTS_PER_GROUP).astype(F32)
    el = jnp.where(lane_grp == grp, lg, low)
    v1 = jnp.max(el, axis=-1, keepdims=True)
    i1 = jnp.min(jnp.where(el == v1, lane_f, big), axis=-1, keepdims=True)
    el2 = jnp.where(lane_f == i1, low, el)
    v2 = jnp.max(el2, axis=-1, keepdims=True)
    i2 = jnp.min(jnp.where(el2 == v2, lane_f, big), axis=-1, keepdims=True)
    e21 = jnp.exp(v2 - v1)
    den = 1.0 + e21
    gate1 = g_prob / den
    gate2 = g_prob * e21 / den
    out = jnp.where(lane == 0, i1, jnp.where(lane == 1, i2, jnp.where(lane == 2, gate1,
                    jnp.where(lane == 3, gate2, 0.0))))
    return out


def _post_kernel(x_ref, oa_ref, ob_ref, mk_ref, mv_ref, wout_ref, wq_ref, wo_ref,
                 g1_ref, b1_ref, g2_ref, b2_ref, wrh_ref, wrl_ref, br_ref, h2_ref, route_ref):
    x = x_ref[0]
    mix = _dot(oa_ref[0], wout_ref[:D_RWKV, :]) + _dot(ob_ref[0], wout_ref[D_RWKV:, :])
    h1 = _layer_norm(DEEPNORM_ALPHA * x + mix, g1_ref[...], b1_ref[...])
    q = (_dot(h1.astype(BF16), wq_ref[...]) * (MEM_HEAD ** -0.5)).astype(BF16)
    heads = []
    for h in range(MEM_HEADS):
        hs = slice(h * MEM_HEAD, (h + 1) * MEM_HEAD)
        s = _dot_nt(q[:, hs], mk_ref[0, :, hs])
        m = jnp.max(s, axis=-1, keepdims=True)
        p = jnp.exp(s - m)
        p = p / jnp.sum(p, axis=-1, keepdims=True)
        heads.append(_dot(p.astype(BF16), mv_ref[0, :, hs]).astype(BF16))
    att = _dot(jnp.concatenate(heads, axis=-1), wo_ref[...])
    h2 = _layer_norm(DEEPNORM_ALPHA * h1 + att, g2_ref[...], b2_ref[...])
    h2_ref[0] = h2
    hi, lo = _split2(h2)
    lg = _dot(hi, wrh_ref[...]) + _dot(hi, wrl_ref[...]) + _dot(lo, wrh_ref[...]) + br_ref[...]
    route_ref[0] = _route(lg)


def _post(x, oa, ob, mk, mv, prm, tm):
    b, s, d = x.shape
    n = mk.shape[1]
    tok = lambda w: pl.BlockSpec((1, tm, w), lambda i, j: (i, j, 0))
    mem = pl.BlockSpec((1, n, d), lambda i, j: (i, 0, 0))
    mat = lambda m_, n_: pl.BlockSpec((m_, n_), lambda i, j: (0, 0))
    return pl.pallas_call(
        _post_kernel,
        grid=(b, s // tm),
        in_specs=[tok(d), tok(D_RWKV), tok(D_DIFF), mem, mem, mat(d, d), mat(d, d), mat(d, d),
                  mat(1, d), mat(1, d), mat(1, d), mat(1, d), mat(d, LANES), mat(d, LANES), mat(1, LANES)],
        out_specs=[tok(d), tok(LANES)],
        out_shape=[jax.ShapeDtypeStruct((b, s, d), F32), jax.ShapeDtypeStruct((b, s, LANES), F32)],
        compiler_params=_params(("arbitrary", "arbitrary")),
        name="post",
    )(x, oa, ob, mk, mv, prm["w_out"], prm["wq"], prm["wo"], prm["ln1_g"], prm["ln1_b"],
      prm["ln2_g"], prm["ln2_b"], prm["wr_hi"], prm["wr_lo"], prm["br"])


def _moe_kernel(bexp_ref, nused_ref, tok_ref, h2_hbm, w1_ref, w3_ref, w2_ref, y_ref, xbuf, sem):
    i = pl.program_id(0)
    tb = xbuf.shape[0]

    def row_copy(r, slot):
        return pltpu.make_async_copy(h2_hbm.at[pl.ds(r, 1)], xbuf.at[pl.ds(slot, 1)], sem)

    @pl.when(i < nused_ref[0])
    def _():
        def issue(s, c):
            row_copy(tok_ref[0, 0, s], s).start()
            return c
        lax.fori_loop(0, tb, issue, 0, unroll=8)

        def drain(s, c):
            row_copy(0, s).wait()
            return c
        lax.fori_loop(0, tb, drain, 0, unroll=8)
        x = xbuf[...].astype(BF16)
        a = _dot(x, w1_ref[0])
        g = _dot(x, w3_ref[0])
        hmid = (a / (1.0 + jnp.exp(-a))) * g
        y_ref[...] = _dot(hmid.astype(BF16), w2_ref[0])

    @pl.when(i >= nused_ref[0])
    def _():
        y_ref[...] = jnp.zeros_like(y_ref)


def _moe_ffn(h2f, blk_exp, nused, tok, w1, w3, w2, tb):
    nblk = tok.shape[0]
    d = h2f.shape[1]
    grid_spec = pltpu.PrefetchScalarGridSpec(
        num_scalar_prefetch=2,
        grid=(nblk,),
        in_specs=[pl.BlockSpec((1, 1, tb), lambda i, be, nu: (i, 0, 0), memory_space=pltpu.SMEM),
                  pl.BlockSpec(memory_space=pl.ANY),
                  pl.BlockSpec((1, d, D_EXPERT), lambda i, be, nu: (be[i], 0, 0)),
                  pl.BlockSpec((1, d, D_EXPERT), lambda i, be, nu: (be[i], 0, 0)),
                  pl.BlockSpec((1, D_EXPERT, d), lambda i, be, nu: (be[i], 0, 0))],
        out_specs=pl.BlockSpec((tb, d), lambda i, be, nu: (i, 0)),
        scratch_shapes=[pltpu.VMEM((tb, d), F32), pltpu.SemaphoreType.DMA(())],
    )
    return pl.pallas_call(
        _moe_kernel,
        grid_spec=grid_spec,
        out_shape=jax.ShapeDtypeStruct((nblk * tb, d), F32),
        compiler_params=_params(("arbitrary",)),
        name="moe_ffn",
    )(blk_exp, nused, tok, h2f, w1, w3, w2)


def _combine_kernel(pos_ref, ys_hbm, h2_ref, route_ref, g_ref, b_ref, y_ref, gbuf, sem):
    tm = h2_ref.shape[0]

    def row_copy(r, slot):
        return pltpu.make_async_copy(ys_hbm.at[pl.ds(r, 1)], gbuf.at[pl.ds(slot, 1)], sem)

    def issue(s, c):
        row_copy(pos_ref[0, 0, s], s).start()
        return c
    lax.fori_loop(0, 2 * tm, issue, 0, unroll=8)

    def drain(s, c):
        row_copy(0, s).wait()
        return c
    lax.fori_loop(0, 2 * tm, drain, 0, unroll=8)
    rt = route_ref[...]
    moe = rt[:, 2:3] * gbuf[:tm, :] + rt[:, 3:4] * gbuf[tm:, :]
    y_ref[...] = _layer_norm(DEEPNORM_ALPHA * h2_ref[...] + moe, g_ref[...], b_ref[...])


def _combine(pos, ysort, h2f, route, g, b, tm):
    t, d = h2f.shape
    return pl.pallas_call(
        _combine_kernel,
        grid=(t // tm,),
        in_specs=[pl.BlockSpec((1, 1, 2 * tm), lambda i: (i, 0, 0), memory_space=pltpu.SMEM),
                  pl.BlockSpec(memory_space=pl.ANY),
                  pl.BlockSpec((tm, d), lambda i: (i, 0)),
                  pl.BlockSpec((tm, LANES), lambda i: (i, 0)),
                  pl.BlockSpec((1, d), lambda i: (0, 0)),
                  pl.BlockSpec((1, d), lambda i: (0, 0))],
        out_specs=pl.BlockSpec((tm, d), lambda i: (i, 0)),
        out_shape=jax.ShapeDtypeStruct((t, d), F32),
        scratch_shapes=[pltpu.VMEM((2 * tm, d), F32), pltpu.SemaphoreType.DMA(())],
        compiler_params=_params(("arbitrary",)),
        name="combine",
    )(pos, ysort, h2f, route, g, b)


def _dispatch(expert, tb, tm):
    t = expert.shape[0]
    a = 2 * t
    ef = expert.reshape(a)
    order = jnp.argsort(ef, stable=True).astype(jnp.int32)
    es = ef[order]
    counts = jnp.sum((ef[:, None] == jnp.arange(N_EXPERTS, dtype=jnp.int32)[None, :]).astype(jnp.int32), axis=0)
    starts = jnp.cumsum(counts) - counts
    nb = (counts + tb - 1) // tb
    bend = jnp.cumsum(nb)
    bstart = bend - nb
    nblk = -(-a // tb) + N_EXPERTS
    blk = jnp.arange(nblk, dtype=jnp.int32)
    bexp = jnp.minimum(jnp.searchsorted(bend, blk, side="right"), N_EXPERTS - 1).astype(jnp.int32)
    row0 = starts[bexp] + (blk - bstart[bexp]) * tb
    idx = jnp.clip(row0[:, None] + jnp.arange(tb, dtype=jnp.int32)[None, :], 0, a - 1)
    tok = (order[idx] // 2).astype(jnp.int32).reshape(nblk, 1, tb)
    dest_sorted = bstart[es] * tb + (jnp.arange(a, dtype=jnp.int32) - starts[es])
    pos = jnp.zeros((a,), jnp.int32).at[order].set(dest_sorted.astype(jnp.int32)).reshape(t, 2)
    pos = pos.reshape(t // tm, tm, 2).transpose(0, 2, 1).reshape(t // tm, 1, 2 * tm)
    return bexp, bend[-1:].astype(jnp.int32), tok, pos


def _moe(h2, route, prm, tb, tm):
    b, s, d = h2.shape
    t = b * s
    h2f = h2.reshape(t, d)
    rf = route.reshape(t, LANES)
    expert = rf[:, :2].astype(jnp.int32)
    bexp, nused, tok, pos = _dispatch(expert, tb, tm)
    ysort = _moe_ffn(h2f, bexp, nused, tok, prm["moe_w1"], prm["moe_w3"], prm["moe_w2"], tb)
    y = _combine(pos, ysort, h2f, rf, prm["ln3_g"], prm["ln3_b"], tm)
    return y.reshape(b, s, d)


def _trunk(x, shift_prev, state0, past_k, past_v, mk, mv, prm, lam, lam_init, cfg):
    b, s, _ = x.shape
    pa, q, k, v, kb, vb = _inproj(x, prm["w_in"], cfg["in_bb"], cfg["in_ts"], past_k is None)
    oa, sfin = _rwkv(pa, shift_prev, _state_to_pairs(state0), prm, cfg["rwkv_tb"], cfg["rwkv_chunk"])
    if past_k is None:
        ob = _attn_prompt(lam, q, kb, vb, prm["subln"].reshape(DIFF_V, 1), cfg["attn_tq"], lam_init)
    else:
        ob = _attn_cached(lam, q, kb, vb, past_k, past_v, prm["subln"], lam_init)
    h2, route = _post(x, oa, ob, mk, mv, prm, cfg["post_tm"])
    y = _moe(h2, route, prm, cfg["moe_tb"], cfg["comb_tm"])
    return y, k, v, _pairs_to_state(sfin), pa[:, s - 1:s, :]


def _prep(w, l, lam_scalar_inputs):
    row = lambda a: a[l].reshape(1, -1).astype(F32)
    idx = jnp.arange(D_RWKV, dtype=jnp.int32) // RWKV_HEAD
    wr = jnp.zeros((D_MODEL, LANES), F32)
    wr = wr.at[:, :N_EXPERTS].set(w["moe_w_expert"][l]).at[:, N_EXPERTS:N_EXPERTS + N_GROUPS].set(w["moe_w_group"][l])
    wr_hi = wr.astype(BF16)
    br = jnp.zeros((1, LANES), F32)
    br = br.at[0, :N_EXPERTS].set(w["moe_b_expert"][l]).at[0, N_EXPERTS:N_EXPERTS + N_GROUPS].set(w["moe_b_group"][l])
    return {
        "w_in": w["w_in"][l].astype(BF16),
        "mu": row(w["rwkv_mu"]), "w0": row(w["rwkv_w0"]), "w2": w["rwkv_w2"][l].astype(BF16),
        "a0": row(w["rwkv_a0"]), "a2": w["rwkv_a2"][l].astype(BF16), "g2": w["rwkv_g2"][l].astype(BF16),
        "k_k": row(w["rwkv_k_k"]), "k_a": row(w["rwkv_k_a"]), "r_k": row(w["rwkv_r_k"]),
        "ln_w": row(w["rwkv_ln_w"]), "ln_b": row(w["rwkv_ln_b"]),
        "hsum": (idx[:, None] == idx[None, :]).astype(BF16),
        "subln": row(w["diff_subln"]),
        "w_out": w["w_out"][l].astype(BF16), "wq": w["mem_wq"][l].astype(BF16), "wo": w["mem_wo"][l].astype(BF16),
        "ln1_g": row(w["ln1_g"]), "ln1_b": row(w["ln1_b"]), "ln2_g": row(w["ln2_g"]), "ln2_b": row(w["ln2_b"]),
        "ln3_g": row(w["ln3_g"]), "ln3_b": row(w["ln3_b"]),
        "wr_hi": wr_hi, "wr_lo": (wr - wr_hi.astype(F32)).astype(BF16), "br": br,
        "moe_w1": w["moe_w1"][l].astype(BF16), "moe_w3": w["moe_w3"][l].astype(BF16),
        "moe_w2": w["moe_w2"][l].astype(BF16),
    }


def _tile(n, pref):
    return pref if n % pref == 0 else n


def kernel(x_prompt, x_sample, mem_prompt, cache_diff_k, cache_diff_v, cache_mem_k, cache_mem_v, state_rwkv, state_shift, w_in, rwkv_mu, rwkv_w0, rwkv_w2, rwkv_a0, rwkv_a2, rwkv_g2, rwkv_k_k, rwkv_k_a, rwkv_r_k, rwkv_ln_w, rwkv_ln_b, diff_lq1, diff_lk1, diff_lq2, diff_lk2, diff_subln, w_out, ln1_g, ln1_b, mem_wq, mem_wk, mem_wv, mem_wo, ln2_g, ln2_b, moe_w_group, moe_b_group, moe_w_expert, moe_b_expert, moe_w1, moe_w3, moe_w2, ln3_g, ln3_b):
    w = dict(w_in=w_in, rwkv_mu=rwkv_mu, rwkv_w0=rwkv_w0, rwkv_w2=rwkv_w2, rwkv_a0=rwkv_a0, rwkv_a2=rwkv_a2,
             rwkv_g2=rwkv_g2, rwkv_k_k=rwkv_k_k, rwkv_k_a=rwkv_k_a, rwkv_r_k=rwkv_r_k, rwkv_ln_w=rwkv_ln_w,
             rwkv_ln_b=rwkv_ln_b, diff_subln=diff_subln, w_out=w_out, ln1_g=ln1_g, ln1_b=ln1_b, mem_wq=mem_wq,
             mem_wo=mem_wo, ln2_g=ln2_g, ln2_b=ln2_b, moe_w_group=moe_w_group, moe_b_group=moe_b_group,
             moe_w_expert=moe_w_expert, moe_b_expert=moe_b_expert, moe_w1=moe_w1, moe_w3=moe_w3, moe_w2=moe_w2,
             ln3_g=ln3_g, ln3_b=ln3_b)
    bp, sp, _ = x_prompt.shape
    bs, ss, _ = x_sample.shape
    depth = w_in.shape[0]
    cfg_p = dict(in_bb=1, in_ts=_tile(sp, 512), rwkv_tb=_tile(sp, 256), rwkv_chunk=CHUNK,
                 attn_tq=_tile(sp, 512), post_tm=_tile(sp, 256), moe_tb=256, comb_tm=_tile(bp * sp, 256))
    cfg_s = dict(in_bb=bs, in_ts=ss, rwkv_tb=ss, rwkv_chunk=ss, attn_tq=ss, post_tm=ss, moe_tb=64,
                 comb_tm=_tile(bs * ss, 256))
    yp, ys = x_prompt, x_sample
    outs = [[] for _ in range(10)]
    for l in range(depth):
        prm = _prep(w, l, None)
        lam_init = 0.8 - 0.6 * math.exp(-0.3 * l)
        f = lambda z: z[l].astype(F32)
        lam = (jnp.exp(jnp.sum(f(diff_lq1) * f(diff_lk1))) - jnp.exp(jnp.sum(f(diff_lq2) * f(diff_lk2)))
               + lam_init).reshape(1)
        slopes = jnp.asarray([2.0 ** (-8.0 * (h + 1) / DIFF_HEADS) for h in range(DIFF_HEADS)], F32)
        lam = jnp.concatenate([lam, slopes, jnp.zeros((3,), F32)])
        mk_p, mv_p, mkb, mvb = _memproj(mem_prompt, mem_wk[l].astype(BF16), mem_wv[l].astype(BF16))
        yp, k_p, v_p, st_p, sh_p = _trunk(
            yp, jnp.zeros((bp, 1, C_RWKV), F32), jnp.zeros((bp, 2 * RWKV_PAIRS, RWKV_HEAD, RWKV_HEAD), F32),
            None, None, mkb, mvb, prm, lam, lam_init, cfg_p)
        ys, k_s, v_s, st_s, sh_s = _trunk(
            ys, state_shift[l], state_rwkv[l], cache_diff_k[l], cache_diff_v[l],
            cache_mem_k[l].astype(BF16), cache_mem_v[l].astype(BF16), prm, lam, lam_init, cfg_s)
        for lst, val in zip(outs, (k_p, v_p, mk_p, mv_p, st_p, sh_p, k_s, v_s, st_s, sh_s)):
            lst.append(val)
    return (yp, ys) + tuple(jnp.stack(o) for o in outs)
```

```python
import functools
import math

import jax
import jax.numpy as jnp
from jax import lax
from jax.experimental import pallas as pl
from jax.experimental.pallas import tpu as pltpu

F32 = jnp.float32
BF16 = jnp.bfloat16

D_MODEL = 1024
CHUNK = 64
D_RWKV = 512
RWKV_HEAD = 64
RWKV_PAIRS = D_RWKV // (2 * RWKV_HEAD)
W_LORA = 64
A_LORA = 64
G_LORA = 128
C_RWKV = 3 * D_RWKV + W_LORA + A_LORA + G_LORA
D_DIFF = 512
DIFF_HEADS = 4
DIFF_V = 128
DIFF_QK = 64
C_IN = C_RWKV + 3 * D_DIFF
MEM_HEADS = 4
MEM_HEAD = D_MODEL // MEM_HEADS
N_GROUPS = 4
EXPERTS_PER_GROUP = 8
N_EXPERTS = 32
D_EXPERT = 512
DEPTH = 1
DEEPNORM_ALPHA = (2.0 * DEPTH) ** 0.25
LN_EPS = 1e-5
RMS_EPS = 1e-5
RWKV_GN_EPS = 64e-5
NEG_INF = -1e30
LOG2E = math.log2(math.e)
LANES = 128
VMEM_LIMIT = 48 * 1024 * 1024


def _params(sem):
    return pltpu.CompilerParams(dimension_semantics=sem, vmem_limit_bytes=VMEM_LIMIT)


def _dot(a, b):
    return jnp.dot(a, b, preferred_element_type=F32)


def _dot_nt(a, b):
    return lax.dot_general(a, b, (((1,), (1,)), ((), ())), preferred_element_type=F32)


def _dot_tn(a, b):
    return lax.dot_general(a, b, (((0,), (0,)), ((), ())), preferred_element_type=F32)


def _split2(x):
    hi = x.astype(BF16)
    lo = (x - hi.astype(F32)).astype(BF16)
    return hi, lo


def _split3(x):
    hi = x.astype(BF16)
    r = x - hi.astype(F32)
    mid = r.astype(BF16)
    lo = (r - mid.astype(F32)).astype(BF16)
    return hi, mid, lo


def _layer_norm(x, g, b):
    mu = jnp.mean(x, axis=-1, keepdims=True)
    xc = x - mu
    var = jnp.mean(xc * xc, axis=-1, keepdims=True)
    return xc * lax.rsqrt(var + LN_EPS) * g + b


def _memproj_kernel(m_ref, wk_ref, wv_ref, k_ref, v_ref, kb_ref, vb_ref):
    m = m_ref[0].astype(BF16)
    k = _dot(m, wk_ref[...])
    v = _dot(m, wv_ref[...])
    k_ref[0] = k
    v_ref[0] = v
    kb_ref[0] = k.astype(BF16)
    vb_ref[0] = v.astype(BF16)


def _memproj(mem, wk, wv):
    b, n, d = mem.shape
    blk = pl.BlockSpec((1, n, d), lambda i: (i, 0, 0))
    wspec = pl.BlockSpec((d, d), lambda i: (0, 0))
    return pl.pallas_call(
        _memproj_kernel,
        grid=(b,),
        in_specs=[blk, wspec, wspec],
        out_specs=[blk, blk, blk, blk],
        out_shape=[jax.ShapeDtypeStruct((b, n, d), F32), jax.ShapeDtypeStruct((b, n, d), F32),
                   jax.ShapeDtypeStruct((b, n, d), BF16), jax.ShapeDtypeStruct((b, n, d), BF16)],
        compiler_params=_params(("arbitrary",)),
        name="memproj",
    )(mem, wk, wv)


def _inproj_kernel(x_ref, w_ref, pa_ref, q_ref, k_ref, v_ref, kb_ref, vb_ref, *, transposed):
    bb, ts, d = x_ref.shape
    x = x_ref[...].reshape(bb * ts, d).astype(BF16)
    p = _dot(x, w_ref[...])
    pa_ref[...] = p[:, :C_RWKV].reshape(bb, ts, C_RWKV)
    for h in range(DIFF_HEADS):
        o = C_RWKV + h * 2 * DIFF_QK
        qh = p[:, o:o + LANES]
        o = C_RWKV + D_DIFF + h * 2 * DIFF_QK
        kh = p[:, o:o + LANES].reshape(bb, ts, LANES)
        k_ref[:, h] = kh
        kb_ref[:, h] = kh.astype(BF16)
        o = C_RWKV + 2 * D_DIFF + h * DIFF_V
        vh = p[:, o:o + LANES]
        v_ref[:, h] = vh.reshape(bb, ts, LANES)
        if transposed:
            q_ref[0, h] = (qh * (DIFF_QK ** -0.5 * LOG2E)).T.astype(BF16)
            vb_ref[0, h] = vh.T.astype(BF16)
        else:
            q_ref[:, h] = (qh * (DIFF_QK ** -0.5)).reshape(bb, ts, LANES).astype(BF16)
            vb_ref[:, h] = vh.reshape(bb, ts, LANES).astype(BF16)


def _inproj(x, w_in_bf, bb, ts, transposed):
    b, s, d = x.shape
    hm = pl.BlockSpec((bb, DIFF_HEADS, ts, LANES), lambda i, j: (i, 0, j, 0))
    hshape = (b, DIFF_HEADS, s, LANES)
    if transposed:
        assert bb == 1
        fm = pl.BlockSpec((1, DIFF_HEADS, LANES, ts), lambda i, j: (i, 0, 0, j))
        fshape = (b, DIFF_HEADS, LANES, s)
    else:
        fm, fshape = hm, hshape
    return pl.pallas_call(
        functools.partial(_inproj_kernel, transposed=transposed),
        grid=(b // bb, s // ts),
        in_specs=[pl.BlockSpec((bb, ts, d), lambda i, j: (i, j, 0)),
                  pl.BlockSpec((d, C_IN), lambda i, j: (0, 0))],
        out_specs=[pl.BlockSpec((bb, ts, C_RWKV), lambda i, j: (i, j, 0)), fm, hm, hm, hm, fm],
        out_shape=[jax.ShapeDtypeStruct((b, s, C_RWKV), F32),
                   jax.ShapeDtypeStruct(fshape, BF16),
                   jax.ShapeDtypeStruct(hshape, F32), jax.ShapeDtypeStruct(hshape, F32),
                   jax.ShapeDtypeStruct(hshape, BF16), jax.ShapeDtypeStruct(fshape, BF16)],
        compiler_params=_params(("arbitrary", "arbitrary")),
        name="inproj",
    )(x, w_in_bf)


def _rwkv_kernel(pa_ref, shift_ref, s0_ref, mu_ref, w0_ref, w2_ref, a0_ref, a2_ref, g2_ref,
                 kk_ref, ka_ref, rk_ref, lnw_ref, lnb_ref, hsum_ref,
                 o_ref, sfin_ref,
                 carry_ref, al_s, be_s, kb_s, rb_s, v_s, gam_s, o_s, bonus_s, g_s, w1_s, n1_s, c0_s, n2_s,
                 *, chunk, group):
    L = chunk
    tb = pa_ref.shape[1]
    nchunk = tb // L
    t = pl.program_id(1)

    @pl.when(t == 0)
    def _():
        sfin_ref[...] = s0_ref[...]
        carry_ref[...] = shift_ref[0]

    p = pa_ref[0]
    prev = pltpu.roll(p, 1, axis=0)
    row = lax.broadcasted_iota(jnp.int32, (tb, 1), 0)
    prev = jnp.where(row == 0, carry_ref[...], prev)
    carry_ref[...] = p[tb - 1:tb, :]
    ps = p + (prev - p) * mu_ref[...]

    r = ps[:, :D_RWKV]
    k = ps[:, D_RWKV:2 * D_RWKV]
    v = ps[:, 2 * D_RWKV:3 * D_RWKV]
    o0 = 3 * D_RWKV
    w_lo = ps[:, o0:o0 + W_LORA]
    a_lo = ps[:, o0 + W_LORA:o0 + W_LORA + A_LORA]
    g_lo = ps[:, o0 + W_LORA + A_LORA:]

    hsum = hsum_ref[...]

    def headsum(x):
        hi, lo = _split2(x)
        return _dot(hi, hsum) + _dot(lo, hsum)

    z = w0_ref[...] + _dot(jnp.tanh(w_lo).astype(BF16), w2_ref[...])
    lw = -math.exp(-0.5) / (1.0 + jnp.exp(-z))
    a = 1.0 / (1.0 + jnp.exp(-(a0_ref[...] + _dot(a_lo.astype(BF16), a2_ref[...]))))
    g_s[...] = _dot((1.0 / (1.0 + jnp.exp(-g_lo))).astype(BF16), g2_ref[...])
    kk = k * kk_ref[...]
    kk = kk / jnp.maximum(jnp.sqrt(headsum(kk * kk)), 1e-12)
    kmod = k * (1.0 + (a - 1.0) * ka_ref[...])
    bonus_s[...] = headsum(r * kmod * rk_ref[...]) * v

    bi = lax.broadcasted_iota(jnp.int32, (tb, tb), 0)
    bj = lax.broadcasted_iota(jnp.int32, (tb, tb), 1)
    tri = (((bi // L) == (bj // L)) & (bi >= bj)).astype(BF16)
    h3 = _split3(lw)
    cum = _dot(tri, h3[0]) + _dot(tri, h3[1]) + _dot(tri, h3[2])
    gam = jnp.exp(cum)
    igam = jnp.exp(-cum)
    gam_s[...] = gam
    al_s[...] = kk * jnp.exp(cum - lw)
    be_s[...] = kk * a * igam
    kb_s[...] = kmod * igam
    rb_s[...] = r * gam
    v_s[...] = v

    l2 = 2 * L
    ri = lax.broadcasted_iota(jnp.int32, (l2, l2), 0)
    ci = lax.broadcasted_iota(jnp.int32, (l2, l2), 1)
    same = (ri // L) == (ci // L)
    strict = same & (ri > ci)
    incl = same & (ri >= ci)
    eye = (ri == ci).astype(F32)
    lane = lax.broadcasted_iota(jnp.int32, (1, LANES), 1)
    first = lane < RWKV_HEAD

    def stack(x):
        return jnp.concatenate([jnp.where(first, x, 0.0), jnp.where(first, 0.0, x)], axis=0)

    def phase_a(cg, carry):
        cs = []
        for gi in range(group):
            c = cg * group + gi
            sl = pl.ds(pl.multiple_of(c * L, L), L)
            gl = gam_s[pl.ds(c * L + (L - 1), 1), :]
            for j in range(RWKV_PAIRS):
                ls = slice(j * LANES, (j + 1) * LANES)
                cs.append(dict(c=c, j=j, gl=gl[:, ls], al2=stack(al_s[sl, ls]), rb2=stack(rb_s[sl, ls]),
                               be2=stack(be_s[sl, ls]), kb2=stack(kb_s[sl, ls]), v2=stack(v_s[sl, ls])))
        for d in cs:
            x = jnp.concatenate([d["al2"], d["rb2"]], axis=0).astype(BF16)
            y = jnp.concatenate([d["be2"], d["kb2"]], axis=0).astype(BF16)
            gmat = _dot_nt(x, y)
            ab = jnp.where(strict, gmat[:l2, :l2], 0.0)
            d["ak"] = jnp.where(strict, gmat[:l2, l2:], 0.0).astype(BF16)
            d["rbm"] = jnp.where(incl, gmat[l2:, :l2], 0.0).astype(BF16)
            d["rkm"] = jnp.where(incl, gmat[l2:, l2:], 0.0).astype(BF16)
            d["pw"] = ab
            d["tm"] = eye - ab
        n = 2
        while n < L:
            for d in cs:
                pb = d["pw"].astype(BF16)
                d["pw"] = _dot(pb, pb)
            for d in cs:
                d["tm"] = _dot(d["tm"].astype(BF16), (eye + d["pw"]).astype(BF16))
            n *= 2
        for d in cs:
            d["akv"] = _dot(d["ak"], d["v2"].astype(BF16))
        for d in cs:
            zz = jnp.concatenate([d["al2"], d["akv"]], axis=1).astype(BF16)
            d["m"] = _dot(d["tm"].astype(BF16), zz)
        for d in cs:
            rm = _dot(d["rbm"], d["m"].astype(BF16))
            rkv = _dot(d["rkm"], d["v2"].astype(BF16))
            n1_s[d["c"], d["j"]] = (d["rb2"] - rm[:, :LANES]).astype(BF16)
            n2_s[d["c"], d["j"]] = rkv - rm[:, LANES:]
        for d in cs:
            m1 = d["m"][:, :LANES].astype(BF16)
            m2 = d["m"][:, LANES:]
            w1 = _dot_tn(m1, d["be2"].astype(BF16)) * d["gl"]
            lhs = jnp.concatenate([d["v2"], -m2], axis=0).astype(BF16)
            rhs = jnp.concatenate([d["kb2"], d["be2"]], axis=0).astype(BF16)
            w1_s[d["c"], d["j"]] = w1.astype(BF16)
            c0_s[d["c"], d["j"]] = _dot_tn(lhs, rhs) * d["gl"]
        return carry

    lax.fori_loop(0, nchunk // group, phase_a, 0)

    def phase_b(c, carry):
        gl = gam_s[pl.ds(c * L + (L - 1), 1), :]
        outs = []
        for j in range(RWKV_PAIRS):
            sb = sfin_ref[0, j]
            sbb = sb.astype(BF16)
            o2 = _dot_nt(n1_s[c, j], sbb) + n2_s[c, j]
            outs.append(o2[:L] + o2[L:])
            sfin_ref[0, j] = sb * gl[:, j * LANES:(j + 1) * LANES] - _dot(sbb, w1_s[c, j]) + c0_s[c, j]
        o_s[pl.ds(pl.multiple_of(c * L, L), L), :] = jnp.concatenate(outs, axis=1)
        return carry

    lax.fori_loop(0, nchunk, phase_b, 0)

    o = o_s[...]
    inv_n = 1.0 / RWKV_HEAD
    mu_h = headsum(o) * inv_n
    oc = o - mu_h
    var_h = headsum(oc * oc) * inv_n
    on = oc * lax.rsqrt(var_h + RWKV_GN_EPS) * lnw_ref[...] + lnb_ref[...]
    o_ref[0] = ((on + bonus_s[...]) * g_s[...]).astype(o_ref.dtype)


def _rwkv(pa, shift_prev, s0_pairs, prm, tb, chunk, group):
    b, s, _ = pa.shape
    nchunk = tb // chunk
    row = lambda n: pl.BlockSpec((1, n), lambda i, j: (0, 0))
    mat = lambda m, n: pl.BlockSpec((m, n), lambda i, j: (0, 0))
    sspec = pl.BlockSpec((1, RWKV_PAIRS, LANES, LANES), lambda i, j: (i, 0, 0, 0))
    coef = lambda rows, dt: pltpu.VMEM((nchunk, RWKV_PAIRS, rows, LANES), dt)
    return pl.pallas_call(
        functools.partial(_rwkv_kernel, chunk=chunk, group=group),
        grid=(b, s // tb),
        in_specs=[pl.BlockSpec((1, tb, C_RWKV), lambda i, j: (i, j, 0)),
                  pl.BlockSpec((1, 1, C_RWKV), lambda i, j: (i, 0, 0)),
                  sspec,
                  row(C_RWKV), row(D_RWKV), mat(W_LORA, D_RWKV), row(D_RWKV), mat(A_LORA, D_RWKV),
                  mat(G_LORA, D_RWKV), row(D_RWKV), row(D_RWKV), row(D_RWKV), row(D_RWKV), row(D_RWKV),
                  mat(D_RWKV, D_RWKV)],
        out_specs=[pl.BlockSpec((1, tb, D_RWKV), lambda i, j: (i, j, 0)), sspec],
        out_shape=[jax.ShapeDtypeStruct((b, s, D_RWKV), BF16),
                   jax.ShapeDtypeStruct((b, RWKV_PAIRS, LANES, LANES), F32)],
        scratch_shapes=[pltpu.VMEM((1, C_RWKV), F32)] + [pltpu.VMEM((tb, D_RWKV), F32)] * 9
        + [coef(LANES, BF16), coef(2 * chunk, BF16), coef(LANES, F32), coef(2 * chunk, F32)],
        compiler_params=_params(("arbitrary", "arbitrary")),
        name="rwkv",
    )(pa, shift_prev, s0_pairs, prm["mu"], prm["w0"], prm["w2"], prm["a0"], prm["a2"], prm["g2"],
      prm["k_k"], prm["k_a"], prm["r_k"], prm["ln_w"], prm["ln_b"], prm["hsum"])


def _state_to_pairs(s):
    b = s.shape[0]
    s = s.reshape(b, RWKV_PAIRS, 2, RWKV_HEAD, RWKV_HEAD)
    z = jnp.zeros_like(s[:, :, 0])
    top = jnp.concatenate([s[:, :, 0], z], axis=-1)
    bot = jnp.concatenate([z, s[:, :, 1]], axis=-1)
    return jnp.concatenate([top, bot], axis=-2)


def _pairs_to_state(sp):
    b = sp.shape[0]
    h = RWKV_HEAD
    s = jnp.stack([sp[:, :, :h, :h], sp[:, :, h:, h:]], axis=2)
    return s.reshape(b, 2 * RWKV_PAIRS, h, h)


def _diff_finish(accs, ls, lam, sub, lam_init):
    o = accs[0] / ls[0] - lam * (accs[1] / ls[1])
    o = o * lax.rsqrt(jnp.mean(o * o, axis=-1, keepdims=True) + RMS_EPS)
    return o * sub * (1.0 - lam_init)


def _attn_kernel(lam_ref, qt_ref, k_ref, vt_ref, tab_ref, sub_ref, o_ref, acc_ref, s_ref, m_ref, l_ref, *, tq,
                 lam_init):
    h = pl.program_id(1)
    qi = pl.program_id(2)
    slope = lam_ref[1 + h] * LOG2E
    qt = qt_ref[0, 0]
    half = lax.broadcasted_iota(jnp.int32, (LANES, 1), 0) < DIFF_QK
    zero = jnp.zeros_like(qt)
    qts = (jnp.where(half, qt, zero), jnp.where(half, zero, qt))
    acc_ref[...] = jnp.zeros_like(acc_ref)
    m_ref[...] = jnp.full_like(m_ref, NEG_INF)
    l_ref[...] = jnp.zeros_like(l_ref)

    def scores(j, slot):
        kt = k_ref[0, 0, pl.ds(pl.multiple_of(j * tq, tq), tq), :]
        for c in range(2):
            s_ref[slot, c] = _dot(kt, qts[c])

    def softmax_pv(j, slot):
        vt = vt_ref[0, 0, :, pl.ds(pl.multiple_of(j * tq, tq), tq)]
        off = slope * ((j - qi) * tq).astype(F32)
        bias = tab_ref[0, (j == qi).astype(jnp.int32)]
        for c in range(2):
            s = s_ref[slot, c] + bias
            m = m_ref[c]
            m_new = jnp.maximum(m, jnp.max(s, axis=0, keepdims=True) + off)
            alpha = jnp.exp2(m - m_new)
            p = jnp.exp2(s - (m_new - off))
            m_ref[c] = m_new
            l_ref[c] = alpha * l_ref[c] + jnp.sum(p, axis=0, keepdims=True)
            acc_ref[c] = alpha * acc_ref[c] + _dot(vt, p.astype(BF16))

    def pair(jj, carry):
        j = 2 * jj
        scores(j + 1, 1)
        softmax_pv(j, 0)
        scores(jnp.minimum(j + 2, qi), 0)
        softmax_pv(j + 1, 1)
        return carry

    n = qi + 1
    scores(0, 0)
    lax.fori_loop(0, n // 2, pair, 0)

    @pl.when(lax.rem(n, 2) == 1)
    def _():
        softmax_pv(qi, 0)
    o = acc_ref[0] / l_ref[0] - lam_ref[0] * (acc_ref[1] / l_ref[1])
    o = o * lax.rsqrt(jnp.mean(o * o, axis=0, keepdims=True) + RMS_EPS)
    o = o * (sub_ref[...] * (1.0 - lam_init))
    o_ref[0] = o.T.astype(o_ref.dtype)


def _attn_bias_tables(tq):
    slopes = jnp.asarray([2.0 ** (-8.0 * (h + 1) / DIFF_HEADS) for h in range(DIFF_HEADS)], F32) * LOG2E
    pos = jnp.arange(tq, dtype=jnp.int32)
    kp, qp = pos[:, None], pos[None, :]
    cb = slopes[:, None, None] * jnp.broadcast_to(kp.astype(F32), (tq, tq))[None]
    rel = jnp.where(kp <= qp, kp, 2 * qp - kp).astype(F32)
    vis = (kp // CHUNK) <= (qp // CHUNK)
    bd = jnp.where(vis[None], slopes[:, None, None] * rel[None], NEG_INF)
    return jnp.stack([cb, bd], axis=1)


def _attn_prompt(lam, qt, kb, vt, subln_col, tq, lam_init):
    b, h, _, s = qt.shape
    return pl.pallas_call(
        functools.partial(_attn_kernel, tq=tq, lam_init=lam_init),
        grid=(b, h, s // tq),
        in_specs=[pl.BlockSpec(memory_space=pltpu.SMEM),
                  pl.BlockSpec((1, 1, LANES, tq), lambda i, j, t: (i, j, 0, t)),
                  pl.BlockSpec((1, 1, s, LANES), lambda i, j, t: (i, j, 0, 0)),
                  pl.BlockSpec((1, 1, LANES, s), lambda i, j, t: (i, j, 0, 0)),
                  pl.BlockSpec((1, 2, tq, tq), lambda i, j, t: (j, 0, 0, 0)),
                  pl.BlockSpec((DIFF_V, 1), lambda i, j, t: (0, 0))],
        out_specs=pl.BlockSpec((1, tq, LANES), lambda i, j, t: (i, t, j)),
        out_shape=jax.ShapeDtypeStruct((b, s, D_DIFF), BF16),
        scratch_shapes=[pltpu.VMEM((2, DIFF_V, tq), F32), pltpu.VMEM((2, 2, tq, tq), F32),
                        pltpu.VMEM((2, 1, tq), F32), pltpu.VMEM((2, 1, tq), F32)],
        compiler_params=_params(("arbitrary", "arbitrary", "arbitrary")),
        name="attn_prompt",
    )(lam, qt, kb, vt, _attn_bias_tables(tq), subln_col)


def _attn_cached_kernel(lam_ref, q_ref, kn_ref, vn_ref, kp_ref, vp_ref, sub_ref, o_ref, *, lam_init):
    h = pl.program_id(1)
    slope = lam_ref[1 + h]
    s_new = q_ref.shape[2]
    past = kp_ref.shape[2]
    q = q_ref[0, 0]
    kp = kp_ref[0, 0].astype(BF16)
    vp = vp_ref[0, 0].astype(BF16)
    kn = kn_ref[0, 0]
    vn = vn_ref[0, 0]
    q_pos = past + lax.broadcasted_iota(jnp.int32, (s_new, 1), 0)

    def bias(k_pos):
        dist = jnp.abs(q_pos - k_pos).astype(F32)
        vis = (k_pos // CHUNK) <= (q_pos // CHUNK)
        return jnp.where(vis, -slope * dist, NEG_INF), vis

    b_past, vis_past = bias(lax.broadcasted_iota(jnp.int32, (1, past), 1))
    b_new, vis_new = bias(past + lax.broadcasted_iota(jnp.int32, (1, s_new), 1))
    accs, ls = [], []
    for c in range(2):
        qc = q[:, c * DIFF_QK:(c + 1) * DIFF_QK]
        sp = jnp.where(vis_past, _dot_nt(qc, kp[:, c * DIFF_QK:(c + 1) * DIFF_QK]) + b_past, NEG_INF)
        sn = jnp.where(vis_new, _dot_nt(qc, kn[:, c * DIFF_QK:(c + 1) * DIFF_QK]) + b_new, NEG_INF)
        m = jnp.maximum(jnp.max(sp, axis=-1, keepdims=True), jnp.max(sn, axis=-1, keepdims=True))
        pp = jnp.exp(sp - m)
        pn = jnp.exp(sn - m)
        ls.append(jnp.sum(pp, axis=-1, keepdims=True) + jnp.sum(pn, axis=-1, keepdims=True))
        accs.append(_dot(pp.astype(BF16), vp) + _dot(pn.astype(BF16), vn))
    o = _diff_finish(accs, ls, lam_ref[0], sub_ref[...], lam_init)
    o_ref[0] = o.astype(o_ref.dtype)


def _attn_cached(lam, q, kb, vb, past_k, past_v, subln, lam_init):
    b, h, s, _ = q.shape
    past = past_k.shape[2]
    new = pl.BlockSpec((1, 1, s, LANES), lambda i, j: (i, j, 0, 0))
    old = pl.BlockSpec((1, 1, past, LANES), lambda i, j: (i, j, 0, 0))
    return pl.pallas_call(
        functools.partial(_attn_cached_kernel, lam_init=lam_init),
        grid=(b, h),
        in_specs=[pl.BlockSpec(memory_space=pltpu.SMEM), new, new, new, old, old,
                  pl.BlockSpec((1, DIFF_V), lambda i, j: (0, 0))],
        out_specs=pl.BlockSpec((1, s, LANES), lambda i, j: (i, 0, j)),
        out_shape=jax.ShapeDtypeStruct((b, s, D_DIFF), BF16),
        compiler_params=_params(("arbitrary", "arbitrary")),
        name="attn_cached",
    )(lam, q, kb, vb, past_k, past_v, subln)


def _route(lg):
    tm = lg.shape[0]
    lane = lax.broadcasted_iota(jnp.int32, (tm, LANES), 1)
    lane_f = lane.astype(F32)
    big = 1e9
    low = -3e38
    is_g = (lane >= N_EXPERTS) & (lane < N_EXPERTS + N_GROUPS)
    gl = jnp.where(is_g, lg, low)
    gmax = jnp.max(gl, axis=-1, keepdims=True)
    grp_lane = jnp.min(jnp.where(gl == gmax, lane_f, big), axis=-1, keepdims=True)
    gsum = jnp.sum(jnp.where(is_g, jnp.exp(gl - gmax), 0.0), axis=-1, keepdims=True)
    g_prob = 1.0 / gsum
    grp = grp_lane - float(N_EXPERTS)
    lane_grp = (lane // EXPERTS_PER_GROUP).astype(F32)
    el = jnp.where(lane_grp == grp, lg, low)
    v1 = jnp.max(el, axis=-1, keepdims=True)
    i1 = jnp.min(jnp.where(el == v1, lane_f, big), axis=-1, keepdims=True)
    el2 = jnp.where(lane_f == i1, low, el)
    v2 = jnp.max(el2, axis=-1, keepdims=True)
    i2 = jnp.min(jnp.where(el2 == v2, lane_f, big), axis=-1, keepdims=True)
    e21 = jnp.exp(v2 - v1)
    den = 1.0 + e21
    gate1 = g_prob / den
    gate2 = g_prob * e21 / den
    out = jnp.where(lane == 0, i1, jnp.where(lane == 1, i2, jnp.where(lane == 2, gate1,
                    jnp.where(lane == 3, gate2, 0.0))))
    return out


def _post_kernel(x_ref, oa_ref, ob_ref, mk_ref, mv_ref, wout_ref, wq_ref, wo_ref,
                 g1_ref, b1_ref, g2_ref, b2_ref, wrh_ref, wrl_ref, br_ref, h2_ref, route_ref):
    x = x_ref[0]
    mix = _dot(oa_ref[0], wout_ref[:D_RWKV, :]) + _dot(ob_ref[0], wout_ref[D_RWKV:, :])
    h1 = _layer_norm(DEEPNORM_ALPHA * x + mix, g1_ref[...], b1_ref[...])
    q = (_dot(h1.astype(BF16), wq_ref[...]) * (MEM_HEAD ** -0.5)).astype(BF16)
    heads = []
    for h in range(MEM_HEADS):
        hs = slice(h * MEM_HEAD, (h + 1) * MEM_HEAD)
        s = _dot_nt(q[:, hs], mk_ref[0, :, hs])
        m = jnp.max(s, axis=-1, keepdims=True)
        p = jnp.exp(s - m)
        p = p / jnp.sum(p, axis=-1, keepdims=True)
        heads.append(_dot(p.astype(BF16), mv_ref[0, :, hs]).astype(BF16))
    att = _dot(jnp.concatenate(heads, axis=-1), wo_ref[...])
    h2 = _layer_norm(DEEPNORM_ALPHA * h1 + att, g2_ref[...], b2_ref[...])
    h2_ref[0] = h2
    hi, lo = _split2(h2)
    lg = _dot(hi, wrh_ref[...]) + _dot(hi, wrl_ref[...]) + _dot(lo, wrh_ref[...]) + br_ref[...]
    route_ref[0] = _route(lg)


def _post(x, oa, ob, mk, mv, prm, tm):
    b, s, d = x.shape
    n = mk.shape[1]
    tok = lambda w: pl.BlockSpec((1, tm, w), lambda i, j: (i, j, 0))
    mem = pl.BlockSpec((1, n, d), lambda i, j: (i, 0, 0))
    mat = lambda m_, n_: pl.BlockSpec((m_, n_), lambda i, j: (0, 0))
    return pl.pallas_call(
        _post_kernel,
        grid=(b, s // tm),
        in_specs=[tok(d), tok(D_RWKV), tok(D_DIFF), mem, mem, mat(d, d), mat(d, d), mat(d, d),
                  mat(1, d), mat(1, d), mat(1, d), mat(1, d), mat(d, LANES), mat(d, LANES), mat(1, LANES)],
        out_specs=[tok(d), tok(LANES)],
        out_shape=[jax.ShapeDtypeStruct((b, s, d), F32), jax.ShapeDtypeStruct((b, s, LANES), F32)],
        compiler_params=_params(("arbitrary", "arbitrary")),
        name="post",
    )(x, oa, ob, mk, mv, prm["w_out"], prm["wq"], prm["wo"], prm["ln1_g"], prm["ln1_b"],
      prm["ln2_g"], prm["ln2_b"], prm["wr_hi"], prm["wr_lo"], prm["br"])


def _moe_kernel(bexp_ref, nused_ref, tok_ref, tokn_ref, h2_hbm, w1_ref, w3_ref, w2_ref, y_ref, xbuf, sem):
    i = pl.program_id(0)
    nblk = pl.num_programs(0)
    nused = nused_ref[0]
    tb = xbuf.shape[1]
    slot = lax.rem(i, 2)

    def row_copy(r, s, sl):
        return pltpu.make_async_copy(h2_hbm.at[pl.ds(r, 1)], xbuf.at[sl, pl.ds(s, 1)], sem.at[sl])

    def wait_rows(sl):
        pltpu.make_async_copy(h2_hbm.at[pl.ds(0, tb)], xbuf.at[sl], sem.at[sl]).wait()

    @pl.when(i == 0)
    def _():
        def issue(s, c):
            row_copy(tok_ref[0, 0, s], s, 0).start()
            return c
        lax.fori_loop(0, tb, issue, 0, unroll=8)

    @pl.when(i < nused)
    def _():
        wait_rows(slot)

        def issue_next(s, c):
            row_copy(tokn_ref[0, 0, s], s, 1 - slot).start()
            return c
        lax.fori_loop(0, tb, issue_next, 0, unroll=8)
        x = xbuf[slot].astype(BF16)
        a = _dot(x, w1_ref[0])
        g = _dot(x, w3_ref[0])
        hmid = (a / (1.0 + jnp.exp(-a))) * g
        y_ref[...] = _dot(hmid.astype(BF16), w2_ref[0])

        @pl.when(i == nblk - 1)
        def _():
            wait_rows(1 - slot)

    @pl.when(i >= nused)
    def _():
        y_ref[...] = jnp.zeros_like(y_ref)

        @pl.when(i == nused)
        def _():
            wait_rows(slot)


def _moe_ffn(h2f, blk_exp, nused, tok, w1, w3, w2, tb):
    nblk = tok.shape[0]
    d = h2f.shape[1]
    grid_spec = pltpu.PrefetchScalarGridSpec(
        num_scalar_prefetch=2,
        grid=(nblk,),
        in_specs=[pl.BlockSpec((1, 1, tb), lambda i, be, nu: (i, 0, 0), memory_space=pltpu.SMEM),
                  pl.BlockSpec((1, 1, tb), lambda i, be, nu: (jnp.minimum(i + 1, nblk - 1), 0, 0),
                               memory_space=pltpu.SMEM),
                  pl.BlockSpec(memory_space=pl.ANY),
                  pl.BlockSpec((1, d, D_EXPERT), lambda i, be, nu: (be[i], 0, 0)),
                  pl.BlockSpec((1, d, D_EXPERT), lambda i, be, nu: (be[i], 0, 0)),
                  pl.BlockSpec((1, D_EXPERT, d), lambda i, be, nu: (be[i], 0, 0))],
        out_specs=pl.BlockSpec((tb, d), lambda i, be, nu: (i, 0)),
        scratch_shapes=[pltpu.VMEM((2, tb, d), F32), pltpu.SemaphoreType.DMA((2,))],
    )
    return pl.pallas_call(
        _moe_kernel,
        grid_spec=grid_spec,
        out_shape=jax.ShapeDtypeStruct((nblk * tb, d), F32),
        compiler_params=_params(("arbitrary",)),
        name="moe_ffn",
    )(blk_exp, nused, tok, tok, h2f, w1, w3, w2)


def _combine_kernel(pos_ref, posn_ref, ys_hbm, h2_ref, route_ref, g_ref, b_ref, y_ref, gbuf, sem):
    i = pl.program_id(0)
    n = pl.num_programs(0)
    tm = h2_ref.shape[0]
    slot = lax.rem(i, 2)

    def row_copy(r, s, sl):
        return pltpu.make_async_copy(ys_hbm.at[pl.ds(r, 1)], gbuf.at[sl, pl.ds(s, 1)], sem.at[sl])

    def wait_rows(sl):
        pltpu.make_async_copy(ys_hbm.at[pl.ds(0, 2 * tm)], gbuf.at[sl], sem.at[sl]).wait()

    @pl.when(i == 0)
    def _():
        def issue(s, c):
            row_copy(pos_ref[0, 0, s], s, 0).start()
            return c
        lax.fori_loop(0, 2 * tm, issue, 0, unroll=8)

    wait_rows(slot)

    def issue_next(s, c):
        row_copy(posn_ref[0, 0, s], s, 1 - slot).start()
        return c
    lax.fori_loop(0, 2 * tm, issue_next, 0, unroll=8)
    rt = route_ref[...]
    moe = rt[:, 2:3] * gbuf[slot, :tm, :] + rt[:, 3:4] * gbuf[slot, tm:, :]
    y_ref[...] = _layer_norm(DEEPNORM_ALPHA * h2_ref[...] + moe, g_ref[...], b_ref[...])

    @pl.when(i == n - 1)
    def _():
        wait_rows(1 - slot)


def _combine(pos, ysort, h2f, route, g, b, tm):
    t, d = h2f.shape
    n = t // tm
    return pl.pallas_call(
        _combine_kernel,
        grid=(n,),
        in_specs=[pl.BlockSpec((1, 1, 2 * tm), lambda i: (i, 0, 0), memory_space=pltpu.SMEM),
                  pl.BlockSpec((1, 1, 2 * tm), lambda i: (jnp.minimum(i + 1, n - 1), 0, 0),
                               memory_space=pltpu.SMEM),
                  pl.BlockSpec(memory_space=pl.ANY),
                  pl.BlockSpec((tm, d), lambda i: (i, 0)),
                  pl.BlockSpec((tm, LANES), lambda i: (i, 0)),
                  pl.BlockSpec((1, d), lambda i: (0, 0)),
                  pl.BlockSpec((1, d), lambda i: (0, 0))],
        out_specs=pl.BlockSpec((tm, d), lambda i: (i, 0)),
        out_shape=jax.ShapeDtypeStruct((t, d), F32),
        scratch_shapes=[pltpu.VMEM((2, 2 * tm, d), F32), pltpu.SemaphoreType.DMA((2,))],
        compiler_params=_params(("arbitrary",)),
        name="combine",
    )(pos, pos, ysort, h2f, route, g, b)


def _dispatch(expert, tb, tm):
    t = expert.shape[0]
    a = 2 * t
    ef = expert.reshape(a)
    order = jnp.argsort(ef, stable=True).astype(jnp.int32)
    es = ef[order]
    counts = jnp.sum((ef[:, None] == jnp.arange(N_EXPERTS, dtype=jnp.int32)[None, :]).astype(jnp.int32), axis=0)
    starts = jnp.cumsum(counts) - counts
    nb = (counts + tb - 1) // tb
    bend = jnp.cumsum(nb)
    bstart = bend - nb
    nblk = -(-a // tb) + N_EXPERTS
    blk = jnp.arange(nblk, dtype=jnp.int32)
    bexp = jnp.minimum(jnp.sum((blk[:, None] >= bend[None, :]).astype(jnp.int32), axis=1), N_EXPERTS - 1)
    row0 = starts[bexp] + (blk - bstart[bexp]) * tb
    idx = jnp.clip(row0[:, None] + jnp.arange(tb, dtype=jnp.int32)[None, :], 0, a - 1)
    tok = (order[idx] // 2).astype(jnp.int32).reshape(nblk, 1, tb)
    dest_sorted = bstart[es] * tb + (jnp.arange(a, dtype=jnp.int32) - starts[es])
    _, pos = lax.sort_key_val(order, dest_sorted.astype(jnp.int32))
    pos = pos.reshape(t, 2)
    pos = pos.reshape(t // tm, tm, 2).transpose(0, 2, 1).reshape(t // tm, 1, 2 * tm)
    return bexp, bend[-1:].astype(jnp.int32), tok, pos


def _moe(h2, route, prm, tb, tm):
    b, s, d = h2.shape
    t = b * s
    h2f = h2.reshape(t, d)
    rf = route.reshape(t, LANES)
    expert = rf[:, :2].astype(jnp.int32)
    bexp, nused, tok, pos = _dispatch(expert, tb, tm)
    ysort = _moe_ffn(h2f, bexp, nused, tok, prm["moe_w1"], prm["moe_w3"], prm["moe_w2"], tb)
    y = _combine(pos, ysort, h2f, rf, prm["ln3_g"], prm["ln3_b"], tm)
    return y.reshape(b, s, d)


def _trunk(x, shift_prev, state0, past_k, past_v, mk, mv, prm, lam, lam_init, cfg):
    b, s, _ = x.shape
    pa, q, k, v, kb, vb = _inproj(x, prm["w_in"], cfg["in_bb"], cfg["in_ts"], past_k is None)
    oa, sfin = _rwkv(pa, shift_prev, _state_to_pairs(state0), prm, cfg["rwkv_tb"], cfg["rwkv_chunk"],
                     cfg["rwkv_group"])
    if past_k is None:
        ob = _attn_prompt(lam, q, kb, vb, prm["subln"].reshape(DIFF_V, 1), cfg["attn_tq"], lam_init)
    else:
        ob = _attn_cached(lam, q, kb, vb, past_k, past_v, prm["subln"], lam_init)
    h2, route = _post(x, oa, ob, mk, mv, prm, cfg["post_tm"])
    y = _moe(h2, route, prm, cfg["moe_tb"], cfg["comb_tm"])
    return y, k, v, _pairs_to_state(sfin), pa[:, s - 1:s, :]


def _prep(w, l, lam_scalar_inputs):
    row = lambda a: a[l].reshape(1, -1).astype(F32)
    idx = jnp.arange(D_RWKV, dtype=jnp.int32) // RWKV_HEAD
    wr = jnp.zeros((D_MODEL, LANES), F32)
    wr = wr.at[:, :N_EXPERTS].set(w["moe_w_expert"][l]).at[:, N_EXPERTS:N_EXPERTS + N_GROUPS].set(w["moe_w_group"][l])
    wr_hi = wr.astype(BF16)
    br = jnp.zeros((1, LANES), F32)
    br = br.at[0, :N_EXPERTS].set(w["moe_b_expert"][l]).at[0, N_EXPERTS:N_EXPERTS + N_GROUPS].set(w["moe_b_group"][l])
    return {
        "w_in": w["w_in"][l].astype(BF16),
        "mu": row(w["rwkv_mu"]), "w0": row(w["rwkv_w0"]), "w2": w["rwkv_w2"][l].astype(BF16),
        "a0": row(w["rwkv_a0"]), "a2": w["rwkv_a2"][l].astype(BF16), "g2": w["rwkv_g2"][l].astype(BF16),
        "k_k": row(w["rwkv_k_k"]), "k_a": row(w["rwkv_k_a"]), "r_k": row(w["rwkv_r_k"]),
        "ln_w": row(w["rwkv_ln_w"]), "ln_b": row(w["rwkv_ln_b"]),
        "hsum": (idx[:, None] == idx[None, :]).astype(BF16),
        "subln": row(w["diff_subln"]),
        "w_out": w["w_out"][l].astype(BF16), "wq": w["mem_wq"][l].astype(BF16), "wo": w["mem_wo"][l].astype(BF16),
        "ln1_g": row(w["ln1_g"]), "ln1_b": row(w["ln1_b"]), "ln2_g": row(w["ln2_g"]), "ln2_b": row(w["ln2_b"]),
        "ln3_g": row(w["ln3_g"]), "ln3_b": row(w["ln3_b"]),
        "wr_hi": wr_hi, "wr_lo": (wr - wr_hi.astype(F32)).astype(BF16), "br": br,
        "moe_w1": w["moe_w1"][l].astype(BF16), "moe_w3": w["moe_w3"][l].astype(BF16),
        "moe_w2": w["moe_w2"][l].astype(BF16),
    }


def _tile(n, pref):
    return pref if n % pref == 0 else n


def kernel(x_prompt, x_sample, mem_prompt, cache_diff_k, cache_diff_v, cache_mem_k, cache_mem_v, state_rwkv, state_shift, w_in, rwkv_mu, rwkv_w0, rwkv_w2, rwkv_a0, rwkv_a2, rwkv_g2, rwkv_k_k, rwkv_k_a, rwkv_r_k, rwkv_ln_w, rwkv_ln_b, diff_lq1, diff_lk1, diff_lq2, diff_lk2, diff_subln, w_out, ln1_g, ln1_b, mem_wq, mem_wk, mem_wv, mem_wo, ln2_g, ln2_b, moe_w_group, moe_b_group, moe_w_expert, moe_b_expert, moe_w1, moe_w3, moe_w2, ln3_g, ln3_b):
    w = dict(w_in=w_in, rwkv_mu=rwkv_mu, rwkv_w0=rwkv_w0, rwkv_w2=rwkv_w2, rwkv_a0=rwkv_a0, rwkv_a2=rwkv_a2,
             rwkv_g2=rwkv_g2, rwkv_k_k=rwkv_k_k, rwkv_k_a=rwkv_k_a, rwkv_r_k=rwkv_r_k, rwkv_ln_w=rwkv_ln_w,
             rwkv_ln_b=rwkv_ln_b, diff_subln=diff_subln, w_out=w_out, ln1_g=ln1_g, ln1_b=ln1_b, mem_wq=mem_wq,
             mem_wo=mem_wo, ln2_g=ln2_g, ln2_b=ln2_b, moe_w_group=moe_w_group, moe_b_group=moe_b_group,
             moe_w_expert=moe_w_expert, moe_b_expert=moe_b_expert, moe_w1=moe_w1, moe_w3=moe_w3, moe_w2=moe_w2,
             ln3_g=ln3_g, ln3_b=ln3_b)
    bp, sp, _ = x_prompt.shape
    bs, ss, _ = x_sample.shape
    depth = w_in.shape[0]
    cfg_p = dict(in_bb=1, in_ts=_tile(sp, 512), rwkv_tb=_tile(sp, 256), rwkv_chunk=CHUNK, rwkv_group=_tile(sp, 256) // CHUNK,
                 attn_tq=_tile(sp, 512), post_tm=_tile(sp, 512), moe_tb=256, comb_tm=_tile(bp * sp, 256))
    cfg_s = dict(in_bb=bs, in_ts=ss, rwkv_tb=ss, rwkv_chunk=ss, rwkv_group=1, attn_tq=ss, post_tm=ss, moe_tb=64,
                 comb_tm=_tile(bs * ss, 256))
    yp, ys = x_prompt, x_sample
    outs = [[] for _ in range(10)]
    for l in range(depth):
        prm = _prep(w, l, None)
        lam_init = 0.8 - 0.6 * math.exp(-0.3 * l)
        f = lambda z: z[l].astype(F32)
        lam = (jnp.exp(jnp.sum(f(diff_lq1) * f(diff_lk1))) - jnp.exp(jnp.sum(f(diff_lq2) * f(diff_lk2)))
               + lam_init).reshape(1)
        slopes = jnp.asarray([2.0 ** (-8.0 * (h + 1) / DIFF_HEADS) for h in range(DIFF_HEADS)], F32)
        lam = jnp.concatenate([lam, slopes, jnp.zeros((3,), F32)])
        mk_p, mv_p, mkb, mvb = _memproj(mem_prompt, mem_wk[l].astype(BF16), mem_wv[l].astype(BF16))
        yp, k_p, v_p, st_p, sh_p = _trunk(
            yp, jnp.zeros((bp, 1, C_RWKV), F32), jnp.zeros((bp, 2 * RWKV_PAIRS, RWKV_HEAD, RWKV_HEAD), F32),
            None, None, mkb, mvb, prm, lam, lam_init, cfg_p)
        ys, k_s, v_s, st_s, sh_s = _trunk(
            ys, state_shift[l], state_rwkv[l], cache_diff_k[l], cache_diff_v[l],
            cache_mem_k[l].astype(BF16), cache_mem_v[l].astype(BF16), prm, lam, lam_init, cfg_s)
        for lst, val in zip(outs, (k_p, v_p, mk_p, mv_p, st_p, sh_p, k_s, v_s, st_s, sh_s)):
            lst.append(val)
    return (yp, ys) + tuple(jnp.stack(o) for o in outs)
```

```python
import functools
import math

import jax
import jax.numpy as jnp
from jax import lax
from jax.experimental import pallas as pl
from jax.experimental.pallas import tpu as pltpu

F32 = jnp.float32
BF16 = jnp.bfloat16

D_MODEL = 1024
CHUNK = 64
D_RWKV = 512
RWKV_HEAD = 64
RWKV_PAIRS = D_RWKV // (2 * RWKV_HEAD)
W_LORA = 64
A_LORA = 64
G_LORA = 128
C_RWKV = 3 * D_RWKV + W_LORA + A_LORA + G_LORA
D_DIFF = 512
DIFF_HEADS = 4
DIFF_V = 128
DIFF_QK = 64
C_IN = C_RWKV + 3 * D_DIFF
MEM_HEADS = 4
MEM_HEAD = D_MODEL // MEM_HEADS
N_GROUPS = 4
EXPERTS_PER_GROUP = 8
N_EXPERTS = 32
D_EXPERT = 512
DEPTH = 1
DEEPNORM_ALPHA = (2.0 * DEPTH) ** 0.25
LN_EPS = 1e-5
RMS_EPS = 1e-5
RWKV_GN_EPS = 64e-5
NEG_INF = -1e30
LOG2E = math.log2(math.e)
ALIBI_SLOPES = tuple(2.0 ** (-8.0 * (h + 1) / DIFF_HEADS) for h in range(DIFF_HEADS))
VT_PAD = 16
LANES = 128
SUBLANES = 8
VMEM_LIMIT = 48 * 1024 * 1024


def _params(sem):
    return pltpu.CompilerParams(dimension_semantics=sem, vmem_limit_bytes=VMEM_LIMIT)


def _dot(a, b):
    return jnp.dot(a, b, preferred_element_type=F32)


def _dot_nt(a, b):
    return lax.dot_general(a, b, (((1,), (1,)), ((), ())), preferred_element_type=F32)


def _dot_tn(a, b):
    return lax.dot_general(a, b, (((0,), (0,)), ((), ())), preferred_element_type=F32)


def _split2(x):
    hi = x.astype(BF16)
    lo = (x - hi.astype(F32)).astype(BF16)
    return hi, lo


def _split3(x):
    hi = x.astype(BF16)
    r = x - hi.astype(F32)
    mid = r.astype(BF16)
    lo = (r - mid.astype(F32)).astype(BF16)
    return hi, mid, lo


def _layer_norm(x, g, b):
    mu = jnp.mean(x, axis=-1, keepdims=True)
    xc = x - mu
    var = jnp.mean(xc * xc, axis=-1, keepdims=True)
    return xc * lax.rsqrt(var + LN_EPS) * g + b


def _memproj_kernel(m_ref, wk_ref, wv_ref, k_ref, v_ref, kb_ref, vb_ref):
    m = m_ref[0].astype(BF16)
    k = _dot(m, wk_ref[...])
    v = _dot(m, wv_ref[...])
    k_ref[0] = k
    v_ref[0] = v
    kb_ref[0] = k.astype(BF16)
    vb_ref[0] = v.astype(BF16)


def _memproj(mem, wk, wv):
    b, n, d = mem.shape
    blk = pl.BlockSpec((1, n, d), lambda i: (i, 0, 0))
    wspec = pl.BlockSpec((d, d), lambda i: (0, 0))
    return pl.pallas_call(
        _memproj_kernel,
        grid=(b,),
        in_specs=[blk, wspec, wspec],
        out_specs=[blk, blk, blk, blk],
        out_shape=[jax.ShapeDtypeStruct((b, n, d), F32), jax.ShapeDtypeStruct((b, n, d), F32),
                   jax.ShapeDtypeStruct((b, n, d), BF16), jax.ShapeDtypeStruct((b, n, d), BF16)],
        compiler_params=_params(("arbitrary",)),
        name="memproj",
    )(mem, wk, wv)


def _inproj_kernel(x_ref, w_ref, pa_ref, q_ref, k_ref, v_ref, kb_ref, vb_ref, kb1_ref, *, attn_tile):
    bb, ts, d = x_ref.shape
    rows = bb * ts
    x = x_ref[...].reshape(rows, d).astype(BF16)
    p = _dot(x, w_ref[...])
    pa_ref[...] = p[:, :C_RWKV].reshape(bb, ts, C_RWKV)
    if attn_tile:
        lane = lax.broadcasted_iota(jnp.int32, (rows, LANES), 1)
        kloc = lax.rem(pl.program_id(1) * ts + lax.broadcasted_iota(jnp.int32, (rows, LANES), 0), attn_tile)
        aug = jnp.where(lax.broadcasted_iota(jnp.int32, (VT_PAD, rows), 0) == 0, 1.0, 0.0).astype(BF16)
    for h in range(DIFF_HEADS):
        o = C_RWKV + h * 2 * DIFF_QK
        qh = p[:, o:o + LANES]
        o = C_RWKV + D_DIFF + h * 2 * DIFF_QK
        kh = p[:, o:o + LANES]
        k_ref[:, h] = kh.reshape(bb, ts, LANES)
        o = C_RWKV + 2 * D_DIFF + h * DIFF_V
        vh = p[:, o:o + LANES]
        v_ref[:, h] = vh.reshape(bb, ts, LANES)
        if attn_tile:
            q_ref[0, h] = (qh * (DIFF_QK ** -0.5 * LOG2E)).T.astype(BF16)
            vb_ref[0, h] = jnp.concatenate([vh.T.astype(BF16), aug], axis=0)
            bias = kloc.astype(F32) * (ALIBI_SLOPES[h] * LOG2E)
            b_hi = bias.astype(BF16).astype(F32)
            b_mid = (bias - b_hi).astype(BF16).astype(F32)
            b_lo = bias - b_hi - b_mid
            cols = jnp.where(lane == DIFF_QK, b_hi, jnp.where(lane == DIFF_QK + 1, b_mid,
                             jnp.where(lane == DIFF_QK + 2, b_lo, 0.0)))
            kb_ref[0, h] = jnp.where(lane < DIFF_QK, kh, cols).astype(BF16)
            kb1_ref[0, h] = jnp.where(lane < DIFF_QK, pltpu.roll(kh, DIFF_QK, axis=1), cols).astype(BF16)
        else:
            q_ref[:, h] = (qh * (DIFF_QK ** -0.5)).reshape(bb, ts, LANES).astype(BF16)
            vb_ref[:, h] = vh.reshape(bb, ts, LANES).astype(BF16)
            kb_ref[:, h] = kh.reshape(bb, ts, LANES).astype(BF16)
            kb1_ref[:, h] = kh.reshape(bb, ts, LANES).astype(BF16)


def _inproj(x, w_in_bf, bb, ts, attn_tile):
    b, s, d = x.shape
    hm = pl.BlockSpec((bb, DIFF_HEADS, ts, LANES), lambda i, j: (i, 0, j, 0))
    hshape = (b, DIFF_HEADS, s, LANES)
    if attn_tile:
        assert bb == 1
        qm = pl.BlockSpec((1, DIFF_HEADS, LANES, ts), lambda i, j: (i, 0, 0, j))
        qshape = (b, DIFF_HEADS, LANES, s)
        vm = pl.BlockSpec((1, DIFF_HEADS, LANES + VT_PAD, ts), lambda i, j: (i, 0, 0, j))
        vshape = (b, DIFF_HEADS, LANES + VT_PAD, s)
    else:
        qm, qshape, vm, vshape = hm, hshape, hm, hshape
    return pl.pallas_call(
        functools.partial(_inproj_kernel, attn_tile=attn_tile),
        grid=(b // bb, s // ts),
        in_specs=[pl.BlockSpec((bb, ts, d), lambda i, j: (i, j, 0)),
                  pl.BlockSpec((d, C_IN), lambda i, j: (0, 0))],
        out_specs=[pl.BlockSpec((bb, ts, C_RWKV), lambda i, j: (i, j, 0)), qm, hm, hm, hm, vm, hm],
        out_shape=[jax.ShapeDtypeStruct((b, s, C_RWKV), F32),
                   jax.ShapeDtypeStruct(qshape, BF16),
                   jax.ShapeDtypeStruct(hshape, F32), jax.ShapeDtypeStruct(hshape, F32),
                   jax.ShapeDtypeStruct(hshape, BF16), jax.ShapeDtypeStruct(vshape, BF16),
                   jax.ShapeDtypeStruct(hshape, BF16)],
        compiler_params=_params(("arbitrary", "arbitrary")),
        name="inproj",
    )(x, w_in_bf)


def _rwkv_kernel(pa_ref, shift_ref, s0_ref, mu_ref, w0_ref, w2_ref, a0_ref, a2_ref, g2_ref,
                 kk_ref, ka_ref, rk_ref, lnw_ref, lnb_ref, hsum_ref,
                 o_ref, sfin_ref,
                 carry_ref, al_s, be_s, kb_s, rb_s, v_s, gam_s, o_s, bonus_s, g_s, w1_s, n1_s, c0_s, n2_s,
                 *, chunk, group):
    L = chunk
    tb = pa_ref.shape[1]
    nchunk = tb // L
    t = pl.program_id(1)

    @pl.when(t == 0)
    def _():
        sfin_ref[...] = s0_ref[...]
        carry_ref[...] = shift_ref[0]

    p = pa_ref[0]
    prev = pltpu.roll(p, 1, axis=0)
    row = lax.broadcasted_iota(jnp.int32, (tb, 1), 0)
    prev = jnp.where(row == 0, carry_ref[...], prev)
    carry_ref[...] = p[tb - 1:tb, :]
    ps = p + (prev - p) * mu_ref[...]

    r = ps[:, :D_RWKV]
    k = ps[:, D_RWKV:2 * D_RWKV]
    v = ps[:, 2 * D_RWKV:3 * D_RWKV]
    o0 = 3 * D_RWKV
    w_lo = ps[:, o0:o0 + W_LORA]
    a_lo = ps[:, o0 + W_LORA:o0 + W_LORA + A_LORA]
    g_lo = ps[:, o0 + W_LORA + A_LORA:]

    hsum = hsum_ref[...]

    def headsum(x):
        hi, lo = _split2(x)
        return _dot(hi, hsum) + _dot(lo, hsum)

    z = w0_ref[...] + _dot(jnp.tanh(w_lo).astype(BF16), w2_ref[...])
    lw = -math.exp(-0.5) / (1.0 + jnp.exp(-z))
    a = 1.0 / (1.0 + jnp.exp(-(a0_ref[...] + _dot(a_lo.astype(BF16), a2_ref[...]))))
    g_s[...] = _dot((1.0 / (1.0 + jnp.exp(-g_lo))).astype(BF16), g2_ref[...])
    kk = k * kk_ref[...]
    kk = kk / jnp.maximum(jnp.sqrt(headsum(kk * kk)), 1e-12)
    kmod = k * (1.0 + (a - 1.0) * ka_ref[...])
    bonus_s[...] = headsum(r * kmod * rk_ref[...]) * v

    bi = lax.broadcasted_iota(jnp.int32, (tb, tb), 0)
    bj = lax.broadcasted_iota(jnp.int32, (tb, tb), 1)
    tri = (((bi // L) == (bj // L)) & (bi >= bj)).astype(BF16)
    h3 = _split3(lw)
    cum = _dot(tri, h3[0]) + _dot(tri, h3[1]) + _dot(tri, h3[2])
    gam = jnp.exp(cum)
    igam = jnp.exp(-cum)
    gam_s[...] = gam
    al_s[...] = kk * jnp.exp(cum - lw)
    be_s[...] = kk * a * igam
    kb_s[...] = kmod * igam
    rb_s[...] = r * gam
    v_s[...] = v

    l2 = 2 * L
    ri = lax.broadcasted_iota(jnp.int32, (l2, l2), 0)
    ci = lax.broadcasted_iota(jnp.int32, (l2, l2), 1)
    same = (ri // L) == (ci // L)
    strict = same & (ri > ci)
    incl = same & (ri >= ci)
    eye = (ri == ci).astype(F32)
    lane = lax.broadcasted_iota(jnp.int32, (1, LANES), 1)
    first = lane < RWKV_HEAD

    def stack(x):
        return jnp.concatenate([jnp.where(first, x, 0.0), jnp.where(first, 0.0, x)], axis=0)

    def phase_a(cg, carry):
        cs = []
        for gi in range(group):
            c = cg * group + gi
            sl = pl.ds(pl.multiple_of(c * L, L), L)
            gl = gam_s[pl.ds(c * L + (L - 1), 1), :]
            for j in range(RWKV_PAIRS):
                ls = slice(j * LANES, (j + 1) * LANES)
                cs.append(dict(c=c, j=j, gl=gl[:, ls], al2=stack(al_s[sl, ls]), rb2=stack(rb_s[sl, ls]),
                               be2=stack(be_s[sl, ls]), kb2=stack(kb_s[sl, ls]), v2=stack(v_s[sl, ls])))
        for d in cs:
            x = jnp.concatenate([d["al2"], d["rb2"]], axis=0).astype(BF16)
            y = jnp.concatenate([d["be2"], d["kb2"]], axis=0).astype(BF16)
            gmat = _dot_nt(x, y)
            ab = jnp.where(strict, gmat[:l2, :l2], 0.0)
            d["ak"] = jnp.where(strict, gmat[:l2, l2:], 0.0).astype(BF16)
            d["rbm"] = jnp.where(incl, gmat[l2:, :l2], 0.0).astype(BF16)
            d["rkm"] = jnp.where(incl, gmat[l2:, l2:], 0.0).astype(BF16)
            d["pw"] = ab
            d["tm"] = eye - ab
        n = 2
        while n < L:
            for d in cs:
                pb = d["pw"].astype(BF16)
                d["pw"] = _dot(pb, pb)
            for d in cs:
                d["tm"] = _dot(d["tm"].astype(BF16), (eye + d["pw"]).astype(BF16))
            n *= 2
        for d in cs:
            d["akv"] = _dot(d["ak"], d["v2"].astype(BF16))
        for d in cs:
            zz = jnp.concatenate([d["al2"], d["akv"]], axis=1).astype(BF16)
            d["m"] = _dot(d["tm"].astype(BF16), zz)
        for d in cs:
            rm = _dot(d["rbm"], d["m"].astype(BF16))
            rkv = _dot(d["rkm"], d["v2"].astype(BF16))
            n1_s[d["c"], d["j"]] = (d["rb2"] - rm[:, :LANES]).astype(BF16)
            n2_s[d["c"], d["j"]] = rkv - rm[:, LANES:]
        for d in cs:
            m1 = d["m"][:, :LANES].astype(BF16)
            m2 = d["m"][:, LANES:]
            w1 = _dot_tn(m1, d["be2"].astype(BF16)) * d["gl"]
            lhs = jnp.concatenate([d["v2"], -m2], axis=0).astype(BF16)
            rhs = jnp.concatenate([d["kb2"], d["be2"]], axis=0).astype(BF16)
            w1_s[d["c"], d["j"]] = w1.astype(BF16)
            c0_s[d["c"], d["j"]] = _dot_tn(lhs, rhs) * d["gl"]
        return carry

    lax.fori_loop(0, nchunk // group, phase_a, 0)

    def phase_b(c, carry):
        gl = gam_s[pl.ds(c * L + (L - 1), 1), :]
        outs = []
        for j in range(RWKV_PAIRS):
            sb = sfin_ref[0, j]
            sbb = sb.astype(BF16)
            o2 = _dot_nt(n1_s[c, j], sbb) + n2_s[c, j]
            outs.append(o2[:L] + o2[L:])
            sfin_ref[0, j] = sb * gl[:, j * LANES:(j + 1) * LANES] - _dot(sbb, w1_s[c, j]) + c0_s[c, j]
        o_s[pl.ds(pl.multiple_of(c * L, L), L), :] = jnp.concatenate(outs, axis=1)
        return carry

    lax.fori_loop(0, nchunk, phase_b, 0)

    o = o_s[...]
    inv_n = 1.0 / RWKV_HEAD
    mu_h = headsum(o) * inv_n
    oc = o - mu_h
    var_h = headsum(oc * oc) * inv_n
    on = oc * lax.rsqrt(var_h + RWKV_GN_EPS) * lnw_ref[...] + lnb_ref[...]
    o_ref[0] = ((on + bonus_s[...]) * g_s[...]).astype(o_ref.dtype)


def _rwkv(pa, shift_prev, s0_pairs, prm, tb, chunk, group):
    b, s, _ = pa.shape
    nchunk = tb // chunk
    row = lambda n: pl.BlockSpec((1, n), lambda i, j: (0, 0))
    mat = lambda m, n: pl.BlockSpec((m, n), lambda i, j: (0, 0))
    sspec = pl.BlockSpec((1, RWKV_PAIRS, LANES, LANES), lambda i, j: (i, 0, 0, 0))
    coef = lambda rows, dt: pltpu.VMEM((nchunk, RWKV_PAIRS, rows, LANES), dt)
    return pl.pallas_call(
        functools.partial(_rwkv_kernel, chunk=chunk, group=group),
        grid=(b, s // tb),
        in_specs=[pl.BlockSpec((1, tb, C_RWKV), lambda i, j: (i, j, 0)),
                  pl.BlockSpec((1, 1, C_RWKV), lambda i, j: (i, 0, 0)),
                  sspec,
                  row(C_RWKV), row(D_RWKV), mat(W_LORA, D_RWKV), row(D_RWKV), mat(A_LORA, D_RWKV),
                  mat(G_LORA, D_RWKV), row(D_RWKV), row(D_RWKV), row(D_RWKV), row(D_RWKV), row(D_RWKV),
                  mat(D_RWKV, D_RWKV)],
        out_specs=[pl.BlockSpec((1, tb, D_RWKV), lambda i, j: (i, j, 0)), sspec],
        out_shape=[jax.ShapeDtypeStruct((b, s, D_RWKV), BF16),
                   jax.ShapeDtypeStruct((b, RWKV_PAIRS, LANES, LANES), F32)],
        scratch_shapes=[pltpu.VMEM((1, C_RWKV), F32)] + [pltpu.VMEM((tb, D_RWKV), F32)] * 9
        + [coef(LANES, BF16), coef(2 * chunk, BF16), coef(LANES, F32), coef(2 * chunk, F32)],
        compiler_params=_params(("arbitrary", "arbitrary")),
        name="rwkv",
    )(pa, shift_prev, s0_pairs, prm["mu"], prm["w0"], prm["w2"], prm["a0"], prm["a2"], prm["g2"],
      prm["k_k"], prm["k_a"], prm["r_k"], prm["ln_w"], prm["ln_b"], prm["hsum"])


def _state_to_pairs(s):
    b = s.shape[0]
    s = s.reshape(b, RWKV_PAIRS, 2, RWKV_HEAD, RWKV_HEAD)
    z = jnp.zeros_like(s[:, :, 0])
    top = jnp.concatenate([s[:, :, 0], z], axis=-1)
    bot = jnp.concatenate([z, s[:, :, 1]], axis=-1)
    return jnp.concatenate([top, bot], axis=-2)


def _pairs_to_state(sp):
    b = sp.shape[0]
    h = RWKV_HEAD
    s = jnp.stack([sp[:, :, :h, :h], sp[:, :, h:, h:]], axis=2)
    return s.reshape(b, 2 * RWKV_PAIRS, h, h)


def _diff_finish(accs, ls, lam, sub, lam_init):
    o = accs[0] / ls[0] - lam * (accs[1] / ls[1])
    o = o * lax.rsqrt(jnp.mean(o * o, axis=-1, keepdims=True) + RMS_EPS)
    return o * sub * (1.0 - lam_init)


def _attn_kernel(lam_ref, qt_ref, k0_ref, k1_ref, vt_ref, tab_ref, sub_ref, o_ref, acc_ref, s_ref, m_ref, *, tq,
                 lam_init):
    h = pl.program_id(1)
    qi = pl.program_id(2)
    slope = lam_ref[1 + h] * LOG2E
    qt = qt_ref[0, 0]
    row = lax.broadcasted_iota(jnp.int32, (LANES, 1), 0)
    ones_rows = (row >= DIFF_QK) & (row < DIFF_QK + 3)
    fill = jnp.where(ones_rows, 1.0, 0.0).astype(BF16)

    def rhs(q_first):
        return jnp.where(row < DIFF_QK, q_first, fill)
    rs = (rhs(qt), rhs(jnp.concatenate([qt[DIFF_QK:], qt[:DIFF_QK]], axis=0)))
    k_refs = (k0_ref, k1_ref)
    acc_ref[...] = jnp.zeros_like(acc_ref)
    m_ref[...] = jnp.full_like(m_ref, NEG_INF)

    def scores(j, slot):
        ds = pl.ds(pl.multiple_of(j * tq, tq), tq)
        for c in range(2):
            s_ref[slot, c] = _dot(k_refs[c][0, 0, ds, :], rs[c])

    def softmax_pv(j, slot, diag):
        vt = vt_ref[0, 0, :, pl.ds(pl.multiple_of(j * tq, tq), tq)]
        off = slope * ((j - qi) * tq).astype(F32)
        for c in range(2):
            s = s_ref[slot, c]
            if diag:
                s = s + tab_ref[0]
            m = m_ref[c]
            m_new = jnp.maximum(m, jnp.max(s, axis=0, keepdims=True) + off)
            alpha = jnp.exp2(m - m_new)
            p = jnp.exp2(s - (m_new - off))
            m_ref[c] = m_new
            acc_ref[c] = alpha * acc_ref[c] + _dot(vt, p.astype(BF16))

    def pair(jj, carry):
        j = 2 * jj
        scores(j + 1, 1)
        softmax_pv(j, 0, False)
        scores(jnp.minimum(j + 2, qi), 0)
        softmax_pv(j + 1, 1, False)
        return carry

    scores(0, 0)
    lax.fori_loop(0, qi // 2, pair, 0)

    @pl.when(lax.rem(qi, 2) == 0)
    def _():
        softmax_pv(qi, 0, True)

    @pl.when(lax.rem(qi, 2) == 1)
    def _():
        scores(qi, 1)
        softmax_pv(qi - 1, 0, False)
        softmax_pv(qi, 1, True)

    a0 = acc_ref[0]
    a1 = acc_ref[1]
    o = a0[:DIFF_V] / a0[DIFF_V:DIFF_V + 1] - lam_ref[0] * (a1[:DIFF_V] / a1[DIFF_V:DIFF_V + 1])
    o = o * lax.rsqrt(jnp.mean(o * o, axis=0, keepdims=True) + RMS_EPS)
    o = o * (sub_ref[...] * (1.0 - lam_init))
    o_ref[0] = o.T.astype(o_ref.dtype)


def _attn_diag_table(tq):
    slopes = jnp.asarray(ALIBI_SLOPES, F32) * LOG2E
    pos = jnp.arange(tq, dtype=jnp.int32)
    kp, qp = pos[:, None], pos[None, :]
    rel = jnp.where(kp <= qp, 0, 2 * (qp - kp)).astype(F32)
    vis = (kp // CHUNK) <= (qp // CHUNK)
    return jnp.where(vis[None], slopes[:, None, None] * rel[None], NEG_INF)


def _attn_prompt(lam, qt, k0, k1, vt, subln_col, tq, lam_init):
    b, h, _, s = qt.shape
    kspec = pl.BlockSpec((1, 1, s, LANES), lambda i, j, t: (i, j, 0, 0))
    return pl.pallas_call(
        functools.partial(_attn_kernel, tq=tq, lam_init=lam_init),
        grid=(b, h, s // tq),
        in_specs=[pl.BlockSpec(memory_space=pltpu.SMEM),
                  pl.BlockSpec((1, 1, LANES, tq), lambda i, j, t: (i, j, 0, t)),
                  kspec, kspec,
                  pl.BlockSpec((1, 1, LANES + VT_PAD, s), lambda i, j, t: (i, j, 0, 0)),
                  pl.BlockSpec((1, tq, tq), lambda i, j, t: (j, 0, 0)),
                  pl.BlockSpec((DIFF_V, 1), lambda i, j, t: (0, 0))],
        out_specs=pl.BlockSpec((1, tq, LANES), lambda i, j, t: (i, t, j)),
        out_shape=jax.ShapeDtypeStruct((b, s, D_DIFF), BF16),
        scratch_shapes=[pltpu.VMEM((2, DIFF_V + VT_PAD, tq), F32), pltpu.VMEM((2, 2, tq, tq), F32),
                        pltpu.VMEM((2, 1, tq), F32)],
        compiler_params=_params(("arbitrary", "arbitrary", "arbitrary")),
        name="attn_prompt",
    )(lam, qt, k0, k1, vt, _attn_diag_table(tq), subln_col)


def _attn_cached_kernel(lam_ref, q_ref, kn_ref, vn_ref, kp_ref, vp_ref, sub_ref, o_ref, *, lam_init):
    h = pl.program_id(1)
    slope = lam_ref[1 + h]
    s_new = q_ref.shape[2]
    past = kp_ref.shape[2]
    q = q_ref[0, 0]
    kp = kp_ref[0, 0].astype(BF16)
    vp = vp_ref[0, 0].astype(BF16)
    kn = kn_ref[0, 0]
    vn = vn_ref[0, 0]
    q_pos = past + lax.broadcasted_iota(jnp.int32, (s_new, 1), 0)

    def bias(k_pos):
        dist = jnp.abs(q_pos - k_pos).astype(F32)
        vis = (k_pos // CHUNK) <= (q_pos // CHUNK)
        return jnp.where(vis, -slope * dist, NEG_INF), vis

    b_past, vis_past = bias(lax.broadcasted_iota(jnp.int32, (1, past), 1))
    b_new, vis_new = bias(past + lax.broadcasted_iota(jnp.int32, (1, s_new), 1))
    accs, ls = [], []
    for c in range(2):
        qc = q[:, c * DIFF_QK:(c + 1) * DIFF_QK]
        sp = jnp.where(vis_past, _dot_nt(qc, kp[:, c * DIFF_QK:(c + 1) * DIFF_QK]) + b_past, NEG_INF)
        sn = jnp.where(vis_new, _dot_nt(qc, kn[:, c * DIFF_QK:(c + 1) * DIFF_QK]) + b_new, NEG_INF)
        m = jnp.maximum(jnp.max(sp, axis=-1, keepdims=True), jnp.max(sn, axis=-1, keepdims=True))
        pp = jnp.exp(sp - m)
        pn = jnp.exp(sn - m)
        ls.append(jnp.sum(pp, axis=-1, keepdims=True) + jnp.sum(pn, axis=-1, keepdims=True))
        accs.append(_dot(pp.astype(BF16), vp) + _dot(pn.astype(BF16), vn))
    o = _diff_finish(accs, ls, lam_ref[0], sub_ref[...], lam_init)
    o_ref[0] = o.astype(o_ref.dtype)


def _attn_cached(lam, q, kb, vb, past_k, past_v, subln, lam_init):
    b, h, s, _ = q.shape
    past = past_k.shape[2]
    new = pl.BlockSpec((1, 1, s, LANES), lambda i, j: (i, j, 0, 0))
    old = pl.BlockSpec((1, 1, past, LANES), lambda i, j: (i, j, 0, 0))
    return pl.pallas_call(
        functools.partial(_attn_cached_kernel, lam_init=lam_init),
        grid=(b, h),
        in_specs=[pl.BlockSpec(memory_space=pltpu.SMEM), new, new, new, old, old,
                  pl.BlockSpec((1, DIFF_V), lambda i, j: (0, 0))],
        out_specs=pl.BlockSpec((1, s, LANES), lambda i, j: (i, 0, j)),
        out_shape=jax.ShapeDtypeStruct((b, s, D_DIFF), BF16),
        compiler_params=_params(("arbitrary", "arbitrary")),
        name="attn_cached",
    )(lam, q, kb, vb, past_k, past_v, subln)


def _route(lg):
    tm = lg.shape[0]
    lane = lax.broadcasted_iota(jnp.int32, (tm, LANES), 1)
    lane_f = lane.astype(F32)
    big = 1e9
    low = -3e38
    is_g = (lane >= N_EXPERTS) & (lane < N_EXPERTS + N_GROUPS)
    gl = jnp.where(is_g, lg, low)
    gmax = jnp.max(gl, axis=-1, keepdims=True)
    grp_lane = jnp.min(jnp.where(gl == gmax, lane_f, big), axis=-1, keepdims=True)
    gsum = jnp.sum(jnp.where(is_g, jnp.exp(gl - gmax), 0.0), axis=-1, keepdims=True)
    g_prob = 1.0 / gsum
    grp = grp_lane - float(N_EXPERTS)
    lane_grp = (lane // EXPERTS_PER_GROUP).astype(F32)
    el = jnp.where(lane_grp == grp, lg, low)
    v1 = jnp.max(el, axis=-1, keepdims=True)
    i1 = jnp.min(jnp.where(el == v1, lane_f, big), axis=-1, keepdims=True)
    el2 = jnp.where(lane_f == i1, low, el)
    v2 = jnp.max(el2, axis=-1, keepdims=True)
    i2 = jnp.min(jnp.where(el2 == v2, lane_f, big), axis=-1, keepdims=True)
    e21 = jnp.exp(v2 - v1)
    den = 1.0 + e21
    gate1 = g_prob / den
    gate2 = g_prob * e21 / den
    out = jnp.where(lane == 0, i1, jnp.where(lane == 1, i2, jnp.where(lane == 2, gate1,
                    jnp.where(lane == 3, gate2, 0.0))))
    return out


def _post_kernel(x_ref, oa_ref, ob_ref, mk_ref, mv_ref, wout_ref, wq_ref, wo_ref,
                 g1_ref, b1_ref, g2_ref, b2_ref, wrh_ref, wrl_ref, br_ref, h2_ref, route_ref):
    x = x_ref[0]
    mix = _dot(oa_ref[0], wout_ref[:D_RWKV, :]) + _dot(ob_ref[0], wout_ref[D_RWKV:, :])
    h1 = _layer_norm(DEEPNORM_ALPHA * x + mix, g1_ref[...], b1_ref[...])
    q = (_dot(h1.astype(BF16), wq_ref[...]) * (MEM_HEAD ** -0.5)).astype(BF16)
    heads = []
    for h in range(MEM_HEADS):
        hs = slice(h * MEM_HEAD, (h + 1) * MEM_HEAD)
        s = _dot_nt(q[:, hs], mk_ref[0, :, hs])
        m = jnp.max(s, axis=-1, keepdims=True)
        p = jnp.exp(s - m)
        p = p / jnp.sum(p, axis=-1, keepdims=True)
        heads.append(_dot(p.astype(BF16), mv_ref[0, :, hs]).astype(BF16))
    att = _dot(jnp.concatenate(heads, axis=-1), wo_ref[...])
    h2 = _layer_norm(DEEPNORM_ALPHA * h1 + att, g2_ref[...], b2_ref[...])
    h2_ref[0] = h2
    hi, lo = _split2(h2)
    lg = _dot(hi, wrh_ref[...]) + _dot(hi, wrl_ref[...]) + _dot(lo, wrh_ref[...]) + br_ref[...]
    route_ref[0] = _route(lg)


def _post(x, oa, ob, mk, mv, prm, tm):
    b, s, d = x.shape
    n = mk.shape[1]
    tok = lambda w: pl.BlockSpec((1, tm, w), lambda i, j: (i, j, 0))
    mem = pl.BlockSpec((1, n, d), lambda i, j: (i, 0, 0))
    mat = lambda m_, n_: pl.BlockSpec((m_, n_), lambda i, j: (0, 0))
    return pl.pallas_call(
        _post_kernel,
        grid=(b, s // tm),
        in_specs=[tok(d), tok(D_RWKV), tok(D_DIFF), mem, mem, mat(d, d), mat(d, d), mat(d, d),
                  mat(1, d), mat(1, d), mat(1, d), mat(1, d), mat(d, LANES), mat(d, LANES), mat(1, LANES)],
        out_specs=[tok(d), tok(LANES)],
        out_shape=[jax.ShapeDtypeStruct((b, s, d), F32), jax.ShapeDtypeStruct((b, s, LANES), F32)],
        compiler_params=_params(("arbitrary", "arbitrary")),
        name="post",
    )(x, oa, ob, mk, mv, prm["w_out"], prm["wq"], prm["wo"], prm["ln1_g"], prm["ln1_b"],
      prm["ln2_g"], prm["ln2_b"], prm["wr_hi"], prm["wr_lo"], prm["br"])


def _moe_kernel(bexp_ref, nused_ref, tok_ref, tokn_ref, h2_hbm, w1_ref, w3_ref, w2_ref, y_ref, xbuf, sem):
    i = pl.program_id(0)
    nblk = pl.num_programs(0)
    nused = nused_ref[0]
    ngrp = xbuf.shape[1]
    tb = ngrp * SUBLANES
    slot = lax.rem(i, 2)

    def issue_rows(idx_ref, sl):
        def group(g, c):
            for u in range(SUBLANES):
                r = idx_ref[0, 0, g * SUBLANES + u]
                pltpu.make_async_copy(h2_hbm.at[pl.ds(r, 1)], xbuf.at[sl, g, pl.ds(u, 1)], sem.at[sl]).start()
            return c
        lax.fori_loop(0, ngrp, group, 0)

    def wait_rows(sl):
        pltpu.make_async_copy(xbuf.at[1 - sl], xbuf.at[sl], sem.at[sl]).wait()

    @pl.when(i == 0)
    def _():
        issue_rows(tok_ref, 0)

    @pl.when(i < nused)
    def _():
        wait_rows(slot)
        issue_rows(tokn_ref, 1 - slot)
        x = xbuf[slot].reshape(tb, xbuf.shape[3]).astype(BF16)
        a = _dot(x, w1_ref[0])
        g = _dot(x, w3_ref[0])
        hmid = (a / (1.0 + jnp.exp(-a))) * g
        y_ref[...] = _dot(hmid.astype(BF16), w2_ref[0])

        @pl.when(i == nblk - 1)
        def _():
            wait_rows(1 - slot)

    @pl.when(i >= nused)
    def _():
        y_ref[...] = jnp.zeros_like(y_ref)

        @pl.when(i == nused)
        def _():
            wait_rows(slot)


def _moe_ffn(h2f, blk_exp, nused, tok, w1, w3, w2, tb):
    nblk = tok.shape[0]
    d = h2f.shape[1]
    grid_spec = pltpu.PrefetchScalarGridSpec(
        num_scalar_prefetch=2,
        grid=(nblk,),
        in_specs=[pl.BlockSpec((1, 1, tb), lambda i, be, nu: (i, 0, 0), memory_space=pltpu.SMEM),
                  pl.BlockSpec((1, 1, tb), lambda i, be, nu: (jnp.minimum(i + 1, nblk - 1), 0, 0),
                               memory_space=pltpu.SMEM),
                  pl.BlockSpec(memory_space=pl.ANY),
                  pl.BlockSpec((1, d, D_EXPERT), lambda i, be, nu: (be[i], 0, 0)),
                  pl.BlockSpec((1, d, D_EXPERT), lambda i, be, nu: (be[i], 0, 0)),
                  pl.BlockSpec((1, D_EXPERT, d), lambda i, be, nu: (be[i], 0, 0))],
        out_specs=pl.BlockSpec((tb, d), lambda i, be, nu: (i, 0)),
        scratch_shapes=[pltpu.VMEM((2, tb // SUBLANES, SUBLANES, d), F32), pltpu.SemaphoreType.DMA((2,))],
    )
    return pl.pallas_call(
        _moe_kernel,
        grid_spec=grid_spec,
        out_shape=jax.ShapeDtypeStruct((nblk * tb, d), F32),
        compiler_params=_params(("arbitrary",)),
        name="moe_ffn",
    )(blk_exp, nused, tok, tok, h2f, w1, w3, w2)


def _combine_kernel(pos_ref, posn_ref, ys_hbm, h2_ref, route_ref, g_ref, b_ref, y_ref, gbuf, sem):
    i = pl.program_id(0)
    n = pl.num_programs(0)
    tm = h2_ref.shape[0]
    slot = lax.rem(i, 2)

    ngrp = gbuf.shape[1]
    d = gbuf.shape[3]

    def issue_rows(idx_ref, sl):
        def group(g, c):
            for u in range(SUBLANES):
                r = idx_ref[0, 0, g * SUBLANES + u]
                pltpu.make_async_copy(ys_hbm.at[pl.ds(r, 1)], gbuf.at[sl, g, pl.ds(u, 1)], sem.at[sl]).start()
            return c
        lax.fori_loop(0, ngrp, group, 0)

    def wait_rows(sl):
        pltpu.make_async_copy(gbuf.at[1 - sl], gbuf.at[sl], sem.at[sl]).wait()

    @pl.when(i == 0)
    def _():
        issue_rows(pos_ref, 0)

    wait_rows(slot)
    issue_rows(posn_ref, 1 - slot)
    rt = route_ref[...]
    half = ngrp // 2
    moe = (rt[:, 2:3] * gbuf[slot, :half].reshape(tm, d) + rt[:, 3:4] * gbuf[slot, half:].reshape(tm, d))
    y_ref[...] = _layer_norm(DEEPNORM_ALPHA * h2_ref[...] + moe, g_ref[...], b_ref[...])

    @pl.when(i == n - 1)
    def _():
        wait_rows(1 - slot)


def _combine(pos, ysort, h2f, route, g, b, tm):
    t, d = h2f.shape
    n = t // tm
    return pl.pallas_call(
        _combine_kernel,
        grid=(n,),
        in_specs=[pl.BlockSpec((1, 1, 2 * tm), lambda i: (i, 0, 0), memory_space=pltpu.SMEM),
                  pl.BlockSpec((1, 1, 2 * tm), lambda i: (jnp.minimum(i + 1, n - 1), 0, 0),
                               memory_space=pltpu.SMEM),
                  pl.BlockSpec(memory_space=pl.ANY),
                  pl.BlockSpec((tm, d), lambda i: (i, 0)),
                  pl.BlockSpec((tm, LANES), lambda i: (i, 0)),
                  pl.BlockSpec((1, d), lambda i: (0, 0)),
                  pl.BlockSpec((1, d), lambda i: (0, 0))],
        out_specs=pl.BlockSpec((tm, d), lambda i: (i, 0)),
        out_shape=jax.ShapeDtypeStruct((t, d), F32),
        scratch_shapes=[pltpu.VMEM((2, 2 * tm // SUBLANES, SUBLANES, d), F32), pltpu.SemaphoreType.DMA((2,))],
        compiler_params=_params(("arbitrary",)),
        name="combine",
    )(pos, pos, ysort, h2f, route, g, b)


def _dispatch(expert, tb, tm):
    t = expert.shape[0]
    a = 2 * t
    ef = expert.reshape(a)
    order = jnp.argsort(ef, stable=True).astype(jnp.int32)
    es = ef[order]
    counts = jnp.sum((ef[:, None] == jnp.arange(N_EXPERTS, dtype=jnp.int32)[None, :]).astype(jnp.int32), axis=0)
    starts = jnp.cumsum(counts) - counts
    nb = (counts + tb - 1) // tb
    bend = jnp.cumsum(nb)
    bstart = bend - nb
    nblk = -(-a // tb) + N_EXPERTS
    blk = jnp.arange(nblk, dtype=jnp.int32)
    bexp = jnp.minimum(jnp.sum((blk[:, None] >= bend[None, :]).astype(jnp.int32), axis=1), N_EXPERTS - 1)
    row0 = starts[bexp] + (blk - bstart[bexp]) * tb
    idx = jnp.clip(row0[:, None] + jnp.arange(tb, dtype=jnp.int32)[None, :], 0, a - 1)
    tok = (order[idx] // 2).astype(jnp.int32).reshape(nblk, 1, tb)
    dest_sorted = bstart[es] * tb + (jnp.arange(a, dtype=jnp.int32) - starts[es])
    _, pos = lax.sort_key_val(order, dest_sorted.astype(jnp.int32))
    pos = pos.reshape(t, 2)
    pos = pos.reshape(t // tm, tm, 2).transpose(0, 2, 1).reshape(t // tm, 1, 2 * tm)
    return bexp, bend[-1:].astype(jnp.int32), tok, pos


def _moe(h2, route, prm, tb, tm):
    b, s, d = h2.shape
    t = b * s
    h2f = h2.reshape(t, d)
    rf = route.reshape(t, LANES)
    expert = rf[:, :2].astype(jnp.int32)
    bexp, nused, tok, pos = _dispatch(expert, tb, tm)
    ysort = _moe_ffn(h2f, bexp, nused, tok, prm["moe_w1"], prm["moe_w3"], prm["moe_w2"], tb)
    y = _combine(pos, ysort, h2f, rf, prm["ln3_g"], prm["ln3_b"], tm)
    return y.reshape(b, s, d)


def _trunk(x, shift_prev, state0, past_k, past_v, mk, mv, prm, lam, lam_init, cfg):
    b, s, _ = x.shape
    pa, q, k, v, kb, vb, kb1 = _inproj(x, prm["w_in"], cfg["in_bb"], cfg["in_ts"],
                                       cfg["attn_tq"] if past_k is None else 0)
    oa, sfin = _rwkv(pa, shift_prev, _state_to_pairs(state0), prm, cfg["rwkv_tb"], cfg["rwkv_chunk"],
                     cfg["rwkv_group"])
    if past_k is None:
        ob = _attn_prompt(lam, q, kb, kb1, vb, prm["subln"].reshape(DIFF_V, 1), cfg["attn_tq"], lam_init)
    else:
        ob = _attn_cached(lam, q, kb, vb, past_k, past_v, prm["subln"], lam_init)
    h2, route = _post(x, oa, ob, mk, mv, prm, cfg["post_tm"])
    y = _moe(h2, route, prm, cfg["moe_tb"], cfg["comb_tm"])
    return y, k, v, _pairs_to_state(sfin), pa[:, s - 1:s, :]


def _prep(w, l, lam_scalar_inputs):
    row = lambda a: a[l].reshape(1, -1).astype(F32)
    idx = jnp.arange(D_RWKV, dtype=jnp.int32) // RWKV_HEAD
    wr = jnp.zeros((D_MODEL, LANES), F32)
    wr = wr.at[:, :N_EXPERTS].set(w["moe_w_expert"][l]).at[:, N_EXPERTS:N_EXPERTS + N_GROUPS].set(w["moe_w_group"][l])
    wr_hi = wr.astype(BF16)
    br = jnp.zeros((1, LANES), F32)
    br = br.at[0, :N_EXPERTS].set(w["moe_b_expert"][l]).at[0, N_EXPERTS:N_EXPERTS + N_GROUPS].set(w["moe_b_group"][l])
    return {
        "w_in": w["w_in"][l].astype(BF16),
        "mu": row(w["rwkv_mu"]), "w0": row(w["rwkv_w0"]), "w2": w["rwkv_w2"][l].astype(BF16),
        "a0": row(w["rwkv_a0"]), "a2": w["rwkv_a2"][l].astype(BF16), "g2": w["rwkv_g2"][l].astype(BF16),
        "k_k": row(w["rwkv_k_k"]), "k_a": row(w["rwkv_k_a"]), "r_k": row(w["rwkv_r_k"]),
        "ln_w": row(w["rwkv_ln_w"]), "ln_b": row(w["rwkv_ln_b"]),
        "hsum": (idx[:, None] == idx[None, :]).astype(BF16),
        "subln": row(w["diff_subln"]),
        "w_out": w["w_out"][l].astype(BF16), "wq": w["mem_wq"][l].astype(BF16), "wo": w["mem_wo"][l].astype(BF16),
        "ln1_g": row(w["ln1_g"]), "ln1_b": row(w["ln1_b"]), "ln2_g": row(w["ln2_g"]), "ln2_b": row(w["ln2_b"]),
        "ln3_g": row(w["ln3_g"]), "ln3_b": row(w["ln3_b"]),
        "wr_hi": wr_hi, "wr_lo": (wr - wr_hi.astype(F32)).astype(BF16), "br": br,
        "moe_w1": w["moe_w1"][l].astype(BF16), "moe_w3": w["moe_w3"][l].astype(BF16),
        "moe_w2": w["moe_w2"][l].astype(BF16),
    }


def _tile(n, pref):
    return pref if n % pref == 0 else n


def kernel(x_prompt, x_sample, mem_prompt, cache_diff_k, cache_diff_v, cache_mem_k, cache_mem_v, state_rwkv, state_shift, w_in, rwkv_mu, rwkv_w0, rwkv_w2, rwkv_a0, rwkv_a2, rwkv_g2, rwkv_k_k, rwkv_k_a, rwkv_r_k, rwkv_ln_w, rwkv_ln_b, diff_lq1, diff_lk1, diff_lq2, diff_lk2, diff_subln, w_out, ln1_g, ln1_b, mem_wq, mem_wk, mem_wv, mem_wo, ln2_g, ln2_b, moe_w_group, moe_b_group, moe_w_expert, moe_b_expert, moe_w1, moe_w3, moe_w2, ln3_g, ln3_b):
    w = dict(w_in=w_in, rwkv_mu=rwkv_mu, rwkv_w0=rwkv_w0, rwkv_w2=rwkv_w2, rwkv_a0=rwkv_a0, rwkv_a2=rwkv_a2,
             rwkv_g2=rwkv_g2, rwkv_k_k=rwkv_k_k, rwkv_k_a=rwkv_k_a, rwkv_r_k=rwkv_r_k, rwkv_ln_w=rwkv_ln_w,
             rwkv_ln_b=rwkv_ln_b, diff_subln=diff_subln, w_out=w_out, ln1_g=ln1_g, ln1_b=ln1_b, mem_wq=mem_wq,
             mem_wo=mem_wo, ln2_g=ln2_g, ln2_b=ln2_b, moe_w_group=moe_w_group, moe_b_group=moe_b_group,
             moe_w_expert=moe_w_expert, moe_b_expert=moe_b_expert, moe_w1=moe_w1, moe_w3=moe_w3, moe_w2=moe_w2,
             ln3_g=ln3_g, ln3_b=ln3_b)
    bp, sp, _ = x_prompt.shape
    bs, ss, _ = x_sample.shape
    depth = w_in.shape[0]
    cfg_p = dict(in_bb=1, in_ts=_tile(sp, 512), rwkv_tb=_tile(sp, 256), rwkv_chunk=CHUNK, rwkv_group=_tile(sp, 256) // CHUNK,
                 attn_tq=_tile(sp, 512), post_tm=_tile(sp, 512), moe_tb=256, comb_tm=_tile(bp * sp, 256))
    cfg_s = dict(in_bb=bs, in_ts=ss, rwkv_tb=ss, rwkv_chunk=ss, rwkv_group=1, attn_tq=ss, post_tm=ss, moe_tb=64,
                 comb_tm=_tile(bs * ss, 256))
    yp, ys = x_prompt, x_sample
    outs = [[] for _ in range(10)]
    for l in range(depth):
        prm = _prep(w, l, None)
        lam_init = 0.8 - 0.6 * math.exp(-0.3 * l)
        f = lambda z: z[l].astype(F32)
        lam = (jnp.exp(jnp.sum(f(diff_lq1) * f(diff_lk1))) - jnp.exp(jnp.sum(f(diff_lq2) * f(diff_lk2)))
               + lam_init).reshape(1)
        slopes = jnp.asarray(ALIBI_SLOPES, F32)
        lam = jnp.concatenate([lam, slopes, jnp.zeros((3,), F32)])
        mk_p, mv_p, mkb, mvb = _memproj(mem_prompt, mem_wk[l].astype(BF16), mem_wv[l].astype(BF16))
        yp, k_p, v_p, st_p, sh_p = _trunk(
            yp, jnp.zeros((bp, 1, C_RWKV), F32), jnp.zeros((bp, 2 * RWKV_PAIRS, RWKV_HEAD, RWKV_HEAD), F32),
            None, None, mkb, mvb, prm, lam, lam_init, cfg_p)
        ys, k_s, v_s, st_s, sh_s = _trunk(
            ys, state_shift[l], state_rwkv[l], cache_diff_k[l], cache_diff_v[l],
            cache_mem_k[l].astype(BF16), cache_mem_v[l].astype(BF16), prm, lam, lam_init, cfg_s)
        for lst, val in zip(outs, (k_p, v_p, mk_p, mv_p, st_p, sh_p, k_s, v_s, st_s, sh_s)):
            lst.append(val)
    return (yp, ys) + tuple(jnp.stack(o) for o in outs)
```

```python
import functools
import math

import jax
import jax.numpy as jnp
from jax import lax
from jax.experimental import pallas as pl
from jax.experimental.pallas import tpu as pltpu

F32 = jnp.float32
BF16 = jnp.bfloat16

D_MODEL = 1024
CHUNK = 64
D_RWKV = 512
RWKV_HEAD = 64
RWKV_PAIRS = D_RWKV // (2 * RWKV_HEAD)
W_LORA = 64
A_LORA = 64
G_LORA = 128
C_RWKV = 3 * D_RWKV + W_LORA + A_LORA + G_LORA
D_DIFF = 512
DIFF_HEADS = 4
DIFF_V = 128
DIFF_QK = 64
C_IN = C_RWKV + 3 * D_DIFF
MEM_HEADS = 4
MEM_HEAD = D_MODEL // MEM_HEADS
N_GROUPS = 4
EXPERTS_PER_GROUP = 8
N_EXPERTS = 32
D_EXPERT = 512
DEPTH = 1
DEEPNORM_ALPHA = (2.0 * DEPTH) ** 0.25
LN_EPS = 1e-5
RMS_EPS = 1e-5
RWKV_GN_EPS = 64e-5
NEG_INF = -1e30
LOG2E = math.log2(math.e)
ALIBI_SLOPES = tuple(2.0 ** (-8.0 * (h + 1) / DIFF_HEADS) for h in range(DIFF_HEADS))
VT_PAD = 16
LANES = 128
SUBLANES = 8
VMEM_LIMIT = 48 * 1024 * 1024


def _params(sem):
    return pltpu.CompilerParams(dimension_semantics=sem, vmem_limit_bytes=VMEM_LIMIT)


def _dot(a, b):
    return jnp.dot(a, b, preferred_element_type=F32)


def _dot_nt(a, b):
    return lax.dot_general(a, b, (((1,), (1,)), ((), ())), preferred_element_type=F32)


def _dot_tn(a, b):
    return lax.dot_general(a, b, (((0,), (0,)), ((), ())), preferred_element_type=F32)


def _split2(x):
    hi = x.astype(BF16)
    lo = (x - hi.astype(F32)).astype(BF16)
    return hi, lo


def _split3(x):
    hi = x.astype(BF16)
    r = x - hi.astype(F32)
    mid = r.astype(BF16)
    lo = (r - mid.astype(F32)).astype(BF16)
    return hi, mid, lo


def _layer_norm(x, g, b):
    mu = jnp.mean(x, axis=-1, keepdims=True)
    xc = x - mu
    var = jnp.mean(xc * xc, axis=-1, keepdims=True)
    return xc * lax.rsqrt(var + LN_EPS) * g + b


def _memproj_kernel(m_ref, wk_ref, wv_ref, k_ref, v_ref, kb_ref, vb_ref):
    m = m_ref[0].astype(BF16)
    k = _dot(m, wk_ref[...])
    v = _dot(m, wv_ref[...])
    k_ref[0] = k
    v_ref[0] = v
    kb_ref[0] = k.astype(BF16)
    vb_ref[0] = v.astype(BF16)


def _memproj(mem, wk, wv):
    b, n, d = mem.shape
    blk = pl.BlockSpec((1, n, d), lambda i: (i, 0, 0))
    wspec = pl.BlockSpec((d, d), lambda i: (0, 0))
    return pl.pallas_call(
        _memproj_kernel,
        grid=(b,),
        in_specs=[blk, wspec, wspec],
        out_specs=[blk, blk, blk, blk],
        out_shape=[jax.ShapeDtypeStruct((b, n, d), F32), jax.ShapeDtypeStruct((b, n, d), F32),
                   jax.ShapeDtypeStruct((b, n, d), BF16), jax.ShapeDtypeStruct((b, n, d), BF16)],
        compiler_params=_params(("arbitrary",)),
        name="memproj",
    )(mem, wk, wv)


def _inproj_kernel(x_ref, w_ref, pa_ref, q_ref, k_ref, v_ref, kb_ref, vb_ref, kb1_ref, *, attn_tile):
    bb, ts, d = x_ref.shape
    rows = bb * ts
    x = x_ref[...].reshape(rows, d).astype(BF16)
    pq = _dot(x, w_ref[:, C_RWKV:C_RWKV + D_DIFF])
    pk = _dot(x, w_ref[:, C_RWKV + D_DIFF:C_RWKV + 2 * D_DIFF])
    pv = _dot(x, w_ref[:, C_RWKV + 2 * D_DIFF:])
    pa_ref[...] = _dot(x, w_ref[:, :C_RWKV]).reshape(bb, ts, C_RWKV)
    if attn_tile:
        lane = lax.broadcasted_iota(jnp.int32, (rows, LANES), 1)
        kloc = lax.rem(pl.program_id(1) * ts + lax.broadcasted_iota(jnp.int32, (rows, LANES), 0), attn_tile)
        aug = jnp.where(lax.broadcasted_iota(jnp.int32, (VT_PAD, rows), 0) == 0, 1.0, 0.0).astype(BF16)
    for h in range(DIFF_HEADS):
        hs = slice(h * LANES, (h + 1) * LANES)
        qh = pq[:, hs]
        kh = pk[:, hs]
        k_ref[:, h] = kh.reshape(bb, ts, LANES)
        vh = pv[:, hs]
        v_ref[:, h] = vh.reshape(bb, ts, LANES)
        if attn_tile:
            q_ref[0, h] = (qh * (DIFF_QK ** -0.5 * LOG2E)).T.astype(BF16)
            vb_ref[0, h] = jnp.concatenate([vh.T.astype(BF16), aug], axis=0)
            bias = kloc.astype(F32) * (ALIBI_SLOPES[h] * LOG2E)
            b_hi = bias.astype(BF16).astype(F32)
            b_mid = (bias - b_hi).astype(BF16).astype(F32)
            b_lo = bias - b_hi - b_mid
            cols = jnp.where(lane == DIFF_QK, b_hi, jnp.where(lane == DIFF_QK + 1, b_mid,
                             jnp.where(lane == DIFF_QK + 2, b_lo, 0.0)))
            kb_ref[0, h] = jnp.where(lane < DIFF_QK, kh, cols).astype(BF16)
            kb1_ref[0, h] = jnp.where(lane < DIFF_QK, pltpu.roll(kh, DIFF_QK, axis=1), cols).astype(BF16)
        else:
            q_ref[:, h] = (qh * (DIFF_QK ** -0.5)).reshape(bb, ts, LANES).astype(BF16)
            vb_ref[:, h] = vh.reshape(bb, ts, LANES).astype(BF16)
            kb_ref[:, h] = kh.reshape(bb, ts, LANES).astype(BF16)
            kb1_ref[:, h] = kh.reshape(bb, ts, LANES).astype(BF16)


def _inproj(x, w_in_bf, bb, ts, attn_tile):
    b, s, d = x.shape
    hm = pl.BlockSpec((bb, DIFF_HEADS, ts, LANES), lambda i, j: (i, 0, j, 0))
    hshape = (b, DIFF_HEADS, s, LANES)
    if attn_tile:
        assert bb == 1
        qm = pl.BlockSpec((1, DIFF_HEADS, LANES, ts), lambda i, j: (i, 0, 0, j))
        qshape = (b, DIFF_HEADS, LANES, s)
        vm = pl.BlockSpec((1, DIFF_HEADS, LANES + VT_PAD, ts), lambda i, j: (i, 0, 0, j))
        vshape = (b, DIFF_HEADS, LANES + VT_PAD, s)
    else:
        qm, qshape, vm, vshape = hm, hshape, hm, hshape
    return pl.pallas_call(
        functools.partial(_inproj_kernel, attn_tile=attn_tile),
        grid=(b // bb, s // ts),
        in_specs=[pl.BlockSpec((bb, ts, d), lambda i, j: (i, j, 0)),
                  pl.BlockSpec((d, C_IN), lambda i, j: (0, 0))],
        out_specs=[pl.BlockSpec((bb, ts, C_RWKV), lambda i, j: (i, j, 0)), qm, hm, hm, hm, vm, hm],
        out_shape=[jax.ShapeDtypeStruct((b, s, C_RWKV), F32),
                   jax.ShapeDtypeStruct(qshape, BF16),
                   jax.ShapeDtypeStruct(hshape, F32), jax.ShapeDtypeStruct(hshape, F32),
                   jax.ShapeDtypeStruct(hshape, BF16), jax.ShapeDtypeStruct(vshape, BF16),
                   jax.ShapeDtypeStruct(hshape, BF16)],
        compiler_params=_params(("arbitrary", "arbitrary")),
        name="inproj",
    )(x, w_in_bf)


def _rwkv_kernel(pa_ref, shift_ref, s0_ref, mu_ref, w0_ref, w2_ref, a0_ref, a2_ref, g2_ref,
                 kk_ref, ka_ref, rk_ref, lnw_ref, lnb_ref, hsum_ref,
                 o_ref, sfin_ref,
                 carry_ref, al_s, be_s, kb_s, rb_s, v_s, gam_s, o_s, bonus_s, g_s, w1_s, n1_s, c0_s, n2_s,
                 *, chunk, group):
    L = chunk
    tb = pa_ref.shape[1]
    nchunk = tb // L
    t = pl.program_id(1)

    @pl.when(t == 0)
    def _():
        sfin_ref[...] = s0_ref[...]
        carry_ref[...] = shift_ref[0]

    p = pa_ref[0]
    prev = pltpu.roll(p, 1, axis=0)
    row = lax.broadcasted_iota(jnp.int32, (tb, 1), 0)
    prev = jnp.where(row == 0, carry_ref[...], prev)
    carry_ref[...] = p[tb - 1:tb, :]
    ps = p + (prev - p) * mu_ref[...]

    r = ps[:, :D_RWKV]
    k = ps[:, D_RWKV:2 * D_RWKV]
    v = ps[:, 2 * D_RWKV:3 * D_RWKV]
    o0 = 3 * D_RWKV
    w_lo = ps[:, o0:o0 + W_LORA]
    a_lo = ps[:, o0 + W_LORA:o0 + W_LORA + A_LORA]
    g_lo = ps[:, o0 + W_LORA + A_LORA:]

    hsum = hsum_ref[...]

    def headsum(x):
        return _dot(x.astype(BF16), hsum)

    z = w0_ref[...] + _dot(jnp.tanh(w_lo).astype(BF16), w2_ref[...])
    lw = -math.exp(-0.5) / (1.0 + jnp.exp(-z))
    a = 1.0 / (1.0 + jnp.exp(-(a0_ref[...] + _dot(a_lo.astype(BF16), a2_ref[...]))))
    g_s[...] = _dot((1.0 / (1.0 + jnp.exp(-g_lo))).astype(BF16), g2_ref[...])
    kk = k * kk_ref[...]
    kk = kk / jnp.maximum(jnp.sqrt(headsum(kk * kk)), 1e-12)
    kmod = k * (1.0 + (a - 1.0) * ka_ref[...])
    bonus_s[...] = headsum(r * kmod * rk_ref[...]) * v

    bi = lax.broadcasted_iota(jnp.int32, (tb, tb), 0)
    bj = lax.broadcasted_iota(jnp.int32, (tb, tb), 1)
    tri = (((bi // L) == (bj // L)) & (bi >= bj)).astype(BF16)
    h3 = _split3(lw)
    cum = _dot(tri, h3[0]) + _dot(tri, h3[1]) + _dot(tri, h3[2])
    gam = jnp.exp(cum)
    igam = jnp.exp(-cum)
    gam_s[...] = gam
    al_s[...] = kk * jnp.exp(cum - lw)
    be_s[...] = kk * a * igam
    kb_s[...] = kmod * igam
    rb_s[...] = r * gam
    v_s[...] = v

    l2 = 2 * L
    ri = lax.broadcasted_iota(jnp.int32, (l2, l2), 0)
    ci = lax.broadcasted_iota(jnp.int32, (l2, l2), 1)
    same = (ri // L) == (ci // L)
    strict = same & (ri > ci)
    incl = same & (ri >= ci)
    eye = (ri == ci).astype(F32)
    lane = lax.broadcasted_iota(jnp.int32, (1, LANES), 1)
    first = lane < RWKV_HEAD

    def stack(x):
        return jnp.concatenate([jnp.where(first, x, 0.0), jnp.where(first, 0.0, x)], axis=0)

    def phase_a(cg, carry):
        cs = []
        for gi in range(group):
            c = cg * group + gi
            sl = pl.ds(pl.multiple_of(c * L, L), L)
            gl = gam_s[pl.ds(c * L + (L - 1), 1), :]
            for j in range(RWKV_PAIRS):
                ls = slice(j * LANES, (j + 1) * LANES)
                cs.append(dict(c=c, j=j, gl=gl[:, ls], al2=stack(al_s[sl, ls]), rb2=stack(rb_s[sl, ls]),
                               be2=stack(be_s[sl, ls]), kb2=stack(kb_s[sl, ls]), v2=stack(v_s[sl, ls])))
        for d in cs:
            x = jnp.concatenate([d["al2"], d["rb2"]], axis=0).astype(BF16)
            y = jnp.concatenate([d["be2"], d["kb2"]], axis=0).astype(BF16)
            gmat = _dot_nt(x, y)
            ab = jnp.where(strict, gmat[:l2, :l2], 0.0)
            d["ak"] = jnp.where(strict, gmat[:l2, l2:], 0.0).astype(BF16)
            d["rbm"] = jnp.where(incl, gmat[l2:, :l2], 0.0).astype(BF16)
            d["rkm"] = jnp.where(incl, gmat[l2:, l2:], 0.0).astype(BF16)
            d["pw"] = ab
            d["tm"] = eye - ab
        n = 2
        while n < L:
            for d in cs:
                pb = d["pw"].astype(BF16)
                d["pw"] = _dot(pb, pb)
            for d in cs:
                d["tm"] = _dot(d["tm"].astype(BF16), (eye + d["pw"]).astype(BF16))
            n *= 2
        for d in cs:
            d["akv"] = _dot(d["ak"], d["v2"].astype(BF16))
        for d in cs:
            zz = jnp.concatenate([d["al2"], d["akv"]], axis=1).astype(BF16)
            d["m"] = _dot(d["tm"].astype(BF16), zz)
        for d in cs:
            rm = _dot(d["rbm"], d["m"].astype(BF16))
            rkv = _dot(d["rkm"], d["v2"].astype(BF16))
            n1_s[d["c"], d["j"]] = (d["rb2"] - rm[:, :LANES]).astype(BF16)
            n2_s[d["c"], d["j"]] = rkv - rm[:, LANES:]
        for d in cs:
            m1 = d["m"][:, :LANES].astype(BF16)
            m2 = d["m"][:, LANES:]
            w1 = _dot_tn(m1, d["be2"].astype(BF16)) * d["gl"]
            lhs = jnp.concatenate([d["v2"], -m2], axis=0).astype(BF16)
            rhs = jnp.concatenate([d["kb2"], d["be2"]], axis=0).astype(BF16)
            w1_s[d["c"], d["j"]] = w1.astype(BF16)
            c0_s[d["c"], d["j"]] = _dot_tn(lhs, rhs) * d["gl"]
        return carry

    lax.fori_loop(0, nchunk // group, phase_a, 0)

    def phase_b(c, carry):
        gl = gam_s[pl.ds(c * L + (L - 1), 1), :]
        outs = []
        for j in range(RWKV_PAIRS):
            sb = sfin_ref[0, j]
            sbb = sb.astype(BF16)
            o2 = _dot_nt(n1_s[c, j], sbb) + n2_s[c, j]
            outs.append(o2[:L] + o2[L:])
            sfin_ref[0, j] = sb * gl[:, j * LANES:(j + 1) * LANES] - _dot(sbb, w1_s[c, j]) + c0_s[c, j]
        o_s[pl.ds(pl.multiple_of(c * L, L), L), :] = jnp.concatenate(outs, axis=1)
        return carry

    lax.fori_loop(0, nchunk, phase_b, 0)

    o = o_s[...]
    inv_n = 1.0 / RWKV_HEAD
    mu_h = headsum(o) * inv_n
    oc = o - mu_h
    var_h = headsum(oc * oc) * inv_n
    on = oc * lax.rsqrt(var_h + RWKV_GN_EPS) * lnw_ref[...] + lnb_ref[...]
    o_ref[0] = ((on + bonus_s[...]) * g_s[...]).astype(o_ref.dtype)


def _rwkv(pa, shift_prev, s0_pairs, prm, tb, chunk, group):
    b, s, _ = pa.shape
    nchunk = tb // chunk
    row = lambda n: pl.BlockSpec((1, n), lambda i, j: (0, 0))
    mat = lambda m, n: pl.BlockSpec((m, n), lambda i, j: (0, 0))
    sspec = pl.BlockSpec((1, RWKV_PAIRS, LANES, LANES), lambda i, j: (i, 0, 0, 0))
    coef = lambda rows, dt: pltpu.VMEM((nchunk, RWKV_PAIRS, rows, LANES), dt)
    return pl.pallas_call(
        functools.partial(_rwkv_kernel, chunk=chunk, group=group),
        grid=(b, s // tb),
        in_specs=[pl.BlockSpec((1, tb, C_RWKV), lambda i, j: (i, j, 0)),
                  pl.BlockSpec((1, 1, C_RWKV), lambda i, j: (i, 0, 0)),
                  sspec,
                  row(C_RWKV), row(D_RWKV), mat(W_LORA, D_RWKV), row(D_RWKV), mat(A_LORA, D_RWKV),
                  mat(G_LORA, D_RWKV), row(D_RWKV), row(D_RWKV), row(D_RWKV), row(D_RWKV), row(D_RWKV),
                  mat(D_RWKV, D_RWKV)],
        out_specs=[pl.BlockSpec((1, tb, D_RWKV), lambda i, j: (i, j, 0)), sspec],
        out_shape=[jax.ShapeDtypeStruct((b, s, D_RWKV), BF16),
                   jax.ShapeDtypeStruct((b, RWKV_PAIRS, LANES, LANES), F32)],
        scratch_shapes=[pltpu.VMEM((1, C_RWKV), F32)] + [pltpu.VMEM((tb, D_RWKV), F32)] * 9
        + [coef(LANES, BF16), coef(2 * chunk, BF16), coef(LANES, F32), coef(2 * chunk, F32)],
        compiler_params=_params(("arbitrary", "arbitrary")),
        name="rwkv",
    )(pa, shift_prev, s0_pairs, prm["mu"], prm["w0"], prm["w2"], prm["a0"], prm["a2"], prm["g2"],
      prm["k_k"], prm["k_a"], prm["r_k"], prm["ln_w"], prm["ln_b"], prm["hsum"])


def _state_to_pairs(s):
    b = s.shape[0]
    s = s.reshape(b, RWKV_PAIRS, 2, RWKV_HEAD, RWKV_HEAD)
    z = jnp.zeros_like(s[:, :, 0])
    top = jnp.concatenate([s[:, :, 0], z], axis=-1)
    bot = jnp.concatenate([z, s[:, :, 1]], axis=-1)
    return jnp.concatenate([top, bot], axis=-2)


def _pairs_to_state(sp):
    b = sp.shape[0]
    h = RWKV_HEAD
    s = jnp.stack([sp[:, :, :h, :h], sp[:, :, h:, h:]], axis=2)
    return s.reshape(b, 2 * RWKV_PAIRS, h, h)


def _diff_finish(accs, ls, lam, sub, lam_init):
    o = accs[0] / ls[0] - lam * (accs[1] / ls[1])
    o = o * lax.rsqrt(jnp.mean(o * o, axis=-1, keepdims=True) + RMS_EPS)
    return o * sub * (1.0 - lam_init)


def _attn_kernel(lam_ref, qt_ref, k0_ref, k1_ref, vt_ref, tab_ref, sub_ref, o_ref, acc_ref, s_ref, m_ref, *, tq,
                 lam_init):
    h = pl.program_id(1)
    qi = pl.program_id(2)
    slope = lam_ref[1 + h] * LOG2E
    qt = qt_ref[0, 0]
    row = lax.broadcasted_iota(jnp.int32, (LANES, 1), 0)
    ones_rows = (row >= DIFF_QK) & (row < DIFF_QK + 3)
    fill = jnp.where(ones_rows, 1.0, 0.0).astype(BF16)

    def rhs(q_first):
        return jnp.where(row < DIFF_QK, q_first, fill)
    rs = (rhs(qt), rhs(jnp.concatenate([qt[DIFF_QK:], qt[:DIFF_QK]], axis=0)))
    k_refs = (k0_ref, k1_ref)
    acc_ref[...] = jnp.zeros_like(acc_ref)
    m_ref[...] = jnp.full_like(m_ref, NEG_INF)

    def scores(j, slot):
        ds = pl.ds(pl.multiple_of(j * tq, tq), tq)
        for c in range(2):
            s_ref[slot, c] = _dot(k_refs[c][0, 0, ds, :], rs[c])

    def softmax_pv(j, slot, diag):
        vt = vt_ref[0, 0, :, pl.ds(pl.multiple_of(j * tq, tq), tq)]
        off = slope * ((j - qi) * tq).astype(F32)
        for c in range(2):
            s = s_ref[slot, c]
            if diag:
                s = s + tab_ref[0]
            m = m_ref[c]
            m_new = jnp.maximum(m, jnp.max(s, axis=0, keepdims=True) + off)
            alpha = jnp.exp2(m - m_new)
            p = jnp.exp2(s - (m_new - off))
            m_ref[c] = m_new
            acc_ref[c] = alpha * acc_ref[c] + _dot(vt, p.astype(BF16))

    def pair(jj, carry):
        j = 2 * jj
        scores(j + 1, 1)
        softmax_pv(j, 0, False)
        scores(jnp.minimum(j + 2, qi), 0)
        softmax_pv(j + 1, 1, False)
        return carry

    scores(0, 0)
    lax.fori_loop(0, qi // 2, pair, 0)

    @pl.when(lax.rem(qi, 2) == 0)
    def _():
        softmax_pv(qi, 0, True)

    @pl.when(lax.rem(qi, 2) == 1)
    def _():
        scores(qi, 1)
        softmax_pv(qi - 1, 0, False)
        softmax_pv(qi, 1, True)

    a0 = acc_ref[0]
    a1 = acc_ref[1]
    o = a0[:DIFF_V] / a0[DIFF_V:DIFF_V + 1] - lam_ref[0] * (a1[:DIFF_V] / a1[DIFF_V:DIFF_V + 1])
    o = o * lax.rsqrt(jnp.mean(o * o, axis=0, keepdims=True) + RMS_EPS)
    o = o * (sub_ref[...] * (1.0 - lam_init))
    o_ref[0] = o.T.astype(o_ref.dtype)


def _attn_diag_table(tq):
    slopes = jnp.asarray(ALIBI_SLOPES, F32) * LOG2E
    pos = jnp.arange(tq, dtype=jnp.int32)
    kp, qp = pos[:, None], pos[None, :]
    rel = jnp.where(kp <= qp, 0, 2 * (qp - kp)).astype(F32)
    vis = (kp // CHUNK) <= (qp // CHUNK)
    return jnp.where(vis[None], slopes[:, None, None] * rel[None], NEG_INF)


def _attn_prompt(lam, qt, k0, k1, vt, subln_col, tq, lam_init):
    b, h, _, s = qt.shape
    kspec = pl.BlockSpec((1, 1, s, LANES), lambda i, j, t: (i, j, 0, 0))
    return pl.pallas_call(
        functools.partial(_attn_kernel, tq=tq, lam_init=lam_init),
        grid=(b, h, s // tq),
        in_specs=[pl.BlockSpec(memory_space=pltpu.SMEM),
                  pl.BlockSpec((1, 1, LANES, tq), lambda i, j, t: (i, j, 0, t)),
                  kspec, kspec,
                  pl.BlockSpec((1, 1, LANES + VT_PAD, s), lambda i, j, t: (i, j, 0, 0)),
                  pl.BlockSpec((1, tq, tq), lambda i, j, t: (j, 0, 0)),
                  pl.BlockSpec((DIFF_V, 1), lambda i, j, t: (0, 0))],
        out_specs=pl.BlockSpec((1, tq, LANES), lambda i, j, t: (i, t, j)),
        out_shape=jax.ShapeDtypeStruct((b, s, D_DIFF), BF16),
        scratch_shapes=[pltpu.VMEM((2, DIFF_V + VT_PAD, tq), F32), pltpu.VMEM((2, 2, tq, tq), F32),
                        pltpu.VMEM((2, 1, tq), F32)],
        compiler_params=_params(("arbitrary", "arbitrary", "arbitrary")),
        name="attn_prompt",
    )(lam, qt, k0, k1, vt, _attn_diag_table(tq), subln_col)


def _attn_cached_kernel(lam_ref, q_ref, kn_ref, vn_ref, kp_ref, vp_ref, sub_ref, o_ref, *, lam_init):
    h = pl.program_id(1)
    slope = lam_ref[1 + h]
    s_new = q_ref.shape[2]
    past = kp_ref.shape[2]
    q = q_ref[0, 0]
    kp = kp_ref[0, 0].astype(BF16)
    vp = vp_ref[0, 0].astype(BF16)
    kn = kn_ref[0, 0]
    vn = vn_ref[0, 0]
    q_pos = past + lax.broadcasted_iota(jnp.int32, (s_new, 1), 0)

    def bias(k_pos):
        dist = jnp.abs(q_pos - k_pos).astype(F32)
        vis = (k_pos // CHUNK) <= (q_pos // CHUNK)
        return jnp.where(vis, -slope * dist, NEG_INF), vis

    b_past, vis_past = bias(lax.broadcasted_iota(jnp.int32, (1, past), 1))
    b_new, vis_new = bias(past + lax.broadcasted_iota(jnp.int32, (1, s_new), 1))
    accs, ls = [], []
    for c in range(2):
        qc = q[:, c * DIFF_QK:(c + 1) * DIFF_QK]
        sp = jnp.where(vis_past, _dot_nt(qc, kp[:, c * DIFF_QK:(c + 1) * DIFF_QK]) + b_past, NEG_INF)
        sn = jnp.where(vis_new, _dot_nt(qc, kn[:, c * DIFF_QK:(c + 1) * DIFF_QK]) + b_new, NEG_INF)
        m = jnp.maximum(jnp.max(sp, axis=-1, keepdims=True), jnp.max(sn, axis=-1, keepdims=True))
        pp = jnp.exp(sp - m)
        pn = jnp.exp(sn - m)
        ls.append(jnp.sum(pp, axis=-1, keepdims=True) + jnp.sum(pn, axis=-1, keepdims=True))
        accs.append(_dot(pp.astype(BF16), vp) + _dot(pn.astype(BF16), vn))
    o = _diff_finish(accs, ls, lam_ref[0], sub_ref[...], lam_init)
    o_ref[0] = o.astype(o_ref.dtype)


def _attn_cached(lam, q, kb, vb, past_k, past_v, subln, lam_init):
    b, h, s, _ = q.shape
    past = past_k.shape[2]
    new = pl.BlockSpec((1, 1, s, LANES), lambda i, j: (i, j, 0, 0))
    old = pl.BlockSpec((1, 1, past, LANES), lambda i, j: (i, j, 0, 0))
    return pl.pallas_call(
        functools.partial(_attn_cached_kernel, lam_init=lam_init),
        grid=(b, h),
        in_specs=[pl.BlockSpec(memory_space=pltpu.SMEM), new, new, new, old, old,
                  pl.BlockSpec((1, DIFF_V), lambda i, j: (0, 0))],
        out_specs=pl.BlockSpec((1, s, LANES), lambda i, j: (i, 0, j)),
        out_shape=jax.ShapeDtypeStruct((b, s, D_DIFF), BF16),
        compiler_params=_params(("arbitrary", "arbitrary")),
        name="attn_cached",
    )(lam, q, kb, vb, past_k, past_v, subln)


def _route(lg):
    tm = lg.shape[0]
    lane = lax.broadcasted_iota(jnp.int32, (tm, LANES), 1)
    lane_f = lane.astype(F32)
    big = 1e9
    low = -3e38
    is_g = (lane >= N_EXPERTS) & (lane < N_EXPERTS + N_GROUPS)
    gl = jnp.where(is_g, lg, low)
    gmax = jnp.max(gl, axis=-1, keepdims=True)
    grp_lane = jnp.min(jnp.where(gl == gmax, lane_f, big), axis=-1, keepdims=True)
    gsum = jnp.sum(jnp.where(is_g, jnp.exp(gl - gmax), 0.0), axis=-1, keepdims=True)
    g_prob = 1.0 / gsum
    grp = grp_lane - float(N_EXPERTS)
    lane_grp = (lane // EXPERTS_PER_GROUP).astype(F32)
    el = jnp.where(lane_grp == grp, lg, low)
    v1 = jnp.max(el, axis=-1, keepdims=True)
    i1 = jnp.min(jnp.where(el == v1, lane_f, big), axis=-1, keepdims=True)
    el2 = jnp.where(lane_f == i1, low, el)
    v2 = jnp.max(el2, axis=-1, keepdims=True)
    i2 = jnp.min(jnp.where(el2 == v2, lane_f, big), axis=-1, keepdims=True)
    e21 = jnp.exp(v2 - v1)
    den = 1.0 + e21
    gate1 = g_prob / den
    gate2 = g_prob * e21 / den
    out = jnp.where(lane == 0, i1, jnp.where(lane == 1, i2, jnp.where(lane == 2, gate1,
                    jnp.where(lane == 3, gate2, 0.0))))
    return out


def _post_kernel(x_ref, oa_ref, ob_ref, mk_ref, mv_ref, wout_ref, wq_ref, wo_ref,
                 g1_ref, b1_ref, g2_ref, b2_ref, wrh_ref, wrl_ref, br_ref, h2_ref, route_ref, *, nsplit):
    tm = x_ref.shape[1]
    rows = tm // nsplit
    parts = [slice(i * rows, (i + 1) * rows) for i in range(nsplit)]
    mix = [_dot(oa_ref[0, r, :], wout_ref[:D_RWKV, :]) + _dot(ob_ref[0, r, :], wout_ref[D_RWKV:, :]) for r in parts]
    h1 = [_layer_norm(DEEPNORM_ALPHA * x_ref[0, r, :] + m, g1_ref[...], b1_ref[...]) for r, m in zip(parts, mix)]
    q = [(_dot(h.astype(BF16), wq_ref[...]) * (MEM_HEAD ** -0.5)).astype(BF16) for h in h1]
    heads = [[] for _ in parts]
    for h in range(MEM_HEADS):
        hs = slice(h * MEM_HEAD, (h + 1) * MEM_HEAD)
        for i in range(nsplit):
            s = _dot_nt(q[i][:, hs], mk_ref[0, :, hs])
            m = jnp.max(s, axis=-1, keepdims=True)
            p = jnp.exp(s - m)
            p = p / jnp.sum(p, axis=-1, keepdims=True)
            heads[i].append(_dot(p.astype(BF16), mv_ref[0, :, hs]).astype(BF16))
    att = [_dot(jnp.concatenate(hd, axis=-1), wo_ref[...]) for hd in heads]
    h2 = [_layer_norm(DEEPNORM_ALPHA * h + a, g2_ref[...], b2_ref[...]) for h, a in zip(h1, att)]
    for r, h in zip(parts, h2):
        h2_ref[0, r, :] = h
        hi, lo = _split2(h)
        lg = _dot(hi, wrh_ref[...]) + _dot(hi, wrl_ref[...]) + _dot(lo, wrh_ref[...]) + br_ref[...]
        route_ref[0, r, :] = _route(lg)


def _post(x, oa, ob, mk, mv, prm, tm, nsplit):
    b, s, d = x.shape
    n = mk.shape[1]
    tok = lambda w: pl.BlockSpec((1, tm, w), lambda i, j: (i, j, 0))
    mem = pl.BlockSpec((1, n, d), lambda i, j: (i, 0, 0))
    mat = lambda m_, n_: pl.BlockSpec((m_, n_), lambda i, j: (0, 0))
    return pl.pallas_call(
        functools.partial(_post_kernel, nsplit=nsplit),
        grid=(b, s // tm),
        in_specs=[tok(d), tok(D_RWKV), tok(D_DIFF), mem, mem, mat(d, d), mat(d, d), mat(d, d),
                  mat(1, d), mat(1, d), mat(1, d), mat(1, d), mat(d, LANES), mat(d, LANES), mat(1, LANES)],
        out_specs=[tok(d), tok(LANES)],
        out_shape=[jax.ShapeDtypeStruct((b, s, d), F32), jax.ShapeDtypeStruct((b, s, LANES), F32)],
        compiler_params=_params(("arbitrary", "arbitrary")),
        name="post",
    )(x, oa, ob, mk, mv, prm["w_out"], prm["wq"], prm["wo"], prm["ln1_g"], prm["ln1_b"],
      prm["ln2_g"], prm["ln2_b"], prm["wr_hi"], prm["wr_lo"], prm["br"])


def _moe_kernel(bexp_ref, nused_ref, tok_ref, tokn_ref, h2_hbm, w1_ref, w3_ref, w2_ref, y_ref, xbuf, sem):
    i = pl.program_id(0)
    nblk = pl.num_programs(0)
    nused = nused_ref[0]
    ngrp = xbuf.shape[1]
    tb = ngrp * SUBLANES
    slot = lax.rem(i, 2)

    def issue_rows(idx_ref, sl):
        def group(g, c):
            for u in range(SUBLANES):
                r = idx_ref[0, 0, g * SUBLANES + u]
                pltpu.make_async_copy(h2_hbm.at[pl.ds(r, 1)], xbuf.at[sl, g, pl.ds(u, 1)], sem.at[sl]).start()
            return c
        lax.fori_loop(0, ngrp, group, 0)

    def wait_rows(sl):
        pltpu.make_async_copy(xbuf.at[1 - sl], xbuf.at[sl], sem.at[sl]).wait()

    @pl.when(i == 0)
    def _():
        issue_rows(tok_ref, 0)

    @pl.when(i < nused)
    def _():
        wait_rows(slot)
        issue_rows(tokn_ref, 1 - slot)
        x = xbuf[slot].reshape(tb, xbuf.shape[3]).astype(BF16)
        a = _dot(x, w1_ref[0])
        g = _dot(x, w3_ref[0])
        hmid = (a / (1.0 + jnp.exp(-a))) * g
        y_ref[...] = _dot(hmid.astype(BF16), w2_ref[0])

        @pl.when(i == nblk - 1)
        def _():
            wait_rows(1 - slot)

    @pl.when(i >= nused)
    def _():
        y_ref[...] = jnp.zeros_like(y_ref)

        @pl.when(i == nused)
        def _():
            wait_rows(slot)


def _moe_ffn(h2f, blk_exp, nused, tok, w1, w3, w2, tb):
    nblk = tok.shape[0]
    d = h2f.shape[1]
    grid_spec = pltpu.PrefetchScalarGridSpec(
        num_scalar_prefetch=2,
        grid=(nblk,),
        in_specs=[pl.BlockSpec((1, 1, tb), lambda i, be, nu: (i, 0, 0), memory_space=pltpu.SMEM),
                  pl.BlockSpec((1, 1, tb), lambda i, be, nu: (jnp.minimum(i + 1, nblk - 1), 0, 0),
                               memory_space=pltpu.SMEM),
                  pl.BlockSpec(memory_space=pl.ANY),
                  pl.BlockSpec((1, d, D_EXPERT), lambda i, be, nu: (be[i], 0, 0)),
                  pl.BlockSpec((1, d, D_EXPERT), lambda i, be, nu: (be[i], 0, 0)),
                  pl.BlockSpec((1, D_EXPERT, d), lambda i, be, nu: (be[i], 0, 0))],
        out_specs=pl.BlockSpec((tb, d), lambda i, be, nu: (i, 0)),
        scratch_shapes=[pltpu.VMEM((2, tb // SUBLANES, SUBLANES, d), F32), pltpu.SemaphoreType.DMA((2,))],
    )
    return pl.pallas_call(
        _moe_kernel,
        grid_spec=grid_spec,
        out_shape=jax.ShapeDtypeStruct((nblk * tb, d), F32),
        compiler_params=_params(("arbitrary",)),
        name="moe_ffn",
    )(blk_exp, nused, tok, tok, h2f, w1, w3, w2)


def _combine_kernel(pos_ref, posn_ref, ys_hbm, h2_ref, route_ref, g_ref, b_ref, y_ref, gbuf, sem):
    i = pl.program_id(0)
    n = pl.num_programs(0)
    tm = h2_ref.shape[0]
    slot = lax.rem(i, 2)

    ngrp = gbuf.shape[1]
    d = gbuf.shape[3]

    def issue_rows(idx_ref, sl):
        def group(g, c):
            for u in range(SUBLANES):
                r = idx_ref[0, 0, g * SUBLANES + u]
                pltpu.make_async_copy(ys_hbm.at[pl.ds(r, 1)], gbuf.at[sl, g, pl.ds(u, 1)], sem.at[sl]).start()
            return c
        lax.fori_loop(0, ngrp, group, 0)

    def wait_rows(sl):
        pltpu.make_async_copy(gbuf.at[1 - sl], gbuf.at[sl], sem.at[sl]).wait()

    @pl.when(i == 0)
    def _():
        issue_rows(pos_ref, 0)

    wait_rows(slot)
    issue_rows(posn_ref, 1 - slot)
    rt = route_ref[...]
    half = ngrp // 2
    moe = (rt[:, 2:3] * gbuf[slot, :half].reshape(tm, d) + rt[:, 3:4] * gbuf[slot, half:].reshape(tm, d))
    y_ref[...] = _layer_norm(DEEPNORM_ALPHA * h2_ref[...] + moe, g_ref[...], b_ref[...])

    @pl.when(i == n - 1)
    def _():
        wait_rows(1 - slot)


def _combine(pos, ysort, h2f, route, g, b, tm):
    t, d = h2f.shape
    n = t // tm
    return pl.pallas_call(
        _combine_kernel,
        grid=(n,),
        in_specs=[pl.BlockSpec((1, 1, 2 * tm), lambda i: (i, 0, 0), memory_space=pltpu.SMEM),
                  pl.BlockSpec((1, 1, 2 * tm), lambda i: (jnp.minimum(i + 1, n - 1), 0, 0),
                               memory_space=pltpu.SMEM),
                  pl.BlockSpec(memory_space=pl.ANY),
                  pl.BlockSpec((tm, d), lambda i: (i, 0)),
                  pl.BlockSpec((tm, LANES), lambda i: (i, 0)),
                  pl.BlockSpec((1, d), lambda i: (0, 0)),
                  pl.BlockSpec((1, d), lambda i: (0, 0))],
        out_specs=pl.BlockSpec((tm, d), lambda i: (i, 0)),
        out_shape=jax.ShapeDtypeStruct((t, d), F32),
        scratch_shapes=[pltpu.VMEM((2, 2 * tm // SUBLANES, SUBLANES, d), F32), pltpu.SemaphoreType.DMA((2,))],
        compiler_params=_params(("arbitrary",)),
        name="combine",
    )(pos, pos, ysort, h2f, route, g, b)


def _dispatch(expert, tb, tm):
    t = expert.shape[0]
    a = 2 * t
    ef = expert.reshape(a)
    order = jnp.argsort(ef, stable=True).astype(jnp.int32)
    es = ef[order]
    counts = jnp.sum((ef[:, None] == jnp.arange(N_EXPERTS, dtype=jnp.int32)[None, :]).astype(jnp.int32), axis=0)
    starts = jnp.cumsum(counts) - counts
    nb = (counts + tb - 1) // tb
    bend = jnp.cumsum(nb)
    bstart = bend - nb
    nblk = -(-a // tb) + N_EXPERTS
    blk = jnp.arange(nblk, dtype=jnp.int32)
    bexp = jnp.minimum(jnp.sum((blk[:, None] >= bend[None, :]).astype(jnp.int32), axis=1), N_EXPERTS - 1)
    row0 = starts[bexp] + (blk - bstart[bexp]) * tb
    idx = jnp.clip(row0[:, None] + jnp.arange(tb, dtype=jnp.int32)[None, :], 0, a - 1)
    tok = (order[idx] // 2).astype(jnp.int32).reshape(nblk, 1, tb)
    dest_sorted = bstart[es] * tb + (jnp.arange(a, dtype=jnp.int32) - starts[es])
    _, pos = lax.sort_key_val(order, dest_sorted.astype(jnp.int32))
    pos = pos.reshape(t, 2)
    pos = pos.reshape(t // tm, tm, 2).transpose(0, 2, 1).reshape(t // tm, 1, 2 * tm)
    return bexp, bend[-1:].astype(jnp.int32), tok, pos


def _moe(h2, route, prm, tb, tm):
    b, s, d = h2.shape
    t = b * s
    h2f = h2.reshape(t, d)
    rf = route.reshape(t, LANES)
    expert = rf[:, :2].astype(jnp.int32)
    bexp, nused, tok, pos = _dispatch(expert, tb, tm)
    ysort = _moe_ffn(h2f, bexp, nused, tok, prm["moe_w1"], prm["moe_w3"], prm["moe_w2"], tb)
    y = _combine(pos, ysort, h2f, rf, prm["ln3_g"], prm["ln3_b"], tm)
    return y.reshape(b, s, d)


N_PAIR_CLASSES = N_GROUPS * EXPERTS_PER_GROUP * EXPERTS_PER_GROUP
N_REAL_PAIRS = N_GROUPS * (EXPERTS_PER_GROUP * (EXPERTS_PER_GROUP - 1) // 2)


def _moe_pair_kernel(elo_ref, ehi_ref, nused_ref, tok_ref, tokn_ref, h2_hbm, gate_ref,
                     w1a_ref, w3a_ref, w2a_ref, w1b_ref, w3b_ref, w2b_ref, y_ref, xbuf, sem):
    i = pl.program_id(0)
    nblk = pl.num_programs(0)
    nused = nused_ref[0]
    ngrp = xbuf.shape[1]
    tb = ngrp * SUBLANES
    slot = lax.rem(i, 2)

    def issue_rows(idx_ref, sl):
        def group(g, c):
            for u in range(SUBLANES):
                r = idx_ref[0, 0, g * SUBLANES + u]
                pltpu.make_async_copy(h2_hbm.at[pl.ds(r, 1)], xbuf.at[sl, g, pl.ds(u, 1)], sem.at[sl]).start()
            return c
        lax.fori_loop(0, ngrp, group, 0)

    def wait_rows(sl):
        pltpu.make_async_copy(xbuf.at[1 - sl], xbuf.at[sl], sem.at[sl]).wait()

    def ffn(x, w1_ref, w3_ref, w2_ref):
        a = _dot(x, w1_ref[0])
        g = _dot(x, w3_ref[0])
        hmid = (a / (1.0 + jnp.exp(-a))) * g
        return _dot(hmid.astype(BF16), w2_ref[0])

    @pl.when(i == 0)
    def _():
        issue_rows(tok_ref, 0)

    @pl.when(i < nused)
    def _():
        wait_rows(slot)
        issue_rows(tokn_ref, 1 - slot)
        x = xbuf[slot].reshape(tb, xbuf.shape[3]).astype(BF16)
        gt = gate_ref[...]
        y_ref[...] = gt[:, 0:1] * ffn(x, w1a_ref, w3a_ref, w2a_ref) + gt[:, 1:2] * ffn(x, w1b_ref, w3b_ref, w2b_ref)

        @pl.when(i == nblk - 1)
        def _():
            wait_rows(1 - slot)

    @pl.when(i >= nused)
    def _():
        y_ref[...] = jnp.zeros_like(y_ref)

        @pl.when(i == nused)
        def _():
            wait_rows(slot)


def _moe_pair_ffn(h2f, elo, ehi, nused, tok, gates, w1, w3, w2, tb):
    nblk = tok.shape[0]
    d = h2f.shape[1]
    wspec = lambda rows, cols, which: pl.BlockSpec(
        (1, rows, cols), (lambda i, lo, hi, nu: (lo[i], 0, 0)) if which == 0 else (lambda i, lo, hi, nu: (hi[i], 0, 0)))
    grid_spec = pltpu.PrefetchScalarGridSpec(
        num_scalar_prefetch=3,
        grid=(nblk,),
        in_specs=[pl.BlockSpec((1, 1, tb), lambda i, lo, hi, nu: (i, 0, 0), memory_space=pltpu.SMEM),
                  pl.BlockSpec((1, 1, tb), lambda i, lo, hi, nu: (jnp.minimum(i + 1, nblk - 1), 0, 0),
                               memory_space=pltpu.SMEM),
                  pl.BlockSpec(memory_space=pl.ANY),
                  pl.BlockSpec((tb, LANES), lambda i, lo, hi, nu: (i, 0)),
                  wspec(d, D_EXPERT, 0), wspec(d, D_EXPERT, 0), wspec(D_EXPERT, d, 0),
                  wspec(d, D_EXPERT, 1), wspec(d, D_EXPERT, 1), wspec(D_EXPERT, d, 1)],
        out_specs=pl.BlockSpec((tb, d), lambda i, lo, hi, nu: (i, 0)),
        scratch_shapes=[pltpu.VMEM((2, tb // SUBLANES, SUBLANES, d), F32), pltpu.SemaphoreType.DMA((2,))],
    )
    return pl.pallas_call(
        _moe_pair_kernel,
        grid_spec=grid_spec,
        out_shape=jax.ShapeDtypeStruct((nblk * tb, d), F32),
        compiler_params=_params(("arbitrary",)),
        name="moe_pair_ffn",
    )(elo, ehi, nused, tok, tok, h2f, gates, w1, w3, w2, w1, w3, w2)


def _combine1_kernel(pos_ref, posn_ref, ys_hbm, h2_ref, g_ref, b_ref, y_ref, gbuf, sem):
    i = pl.program_id(0)
    n = pl.num_programs(0)
    tm = h2_ref.shape[0]
    slot = lax.rem(i, 2)
    ngrp = gbuf.shape[1]

    def issue_rows(idx_ref, sl):
        def group(g, c):
            for u in range(SUBLANES):
                r = idx_ref[0, 0, g * SUBLANES + u]
                pltpu.make_async_copy(ys_hbm.at[pl.ds(r, 1)], gbuf.at[sl, g, pl.ds(u, 1)], sem.at[sl]).start()
            return c
        lax.fori_loop(0, ngrp, group, 0)

    def wait_rows(sl):
        pltpu.make_async_copy(gbuf.at[1 - sl], gbuf.at[sl], sem.at[sl]).wait()

    @pl.when(i == 0)
    def _():
        issue_rows(pos_ref, 0)

    wait_rows(slot)
    issue_rows(posn_ref, 1 - slot)
    moe = gbuf[slot].reshape(tm, gbuf.shape[3])
    y_ref[...] = _layer_norm(DEEPNORM_ALPHA * h2_ref[...] + moe, g_ref[...], b_ref[...])

    @pl.when(i == n - 1)
    def _():
        wait_rows(1 - slot)


def _combine1(pos, ysort, h2f, g, b, tm):
    t, d = h2f.shape
    n = t // tm
    return pl.pallas_call(
        _combine1_kernel,
        grid=(n,),
        in_specs=[pl.BlockSpec((1, 1, tm), lambda i: (i, 0, 0), memory_space=pltpu.SMEM),
                  pl.BlockSpec((1, 1, tm), lambda i: (jnp.minimum(i + 1, n - 1), 0, 0), memory_space=pltpu.SMEM),
                  pl.BlockSpec(memory_space=pl.ANY),
                  pl.BlockSpec((tm, d), lambda i: (i, 0)),
                  pl.BlockSpec((1, d), lambda i: (0, 0)),
                  pl.BlockSpec((1, d), lambda i: (0, 0))],
        out_specs=pl.BlockSpec((tm, d), lambda i: (i, 0)),
        out_shape=jax.ShapeDtypeStruct((t, d), F32),
        scratch_shapes=[pltpu.VMEM((2, tm // SUBLANES, SUBLANES, d), F32), pltpu.SemaphoreType.DMA((2,))],
        compiler_params=_params(("arbitrary",)),
        name="combine1",
    )(pos, pos, ysort, h2f, g, b)


def _dispatch_pairs(rf, tb, tm):
    t = rf.shape[0]
    e1 = rf[:, 0].astype(jnp.int32)
    e2 = rf[:, 1].astype(jnp.int32)
    first_low = e1 < e2
    lo = jnp.minimum(e1, e2)
    hi = jnp.maximum(e1, e2)
    g_lo = jnp.where(first_low, rf[:, 2], rf[:, 3])
    g_hi = jnp.where(first_low, rf[:, 3], rf[:, 2])
    epg = EXPERTS_PER_GROUP
    cls = (lo // epg) * (epg * epg) + (lo % epg) * epg + (hi % epg)
    order = jnp.argsort(cls, stable=True).astype(jnp.int32)
    cs = cls[order]
    counts = jnp.sum((cls[:, None] == jnp.arange(N_PAIR_CLASSES, dtype=jnp.int32)[None, :]).astype(jnp.int32), axis=0)
    starts = jnp.cumsum(counts) - counts
    nb = (counts + tb - 1) // tb
    bend = jnp.cumsum(nb)
    bstart = bend - nb
    nblk = -(-t // tb) + N_REAL_PAIRS
    blk = jnp.arange(nblk, dtype=jnp.int32)
    bcls = jnp.minimum(jnp.sum((blk[:, None] >= bend[None, :]).astype(jnp.int32), axis=1), N_PAIR_CLASSES - 1)
    grp = bcls // (epg * epg)
    elo = (grp * epg + (bcls % (epg * epg)) // epg).astype(jnp.int32)
    ehi = (grp * epg + bcls % epg).astype(jnp.int32)
    row0 = starts[bcls] + (blk - bstart[bcls]) * tb
    idx = jnp.clip(row0[:, None] + jnp.arange(tb, dtype=jnp.int32)[None, :], 0, t - 1)
    tok = order[idx].astype(jnp.int32)
    tokf = tok.reshape(nblk * tb)
    gates = jnp.zeros((nblk * tb, LANES), F32).at[:, 0].set(g_lo[tokf]).at[:, 1].set(g_hi[tokf])
    dest_sorted = bstart[cs] * tb + (jnp.arange(t, dtype=jnp.int32) - starts[cs])
    _, pos = lax.sort_key_val(order, dest_sorted.astype(jnp.int32))
    return elo, ehi, bend[-1:].astype(jnp.int32), tok.reshape(nblk, 1, tb), gates, pos.reshape(t // tm, 1, tm)


def _moe_pairs(h2, route, prm, tb, tm):
    b, s, d = h2.shape
    t = b * s
    h2f = h2.reshape(t, d)
    elo, ehi, nused, tok, gates, pos = _dispatch_pairs(route.reshape(t, LANES), tb, tm)
    ysort = _moe_pair_ffn(h2f, elo, ehi, nused, tok, gates, prm["moe_w1"], prm["moe_w3"], prm["moe_w2"], tb)
    y = _combine1(pos, ysort, h2f, prm["ln3_g"], prm["ln3_b"], tm)
    return y.reshape(b, s, d)


def _trunk(x, shift_prev, state0, past_k, past_v, mk, mv, prm, lam, lam_init, cfg):
    b, s, _ = x.shape
    pa, q, k, v, kb, vb, kb1 = _inproj(x, prm["w_in"], cfg["in_bb"], cfg["in_ts"],
                                       cfg["attn_tq"] if past_k is None else 0)
    oa, sfin = _rwkv(pa, shift_prev, _state_to_pairs(state0), prm, cfg["rwkv_tb"], cfg["rwkv_chunk"],
                     cfg["rwkv_group"])
    if past_k is None:
        ob = _attn_prompt(lam, q, kb, kb1, vb, prm["subln"].reshape(DIFF_V, 1), cfg["attn_tq"], lam_init)
    else:
        ob = _attn_cached(lam, q, kb, vb, past_k, past_v, prm["subln"], lam_init)
    h2, route = _post(x, oa, ob, mk, mv, prm, cfg["post_tm"], cfg["post_split"])
    moe = _moe_pairs if cfg["moe_pairs"] else _moe
    y = moe(h2, route, prm, cfg["moe_tb"], cfg["comb_tm"])
    return y, k, v, _pairs_to_state(sfin), pa[:, s - 1:s, :]


def _prep(w, l, lam_scalar_inputs):
    row = lambda a: a[l].reshape(1, -1).astype(F32)
    idx = jnp.arange(D_RWKV, dtype=jnp.int32) // RWKV_HEAD
    wr = jnp.zeros((D_MODEL, LANES), F32)
    wr = wr.at[:, :N_EXPERTS].set(w["moe_w_expert"][l]).at[:, N_EXPERTS:N_EXPERTS + N_GROUPS].set(w["moe_w_group"][l])
    wr_hi = wr.astype(BF16)
    br = jnp.zeros((1, LANES), F32)
    br = br.at[0, :N_EXPERTS].set(w["moe_b_expert"][l]).at[0, N_EXPERTS:N_EXPERTS + N_GROUPS].set(w["moe_b_group"][l])
    return {
        "w_in": w["w_in"][l].astype(BF16),
        "mu": row(w["rwkv_mu"]), "w0": row(w["rwkv_w0"]), "w2": w["rwkv_w2"][l].astype(BF16),
        "a0": row(w["rwkv_a0"]), "a2": w["rwkv_a2"][l].astype(BF16), "g2": w["rwkv_g2"][l].astype(BF16),
        "k_k": row(w["rwkv_k_k"]), "k_a": row(w["rwkv_k_a"]), "r_k": row(w["rwkv_r_k"]),
        "ln_w": row(w["rwkv_ln_w"]), "ln_b": row(w["rwkv_ln_b"]),
        "hsum": (idx[:, None] == idx[None, :]).astype(BF16),
        "subln": row(w["diff_subln"]),
        "w_out": w["w_out"][l].astype(BF16), "wq": w["mem_wq"][l].astype(BF16), "wo": w["mem_wo"][l].astype(BF16),
        "ln1_g": row(w["ln1_g"]), "ln1_b": row(w["ln1_b"]), "ln2_g": row(w["ln2_g"]), "ln2_b": row(w["ln2_b"]),
        "ln3_g": row(w["ln3_g"]), "ln3_b": row(w["ln3_b"]),
        "wr_hi": wr_hi, "wr_lo": (wr - wr_hi.astype(F32)).astype(BF16), "br": br,
        "moe_w1": w["moe_w1"][l].astype(BF16), "moe_w3": w["moe_w3"][l].astype(BF16),
        "moe_w2": w["moe_w2"][l].astype(BF16),
    }


def _tile(n, pref):
    return pref if n % pref == 0 else n


def kernel(x_prompt, x_sample, mem_prompt, cache_diff_k, cache_diff_v, cache_mem_k, cache_mem_v, state_rwkv, state_shift, w_in, rwkv_mu, rwkv_w0, rwkv_w2, rwkv_a0, rwkv_a2, rwkv_g2, rwkv_k_k, rwkv_k_a, rwkv_r_k, rwkv_ln_w, rwkv_ln_b, diff_lq1, diff_lk1, diff_lq2, diff_lk2, diff_subln, w_out, ln1_g, ln1_b, mem_wq, mem_wk, mem_wv, mem_wo, ln2_g, ln2_b, moe_w_group, moe_b_group, moe_w_expert, moe_b_expert, moe_w1, moe_w3, moe_w2, ln3_g, ln3_b):
    w = dict(w_in=w_in, rwkv_mu=rwkv_mu, rwkv_w0=rwkv_w0, rwkv_w2=rwkv_w2, rwkv_a0=rwkv_a0, rwkv_a2=rwkv_a2,
             rwkv_g2=rwkv_g2, rwkv_k_k=rwkv_k_k, rwkv_k_a=rwkv_k_a, rwkv_r_k=rwkv_r_k, rwkv_ln_w=rwkv_ln_w,
             rwkv_ln_b=rwkv_ln_b, diff_subln=diff_subln, w_out=w_out, ln1_g=ln1_g, ln1_b=ln1_b, mem_wq=mem_wq,
             mem_wo=mem_wo, ln2_g=ln2_g, ln2_b=ln2_b, moe_w_group=moe_w_group, moe_b_group=moe_b_group,
             moe_w_expert=moe_w_expert, moe_b_expert=moe_b_expert, moe_w1=moe_w1, moe_w3=moe_w3, moe_w2=moe_w2,
             ln3_g=ln3_g, ln3_b=ln3_b)
    bp, sp, _ = x_prompt.shape
    bs, ss, _ = x_sample.shape
    depth = w_in.shape[0]
    cfg_p = dict(in_bb=1, in_ts=_tile(sp, 512), rwkv_tb=_tile(sp, 256), rwkv_chunk=CHUNK, rwkv_group=_tile(sp, 256) // CHUNK,
                 attn_tq=_tile(sp, 512), post_tm=_tile(sp, 512), post_split=2, moe_tb=256, moe_pairs=True, comb_tm=_tile(bp * sp, 256))
    cfg_s = dict(in_bb=bs, in_ts=ss, rwkv_tb=ss, rwkv_chunk=ss, rwkv_group=1, attn_tq=ss, post_tm=ss, post_split=1, moe_tb=64, moe_pairs=False,
                 comb_tm=_tile(bs * ss, 256))
    yp, ys = x_prompt, x_sample
    outs = [[] for _ in range(10)]
    for l in range(depth):
        prm = _prep(w, l, None)
        lam_init = 0.8 - 0.6 * math.exp(-0.3 * l)
        f = lambda z: z[l].astype(F32)
        lam = (jnp.exp(jnp.sum(f(diff_lq1) * f(diff_lk1))) - jnp.exp(jnp.sum(f(diff_lq2) * f(diff_lk2)))
               + lam_init).reshape(1)
        slopes = jnp.asarray(ALIBI_SLOPES, F32)
        lam = jnp.concatenate([lam, slopes, jnp.zeros((3,), F32)])
        mk_p, mv_p, mkb, mvb = _memproj(mem_prompt, mem_wk[l].astype(BF16), mem_wv[l].astype(BF16))
        yp, k_p, v_p, st_p, sh_p = _trunk(
            yp, jnp.zeros((bp, 1, C_RWKV), F32), jnp.zeros((bp, 2 * RWKV_PAIRS, RWKV_HEAD, RWKV_HEAD), F32),
            None, None, mkb, mvb, prm, lam, lam_init, cfg_p)
        ys, k_s, v_s, st_s, sh_s = _trunk(
            ys, state_shift[l], state_rwkv[l], cache_diff_k[l], cache_diff_v[l],
            cache_mem_k[l].astype(BF16), cache_mem_v[l].astype(BF16), prm, lam, lam_init, cfg_s)
        for lst, val in zip(outs, (k_p, v_p, mk_p, mv_p, st_p, sh_p, k_s, v_s, st_s, sh_s)):
            lst.append(val)
    return (yp, ys) + tuple(jnp.stack(o) for o in outs)
```

```python
import functools
import math

import jax
import jax.numpy as jnp
from jax import lax
from jax.experimental import pallas as pl
from jax.experimental.pallas import tpu as pltpu

F32 = jnp.float32
BF16 = jnp.bfloat16

D_MODEL = 1024
CHUNK = 64
D_RWKV = 512
RWKV_HEAD = 64
RWKV_PAIRS = D_RWKV // (2 * RWKV_HEAD)
W_LORA = 64
A_LORA = 64
G_LORA = 128
C_RWKV = 3 * D_RWKV + W_LORA + A_LORA + G_LORA
D_DIFF = 512
DIFF_HEADS = 4
DIFF_V = 128
DIFF_QK = 64
C_IN = C_RWKV + 3 * D_DIFF
MEM_HEADS = 4
MEM_HEAD = D_MODEL // MEM_HEADS
N_GROUPS = 4
EXPERTS_PER_GROUP = 8
N_EXPERTS = 32
D_EXPERT = 512
DEPTH = 1
DEEPNORM_ALPHA = (2.0 * DEPTH) ** 0.25
LN_EPS = 1e-5
RMS_EPS = 1e-5
RWKV_GN_EPS = 64e-5
NEG_INF = -1e30
LOG2E = math.log2(math.e)
ALIBI_SLOPES = tuple(2.0 ** (-8.0 * (h + 1) / DIFF_HEADS) for h in range(DIFF_HEADS))
VT_PAD = 16
LANES = 128
SUBLANES = 8
VMEM_LIMIT = 48 * 1024 * 1024


def _params(sem):
    return pltpu.CompilerParams(dimension_semantics=sem, vmem_limit_bytes=VMEM_LIMIT)


def _dot(a, b):
    return jnp.dot(a, b, preferred_element_type=F32)


def _dot_nt(a, b):
    return lax.dot_general(a, b, (((1,), (1,)), ((), ())), preferred_element_type=F32)


def _dot_tn(a, b):
    return lax.dot_general(a, b, (((0,), (0,)), ((), ())), preferred_element_type=F32)


def _split2(x):
    hi = x.astype(BF16)
    lo = (x - hi.astype(F32)).astype(BF16)
    return hi, lo


def _split3(x):
    hi = x.astype(BF16)
    r = x - hi.astype(F32)
    mid = r.astype(BF16)
    lo = (r - mid.astype(F32)).astype(BF16)
    return hi, mid, lo


def _layer_norm(x, g, b):
    mu = jnp.mean(x, axis=-1, keepdims=True)
    xc = x - mu
    var = jnp.mean(xc * xc, axis=-1, keepdims=True)
    return xc * lax.rsqrt(var + LN_EPS) * g + b


def _memproj_kernel(m_ref, wk_ref, wv_ref, k_ref, v_ref, kb_ref, vb_ref):
    m = m_ref[0].astype(BF16)
    k = _dot(m, wk_ref[...])
    v = _dot(m, wv_ref[...])
    k_ref[0] = k
    v_ref[0] = v
    kb_ref[0] = k.astype(BF16)
    vb_ref[0] = v.astype(BF16)


def _memproj(mem, wk, wv):
    b, n, d = mem.shape
    blk = pl.BlockSpec((1, n, d), lambda i: (i, 0, 0))
    wspec = pl.BlockSpec((d, d), lambda i: (0, 0))
    return pl.pallas_call(
        _memproj_kernel,
        grid=(b,),
        in_specs=[blk, wspec, wspec],
        out_specs=[blk, blk, blk, blk],
        out_shape=[jax.ShapeDtypeStruct((b, n, d), F32), jax.ShapeDtypeStruct((b, n, d), F32),
                   jax.ShapeDtypeStruct((b, n, d), BF16), jax.ShapeDtypeStruct((b, n, d), BF16)],
        compiler_params=_params(("arbitrary",)),
        name="memproj",
    )(mem, wk, wv)


def _inproj_kernel(x_ref, w_ref, pa_ref, q_ref, k_ref, v_ref, kb_ref, vb_ref, kb1_ref, *, attn_tile):
    bb, ts, d = x_ref.shape
    rows = bb * ts
    x = x_ref[...].reshape(rows, d).astype(BF16)
    pq = _dot(x, w_ref[:, C_RWKV:C_RWKV + D_DIFF])
    pk = _dot(x, w_ref[:, C_RWKV + D_DIFF:C_RWKV + 2 * D_DIFF])
    pv = _dot(x, w_ref[:, C_RWKV + 2 * D_DIFF:])
    pa_ref[...] = _dot(x, w_ref[:, :C_RWKV]).reshape(bb, ts, C_RWKV)
    if attn_tile:
        lane = lax.broadcasted_iota(jnp.int32, (rows, LANES), 1)
        kloc = lax.rem(pl.program_id(1) * ts + lax.broadcasted_iota(jnp.int32, (rows, LANES), 0), attn_tile)
        aug = jnp.where(lax.broadcasted_iota(jnp.int32, (VT_PAD, rows), 0) == 0, 1.0, 0.0).astype(BF16)
    for h in range(DIFF_HEADS):
        hs = slice(h * LANES, (h + 1) * LANES)
        qh = pq[:, hs]
        kh = pk[:, hs]
        k_ref[:, h] = kh.reshape(bb, ts, LANES)
        vh = pv[:, hs]
        v_ref[:, h] = vh.reshape(bb, ts, LANES)
        if attn_tile:
            q_ref[0, h] = (qh * (DIFF_QK ** -0.5 * LOG2E)).T.astype(BF16)
            vb_ref[0, h] = jnp.concatenate([vh.T.astype(BF16), aug], axis=0)
            bias = kloc.astype(F32) * (ALIBI_SLOPES[h] * LOG2E)
            b_hi = bias.astype(BF16).astype(F32)
            b_mid = (bias - b_hi).astype(BF16).astype(F32)
            b_lo = bias - b_hi - b_mid
            cols = jnp.where(lane == DIFF_QK, b_hi, jnp.where(lane == DIFF_QK + 1, b_mid,
                             jnp.where(lane == DIFF_QK + 2, b_lo, 0.0)))
            kb_ref[0, h] = jnp.where(lane < DIFF_QK, kh, cols).astype(BF16)
            kb1_ref[0, h] = jnp.where(lane < DIFF_QK, pltpu.roll(kh, DIFF_QK, axis=1), cols).astype(BF16)
        else:
            q_ref[:, h] = (qh * (DIFF_QK ** -0.5)).reshape(bb, ts, LANES).astype(BF16)
            vb_ref[:, h] = vh.reshape(bb, ts, LANES).astype(BF16)
            kb_ref[:, h] = kh.reshape(bb, ts, LANES).astype(BF16)
            kb1_ref[:, h] = kh.reshape(bb, ts, LANES).astype(BF16)


def _inproj(x, w_in_bf, bb, ts, attn_tile):
    b, s, d = x.shape
    hm = pl.BlockSpec((bb, DIFF_HEADS, ts, LANES), lambda i, j: (i, 0, j, 0))
    hshape = (b, DIFF_HEADS, s, LANES)
    if attn_tile:
        assert bb == 1
        qm = pl.BlockSpec((1, DIFF_HEADS, LANES, ts), lambda i, j: (i, 0, 0, j))
        qshape = (b, DIFF_HEADS, LANES, s)
        vm = pl.BlockSpec((1, DIFF_HEADS, LANES + VT_PAD, ts), lambda i, j: (i, 0, 0, j))
        vshape = (b, DIFF_HEADS, LANES + VT_PAD, s)
    else:
        qm, qshape, vm, vshape = hm, hshape, hm, hshape
    return pl.pallas_call(
        functools.partial(_inproj_kernel, attn_tile=attn_tile),
        grid=(b // bb, s // ts),
        in_specs=[pl.BlockSpec((bb, ts, d), lambda i, j: (i, j, 0)),
                  pl.BlockSpec((d, C_IN), lambda i, j: (0, 0))],
        out_specs=[pl.BlockSpec((bb, ts, C_RWKV), lambda i, j: (i, j, 0)), qm, hm, hm, hm, vm, hm],
        out_shape=[jax.ShapeDtypeStruct((b, s, C_RWKV), F32),
                   jax.ShapeDtypeStruct(qshape, BF16),
                   jax.ShapeDtypeStruct(hshape, F32), jax.ShapeDtypeStruct(hshape, F32),
                   jax.ShapeDtypeStruct(hshape, BF16), jax.ShapeDtypeStruct(vshape, BF16),
                   jax.ShapeDtypeStruct(hshape, BF16)],
        compiler_params=_params(("arbitrary", "arbitrary")),
        name="inproj",
    )(x, w_in_bf)


def _rwkv_kernel(pa_ref, shift_ref, s0_ref, mu_ref, w0_ref, w2_ref, a0_ref, a2_ref, g2_ref,
                 kk_ref, ka_ref, rk_ref, lnw_ref, lnb_ref, hsum_ref,
                 o_ref, sfin_ref,
                 carry_ref, al_s, be_s, kb_s, rb_s, v_s, gam_s, o_s, bonus_s, g_s, w1_s, n1_s, c0_s, n2_s,
                 *, chunk, group):
    L = chunk
    tb = pa_ref.shape[1]
    nchunk = tb // L
    t = pl.program_id(1)

    @pl.when(t == 0)
    def _():
        sfin_ref[...] = s0_ref[...]
        carry_ref[...] = shift_ref[0]

    p = pa_ref[0]
    prev = pltpu.roll(p, 1, axis=0)
    row = lax.broadcasted_iota(jnp.int32, (tb, 1), 0)
    prev = jnp.where(row == 0, carry_ref[...], prev)
    carry_ref[...] = p[tb - 1:tb, :]
    ps = p + (prev - p) * mu_ref[...]

    r = ps[:, :D_RWKV]
    k = ps[:, D_RWKV:2 * D_RWKV]
    v = ps[:, 2 * D_RWKV:3 * D_RWKV]
    o0 = 3 * D_RWKV
    w_lo = ps[:, o0:o0 + W_LORA]
    a_lo = ps[:, o0 + W_LORA:o0 + W_LORA + A_LORA]
    g_lo = ps[:, o0 + W_LORA + A_LORA:]

    hsum = hsum_ref[...]

    def headsum(x):
        return _dot(x.astype(BF16), hsum)

    z = w0_ref[...] + _dot(jnp.tanh(w_lo).astype(BF16), w2_ref[...])
    lw = -math.exp(-0.5) / (1.0 + jnp.exp(-z))
    a = 1.0 / (1.0 + jnp.exp(-(a0_ref[...] + _dot(a_lo.astype(BF16), a2_ref[...]))))
    g_s[...] = _dot((1.0 / (1.0 + jnp.exp(-g_lo))).astype(BF16), g2_ref[...])
    kk = k * kk_ref[...]
    kk = kk / jnp.maximum(jnp.sqrt(headsum(kk * kk)), 1e-12)
    kmod = k * (1.0 + (a - 1.0) * ka_ref[...])
    bonus_s[...] = headsum(r * kmod * rk_ref[...]) * v

    bi = lax.broadcasted_iota(jnp.int32, (tb, tb), 0)
    bj = lax.broadcasted_iota(jnp.int32, (tb, tb), 1)
    tri = (((bi // L) == (bj // L)) & (bi >= bj)).astype(BF16)
    h3 = _split3(lw)
    cum = _dot(tri, h3[0]) + _dot(tri, h3[1]) + _dot(tri, h3[2])
    gam = jnp.exp(cum)
    igam = jnp.exp(-cum)
    gam_s[...] = gam
    al_s[...] = kk * jnp.exp(cum - lw)
    be_s[...] = kk * a * igam
    kb_s[...] = kmod * igam
    rb_s[...] = r * gam
    v_s[...] = v

    l2 = 2 * L
    ri = lax.broadcasted_iota(jnp.int32, (l2, l2), 0)
    ci = lax.broadcasted_iota(jnp.int32, (l2, l2), 1)
    same = (ri // L) == (ci // L)
    strict = same & (ri > ci)
    incl = same & (ri >= ci)
    eye = (ri == ci).astype(F32)
    lane = lax.broadcasted_iota(jnp.int32, (1, LANES), 1)
    first = lane < RWKV_HEAD

    def stack(x):
        return jnp.concatenate([jnp.where(first, x, 0.0), jnp.where(first, 0.0, x)], axis=0)

    def phase_a(cg, carry):
        cs = []
        for gi in range(group):
            c = cg * group + gi
            sl = pl.ds(pl.multiple_of(c * L, L), L)
            gl = gam_s[pl.ds(c * L + (L - 1), 1), :]
            for j in range(RWKV_PAIRS):
                ls = slice(j * LANES, (j + 1) * LANES)
                cs.append(dict(c=c, j=j, gl=gl[:, ls], al2=stack(al_s[sl, ls]), rb2=stack(rb_s[sl, ls]),
                               be2=stack(be_s[sl, ls]), kb2=stack(kb_s[sl, ls]), v2=stack(v_s[sl, ls])))
        for d in cs:
            x = jnp.concatenate([d["al2"], d["rb2"]], axis=0).astype(BF16)
            y = jnp.concatenate([d["be2"], d["kb2"]], axis=0).astype(BF16)
            gmat = _dot_nt(x, y)
            ab = jnp.where(strict, gmat[:l2, :l2], 0.0)
            d["ak"] = jnp.where(strict, gmat[:l2, l2:], 0.0).astype(BF16)
            d["rbm"] = jnp.where(incl, gmat[l2:, :l2], 0.0).astype(BF16)
            d["rkm"] = jnp.where(incl, gmat[l2:, l2:], 0.0).astype(BF16)
            d["pw"] = ab
            d["tm"] = eye - ab
        n = 2
        while n < L:
            for d in cs:
                pb = d["pw"].astype(BF16)
                d["pw"] = _dot(pb, pb)
            for d in cs:
                d["tm"] = _dot(d["tm"].astype(BF16), (eye + d["pw"]).astype(BF16))
            n *= 2
        for d in cs:
            d["akv"] = _dot(d["ak"], d["v2"].astype(BF16))
        for d in cs:
            zz = jnp.concatenate([d["al2"], d["akv"]], axis=1).astype(BF16)
            d["m"] = _dot(d["tm"].astype(BF16), zz)
        for d in cs:
            rm = _dot(d["rbm"], d["m"].astype(BF16))
            rkv = _dot(d["rkm"], d["v2"].astype(BF16))
            n1_s[d["c"], d["j"]] = (d["rb2"] - rm[:, :LANES]).astype(BF16)
            n2_s[d["c"], d["j"]] = rkv - rm[:, LANES:]
        for d in cs:
            m1 = d["m"][:, :LANES].astype(BF16)
            m2 = d["m"][:, LANES:]
            w1 = _dot_tn(m1, d["be2"].astype(BF16)) * d["gl"]
            lhs = jnp.concatenate([d["v2"], -m2], axis=0).astype(BF16)
            rhs = jnp.concatenate([d["kb2"], d["be2"]], axis=0).astype(BF16)
            w1_s[d["c"], d["j"]] = w1.astype(BF16)
            c0_s[d["c"], d["j"]] = _dot_tn(lhs, rhs) * d["gl"]
        return carry

    lax.fori_loop(0, nchunk // group, phase_a, 0)

    def phase_b(c, carry):
        gl = gam_s[pl.ds(c * L + (L - 1), 1), :]
        outs = []
        for j in range(RWKV_PAIRS):
            sb = sfin_ref[0, j]
            sbb = sb.astype(BF16)
            o2 = _dot_nt(n1_s[c, j], sbb) + n2_s[c, j]
            outs.append(o2[:L] + o2[L:])
            sfin_ref[0, j] = sb * gl[:, j * LANES:(j + 1) * LANES] - _dot(sbb, w1_s[c, j]) + c0_s[c, j]
        o_s[pl.ds(pl.multiple_of(c * L, L), L), :] = jnp.concatenate(outs, axis=1)
        return carry

    lax.fori_loop(0, nchunk, phase_b, 0)

    o = o_s[...]
    inv_n = 1.0 / RWKV_HEAD
    mu_h = headsum(o) * inv_n
    oc = o - mu_h
    var_h = headsum(oc * oc) * inv_n
    on = oc * lax.rsqrt(var_h + RWKV_GN_EPS) * lnw_ref[...] + lnb_ref[...]
    o_ref[0] = ((on + bonus_s[...]) * g_s[...]).astype(o_ref.dtype)


def _rwkv(pa, shift_prev, s0_pairs, prm, tb, chunk, group):
    b, s, _ = pa.shape
    nchunk = tb // chunk
    row = lambda n: pl.BlockSpec((1, n), lambda i, j: (0, 0))
    mat = lambda m, n: pl.BlockSpec((m, n), lambda i, j: (0, 0))
    sspec = pl.BlockSpec((1, RWKV_PAIRS, LANES, LANES), lambda i, j: (i, 0, 0, 0))
    coef = lambda rows, dt: pltpu.VMEM((nchunk, RWKV_PAIRS, rows, LANES), dt)
    return pl.pallas_call(
        functools.partial(_rwkv_kernel, chunk=chunk, group=group),
        grid=(b, s // tb),
        in_specs=[pl.BlockSpec((1, tb, C_RWKV), lambda i, j: (i, j, 0)),
                  pl.BlockSpec((1, 1, C_RWKV), lambda i, j: (i, 0, 0)),
                  sspec,
                  row(C_RWKV), row(D_RWKV), mat(W_LORA, D_RWKV), row(D_RWKV), mat(A_LORA, D_RWKV),
                  mat(G_LORA, D_RWKV), row(D_RWKV), row(D_RWKV), row(D_RWKV), row(D_RWKV), row(D_RWKV),
                  mat(D_RWKV, D_RWKV)],
        out_specs=[pl.BlockSpec((1, tb, D_RWKV), lambda i, j: (i, j, 0)), sspec],
        out_shape=[jax.ShapeDtypeStruct((b, s, D_RWKV), BF16),
                   jax.ShapeDtypeStruct((b, RWKV_PAIRS, LANES, LANES), F32)],
        scratch_shapes=[pltpu.VMEM((1, C_RWKV), F32)] + [pltpu.VMEM((tb, D_RWKV), F32)] * 9
        + [coef(LANES, BF16), coef(2 * chunk, BF16), coef(LANES, F32), coef(2 * chunk, F32)],
        compiler_params=_params(("arbitrary", "arbitrary")),
        name="rwkv",
    )(pa, shift_prev, s0_pairs, prm["mu"], prm["w0"], prm["w2"], prm["a0"], prm["a2"], prm["g2"],
      prm["k_k"], prm["k_a"], prm["r_k"], prm["ln_w"], prm["ln_b"], prm["hsum"])


def _state_to_pairs(s):
    b = s.shape[0]
    s = s.reshape(b, RWKV_PAIRS, 2, RWKV_HEAD, RWKV_HEAD)
    z = jnp.zeros_like(s[:, :, 0])
    top = jnp.concatenate([s[:, :, 0], z], axis=-1)
    bot = jnp.concatenate([z, s[:, :, 1]], axis=-1)
    return jnp.concatenate([top, bot], axis=-2)


def _pairs_to_state(sp):
    b = sp.shape[0]
    h = RWKV_HEAD
    s = jnp.stack([sp[:, :, :h, :h], sp[:, :, h:, h:]], axis=2)
    return s.reshape(b, 2 * RWKV_PAIRS, h, h)


def _diff_finish(accs, ls, lam, sub, lam_init):
    o = accs[0] / ls[0] - lam * (accs[1] / ls[1])
    o = o * lax.rsqrt(jnp.mean(o * o, axis=-1, keepdims=True) + RMS_EPS)
    return o * sub * (1.0 - lam_init)


def _attn_kernel(lam_ref, qt_ref, k0_ref, k1_ref, vt_ref, tab_ref, sub_ref, o_ref, acc_ref, s_ref, m_ref, *, tq,
                 lam_init):
    h = pl.program_id(1)
    qi = pl.program_id(2)
    slope = lam_ref[1 + h] * LOG2E
    qt = qt_ref[0, 0]
    row = lax.broadcasted_iota(jnp.int32, (LANES, 1), 0)
    ones_rows = (row >= DIFF_QK) & (row < DIFF_QK + 3)
    fill = jnp.where(ones_rows, 1.0, 0.0).astype(BF16)

    def rhs(q_first):
        return jnp.where(row < DIFF_QK, q_first, fill)
    rs = (rhs(qt), rhs(jnp.concatenate([qt[DIFF_QK:], qt[:DIFF_QK]], axis=0)))
    k_refs = (k0_ref, k1_ref)
    acc_ref[...] = jnp.zeros_like(acc_ref)
    m_ref[...] = jnp.full_like(m_ref, NEG_INF)

    def scores(j, slot):
        ds = pl.ds(pl.multiple_of(j * tq, tq), tq)
        for c in range(2):
            s_ref[slot, c] = _dot(k_refs[c][0, 0, ds, :], rs[c])

    def softmax_pv(j, slot, diag):
        vt = vt_ref[0, 0, :, pl.ds(pl.multiple_of(j * tq, tq), tq)]
        off = slope * ((j - qi) * tq).astype(F32)
        for c in range(2):
            s = s_ref[slot, c]
            if diag:
                s = s + tab_ref[0]
            m = m_ref[c]
            m_new = jnp.maximum(m, jnp.max(s, axis=0, keepdims=True) + off)
            alpha = jnp.exp2(m - m_new)
            p = jnp.exp2(s - (m_new - off))
            m_ref[c] = m_new
            acc_ref[c] = alpha * acc_ref[c] + _dot(vt, p.astype(BF16))

    def pair(jj, carry):
        j = 2 * jj
        scores(j + 1, 1)
        softmax_pv(j, 0, False)
        scores(jnp.minimum(j + 2, qi), 0)
        softmax_pv(j + 1, 1, False)
        return carry

    scores(0, 0)
    lax.fori_loop(0, qi // 2, pair, 0)

    @pl.when(lax.rem(qi, 2) == 0)
    def _():
        softmax_pv(qi, 0, True)

    @pl.when(lax.rem(qi, 2) == 1)
    def _():
        scores(qi, 1)
        softmax_pv(qi - 1, 0, False)
        softmax_pv(qi, 1, True)

    a0 = acc_ref[0]
    a1 = acc_ref[1]
    o = a0[:DIFF_V] / a0[DIFF_V:DIFF_V + 1] - lam_ref[0] * (a1[:DIFF_V] / a1[DIFF_V:DIFF_V + 1])
    o = o * lax.rsqrt(jnp.mean(o * o, axis=0, keepdims=True) + RMS_EPS)
    o = o * (sub_ref[...] * (1.0 - lam_init))
    o_ref[0] = o.T.astype(o_ref.dtype)


def _attn_diag_table(tq):
    slopes = jnp.asarray(ALIBI_SLOPES, F32) * LOG2E
    pos = jnp.arange(tq, dtype=jnp.int32)
    kp, qp = pos[:, None], pos[None, :]
    rel = jnp.where(kp <= qp, 0, 2 * (qp - kp)).astype(F32)
    vis = (kp // CHUNK) <= (qp // CHUNK)
    return jnp.where(vis[None], slopes[:, None, None] * rel[None], NEG_INF)


def _attn_prompt(lam, qt, k0, k1, vt, subln_col, tq, lam_init):
    b, h, _, s = qt.shape
    kspec = pl.BlockSpec((1, 1, s, LANES), lambda i, j, t: (i, j, 0, 0))
    return pl.pallas_call(
        functools.partial(_attn_kernel, tq=tq, lam_init=lam_init),
        grid=(b, h, s // tq),
        in_specs=[pl.BlockSpec(memory_space=pltpu.SMEM),
                  pl.BlockSpec((1, 1, LANES, tq), lambda i, j, t: (i, j, 0, t)),
                  kspec, kspec,
                  pl.BlockSpec((1, 1, LANES + VT_PAD, s), lambda i, j, t: (i, j, 0, 0)),
                  pl.BlockSpec((1, tq, tq), lambda i, j, t: (j, 0, 0)),
                  pl.BlockSpec((DIFF_V, 1), lambda i, j, t: (0, 0))],
        out_specs=pl.BlockSpec((1, tq, LANES), lambda i, j, t: (i, t, j)),
        out_shape=jax.ShapeDtypeStruct((b, s, D_DIFF), BF16),
        scratch_shapes=[pltpu.VMEM((2, DIFF_V + VT_PAD, tq), F32), pltpu.VMEM((2, 2, tq, tq), F32),
                        pltpu.VMEM((2, 1, tq), F32)],
        compiler_params=_params(("arbitrary", "arbitrary", "arbitrary")),
        name="attn_prompt",
    )(lam, qt, k0, k1, vt, _attn_diag_table(tq), subln_col)


def _attn_cached_kernel(lam_ref, q_ref, kn_ref, vn_ref, kp_ref, vp_ref, sub_ref, o_ref, *, lam_init):
    h = pl.program_id(1)
    slope = lam_ref[1 + h]
    s_new = q_ref.shape[2]
    past = kp_ref.shape[2]
    q = q_ref[0, 0]
    kp = kp_ref[0, 0].astype(BF16)
    vp = vp_ref[0, 0].astype(BF16)
    kn = kn_ref[0, 0]
    vn = vn_ref[0, 0]
    q_pos = past + lax.broadcasted_iota(jnp.int32, (s_new, 1), 0)

    def bias(k_pos):
        dist = jnp.abs(q_pos - k_pos).astype(F32)
        vis = (k_pos // CHUNK) <= (q_pos // CHUNK)
        return jnp.where(vis, -slope * dist, NEG_INF), vis

    b_past, vis_past = bias(lax.broadcasted_iota(jnp.int32, (1, past), 1))
    b_new, vis_new = bias(past + lax.broadcasted_iota(jnp.int32, (1, s_new), 1))
    accs, ls = [], []
    for c in range(2):
        qc = q[:, c * DIFF_QK:(c + 1) * DIFF_QK]
        sp = jnp.where(vis_past, _dot_nt(qc, kp[:, c * DIFF_QK:(c + 1) * DIFF_QK]) + b_past, NEG_INF)
        sn = jnp.where(vis_new, _dot_nt(qc, kn[:, c * DIFF_QK:(c + 1) * DIFF_QK]) + b_new, NEG_INF)
        m = jnp.maximum(jnp.max(sp, axis=-1, keepdims=True), jnp.max(sn, axis=-1, keepdims=True))
        pp = jnp.exp(sp - m)
        pn = jnp.exp(sn - m)
        ls.append(jnp.sum(pp, axis=-1, keepdims=True) + jnp.sum(pn, axis=-1, keepdims=True))
        accs.append(_dot(pp.astype(BF16), vp) + _dot(pn.astype(BF16), vn))
    o = _diff_finish(accs, ls, lam_ref[0], sub_ref[...], lam_init)
    o_ref[0] = o.astype(o_ref.dtype)


def _attn_cached(lam, q, kb, vb, past_k, past_v, subln, lam_init):
    b, h, s, _ = q.shape
    past = past_k.shape[2]
    new = pl.BlockSpec((1, 1, s, LANES), lambda i, j: (i, j, 0, 0))
    old = pl.BlockSpec((1, 1, past, LANES), lambda i, j: (i, j, 0, 0))
    return pl.pallas_call(
        functools.partial(_attn_cached_kernel, lam_init=lam_init),
        grid=(b, h),
        in_specs=[pl.BlockSpec(memory_space=pltpu.SMEM), new, new, new, old, old,
                  pl.BlockSpec((1, DIFF_V), lambda i, j: (0, 0))],
        out_specs=pl.BlockSpec((1, s, LANES), lambda i, j: (i, 0, j)),
        out_shape=jax.ShapeDtypeStruct((b, s, D_DIFF), BF16),
        compiler_params=_params(("arbitrary", "arbitrary")),
        name="attn_cached",
    )(lam, q, kb, vb, past_k, past_v, subln)


def _route(lg):
    tm = lg.shape[0]
    lane = lax.broadcasted_iota(jnp.int32, (tm, LANES), 1)
    lane_f = lane.astype(F32)
    big = 1e9
    low = -3e38
    is_g = (lane >= N_EXPERTS) & (lane < N_EXPERTS + N_GROUPS)
    gl = jnp.where(is_g, lg, low)
    gmax = jnp.max(gl, axis=-1, keepdims=True)
    grp_lane = jnp.min(jnp.where(gl == gmax, lane_f, big), axis=-1, keepdims=True)
    gsum = jnp.sum(jnp.where(is_g, jnp.exp(gl - gmax), 0.0), axis=-1, keepdims=True)
    g_prob = 1.0 / gsum
    grp = grp_lane - float(N_EXPERTS)
    lane_grp = (lane // EXPERTS_PER_GROUP).astype(F32)
    el = jnp.where(lane_grp == grp, lg, low)
    v1 = jnp.max(el, axis=-1, keepdims=True)
    i1 = jnp.min(jnp.where(el == v1, lane_f, big), axis=-1, keepdims=True)
    el2 = jnp.where(lane_f == i1, low, el)
    v2 = jnp.max(el2, axis=-1, keepdims=True)
    i2 = jnp.min(jnp.where(el2 == v2, lane_f, big), axis=-1, keepdims=True)
    e21 = jnp.exp(v2 - v1)
    den = 1.0 + e21
    gate1 = g_prob / den
    gate2 = g_prob * e21 / den
    out = jnp.where(lane == 0, i1, jnp.where(lane == 1, i2, jnp.where(lane == 2, gate1,
                    jnp.where(lane == 3, gate2, 0.0))))
    return out


def _post_kernel(x_ref, oa_ref, ob_ref, mk_ref, mv_ref, wout_ref, wq_ref, wo_ref,
                 g1_ref, b1_ref, g2_ref, b2_ref, wrh_ref, wrl_ref, br_ref, h2_ref, route_ref, *, nsplit):
    tm = x_ref.shape[1]
    rows = tm // nsplit
    parts = [slice(i * rows, (i + 1) * rows) for i in range(nsplit)]
    mix = [_dot(oa_ref[0, r, :], wout_ref[:D_RWKV, :]) + _dot(ob_ref[0, r, :], wout_ref[D_RWKV:, :]) for r in parts]
    h1 = [_layer_norm(DEEPNORM_ALPHA * x_ref[0, r, :] + m, g1_ref[...], b1_ref[...]) for r, m in zip(parts, mix)]
    q = [(_dot(h.astype(BF16), wq_ref[...]) * (MEM_HEAD ** -0.5)).astype(BF16) for h in h1]
    heads = [[] for _ in parts]
    for h in range(MEM_HEADS):
        hs = slice(h * MEM_HEAD, (h + 1) * MEM_HEAD)
        for i in range(nsplit):
            s = _dot_nt(q[i][:, hs], mk_ref[0, :, hs])
            m = jnp.max(s, axis=-1, keepdims=True)
            p = jnp.exp(s - m)
            p = p / jnp.sum(p, axis=-1, keepdims=True)
            heads[i].append(_dot(p.astype(BF16), mv_ref[0, :, hs]).astype(BF16))
    att = [_dot(jnp.concatenate(hd, axis=-1), wo_ref[...]) for hd in heads]
    h2 = [_layer_norm(DEEPNORM_ALPHA * h + a, g2_ref[...], b2_ref[...]) for h, a in zip(h1, att)]
    for r, h in zip(parts, h2):
        h2_ref[0, r, :] = h
        hi, lo = _split2(h)
        lg = _dot(hi, wrh_ref[...]) + _dot(hi, wrl_ref[...]) + _dot(lo, wrh_ref[...]) + br_ref[...]
        route_ref[0, r, :] = _route(lg)


def _post(x, oa, ob, mk, mv, prm, tm, nsplit):
    b, s, d = x.shape
    n = mk.shape[1]
    tok = lambda w: pl.BlockSpec((1, tm, w), lambda i, j: (i, j, 0))
    mem = pl.BlockSpec((1, n, d), lambda i, j: (i, 0, 0))
    mat = lambda m_, n_: pl.BlockSpec((m_, n_), lambda i, j: (0, 0))
    return pl.pallas_call(
        functools.partial(_post_kernel, nsplit=nsplit),
        grid=(b, s // tm),
        in_specs=[tok(d), tok(D_RWKV), tok(D_DIFF), mem, mem, mat(d, d), mat(d, d), mat(d, d),
                  mat(1, d), mat(1, d), mat(1, d), mat(1, d), mat(d, LANES), mat(d, LANES), mat(1, LANES)],
        out_specs=[tok(d), tok(LANES)],
        out_shape=[jax.ShapeDtypeStruct((b, s, d), F32), jax.ShapeDtypeStruct((b, s, LANES), F32)],
        compiler_params=_params(("arbitrary", "arbitrary")),
        name="post",
    )(x, oa, ob, mk, mv, prm["w_out"], prm["wq"], prm["wo"], prm["ln1_g"], prm["ln1_b"],
      prm["ln2_g"], prm["ln2_b"], prm["wr_hi"], prm["wr_lo"], prm["br"])


def _moe_kernel(bexp_ref, nused_ref, tok_ref, tokn_ref, h2_hbm, w1_ref, w3_ref, w2_ref, y_ref, xbuf, sem):
    i = pl.program_id(0)
    nblk = pl.num_programs(0)
    nused = nused_ref[0]
    ngrp = xbuf.shape[1]
    tb = ngrp * SUBLANES
    slot = lax.rem(i, 2)

    def issue_rows(idx_ref, sl):
        def group(g, c):
            for u in range(SUBLANES):
                r = idx_ref[0, 0, g * SUBLANES + u]
                pltpu.make_async_copy(h2_hbm.at[pl.ds(r, 1)], xbuf.at[sl, g, pl.ds(u, 1)], sem.at[sl]).start()
            return c
        lax.fori_loop(0, ngrp, group, 0)

    def wait_rows(sl):
        pltpu.make_async_copy(xbuf.at[1 - sl], xbuf.at[sl], sem.at[sl]).wait()

    @pl.when(i == 0)
    def _():
        issue_rows(tok_ref, 0)

    @pl.when(i < nused)
    def _():
        wait_rows(slot)
        issue_rows(tokn_ref, 1 - slot)
        x = xbuf[slot].reshape(tb, xbuf.shape[3]).astype(BF16)
        a = _dot(x, w1_ref[0])
        g = _dot(x, w3_ref[0])
        hmid = (a / (1.0 + jnp.exp(-a))) * g
        y_ref[...] = _dot(hmid.astype(BF16), w2_ref[0])

        @pl.when(i == nblk - 1)
        def _():
            wait_rows(1 - slot)

    @pl.when(i >= nused)
    def _():
        y_ref[...] = jnp.zeros_like(y_ref)

        @pl.when(i == nused)
        def _():
            wait_rows(slot)


def _moe_ffn(h2f, blk_exp, nused, tok, w1, w3, w2, tb):
    nblk = tok.shape[0]
    d = h2f.shape[1]
    grid_spec = pltpu.PrefetchScalarGridSpec(
        num_scalar_prefetch=2,
        grid=(nblk,),
        in_specs=[pl.BlockSpec((1, 1, tb), lambda i, be, nu: (i, 0, 0), memory_space=pltpu.SMEM),
                  pl.BlockSpec((1, 1, tb), lambda i, be, nu: (jnp.minimum(i + 1, nblk - 1), 0, 0),
                               memory_space=pltpu.SMEM),
                  pl.BlockSpec(memory_space=pl.ANY),
                  pl.BlockSpec((1, d, D_EXPERT), lambda i, be, nu: (be[i], 0, 0)),
                  pl.BlockSpec((1, d, D_EXPERT), lambda i, be, nu: (be[i], 0, 0)),
                  pl.BlockSpec((1, D_EXPERT, d), lambda i, be, nu: (be[i], 0, 0))],
        out_specs=pl.BlockSpec((tb, d), lambda i, be, nu: (i, 0)),
        scratch_shapes=[pltpu.VMEM((2, tb // SUBLANES, SUBLANES, d), F32), pltpu.SemaphoreType.DMA((2,))],
    )
    return pl.pallas_call(
        _moe_kernel,
        grid_spec=grid_spec,
        out_shape=jax.ShapeDtypeStruct((nblk * tb, d), F32),
        compiler_params=_params(("arbitrary",)),
        name="moe_ffn",
    )(blk_exp, nused, tok, tok, h2f, w1, w3, w2)


def _combine_kernel(pos_ref, posn_ref, ys_hbm, h2_ref, route_ref, g_ref, b_ref, y_ref, gbuf, sem):
    i = pl.program_id(0)
    n = pl.num_programs(0)
    tm = h2_ref.shape[0]
    slot = lax.rem(i, 2)

    ngrp = gbuf.shape[1]
    d = gbuf.shape[3]

    def issue_rows(idx_ref, sl):
        def group(g, c):
            for u in range(SUBLANES):
                r = idx_ref[0, 0, g * SUBLANES + u]
                pltpu.make_async_copy(ys_hbm.at[pl.ds(r, 1)], gbuf.at[sl, g, pl.ds(u, 1)], sem.at[sl]).start()
            return c
        lax.fori_loop(0, ngrp, group, 0)

    def wait_rows(sl):
        pltpu.make_async_copy(gbuf.at[1 - sl], gbuf.at[sl], sem.at[sl]).wait()

    @pl.when(i == 0)
    def _():
        issue_rows(pos_ref, 0)

    wait_rows(slot)
    issue_rows(posn_ref, 1 - slot)
    rt = route_ref[...]
    half = ngrp // 2
    moe = (rt[:, 2:3] * gbuf[slot, :half].reshape(tm, d) + rt[:, 3:4] * gbuf[slot, half:].reshape(tm, d))
    y_ref[...] = _layer_norm(DEEPNORM_ALPHA * h2_ref[...] + moe, g_ref[...], b_ref[...])

    @pl.when(i == n - 1)
    def _():
        wait_rows(1 - slot)


def _combine(pos, ysort, h2f, route, g, b, tm):
    t, d = h2f.shape
    n = t // tm
    return pl.pallas_call(
        _combine_kernel,
        grid=(n,),
        in_specs=[pl.BlockSpec((1, 1, 2 * tm), lambda i: (i, 0, 0), memory_space=pltpu.SMEM),
                  pl.BlockSpec((1, 1, 2 * tm), lambda i: (jnp.minimum(i + 1, n - 1), 0, 0),
                               memory_space=pltpu.SMEM),
                  pl.BlockSpec(memory_space=pl.ANY),
                  pl.BlockSpec((tm, d), lambda i: (i, 0)),
                  pl.BlockSpec((tm, LANES), lambda i: (i, 0)),
                  pl.BlockSpec((1, d), lambda i: (0, 0)),
                  pl.BlockSpec((1, d), lambda i: (0, 0))],
        out_specs=pl.BlockSpec((tm, d), lambda i: (i, 0)),
        out_shape=jax.ShapeDtypeStruct((t, d), F32),
        scratch_shapes=[pltpu.VMEM((2, 2 * tm // SUBLANES, SUBLANES, d), F32), pltpu.SemaphoreType.DMA((2,))],
        compiler_params=_params(("arbitrary",)),
        name="combine",
    )(pos, pos, ysort, h2f, route, g, b)


def _dispatch(expert, tb, tm):
    t = expert.shape[0]
    a = 2 * t
    ef = expert.reshape(a)
    order = jnp.argsort(ef, stable=True).astype(jnp.int32)
    es = ef[order]
    counts = jnp.sum((ef[:, None] == jnp.arange(N_EXPERTS, dtype=jnp.int32)[None, :]).astype(jnp.int32), axis=0)
    starts = jnp.cumsum(counts) - counts
    nb = (counts + tb - 1) // tb
    bend = jnp.cumsum(nb)
    bstart = bend - nb
    nblk = -(-a // tb) + N_EXPERTS
    blk = jnp.arange(nblk, dtype=jnp.int32)
    bexp = jnp.minimum(jnp.sum((blk[:, None] >= bend[None, :]).astype(jnp.int32), axis=1), N_EXPERTS - 1)
    row0 = starts[bexp] + (blk - bstart[bexp]) * tb
    idx = jnp.clip(row0[:, None] + jnp.arange(tb, dtype=jnp.int32)[None, :], 0, a - 1)
    tok = (order[idx] // 2).astype(jnp.int32).reshape(nblk, 1, tb)
    dest_sorted = bstart[es] * tb + (jnp.arange(a, dtype=jnp.int32) - starts[es])
    _, pos = lax.sort_key_val(order, dest_sorted.astype(jnp.int32))
    pos = pos.reshape(t, 2)
    pos = pos.reshape(t // tm, tm, 2).transpose(0, 2, 1).reshape(t // tm, 1, 2 * tm)
    return bexp, bend[-1:].astype(jnp.int32), tok, pos


def _moe(h2, route, prm, tb, tm):
    b, s, d = h2.shape
    t = b * s
    h2f = h2.reshape(t, d)
    rf = route.reshape(t, LANES)
    expert = rf[:, :2].astype(jnp.int32)
    bexp, nused, tok, pos = _dispatch(expert, tb, tm)
    ysort = _moe_ffn(h2f, bexp, nused, tok, prm["moe_w1"], prm["moe_w3"], prm["moe_w2"], tb)
    y = _combine(pos, ysort, h2f, rf, prm["ln3_g"], prm["ln3_b"], tm)
    return y.reshape(b, s, d)


N_PAIR_CLASSES = N_GROUPS * EXPERTS_PER_GROUP * EXPERTS_PER_GROUP
N_REAL_PAIRS = N_GROUPS * (EXPERTS_PER_GROUP * (EXPERTS_PER_GROUP - 1) // 2)


def _moe_pair_kernel(elo_ref, ehi_ref, nused_ref, tok_ref, tokn_ref, h2_hbm, gate_ref,
                     w1a_ref, w3a_ref, w2a_ref, w1b_ref, w3b_ref, w2b_ref, y_ref, xbuf, sem):
    i = pl.program_id(0)
    nblk = pl.num_programs(0)
    nused = nused_ref[0]
    ngrp = xbuf.shape[1]
    tb = ngrp * SUBLANES
    slot = lax.rem(i, 2)

    def issue_rows(idx_ref, sl):
        def group(g, c):
            for u in range(SUBLANES):
                r = idx_ref[0, 0, g * SUBLANES + u]
                pltpu.make_async_copy(h2_hbm.at[pl.ds(r, 1)], xbuf.at[sl, g, pl.ds(u, 1)], sem.at[sl]).start()
            return c
        lax.fori_loop(0, ngrp, group, 0)

    def wait_rows(sl):
        pltpu.make_async_copy(xbuf.at[1 - sl], xbuf.at[sl], sem.at[sl]).wait()

    def ffn(x, w1_ref, w3_ref, w2_ref):
        a = _dot(x, w1_ref[0])
        g = _dot(x, w3_ref[0])
        hmid = (a / (1.0 + jnp.exp(-a))) * g
        return _dot(hmid.astype(BF16), w2_ref[0])

    @pl.when(i == 0)
    def _():
        issue_rows(tok_ref, 0)

    @pl.when(i < nused)
    def _():
        wait_rows(slot)
        issue_rows(tokn_ref, 1 - slot)
        x = xbuf[slot].reshape(tb, xbuf.shape[3]).astype(BF16)
        gt = gate_ref[...]
        y_ref[...] = gt[:, 0:1] * ffn(x, w1a_ref, w3a_ref, w2a_ref) + gt[:, 1:2] * ffn(x, w1b_ref, w3b_ref, w2b_ref)

        @pl.when(i == nblk - 1)
        def _():
            wait_rows(1 - slot)

    @pl.when(i >= nused)
    def _():
        y_ref[...] = jnp.zeros_like(y_ref)

        @pl.when(i == nused)
        def _():
            wait_rows(slot)


def _moe_pair_ffn(h2f, elo, ehi, nused, tok, gates, w1, w3, w2, tb):
    nblk = tok.shape[0]
    d = h2f.shape[1]
    wspec = lambda rows, cols, which: pl.BlockSpec(
        (1, rows, cols), (lambda i, lo, hi, nu: (lo[i], 0, 0)) if which == 0 else (lambda i, lo, hi, nu: (hi[i], 0, 0)))
    grid_spec = pltpu.PrefetchScalarGridSpec(
        num_scalar_prefetch=3,
        grid=(nblk,),
        in_specs=[pl.BlockSpec((1, 1, tb), lambda i, lo, hi, nu: (i, 0, 0), memory_space=pltpu.SMEM),
                  pl.BlockSpec((1, 1, tb), lambda i, lo, hi, nu: (jnp.minimum(i + 1, nblk - 1), 0, 0),
                               memory_space=pltpu.SMEM),
                  pl.BlockSpec(memory_space=pl.ANY),
                  pl.BlockSpec((tb, LANES), lambda i, lo, hi, nu: (i, 0)),
                  wspec(d, D_EXPERT, 0), wspec(d, D_EXPERT, 0), wspec(D_EXPERT, d, 0),
                  wspec(d, D_EXPERT, 1), wspec(d, D_EXPERT, 1), wspec(D_EXPERT, d, 1)],
        out_specs=pl.BlockSpec((tb, d), lambda i, lo, hi, nu: (i, 0)),
        scratch_shapes=[pltpu.VMEM((2, tb // SUBLANES, SUBLANES, d), F32), pltpu.SemaphoreType.DMA((2,))],
    )
    return pl.pallas_call(
        _moe_pair_kernel,
        grid_spec=grid_spec,
        out_shape=jax.ShapeDtypeStruct((nblk * tb, d), F32),
        compiler_params=_params(("arbitrary",)),
        name="moe_pair_ffn",
    )(elo, ehi, nused, tok, tok, h2f, gates, w1, w3, w2, w1, w3, w2)


def _combine1_kernel(pos_ref, posn_ref, ys_hbm, h2_ref, g_ref, b_ref, y_ref, gbuf, sem):
    i = pl.program_id(0)
    n = pl.num_programs(0)
    tm = h2_ref.shape[0]
    slot = lax.rem(i, 2)
    ngrp = gbuf.shape[1]

    def issue_rows(idx_ref, sl):
        def group(g, c):
            for u in range(SUBLANES):
                r = idx_ref[0, 0, g * SUBLANES + u]
                pltpu.make_async_copy(ys_hbm.at[pl.ds(r, 1)], gbuf.at[sl, g, pl.ds(u, 1)], sem.at[sl]).start()
            return c
        lax.fori_loop(0, ngrp, group, 0)

    def wait_rows(sl):
        pltpu.make_async_copy(gbuf.at[1 - sl], gbuf.at[sl], sem.at[sl]).wait()

    @pl.when(i == 0)
    def _():
        issue_rows(pos_ref, 0)

    wait_rows(slot)
    issue_rows(posn_ref, 1 - slot)
    moe = gbuf[slot].reshape(tm, gbuf.shape[3])
    y_ref[...] = _layer_norm(DEEPNORM_ALPHA * h2_ref[...] + moe, g_ref[...], b_ref[...])

    @pl.when(i == n - 1)
    def _():
        wait_rows(1 - slot)


def _combine1(pos, ysort, h2f, g, b, tm):
    t, d = h2f.shape
    n = t // tm
    return pl.pallas_call(
        _combine1_kernel,
        grid=(n,),
        in_specs=[pl.BlockSpec((1, 1, tm), lambda i: (i, 0, 0), memory_space=pltpu.SMEM),
                  pl.BlockSpec((1, 1, tm), lambda i: (jnp.minimum(i + 1, n - 1), 0, 0), memory_space=pltpu.SMEM),
                  pl.BlockSpec(memory_space=pl.ANY),
                  pl.BlockSpec((tm, d), lambda i: (i, 0)),
                  pl.BlockSpec((1, d), lambda i: (0, 0)),
                  pl.BlockSpec((1, d), lambda i: (0, 0))],
        out_specs=pl.BlockSpec((tm, d), lambda i: (i, 0)),
        out_shape=jax.ShapeDtypeStruct((t, d), F32),
        scratch_shapes=[pltpu.VMEM((2, tm // SUBLANES, SUBLANES, d), F32), pltpu.SemaphoreType.DMA((2,))],
        compiler_params=_params(("arbitrary",)),
        name="combine1",
    )(pos, pos, ysort, h2f, g, b)


def _dispatch_pairs(rf, tb, tm):
    t = rf.shape[0]
    e1 = rf[:, 0].astype(jnp.int32)
    e2 = rf[:, 1].astype(jnp.int32)
    first_low = e1 < e2
    lo = jnp.minimum(e1, e2)
    hi = jnp.maximum(e1, e2)
    g_lo = jnp.where(first_low, rf[:, 2], rf[:, 3])
    g_hi = jnp.where(first_low, rf[:, 3], rf[:, 2])
    epg = EXPERTS_PER_GROUP
    cls = (lo // epg) * (epg * epg) + (lo % epg) * epg + (hi % epg)
    cs, order, g_lo, g_hi = lax.sort((cls, jnp.arange(t, dtype=jnp.int32), g_lo, g_hi), num_keys=1, is_stable=True)
    counts = jnp.sum((cls[:, None] == jnp.arange(N_PAIR_CLASSES, dtype=jnp.int32)[None, :]).astype(jnp.int32), axis=0)
    starts = jnp.cumsum(counts) - counts
    nb = (counts + tb - 1) // tb
    bend = jnp.cumsum(nb)
    bstart = bend - nb
    nblk = -(-t // tb) + N_REAL_PAIRS
    blk = jnp.arange(nblk, dtype=jnp.int32)
    bcls = jnp.minimum(jnp.sum((blk[:, None] >= bend[None, :]).astype(jnp.int32), axis=1), N_PAIR_CLASSES - 1)
    grp = bcls // (epg * epg)
    elo = (grp * epg + (bcls % (epg * epg)) // epg).astype(jnp.int32)
    ehi = (grp * epg + bcls % epg).astype(jnp.int32)
    row0 = starts[bcls] + (blk - bstart[bcls]) * tb
    idx = jnp.clip(row0[:, None] + jnp.arange(tb, dtype=jnp.int32)[None, :], 0, t - 1)
    tok = order[idx].astype(jnp.int32)
    lane = jnp.arange(LANES, dtype=jnp.int32)[None, :]
    gates = (g_lo[idx].reshape(nblk * tb, 1) * (lane == 0).astype(F32)
             + g_hi[idx].reshape(nblk * tb, 1) * (lane == 1).astype(F32))
    dest_sorted = bstart[cs] * tb + (jnp.arange(t, dtype=jnp.int32) - starts[cs])
    _, pos = lax.sort_key_val(order, dest_sorted.astype(jnp.int32))
    return elo, ehi, bend[-1:].astype(jnp.int32), tok.reshape(nblk, 1, tb), gates, pos.reshape(t // tm, 1, tm)


def _moe_pairs(h2, route, prm, tb, tm):
    b, s, d = h2.shape
    t = b * s
    h2f = h2.reshape(t, d)
    elo, ehi, nused, tok, gates, pos = _dispatch_pairs(route.reshape(t, LANES), tb, tm)
    ysort = _moe_pair_ffn(h2f, elo, ehi, nused, tok, gates, prm["moe_w1"], prm["moe_w3"], prm["moe_w2"], tb)
    y = _combine1(pos, ysort, h2f, prm["ln3_g"], prm["ln3_b"], tm)
    return y.reshape(b, s, d)


def _trunk(x, shift_prev, state0, past_k, past_v, mk, mv, prm, lam, lam_init, cfg):
    b, s, _ = x.shape
    pa, q, k, v, kb, vb, kb1 = _inproj(x, prm["w_in"], cfg["in_bb"], cfg["in_ts"],
                                       cfg["attn_tq"] if past_k is None else 0)
    oa, sfin = _rwkv(pa, shift_prev, _state_to_pairs(state0), prm, cfg["rwkv_tb"], cfg["rwkv_chunk"],
                     cfg["rwkv_group"])
    if past_k is None:
        ob = _attn_prompt(lam, q, kb, kb1, vb, prm["subln"].reshape(DIFF_V, 1), cfg["attn_tq"], lam_init)
    else:
        ob = _attn_cached(lam, q, kb, vb, past_k, past_v, prm["subln"], lam_init)
    h2, route = _post(x, oa, ob, mk, mv, prm, cfg["post_tm"], cfg["post_split"])
    moe = _moe_pairs if cfg["moe_pairs"] else _moe
    y = moe(h2, route, prm, cfg["moe_tb"], cfg["comb_tm"])
    return y, k, v, _pairs_to_state(sfin), pa[:, s - 1:s, :]


def _prep(w, l, lam_scalar_inputs):
    row = lambda a: a[l].reshape(1, -1).astype(F32)
    idx = jnp.arange(D_RWKV, dtype=jnp.int32) // RWKV_HEAD
    wr = jnp.zeros((D_MODEL, LANES), F32)
    wr = wr.at[:, :N_EXPERTS].set(w["moe_w_expert"][l]).at[:, N_EXPERTS:N_EXPERTS + N_GROUPS].set(w["moe_w_group"][l])
    wr_hi = wr.astype(BF16)
    br = jnp.zeros((1, LANES), F32)
    br = br.at[0, :N_EXPERTS].set(w["moe_b_expert"][l]).at[0, N_EXPERTS:N_EXPERTS + N_GROUPS].set(w["moe_b_group"][l])
    return {
        "w_in": w["w_in"][l].astype(BF16),
        "mu": row(w["rwkv_mu"]), "w0": row(w["rwkv_w0"]), "w2": w["rwkv_w2"][l].astype(BF16),
        "a0": row(w["rwkv_a0"]), "a2": w["rwkv_a2"][l].astype(BF16), "g2": w["rwkv_g2"][l].astype(BF16),
        "k_k": row(w["rwkv_k_k"]), "k_a": row(w["rwkv_k_a"]), "r_k": row(w["rwkv_r_k"]),
        "ln_w": row(w["rwkv_ln_w"]), "ln_b": row(w["rwkv_ln_b"]),
        "hsum": (idx[:, None] == idx[None, :]).astype(BF16),
        "subln": row(w["diff_subln"]),
        "w_out": w["w_out"][l].astype(BF16), "wq": w["mem_wq"][l].astype(BF16), "wo": w["mem_wo"][l].astype(BF16),
        "ln1_g": row(w["ln1_g"]), "ln1_b": row(w["ln1_b"]), "ln2_g": row(w["ln2_g"]), "ln2_b": row(w["ln2_b"]),
        "ln3_g": row(w["ln3_g"]), "ln3_b": row(w["ln3_b"]),
        "wr_hi": wr_hi, "wr_lo": (wr - wr_hi.astype(F32)).astype(BF16), "br": br,
        "moe_w1": w["moe_w1"][l].astype(BF16), "moe_w3": w["moe_w3"][l].astype(BF16),
        "moe_w2": w["moe_w2"][l].astype(BF16),
    }


def _tile(n, pref):
    return pref if n % pref == 0 else n


def kernel(x_prompt, x_sample, mem_prompt, cache_diff_k, cache_diff_v, cache_mem_k, cache_mem_v, state_rwkv, state_shift, w_in, rwkv_mu, rwkv_w0, rwkv_w2, rwkv_a0, rwkv_a2, rwkv_g2, rwkv_k_k, rwkv_k_a, rwkv_r_k, rwkv_ln_w, rwkv_ln_b, diff_lq1, diff_lk1, diff_lq2, diff_lk2, diff_subln, w_out, ln1_g, ln1_b, mem_wq, mem_wk, mem_wv, mem_wo, ln2_g, ln2_b, moe_w_group, moe_b_group, moe_w_expert, moe_b_expert, moe_w1, moe_w3, moe_w2, ln3_g, ln3_b):
    w = dict(w_in=w_in, rwkv_mu=rwkv_mu, rwkv_w0=rwkv_w0, rwkv_w2=rwkv_w2, rwkv_a0=rwkv_a0, rwkv_a2=rwkv_a2,
             rwkv_g2=rwkv_g2, rwkv_k_k=rwkv_k_k, rwkv_k_a=rwkv_k_a, rwkv_r_k=rwkv_r_k, rwkv_ln_w=rwkv_ln_w,
             rwkv_ln_b=rwkv_ln_b, diff_subln=diff_subln, w_out=w_out, ln1_g=ln1_g, ln1_b=ln1_b, mem_wq=mem_wq,
             mem_wo=mem_wo, ln2_g=ln2_g, ln2_b=ln2_b, moe_w_group=moe_w_group, moe_b_group=moe_b_group,
             moe_w_expert=moe_w_expert, moe_b_expert=moe_b_expert, moe_w1=moe_w1, moe_w3=moe_w3, moe_w2=moe_w2,
             ln3_g=ln3_g, ln3_b=ln3_b)
    bp, sp, _ = x_prompt.shape
    bs, ss, _ = x_sample.shape
    depth = w_in.shape[0]
    cfg_p = dict(in_bb=1, in_ts=_tile(sp, 512), rwkv_tb=_tile(sp, 256), rwkv_chunk=CHUNK, rwkv_group=_tile(sp, 256) // CHUNK,
                 attn_tq=_tile(sp, 512), post_tm=_tile(sp, 512), post_split=2, moe_tb=256, moe_pairs=True, comb_tm=_tile(bp * sp, 256))
    cfg_s = dict(in_bb=bs, in_ts=ss, rwkv_tb=ss, rwkv_chunk=ss, rwkv_group=1, attn_tq=ss, post_tm=ss, post_split=1, moe_tb=64, moe_pairs=False,
                 comb_tm=_tile(bs * ss, 256))
    yp, ys = x_prompt, x_sample
    outs = [[] for _ in range(10)]
    for l in range(depth):
        prm = _prep(w, l, None)
        lam_init = 0.8 - 0.6 * math.exp(-0.3 * l)
        f = lambda z: z[l].astype(F32)
        lam = (jnp.exp(jnp.sum(f(diff_lq1) * f(diff_lk1))) - jnp.exp(jnp.sum(f(diff_lq2) * f(diff_lk2)))
               + lam_init).reshape(1)
        slopes = jnp.asarray(ALIBI_SLOPES, F32)
        lam = jnp.concatenate([lam, slopes, jnp.zeros((3,), F32)])
        mk_p, mv_p, mkb, mvb = _memproj(mem_prompt, mem_wk[l].astype(BF16), mem_wv[l].astype(BF16))
        yp, k_p, v_p, st_p, sh_p = _trunk(
            yp, jnp.zeros((bp, 1, C_RWKV), F32), jnp.zeros((bp, 2 * RWKV_PAIRS, RWKV_HEAD, RWKV_HEAD), F32),
            None, None, mkb, mvb, prm, lam, lam_init, cfg_p)
        ys, k_s, v_s, st_s, sh_s = _trunk(
            ys, state_shift[l], state_rwkv[l], cache_diff_k[l], cache_diff_v[l],
            cache_mem_k[l].astype(BF16), cache_mem_v[l].astype(BF16), prm, lam, lam_init, cfg_s)
        for lst, val in zip(outs, (k_p, v_p, mk_p, mv_p, st_p, sh_p, k_s, v_s, st_s, sh_s)):
            lst.append(val)
    return (yp, ys) + tuple(jnp.stack(o) for o in outs)
```

```python
import functools
import math

import jax
import jax.numpy as jnp
from jax import lax
from jax.experimental import pallas as pl
from jax.experimental.pallas import tpu as pltpu

F32 = jnp.float32
BF16 = jnp.bfloat16

D_MODEL = 1024
CHUNK = 64
D_RWKV = 512
RWKV_HEAD = 64
RWKV_PAIRS = D_RWKV // (2 * RWKV_HEAD)
W_LORA = 64
A_LORA = 64
G_LORA = 128
C_RWKV = 3 * D_RWKV + W_LORA + A_LORA + G_LORA
D_DIFF = 512
DIFF_HEADS = 4
DIFF_V = 128
DIFF_QK = 64
C_IN = C_RWKV + 3 * D_DIFF
MEM_HEADS = 4
MEM_HEAD = D_MODEL // MEM_HEADS
N_GROUPS = 4
EXPERTS_PER_GROUP = 8
N_EXPERTS = 32
D_EXPERT = 512
DEPTH = 1
DEEPNORM_ALPHA = (2.0 * DEPTH) ** 0.25
LN_EPS = 1e-5
RMS_EPS = 1e-5
RWKV_GN_EPS = 64e-5
NEG_INF = -1e30
LOG2E = math.log2(math.e)
ALIBI_SLOPES = tuple(2.0 ** (-8.0 * (h + 1) / DIFF_HEADS) for h in range(DIFF_HEADS))
VT_PAD = 16
LANES = 128
SUBLANES = 8
VMEM_LIMIT = 48 * 1024 * 1024


def _params(sem):
    return pltpu.CompilerParams(dimension_semantics=sem, vmem_limit_bytes=VMEM_LIMIT)


def _dot(a, b):
    return jnp.dot(a, b, preferred_element_type=F32)


def _dot_nt(a, b):
    return lax.dot_general(a, b, (((1,), (1,)), ((), ())), preferred_element_type=F32)


def _dot_tn(a, b):
    return lax.dot_general(a, b, (((0,), (0,)), ((), ())), preferred_element_type=F32)


def _split2(x):
    hi = x.astype(BF16)
    lo = (x - hi.astype(F32)).astype(BF16)
    return hi, lo


def _split3(x):
    hi = x.astype(BF16)
    r = x - hi.astype(F32)
    mid = r.astype(BF16)
    lo = (r - mid.astype(F32)).astype(BF16)
    return hi, mid, lo


def _layer_norm(x, g, b):
    mu = jnp.mean(x, axis=-1, keepdims=True)
    xc = x - mu
    var = jnp.mean(xc * xc, axis=-1, keepdims=True)
    return xc * lax.rsqrt(var + LN_EPS) * g + b


def _memproj_kernel(m_ref, wk_ref, wv_ref, k_ref, v_ref, kb_ref, vb_ref):
    m = m_ref[0].astype(BF16)
    k = _dot(m, wk_ref[...])
    v = _dot(m, wv_ref[...])
    k_ref[0] = k
    v_ref[0] = v
    kb_ref[0] = k.astype(BF16)
    vb_ref[0] = v.astype(BF16)


def _memproj(mem, wk, wv):
    b, n, d = mem.shape
    blk = pl.BlockSpec((1, n, d), lambda i: (i, 0, 0))
    wspec = pl.BlockSpec((d, d), lambda i: (0, 0))
    return pl.pallas_call(
        _memproj_kernel,
        grid=(b,),
        in_specs=[blk, wspec, wspec],
        out_specs=[blk, blk, blk, blk],
        out_shape=[jax.ShapeDtypeStruct((b, n, d), F32), jax.ShapeDtypeStruct((b, n, d), F32),
                   jax.ShapeDtypeStruct((b, n, d), BF16), jax.ShapeDtypeStruct((b, n, d), BF16)],
        compiler_params=_params(("arbitrary",)),
        name="memproj",
    )(mem, wk, wv)


def _inproj_kernel(x_ref, w_ref, pa_ref, q_ref, k_ref, v_ref, kb_ref, vb_ref, kb1_ref, *, attn_tile):
    bb, ts, d = x_ref.shape
    rows = bb * ts
    x = x_ref[...].reshape(rows, d).astype(BF16)
    pq = _dot(x, w_ref[:, C_RWKV:C_RWKV + D_DIFF])
    pk = _dot(x, w_ref[:, C_RWKV + D_DIFF:C_RWKV + 2 * D_DIFF])
    pv = _dot(x, w_ref[:, C_RWKV + 2 * D_DIFF:])
    pa_ref[...] = _dot(x, w_ref[:, :C_RWKV]).reshape(bb, ts, C_RWKV)
    if attn_tile:
        lane = lax.broadcasted_iota(jnp.int32, (rows, LANES), 1)
        kloc = lax.rem(pl.program_id(1) * ts + lax.broadcasted_iota(jnp.int32, (rows, LANES), 0), attn_tile)
        aug = jnp.where(lax.broadcasted_iota(jnp.int32, (VT_PAD, rows), 0) == 0, 1.0, 0.0).astype(BF16)
    for h in range(DIFF_HEADS):
        hs = slice(h * LANES, (h + 1) * LANES)
        qh = pq[:, hs]
        kh = pk[:, hs]
        k_ref[:, h] = kh.reshape(bb, ts, LANES)
        vh = pv[:, hs]
        v_ref[:, h] = vh.reshape(bb, ts, LANES)
        if attn_tile:
            q_ref[0, h] = (qh * (DIFF_QK ** -0.5 * LOG2E)).T.astype(BF16)
            vb_ref[0, h] = jnp.concatenate([vh.T.astype(BF16), aug], axis=0)
            bias = kloc.astype(F32) * (ALIBI_SLOPES[h] * LOG2E)
            b_hi = bias.astype(BF16).astype(F32)
            b_mid = (bias - b_hi).astype(BF16).astype(F32)
            b_lo = bias - b_hi - b_mid
            cols = jnp.where(lane == DIFF_QK, b_hi, jnp.where(lane == DIFF_QK + 1, b_mid,
                             jnp.where(lane == DIFF_QK + 2, b_lo, 0.0)))
            kb_ref[0, h] = jnp.where(lane < DIFF_QK, kh, cols).astype(BF16)
            kb1_ref[0, h] = jnp.where(lane < DIFF_QK, pltpu.roll(kh, DIFF_QK, axis=1), cols).astype(BF16)
        else:
            q_ref[:, h] = (qh * (DIFF_QK ** -0.5)).reshape(bb, ts, LANES).astype(BF16)
            vb_ref[:, h] = vh.reshape(bb, ts, LANES).astype(BF16)
            kb_ref[:, h] = kh.reshape(bb, ts, LANES).astype(BF16)
            kb1_ref[:, h] = kh.reshape(bb, ts, LANES).astype(BF16)


def _inproj(x, w_in_bf, bb, ts, attn_tile):
    b, s, d = x.shape
    hm = pl.BlockSpec((bb, DIFF_HEADS, ts, LANES), lambda i, j: (i, 0, j, 0))
    hshape = (b, DIFF_HEADS, s, LANES)
    if attn_tile:
        assert bb == 1
        qm = pl.BlockSpec((1, DIFF_HEADS, LANES, ts), lambda i, j: (i, 0, 0, j))
        qshape = (b, DIFF_HEADS, LANES, s)
        vm = pl.BlockSpec((1, DIFF_HEADS, LANES + VT_PAD, ts), lambda i, j: (i, 0, 0, j))
        vshape = (b, DIFF_HEADS, LANES + VT_PAD, s)
    else:
        qm, qshape, vm, vshape = hm, hshape, hm, hshape
    return pl.pallas_call(
        functools.partial(_inproj_kernel, attn_tile=attn_tile),
        grid=(b // bb, s // ts),
        in_specs=[pl.BlockSpec((bb, ts, d), lambda i, j: (i, j, 0)),
                  pl.BlockSpec((d, C_IN), lambda i, j: (0, 0))],
        out_specs=[pl.BlockSpec((bb, ts, C_RWKV), lambda i, j: (i, j, 0)), qm, hm, hm, hm, vm, hm],
        out_shape=[jax.ShapeDtypeStruct((b, s, C_RWKV), F32),
                   jax.ShapeDtypeStruct(qshape, BF16),
                   jax.ShapeDtypeStruct(hshape, F32), jax.ShapeDtypeStruct(hshape, F32),
                   jax.ShapeDtypeStruct(hshape, BF16), jax.ShapeDtypeStruct(vshape, BF16),
                   jax.ShapeDtypeStruct(hshape, BF16)],
        compiler_params=_params(("arbitrary", "arbitrary")),
        name="inproj",
    )(x, w_in_bf)


def _rwkv_kernel(pa_ref, shift_ref, s0_ref, mu_ref, w0_ref, w2_ref, a0_ref, a2_ref, g2_ref,
                 kk_ref, ka_ref, rk_ref, lnw_ref, lnb_ref, hsum_ref,
                 o_ref, sfin_ref,
                 carry_ref, al_s, be_s, kb_s, rb_s, v_s, gam_s, o_s, bonus_s, g_s, w1_s, n1_s, c0_s, n2_s,
                 *, chunk, group):
    L = chunk
    tb = pa_ref.shape[1]
    nchunk = tb // L
    t = pl.program_id(1)

    @pl.when(t == 0)
    def _():
        sfin_ref[...] = s0_ref[...]
        carry_ref[...] = shift_ref[0]

    p = pa_ref[0]
    prev = pltpu.roll(p, 1, axis=0)
    row = lax.broadcasted_iota(jnp.int32, (tb, 1), 0)
    prev = jnp.where(row == 0, carry_ref[...], prev)
    carry_ref[...] = p[tb - 1:tb, :]
    ps = p + (prev - p) * mu_ref[...]

    r = ps[:, :D_RWKV]
    k = ps[:, D_RWKV:2 * D_RWKV]
    v = ps[:, 2 * D_RWKV:3 * D_RWKV]
    o0 = 3 * D_RWKV
    w_lo = ps[:, o0:o0 + W_LORA]
    a_lo = ps[:, o0 + W_LORA:o0 + W_LORA + A_LORA]
    g_lo = ps[:, o0 + W_LORA + A_LORA:]

    hsum = hsum_ref[...]

    def headsum(x):
        return _dot(x.astype(BF16), hsum)

    z = w0_ref[...] + _dot(jnp.tanh(w_lo).astype(BF16), w2_ref[...])
    lw = -math.exp(-0.5) / (1.0 + jnp.exp(-z))
    a = 1.0 / (1.0 + jnp.exp(-(a0_ref[...] + _dot(a_lo.astype(BF16), a2_ref[...]))))
    g_s[...] = _dot((1.0 / (1.0 + jnp.exp(-g_lo))).astype(BF16), g2_ref[...])
    kk = k * kk_ref[...]
    kk = kk / jnp.maximum(jnp.sqrt(headsum(kk * kk)), 1e-12)
    kmod = k * (1.0 + (a - 1.0) * ka_ref[...])
    bonus_s[...] = headsum(r * kmod * rk_ref[...]) * v

    bi = lax.broadcasted_iota(jnp.int32, (tb, tb), 0)
    bj = lax.broadcasted_iota(jnp.int32, (tb, tb), 1)
    tri = (((bi // L) == (bj // L)) & (bi >= bj)).astype(BF16)
    h3 = _split3(lw)
    cum = _dot(tri, h3[0]) + _dot(tri, h3[1]) + _dot(tri, h3[2])
    gam = jnp.exp(cum)
    igam = jnp.exp(-cum)
    gam_s[...] = gam
    al_s[...] = kk * jnp.exp(cum - lw)
    be_s[...] = kk * a * igam
    kb_s[...] = kmod * igam
    rb_s[...] = r * gam
    v_s[...] = v

    l2 = 2 * L
    ri = lax.broadcasted_iota(jnp.int32, (l2, l2), 0)
    ci = lax.broadcasted_iota(jnp.int32, (l2, l2), 1)
    same = (ri // L) == (ci // L)
    strict = same & (ri > ci)
    incl = same & (ri >= ci)
    eye = (ri == ci).astype(F32)
    lane = lax.broadcasted_iota(jnp.int32, (1, LANES), 1)
    first = lane < RWKV_HEAD

    def stack(x):
        return jnp.concatenate([jnp.where(first, x, 0.0), jnp.where(first, 0.0, x)], axis=0)

    def phase_a(cg, carry):
        cs = []
        for gi in range(group):
            c = cg * group + gi
            sl = pl.ds(pl.multiple_of(c * L, L), L)
            gl = gam_s[pl.ds(c * L + (L - 1), 1), :]
            for j in range(RWKV_PAIRS):
                ls = slice(j * LANES, (j + 1) * LANES)
                cs.append(dict(c=c, j=j, gl=gl[:, ls], al2=stack(al_s[sl, ls]), rb2=stack(rb_s[sl, ls]),
                               be2=stack(be_s[sl, ls]), kb2=stack(kb_s[sl, ls]), v2=stack(v_s[sl, ls])))
        for d in cs:
            x = jnp.concatenate([d["al2"], d["rb2"]], axis=0).astype(BF16)
            y = jnp.concatenate([d["be2"], d["kb2"]], axis=0).astype(BF16)
            gmat = _dot_nt(x, y)
            ab = jnp.where(strict, gmat[:l2, :l2], 0.0)
            d["ak"] = jnp.where(strict, gmat[:l2, l2:], 0.0).astype(BF16)
            d["rbm"] = jnp.where(incl, gmat[l2:, :l2], 0.0).astype(BF16)
            d["rkm"] = jnp.where(incl, gmat[l2:, l2:], 0.0).astype(BF16)
            d["pw"] = ab
            d["tm"] = eye - ab
        n = 2
        while n < L:
            for d in cs:
                pb = d["pw"].astype(BF16)
                d["pw"] = _dot(pb, pb)
            for d in cs:
                d["tm"] = _dot(d["tm"].astype(BF16), (eye + d["pw"]).astype(BF16))
            n *= 2
        for d in cs:
            d["akv"] = _dot(d["ak"], d["v2"].astype(BF16))
        for d in cs:
            zz = jnp.concatenate([d["al2"], d["akv"]], axis=1).astype(BF16)
            d["m"] = _dot(d["tm"].astype(BF16), zz)
        for d in cs:
            rm = _dot(d["rbm"], d["m"].astype(BF16))
            rkv = _dot(d["rkm"], d["v2"].astype(BF16))
            n1_s[d["c"], d["j"]] = (d["rb2"] - rm[:, :LANES]).astype(BF16)
            n2_s[d["c"], d["j"]] = rkv - rm[:, LANES:]
        for d in cs:
            m1 = d["m"][:, :LANES].astype(BF16)
            m2 = d["m"][:, LANES:]
            w1 = _dot_tn(m1, d["be2"].astype(BF16)) * d["gl"]
            lhs = jnp.concatenate([d["v2"], -m2], axis=0).astype(BF16)
            rhs = jnp.concatenate([d["kb2"], d["be2"]], axis=0).astype(BF16)
            w1_s[d["c"], d["j"]] = w1.astype(BF16)
            c0_s[d["c"], d["j"]] = _dot_tn(lhs, rhs) * d["gl"]
        return carry

    lax.fori_loop(0, nchunk // group, phase_a, 0)

    def phase_b(c, carry):
        gl = gam_s[pl.ds(c * L + (L - 1), 1), :]
        outs = []
        for j in range(RWKV_PAIRS):
            sb = sfin_ref[0, j]
            sbb = sb.astype(BF16)
            o2 = _dot_nt(n1_s[c, j], sbb) + n2_s[c, j]
            outs.append(o2[:L] + o2[L:])
            sfin_ref[0, j] = sb * gl[:, j * LANES:(j + 1) * LANES] - _dot(sbb, w1_s[c, j]) + c0_s[c, j]
        o_s[pl.ds(pl.multiple_of(c * L, L), L), :] = jnp.concatenate(outs, axis=1)
        return carry

    lax.fori_loop(0, nchunk, phase_b, 0)

    o = o_s[...]
    inv_n = 1.0 / RWKV_HEAD
    mu_h = headsum(o) * inv_n
    oc = o - mu_h
    var_h = headsum(oc * oc) * inv_n
    on = oc * lax.rsqrt(var_h + RWKV_GN_EPS) * lnw_ref[...] + lnb_ref[...]
    o_ref[0] = ((on + bonus_s[...]) * g_s[...]).astype(o_ref.dtype)


def _rwkv(pa, shift_prev, s0_pairs, prm, tb, chunk, group):
    b, s, _ = pa.shape
    nchunk = tb // chunk
    row = lambda n: pl.BlockSpec((1, n), lambda i, j: (0, 0))
    mat = lambda m, n: pl.BlockSpec((m, n), lambda i, j: (0, 0))
    sspec = pl.BlockSpec((1, RWKV_PAIRS, LANES, LANES), lambda i, j: (i, 0, 0, 0))
    coef = lambda rows, dt: pltpu.VMEM((nchunk, RWKV_PAIRS, rows, LANES), dt)
    return pl.pallas_call(
        functools.partial(_rwkv_kernel, chunk=chunk, group=group),
        grid=(b, s // tb),
        in_specs=[pl.BlockSpec((1, tb, C_RWKV), lambda i, j: (i, j, 0)),
                  pl.BlockSpec((1, 1, C_RWKV), lambda i, j: (i, 0, 0)),
                  sspec,
                  row(C_RWKV), row(D_RWKV), mat(W_LORA, D_RWKV), row(D_RWKV), mat(A_LORA, D_RWKV),
                  mat(G_LORA, D_RWKV), row(D_RWKV), row(D_RWKV), row(D_RWKV), row(D_RWKV), row(D_RWKV),
                  mat(D_RWKV, D_RWKV)],
        out_specs=[pl.BlockSpec((1, tb, D_RWKV), lambda i, j: (i, j, 0)), sspec],
        out_shape=[jax.ShapeDtypeStruct((b, s, D_RWKV), BF16),
                   jax.ShapeDtypeStruct((b, RWKV_PAIRS, LANES, LANES), F32)],
        scratch_shapes=[pltpu.VMEM((1, C_RWKV), F32)] + [pltpu.VMEM((tb, D_RWKV), F32)] * 9
        + [coef(LANES, BF16), coef(2 * chunk, BF16), coef(LANES, F32), coef(2 * chunk, F32)],
        compiler_params=_params(("arbitrary", "arbitrary")),
        name="rwkv",
    )(pa, shift_prev, s0_pairs, prm["mu"], prm["w0"], prm["w2"], prm["a0"], prm["a2"], prm["g2"],
      prm["k_k"], prm["k_a"], prm["r_k"], prm["ln_w"], prm["ln_b"], prm["hsum"])


def _state_to_pairs(s):
    b = s.shape[0]
    s = s.reshape(b, RWKV_PAIRS, 2, RWKV_HEAD, RWKV_HEAD)
    z = jnp.zeros_like(s[:, :, 0])
    top = jnp.concatenate([s[:, :, 0], z], axis=-1)
    bot = jnp.concatenate([z, s[:, :, 1]], axis=-1)
    return jnp.concatenate([top, bot], axis=-2)


def _pairs_to_state(sp):
    b = sp.shape[0]
    h = RWKV_HEAD
    s = jnp.stack([sp[:, :, :h, :h], sp[:, :, h:, h:]], axis=2)
    return s.reshape(b, 2 * RWKV_PAIRS, h, h)


def _diff_finish(accs, ls, lam, sub, lam_init):
    o = accs[0] / ls[0] - lam * (accs[1] / ls[1])
    o = o * lax.rsqrt(jnp.mean(o * o, axis=-1, keepdims=True) + RMS_EPS)
    return o * sub * (1.0 - lam_init)


def _attn_kernel(lam_ref, qt_ref, k0_ref, k1_ref, vt_ref, tab_ref, sub_ref, o_ref, acc_ref, s_ref, m_ref, *, tq,
                 lam_init):
    h = pl.program_id(1)
    qi = pl.program_id(2)
    slope = lam_ref[1 + h] * LOG2E
    qt = qt_ref[0, 0]
    row = lax.broadcasted_iota(jnp.int32, (LANES, 1), 0)
    ones_rows = (row >= DIFF_QK) & (row < DIFF_QK + 3)
    fill = jnp.where(ones_rows, 1.0, 0.0).astype(BF16)

    def rhs(q_first):
        return jnp.where(row < DIFF_QK, q_first, fill)
    rs = (rhs(qt), rhs(jnp.concatenate([qt[DIFF_QK:], qt[:DIFF_QK]], axis=0)))
    k_refs = (k0_ref, k1_ref)
    acc_ref[...] = jnp.zeros_like(acc_ref)
    m_ref[...] = jnp.full_like(m_ref, NEG_INF)

    def scores(j, slot):
        ds = pl.ds(pl.multiple_of(j * tq, tq), tq)
        for c in range(2):
            s_ref[slot, c] = _dot(k_refs[c][0, 0, ds, :], rs[c])

    def softmax_pv(j, slot, diag):
        vt = vt_ref[0, 0, :, pl.ds(pl.multiple_of(j * tq, tq), tq)]
        off = slope * ((j - qi) * tq).astype(F32)
        for c in range(2):
            s = s_ref[slot, c]
            if diag:
                s = s + tab_ref[0]
            m = m_ref[c]
            m_new = jnp.maximum(m, jnp.max(s, axis=0, keepdims=True) + off)
            alpha = jnp.exp2(m - m_new)
            p = jnp.exp2(s - (m_new - off))
            m_ref[c] = m_new
            acc_ref[c] = alpha * acc_ref[c] + _dot(vt, p.astype(BF16))

    def pair(jj, carry):
        j = 2 * jj
        scores(j + 1, 1)
        softmax_pv(j, 0, False)
        scores(jnp.minimum(j + 2, qi), 0)
        softmax_pv(j + 1, 1, False)
        return carry

    scores(0, 0)
    lax.fori_loop(0, qi // 2, pair, 0)

    @pl.when(lax.rem(qi, 2) == 0)
    def _():
        softmax_pv(qi, 0, True)

    @pl.when(lax.rem(qi, 2) == 1)
    def _():
        scores(qi, 1)
        softmax_pv(qi - 1, 0, False)
        softmax_pv(qi, 1, True)

    a0 = acc_ref[0]
    a1 = acc_ref[1]
    o = a0[:DIFF_V] / a0[DIFF_V:DIFF_V + 1] - lam_ref[0] * (a1[:DIFF_V] / a1[DIFF_V:DIFF_V + 1])
    o = o * lax.rsqrt(jnp.mean(o * o, axis=0, keepdims=True) + RMS_EPS)
    o = o * (sub_ref[...] * (1.0 - lam_init))
    o_ref[0] = o.T.astype(o_ref.dtype)


def _attn_diag_table(tq):
    slopes = jnp.asarray(ALIBI_SLOPES, F32) * LOG2E
    pos = jnp.arange(tq, dtype=jnp.int32)
    kp, qp = pos[:, None], pos[None, :]
    rel = jnp.where(kp <= qp, 0, 2 * (qp - kp)).astype(F32)
    vis = (kp // CHUNK) <= (qp // CHUNK)
    return jnp.where(vis[None], slopes[:, None, None] * rel[None], NEG_INF)


def _attn_prompt(lam, qt, k0, k1, vt, subln_col, tq, lam_init):
    b, h, _, s = qt.shape
    kspec = pl.BlockSpec((1, 1, s, LANES), lambda i, j, t: (i, j, 0, 0))
    return pl.pallas_call(
        functools.partial(_attn_kernel, tq=tq, lam_init=lam_init),
        grid=(b, h, s // tq),
        in_specs=[pl.BlockSpec(memory_space=pltpu.SMEM),
                  pl.BlockSpec((1, 1, LANES, tq), lambda i, j, t: (i, j, 0, t)),
                  kspec, kspec,
                  pl.BlockSpec((1, 1, LANES + VT_PAD, s), lambda i, j, t: (i, j, 0, 0)),
                  pl.BlockSpec((1, tq, tq), lambda i, j, t: (j, 0, 0)),
                  pl.BlockSpec((DIFF_V, 1), lambda i, j, t: (0, 0))],
        out_specs=pl.BlockSpec((1, tq, LANES), lambda i, j, t: (i, t, j)),
        out_shape=jax.ShapeDtypeStruct((b, s, D_DIFF), BF16),
        scratch_shapes=[pltpu.VMEM((2, DIFF_V + VT_PAD, tq), F32), pltpu.VMEM((2, 2, tq, tq), F32),
                        pltpu.VMEM((2, 1, tq), F32)],
        compiler_params=_params(("arbitrary", "arbitrary", "arbitrary")),
        name="attn_prompt",
    )(lam, qt, k0, k1, vt, _attn_diag_table(tq), subln_col)


def _attn_cached_kernel(lam_ref, q_ref, kn_ref, vn_ref, kp_ref, vp_ref, sub_ref, o_ref, *, lam_init):
    h = pl.program_id(1)
    slope = lam_ref[1 + h]
    s_new = q_ref.shape[2]
    past = kp_ref.shape[2]
    q = q_ref[0, 0]
    kp = kp_ref[0, 0].astype(BF16)
    vp = vp_ref[0, 0].astype(BF16)
    kn = kn_ref[0, 0]
    vn = vn_ref[0, 0]
    q_pos = past + lax.broadcasted_iota(jnp.int32, (s_new, 1), 0)

    def bias(k_pos):
        dist = jnp.abs(q_pos - k_pos).astype(F32)
        vis = (k_pos // CHUNK) <= (q_pos // CHUNK)
        return jnp.where(vis, -slope * dist, NEG_INF), vis

    b_past, vis_past = bias(lax.broadcasted_iota(jnp.int32, (1, past), 1))
    b_new, vis_new = bias(past + lax.broadcasted_iota(jnp.int32, (1, s_new), 1))
    accs, ls = [], []
    for c in range(2):
        qc = q[:, c * DIFF_QK:(c + 1) * DIFF_QK]
        sp = jnp.where(vis_past, _dot_nt(qc, kp[:, c * DIFF_QK:(c + 1) * DIFF_QK]) + b_past, NEG_INF)
        sn = jnp.where(vis_new, _dot_nt(qc, kn[:, c * DIFF_QK:(c + 1) * DIFF_QK]) + b_new, NEG_INF)
        m = jnp.maximum(jnp.max(sp, axis=-1, keepdims=True), jnp.max(sn, axis=-1, keepdims=True))
        pp = jnp.exp(sp - m)
        pn = jnp.exp(sn - m)
        ls.append(jnp.sum(pp, axis=-1, keepdims=True) + jnp.sum(pn, axis=-1, keepdims=True))
        accs.append(_dot(pp.astype(BF16), vp) + _dot(pn.astype(BF16), vn))
    o = _diff_finish(accs, ls, lam_ref[0], sub_ref[...], lam_init)
    o_ref[0] = o.astype(o_ref.dtype)


def _attn_cached(lam, q, kb, vb, past_k, past_v, subln, lam_init):
    b, h, s, _ = q.shape
    past = past_k.shape[2]
    new = pl.BlockSpec((1, 1, s, LANES), lambda i, j: (i, j, 0, 0))
    old = pl.BlockSpec((1, 1, past, LANES), lambda i, j: (i, j, 0, 0))
    return pl.pallas_call(
        functools.partial(_attn_cached_kernel, lam_init=lam_init),
        grid=(b, h),
        in_specs=[pl.BlockSpec(memory_space=pltpu.SMEM), new, new, new, old, old,
                  pl.BlockSpec((1, DIFF_V), lambda i, j: (0, 0))],
        out_specs=pl.BlockSpec((1, s, LANES), lambda i, j: (i, 0, j)),
        out_shape=jax.ShapeDtypeStruct((b, s, D_DIFF), BF16),
        compiler_params=_params(("arbitrary", "arbitrary")),
        name="attn_cached",
    )(lam, q, kb, vb, past_k, past_v, subln)


def _route(lg):
    tm = lg.shape[0]
    lane = lax.broadcasted_iota(jnp.int32, (tm, LANES), 1)
    lane_f = lane.astype(F32)
    big = 1e9
    low = -3e38
    is_g = (lane >= N_EXPERTS) & (lane < N_EXPERTS + N_GROUPS)
    gl = jnp.where(is_g, lg, low)
    gmax = jnp.max(gl, axis=-1, keepdims=True)
    grp_lane = jnp.min(jnp.where(gl == gmax, lane_f, big), axis=-1, keepdims=True)
    gsum = jnp.sum(jnp.where(is_g, jnp.exp(gl - gmax), 0.0), axis=-1, keepdims=True)
    g_prob = 1.0 / gsum
    grp = grp_lane - float(N_EXPERTS)
    lane_grp = (lane // EXPERTS_PER_GROUP).astype(F32)
    el = jnp.where(lane_grp == grp, lg, low)
    v1 = jnp.max(el, axis=-1, keepdims=True)
    i1 = jnp.min(jnp.where(el == v1, lane_f, big), axis=-1, keepdims=True)
    el2 = jnp.where(lane_f == i1, low, el)
    v2 = jnp.max(el2, axis=-1, keepdims=True)
    i2 = jnp.min(jnp.where(el2 == v2, lane_f, big), axis=-1, keepdims=True)
    e21 = jnp.exp(v2 - v1)
    den = 1.0 + e21
    gate1 = g_prob / den
    gate2 = g_prob * e21 / den
    out = jnp.where(lane == 0, i1, jnp.where(lane == 1, i2, jnp.where(lane == 2, gate1,
                    jnp.where(lane == 3, gate2, 0.0))))
    return out


def _post_kernel(x_ref, oa_ref, ob_ref, mk_ref, mv_ref, wout_ref, wq_ref, wo_ref,
                 g1_ref, b1_ref, g2_ref, b2_ref, wrh_ref, wrl_ref, br_ref, hx_ref, *, nsplit):
    tm = x_ref.shape[1]
    rows = tm // nsplit
    parts = [slice(i * rows, (i + 1) * rows) for i in range(nsplit)]
    mix = [_dot(oa_ref[0, r, :], wout_ref[:D_RWKV, :]) + _dot(ob_ref[0, r, :], wout_ref[D_RWKV:, :]) for r in parts]
    h1 = [_layer_norm(DEEPNORM_ALPHA * x_ref[0, r, :] + m, g1_ref[...], b1_ref[...]) for r, m in zip(parts, mix)]
    q = [(_dot(h.astype(BF16), wq_ref[...]) * (MEM_HEAD ** -0.5)).astype(BF16) for h in h1]
    heads = [[] for _ in parts]
    for h in range(MEM_HEADS):
        hs = slice(h * MEM_HEAD, (h + 1) * MEM_HEAD)
        for i in range(nsplit):
            s = _dot_nt(q[i][:, hs], mk_ref[0, :, hs])
            m = jnp.max(s, axis=-1, keepdims=True)
            p = jnp.exp(s - m)
            p = p / jnp.sum(p, axis=-1, keepdims=True)
            heads[i].append(_dot(p.astype(BF16), mv_ref[0, :, hs]).astype(BF16))
    att = [_dot(jnp.concatenate(hd, axis=-1), wo_ref[...]) for hd in heads]
    h2 = [_layer_norm(DEEPNORM_ALPHA * h + a, g2_ref[...], b2_ref[...]) for h, a in zip(h1, att)]
    for r, h in zip(parts, h2):
        hx_ref[0, r, :D_MODEL] = h
        hi, lo = _split2(h)
        lg = _dot(hi, wrh_ref[...]) + _dot(hi, wrl_ref[...]) + _dot(lo, wrh_ref[...]) + br_ref[...]
        hx_ref[0, r, D_MODEL:] = _route(lg)


def _post(x, oa, ob, mk, mv, prm, tm, nsplit):
    b, s, d = x.shape
    n = mk.shape[1]
    tok = lambda w: pl.BlockSpec((1, tm, w), lambda i, j: (i, j, 0))
    mem = pl.BlockSpec((1, n, d), lambda i, j: (i, 0, 0))
    mat = lambda m_, n_: pl.BlockSpec((m_, n_), lambda i, j: (0, 0))
    return pl.pallas_call(
        functools.partial(_post_kernel, nsplit=nsplit),
        grid=(b, s // tm),
        in_specs=[tok(d), tok(D_RWKV), tok(D_DIFF), mem, mem, mat(d, d), mat(d, d), mat(d, d),
                  mat(1, d), mat(1, d), mat(1, d), mat(1, d), mat(d, LANES), mat(d, LANES), mat(1, LANES)],
        out_specs=tok(d + LANES),
        out_shape=jax.ShapeDtypeStruct((b, s, d + LANES), F32),
        compiler_params=_params(("arbitrary", "arbitrary")),
        name="post",
    )(x, oa, ob, mk, mv, prm["w_out"], prm["wq"], prm["wo"], prm["ln1_g"], prm["ln1_b"],
      prm["ln2_g"], prm["ln2_b"], prm["wr_hi"], prm["wr_lo"], prm["br"])


def _moe_kernel(bexp_ref, nused_ref, tok_ref, tokn_ref, h2_hbm, w1_ref, w3_ref, w2_ref, y_ref, xbuf, sem):
    i = pl.program_id(0)
    nblk = pl.num_programs(0)
    nused = nused_ref[0]
    ngrp = xbuf.shape[1]
    tb = ngrp * SUBLANES
    slot = lax.rem(i, 2)

    def issue_rows(idx_ref, sl):
        def group(g, c):
            for u in range(SUBLANES):
                r = idx_ref[0, 0, g * SUBLANES + u]
                pltpu.make_async_copy(h2_hbm.at[pl.ds(r, 1)], xbuf.at[sl, g, pl.ds(u, 1)], sem.at[sl]).start()
            return c
        lax.fori_loop(0, ngrp, group, 0)

    def wait_rows(sl):
        pltpu.make_async_copy(xbuf.at[1 - sl], xbuf.at[sl], sem.at[sl]).wait()

    @pl.when(i == 0)
    def _():
        issue_rows(tok_ref, 0)

    @pl.when(i < nused)
    def _():
        wait_rows(slot)
        issue_rows(tokn_ref, 1 - slot)
        x = xbuf[slot].reshape(tb, xbuf.shape[3]).astype(BF16)
        a = _dot(x, w1_ref[0])
        g = _dot(x, w3_ref[0])
        hmid = (a / (1.0 + jnp.exp(-a))) * g
        y_ref[...] = _dot(hmid.astype(BF16), w2_ref[0])

        @pl.when(i == nblk - 1)
        def _():
            wait_rows(1 - slot)

    @pl.when(i >= nused)
    def _():
        y_ref[...] = jnp.zeros_like(y_ref)

        @pl.when(i == nused)
        def _():
            wait_rows(slot)


def _moe_ffn(h2f, blk_exp, nused, tok, w1, w3, w2, tb):
    nblk = tok.shape[0]
    d = h2f.shape[1]
    grid_spec = pltpu.PrefetchScalarGridSpec(
        num_scalar_prefetch=2,
        grid=(nblk,),
        in_specs=[pl.BlockSpec((1, 1, tb), lambda i, be, nu: (i, 0, 0), memory_space=pltpu.SMEM),
                  pl.BlockSpec((1, 1, tb), lambda i, be, nu: (jnp.minimum(i + 1, nblk - 1), 0, 0),
                               memory_space=pltpu.SMEM),
                  pl.BlockSpec(memory_space=pl.ANY),
                  pl.BlockSpec((1, d, D_EXPERT), lambda i, be, nu: (be[i], 0, 0)),
                  pl.BlockSpec((1, d, D_EXPERT), lambda i, be, nu: (be[i], 0, 0)),
                  pl.BlockSpec((1, D_EXPERT, d), lambda i, be, nu: (be[i], 0, 0))],
        out_specs=pl.BlockSpec((tb, d), lambda i, be, nu: (i, 0)),
        scratch_shapes=[pltpu.VMEM((2, tb // SUBLANES, SUBLANES, d), F32), pltpu.SemaphoreType.DMA((2,))],
    )
    return pl.pallas_call(
        _moe_kernel,
        grid_spec=grid_spec,
        out_shape=jax.ShapeDtypeStruct((nblk * tb, d), F32),
        compiler_params=_params(("arbitrary",)),
        name="moe_ffn",
    )(blk_exp, nused, tok, tok, h2f, w1, w3, w2)


def _combine_kernel(pos_ref, posn_ref, ys_hbm, h2_ref, route_ref, g_ref, b_ref, y_ref, gbuf, sem):
    i = pl.program_id(0)
    n = pl.num_programs(0)
    tm = h2_ref.shape[0]
    slot = lax.rem(i, 2)

    ngrp = gbuf.shape[1]
    d = gbuf.shape[3]

    def issue_rows(idx_ref, sl):
        def group(g, c):
            for u in range(SUBLANES):
                r = idx_ref[0, 0, g * SUBLANES + u]
                pltpu.make_async_copy(ys_hbm.at[pl.ds(r, 1)], gbuf.at[sl, g, pl.ds(u, 1)], sem.at[sl]).start()
            return c
        lax.fori_loop(0, ngrp, group, 0)

    def wait_rows(sl):
        pltpu.make_async_copy(gbuf.at[1 - sl], gbuf.at[sl], sem.at[sl]).wait()

    @pl.when(i == 0)
    def _():
        issue_rows(pos_ref, 0)

    wait_rows(slot)
    issue_rows(posn_ref, 1 - slot)
    rt = route_ref[...]
    half = ngrp // 2
    moe = (rt[:, 2:3] * gbuf[slot, :half].reshape(tm, d) + rt[:, 3:4] * gbuf[slot, half:].reshape(tm, d))
    y_ref[...] = _layer_norm(DEEPNORM_ALPHA * h2_ref[...] + moe, g_ref[...], b_ref[...])

    @pl.when(i == n - 1)
    def _():
        wait_rows(1 - slot)


def _combine(pos, ysort, h2f, route, g, b, tm):
    t, d = h2f.shape
    n = t // tm
    return pl.pallas_call(
        _combine_kernel,
        grid=(n,),
        in_specs=[pl.BlockSpec((1, 1, 2 * tm), lambda i: (i, 0, 0), memory_space=pltpu.SMEM),
                  pl.BlockSpec((1, 1, 2 * tm), lambda i: (jnp.minimum(i + 1, n - 1), 0, 0),
                               memory_space=pltpu.SMEM),
                  pl.BlockSpec(memory_space=pl.ANY),
                  pl.BlockSpec((tm, d), lambda i: (i, 0)),
                  pl.BlockSpec((tm, LANES), lambda i: (i, 0)),
                  pl.BlockSpec((1, d), lambda i: (0, 0)),
                  pl.BlockSpec((1, d), lambda i: (0, 0))],
        out_specs=pl.BlockSpec((tm, d), lambda i: (i, 0)),
        out_shape=jax.ShapeDtypeStruct((t, d), F32),
        scratch_shapes=[pltpu.VMEM((2, 2 * tm // SUBLANES, SUBLANES, d), F32), pltpu.SemaphoreType.DMA((2,))],
        compiler_params=_params(("arbitrary",)),
        name="combine",
    )(pos, pos, ysort, h2f, route, g, b)


def _dispatch(expert, tb, tm):
    t = expert.shape[0]
    a = 2 * t
    ef = expert.reshape(a)
    order = jnp.argsort(ef, stable=True).astype(jnp.int32)
    es = ef[order]
    counts = jnp.sum((ef[:, None] == jnp.arange(N_EXPERTS, dtype=jnp.int32)[None, :]).astype(jnp.int32), axis=0)
    starts = jnp.cumsum(counts) - counts
    nb = (counts + tb - 1) // tb
    bend = jnp.cumsum(nb)
    bstart = bend - nb
    nblk = -(-a // tb) + N_EXPERTS
    blk = jnp.arange(nblk, dtype=jnp.int32)
    bexp = jnp.minimum(jnp.sum((blk[:, None] >= bend[None, :]).astype(jnp.int32), axis=1), N_EXPERTS - 1)
    row0 = starts[bexp] + (blk - bstart[bexp]) * tb
    idx = jnp.clip(row0[:, None] + jnp.arange(tb, dtype=jnp.int32)[None, :], 0, a - 1)
    tok = (order[idx] // 2).astype(jnp.int32).reshape(nblk, 1, tb)
    dest_sorted = bstart[es] * tb + (jnp.arange(a, dtype=jnp.int32) - starts[es])
    _, pos = lax.sort_key_val(order, dest_sorted.astype(jnp.int32))
    pos = pos.reshape(t, 2)
    pos = pos.reshape(t // tm, tm, 2).transpose(0, 2, 1).reshape(t // tm, 1, 2 * tm)
    return bexp, bend[-1:].astype(jnp.int32), tok, pos


def _moe(hx, prm, tb, tm):
    b, s, _ = hx.shape
    d = D_MODEL
    t = b * s
    h2f = hx[..., :d].reshape(t, d)
    rf = hx[..., d:].reshape(t, LANES)
    expert = rf[:, :2].astype(jnp.int32)
    bexp, nused, tok, pos = _dispatch(expert, tb, tm)
    ysort = _moe_ffn(h2f, bexp, nused, tok, prm["moe_w1"], prm["moe_w3"], prm["moe_w2"], tb)
    y = _combine(pos, ysort, h2f, rf, prm["ln3_g"], prm["ln3_b"], tm)
    return y.reshape(b, s, d)


N_PAIR_CLASSES = N_GROUPS * EXPERTS_PER_GROUP * EXPERTS_PER_GROUP
N_REAL_PAIRS = N_GROUPS * (EXPERTS_PER_GROUP * (EXPERTS_PER_GROUP - 1) // 2)


def _moe_pair_kernel(elo_ref, ehi_ref, nused_ref, tok_ref, tokn_ref, h2_hbm,
                     w1a_ref, w3a_ref, w2a_ref, w1b_ref, w3b_ref, w2b_ref, y_ref, xbuf, sem):
    i = pl.program_id(0)
    nblk = pl.num_programs(0)
    nused = nused_ref[0]
    ngrp = xbuf.shape[1]
    tb = ngrp * SUBLANES
    slot = lax.rem(i, 2)

    def issue_rows(idx_ref, sl):
        def group(g, c):
            for u in range(SUBLANES):
                r = idx_ref[0, 0, g * SUBLANES + u]
                pltpu.make_async_copy(h2_hbm.at[pl.ds(r, 1)], xbuf.at[sl, g, pl.ds(u, 1)], sem.at[sl]).start()
            return c
        lax.fori_loop(0, ngrp, group, 0)

    def wait_rows(sl):
        pltpu.make_async_copy(xbuf.at[1 - sl], xbuf.at[sl], sem.at[sl]).wait()

    def ffn(x, w1_ref, w3_ref, w2_ref):
        a = _dot(x, w1_ref[0])
        g = _dot(x, w3_ref[0])
        hmid = (a / (1.0 + jnp.exp(-a))) * g
        return _dot(hmid.astype(BF16), w2_ref[0])

    @pl.when(i == 0)
    def _():
        issue_rows(tok_ref, 0)

    @pl.when(i < nused)
    def _():
        wait_rows(slot)
        issue_rows(tokn_ref, 1 - slot)
        rows = xbuf[slot].reshape(tb, xbuf.shape[3])
        x = rows[:, :D_MODEL].astype(BF16)
        rt = rows[:, D_MODEL:]
        first_low = rt[:, 0:1] < rt[:, 1:2]
        g_lo = jnp.where(first_low, rt[:, 2:3], rt[:, 3:4])
        g_hi = jnp.where(first_low, rt[:, 3:4], rt[:, 2:3])
        y_ref[...] = g_lo * ffn(x, w1a_ref, w3a_ref, w2a_ref) + g_hi * ffn(x, w1b_ref, w3b_ref, w2b_ref)

        @pl.when(i == nblk - 1)
        def _():
            wait_rows(1 - slot)

    @pl.when(i >= nused)
    def _():
        y_ref[...] = jnp.zeros_like(y_ref)

        @pl.when(i == nused)
        def _():
            wait_rows(slot)


def _moe_pair_ffn(hxf, elo, ehi, nused, tok, w1, w3, w2, tb):
    nblk = tok.shape[0]
    d = D_MODEL
    wspec = lambda rows, cols, which: pl.BlockSpec(
        (1, rows, cols), (lambda i, lo, hi, nu: (lo[i], 0, 0)) if which == 0 else (lambda i, lo, hi, nu: (hi[i], 0, 0)))
    grid_spec = pltpu.PrefetchScalarGridSpec(
        num_scalar_prefetch=3,
        grid=(nblk,),
        in_specs=[pl.BlockSpec((1, 1, tb), lambda i, lo, hi, nu: (i, 0, 0), memory_space=pltpu.SMEM),
                  pl.BlockSpec((1, 1, tb), lambda i, lo, hi, nu: (jnp.minimum(i + 1, nblk - 1), 0, 0),
                               memory_space=pltpu.SMEM),
                  pl.BlockSpec(memory_space=pl.ANY),
                  wspec(d, D_EXPERT, 0), wspec(d, D_EXPERT, 0), wspec(D_EXPERT, d, 0),
                  wspec(d, D_EXPERT, 1), wspec(d, D_EXPERT, 1), wspec(D_EXPERT, d, 1)],
        out_specs=pl.BlockSpec((tb, d), lambda i, lo, hi, nu: (i, 0)),
        scratch_shapes=[pltpu.VMEM((2, tb // SUBLANES, SUBLANES, hxf.shape[1]), F32),
                        pltpu.SemaphoreType.DMA((2,))],
    )
    return pl.pallas_call(
        _moe_pair_kernel,
        grid_spec=grid_spec,
        out_shape=jax.ShapeDtypeStruct((nblk * tb, d), F32),
        compiler_params=_params(("arbitrary",)),
        name="moe_pair_ffn",
    )(elo, ehi, nused, tok, tok, hxf, w1, w3, w2, w1, w3, w2)


def _combine1_kernel(pos_ref, posn_ref, ys_hbm, hx_ref, g_ref, b_ref, y_ref, gbuf, sem):
    i = pl.program_id(0)
    n = pl.num_programs(0)
    tm = hx_ref.shape[0]
    slot = lax.rem(i, 2)
    ngrp = gbuf.shape[1]

    def issue_rows(idx_ref, sl):
        def group(g, c):
            for u in range(SUBLANES):
                r = idx_ref[0, 0, g * SUBLANES + u]
                pltpu.make_async_copy(ys_hbm.at[pl.ds(r, 1)], gbuf.at[sl, g, pl.ds(u, 1)], sem.at[sl]).start()
            return c
        lax.fori_loop(0, ngrp, group, 0)

    def wait_rows(sl):
        pltpu.make_async_copy(gbuf.at[1 - sl], gbuf.at[sl], sem.at[sl]).wait()

    @pl.when(i == 0)
    def _():
        issue_rows(pos_ref, 0)

    wait_rows(slot)
    issue_rows(posn_ref, 1 - slot)
    moe = gbuf[slot].reshape(tm, gbuf.shape[3])
    y_ref[...] = _layer_norm(DEEPNORM_ALPHA * hx_ref[:, :D_MODEL] + moe, g_ref[...], b_ref[...])

    @pl.when(i == n - 1)
    def _():
        wait_rows(1 - slot)


def _combine1(pos, ysort, hxf, g, b, tm):
    t, dx = hxf.shape
    d = D_MODEL
    n = t // tm
    return pl.pallas_call(
        _combine1_kernel,
        grid=(n,),
        in_specs=[pl.BlockSpec((1, 1, tm), lambda i: (i, 0, 0), memory_space=pltpu.SMEM),
                  pl.BlockSpec((1, 1, tm), lambda i: (jnp.minimum(i + 1, n - 1), 0, 0), memory_space=pltpu.SMEM),
                  pl.BlockSpec(memory_space=pl.ANY),
                  pl.BlockSpec((tm, dx), lambda i: (i, 0)),
                  pl.BlockSpec((1, d), lambda i: (0, 0)),
                  pl.BlockSpec((1, d), lambda i: (0, 0))],
        out_specs=pl.BlockSpec((tm, d), lambda i: (i, 0)),
        out_shape=jax.ShapeDtypeStruct((t, d), F32),
        scratch_shapes=[pltpu.VMEM((2, tm // SUBLANES, SUBLANES, d), F32), pltpu.SemaphoreType.DMA((2,))],
        compiler_params=_params(("arbitrary",)),
        name="combine1",
    )(pos, pos, ysort, hxf, g, b)


def _dispatch_pairs(e1, e2, tb, tm):
    t = e1.shape[0]
    lo = jnp.minimum(e1, e2)
    hi = jnp.maximum(e1, e2)
    epg = EXPERTS_PER_GROUP
    cls = (lo // epg) * (epg * epg) + (lo % epg) * epg + (hi % epg)
    cs, order = lax.sort((cls, jnp.arange(t, dtype=jnp.int32)), num_keys=1, is_stable=True)
    counts = jnp.sum((cls[:, None] == jnp.arange(N_PAIR_CLASSES, dtype=jnp.int32)[None, :]).astype(jnp.int32), axis=0)
    starts = jnp.cumsum(counts) - counts
    nb = (counts + tb - 1) // tb
    bend = jnp.cumsum(nb)
    bstart = bend - nb
    nblk = -(-t // tb) + N_REAL_PAIRS
    blk = jnp.arange(nblk, dtype=jnp.int32)
    bcls = jnp.minimum(jnp.sum((blk[:, None] >= bend[None, :]).astype(jnp.int32), axis=1), N_PAIR_CLASSES - 1)
    grp = bcls // (epg * epg)
    elo = (grp * epg + (bcls % (epg * epg)) // epg).astype(jnp.int32)
    ehi = (grp * epg + bcls % epg).astype(jnp.int32)
    row0 = starts[bcls] + (blk - bstart[bcls]) * tb
    idx = jnp.clip(row0[:, None] + jnp.arange(tb, dtype=jnp.int32)[None, :], 0, t - 1)
    tok = order[idx].astype(jnp.int32)
    dest_sorted = bstart[cs] * tb + (jnp.arange(t, dtype=jnp.int32) - starts[cs])
    _, pos = lax.sort_key_val(order, dest_sorted.astype(jnp.int32))
    return elo, ehi, bend[-1:].astype(jnp.int32), tok.reshape(nblk, 1, tb), pos.reshape(t // tm, 1, tm)


def _moe_pairs(hx, prm, tb, tm):
    b, s, dx = hx.shape
    t = b * s
    hxf = hx.reshape(t, dx)
    experts = hxf[:, D_MODEL:D_MODEL + 2].astype(jnp.int32)
    elo, ehi, nused, tok, pos = _dispatch_pairs(experts[:, 0], experts[:, 1], tb, tm)
    ysort = _moe_pair_ffn(hxf, elo, ehi, nused, tok, prm["moe_w1"], prm["moe_w3"], prm["moe_w2"], tb)
    y = _combine1(pos, ysort, hxf, prm["ln3_g"], prm["ln3_b"], tm)
    return y.reshape(b, s, D_MODEL)


def _trunk(x, shift_prev, state0, past_k, past_v, mk, mv, prm, lam, lam_init, cfg):
    b, s, _ = x.shape
    pa, q, k, v, kb, vb, kb1 = _inproj(x, prm["w_in"], cfg["in_bb"], cfg["in_ts"],
                                       cfg["attn_tq"] if past_k is None else 0)
    oa, sfin = _rwkv(pa, shift_prev, _state_to_pairs(state0), prm, cfg["rwkv_tb"], cfg["rwkv_chunk"],
                     cfg["rwkv_group"])
    if past_k is None:
        ob = _attn_prompt(lam, q, kb, kb1, vb, prm["subln"].reshape(DIFF_V, 1), cfg["attn_tq"], lam_init)
    else:
        ob = _attn_cached(lam, q, kb, vb, past_k, past_v, prm["subln"], lam_init)
    hx = _post(x, oa, ob, mk, mv, prm, cfg["post_tm"], cfg["post_split"])
    moe = _moe_pairs if cfg["moe_pairs"] else _moe
    y = moe(hx, prm, cfg["moe_tb"], cfg["comb_tm"])
    return y, k, v, _pairs_to_state(sfin), pa[:, s - 1:s, :]


def _prep(w, l, lam_scalar_inputs):
    row = lambda a: a[l].reshape(1, -1).astype(F32)
    idx = jnp.arange(D_RWKV, dtype=jnp.int32) // RWKV_HEAD
    wr = jnp.zeros((D_MODEL, LANES), F32)
    wr = wr.at[:, :N_EXPERTS].set(w["moe_w_expert"][l]).at[:, N_EXPERTS:N_EXPERTS + N_GROUPS].set(w["moe_w_group"][l])
    wr_hi = wr.astype(BF16)
    br = jnp.zeros((1, LANES), F32)
    br = br.at[0, :N_EXPERTS].set(w["moe_b_expert"][l]).at[0, N_EXPERTS:N_EXPERTS + N_GROUPS].set(w["moe_b_group"][l])
    return {
        "w_in": w["w_in"][l].astype(BF16),
        "mu": row(w["rwkv_mu"]), "w0": row(w["rwkv_w0"]), "w2": w["rwkv_w2"][l].astype(BF16),
        "a0": row(w["rwkv_a0"]), "a2": w["rwkv_a2"][l].astype(BF16), "g2": w["rwkv_g2"][l].astype(BF16),
        "k_k": row(w["rwkv_k_k"]), "k_a": row(w["rwkv_k_a"]), "r_k": row(w["rwkv_r_k"]),
        "ln_w": row(w["rwkv_ln_w"]), "ln_b": row(w["rwkv_ln_b"]),
        "hsum": (idx[:, None] == idx[None, :]).astype(BF16),
        "subln": row(w["diff_subln"]),
        "w_out": w["w_out"][l].astype(BF16), "wq": w["mem_wq"][l].astype(BF16), "wo": w["mem_wo"][l].astype(BF16),
        "ln1_g": row(w["ln1_g"]), "ln1_b": row(w["ln1_b"]), "ln2_g": row(w["ln2_g"]), "ln2_b": row(w["ln2_b"]),
        "ln3_g": row(w["ln3_g"]), "ln3_b": row(w["ln3_b"]),
        "wr_hi": wr_hi, "wr_lo": (wr - wr_hi.astype(F32)).astype(BF16), "br": br,
        "moe_w1": w["moe_w1"][l].astype(BF16), "moe_w3": w["moe_w3"][l].astype(BF16),
        "moe_w2": w["moe_w2"][l].astype(BF16),
    }


def _tile(n, pref):
    return pref if n % pref == 0 else n


def kernel(x_prompt, x_sample, mem_prompt, cache_diff_k, cache_diff_v, cache_mem_k, cache_mem_v, state_rwkv, state_shift, w_in, rwkv_mu, rwkv_w0, rwkv_w2, rwkv_a0, rwkv_a2, rwkv_g2, rwkv_k_k, rwkv_k_a, rwkv_r_k, rwkv_ln_w, rwkv_ln_b, diff_lq1, diff_lk1, diff_lq2, diff_lk2, diff_subln, w_out, ln1_g, ln1_b, mem_wq, mem_wk, mem_wv, mem_wo, ln2_g, ln2_b, moe_w_group, moe_b_group, moe_w_expert, moe_b_expert, moe_w1, moe_w3, moe_w2, ln3_g, ln3_b):
    w = dict(w_in=w_in, rwkv_mu=rwkv_mu, rwkv_w0=rwkv_w0, rwkv_w2=rwkv_w2, rwkv_a0=rwkv_a0, rwkv_a2=rwkv_a2,
             rwkv_g2=rwkv_g2, rwkv_k_k=rwkv_k_k, rwkv_k_a=rwkv_k_a, rwkv_r_k=rwkv_r_k, rwkv_ln_w=rwkv_ln_w,
             rwkv_ln_b=rwkv_ln_b, diff_subln=diff_subln, w_out=w_out, ln1_g=ln1_g, ln1_b=ln1_b, mem_wq=mem_wq,
             mem_wo=mem_wo, ln2_g=ln2_g, ln2_b=ln2_b, moe_w_group=moe_w_group, moe_b_group=moe_b_group,
             moe_w_expert=moe_w_expert, moe_b_expert=moe_b_expert, moe_w1=moe_w1, moe_w3=moe_w3, moe_w2=moe_w2,
             ln3_g=ln3_g, ln3_b=ln3_b)
    bp, sp, _ = x_prompt.shape
    bs, ss, _ = x_sample.shape
    depth = w_in.shape[0]
    cfg_p = dict(in_bb=1, in_ts=_tile(sp, 512), rwkv_tb=_tile(sp, 256), rwkv_chunk=CHUNK, rwkv_group=_tile(sp, 256) // CHUNK,
                 attn_tq=_tile(sp, 512), post_tm=_tile(sp, 512), post_split=2, moe_tb=256, moe_pairs=True, comb_tm=_tile(bp * sp, 256))
    cfg_s = dict(in_bb=bs, in_ts=ss, rwkv_tb=ss, rwkv_chunk=ss, rwkv_group=1, attn_tq=ss, post_tm=ss, post_split=1, moe_tb=64, moe_pairs=False,
                 comb_tm=_tile(bs * ss, 256))
    yp, ys = x_prompt, x_sample
    outs = [[] for _ in range(10)]
    for l in range(depth):
        prm = _prep(w, l, None)
        lam_init = 0.8 - 0.6 * math.exp(-0.3 * l)
        f = lambda z: z[l].astype(F32)
        lam = (jnp.exp(jnp.sum(f(diff_lq1) * f(diff_lk1))) - jnp.exp(jnp.sum(f(diff_lq2) * f(diff_lk2)))
               + lam_init).reshape(1)
        slopes = jnp.asarray(ALIBI_SLOPES, F32)
        lam = jnp.concatenate([lam, slopes, jnp.zeros((3,), F32)])
        mk_p, mv_p, mkb, mvb = _memproj(mem_prompt, mem_wk[l].astype(BF16), mem_wv[l].astype(BF16))
        yp, k_p, v_p, st_p, sh_p = _trunk(
            yp, jnp.zeros((bp, 1, C_RWKV), F32), jnp.zeros((bp, 2 * RWKV_PAIRS, RWKV_HEAD, RWKV_HEAD), F32),
            None, None, mkb, mvb, prm, lam, lam_init, cfg_p)
        ys, k_s, v_s, st_s, sh_s = _trunk(
            ys, state_shift[l], state_rwkv[l], cache_diff_k[l], cache_diff_v[l],
            cache_mem_k[l].astype(BF16), cache_mem_v[l].astype(BF16), prm, lam, lam_init, cfg_s)
        for lst, val in zip(outs, (k_p, v_p, mk_p, mv_p, st_p, sh_p, k_s, v_s, st_s, sh_s)):
            lst.append(val)
    return (yp, ys) + tuple(jnp.stack(o) for o in outs)
```

```python
import functools
import math

import jax
import jax.numpy as jnp
from jax import lax
from jax.experimental import pallas as pl
from jax.experimental.pallas import tpu as pltpu

F32 = jnp.float32
BF16 = jnp.bfloat16

D_MODEL = 1024
CHUNK = 64
D_RWKV = 512
RWKV_HEAD = 64
RWKV_PAIRS = D_RWKV // (2 * RWKV_HEAD)
W_LORA = 64
A_LORA = 64
G_LORA = 128
C_RWKV = 3 * D_RWKV + W_LORA + A_LORA + G_LORA
D_DIFF = 512
DIFF_HEADS = 4
DIFF_V = 128
DIFF_QK = 64
C_IN = C_RWKV + 3 * D_DIFF
MEM_HEADS = 4
MEM_HEAD = D_MODEL // MEM_HEADS
N_GROUPS = 4
EXPERTS_PER_GROUP = 8
N_EXPERTS = 32
D_EXPERT = 512
DEPTH = 1
DEEPNORM_ALPHA = (2.0 * DEPTH) ** 0.25
LN_EPS = 1e-5
RMS_EPS = 1e-5
RWKV_GN_EPS = 64e-5
NEG_INF = -1e30
LOG2E = math.log2(math.e)
ALIBI_SLOPES = tuple(2.0 ** (-8.0 * (h + 1) / DIFF_HEADS) for h in range(DIFF_HEADS))
VT_PAD = 16
LANES = 128
SUBLANES = 8
VMEM_LIMIT = 48 * 1024 * 1024


def _params(sem):
    return pltpu.CompilerParams(dimension_semantics=sem, vmem_limit_bytes=VMEM_LIMIT)


def _dot(a, b):
    return jnp.dot(a, b, preferred_element_type=F32)


def _dot_nt(a, b):
    return lax.dot_general(a, b, (((1,), (1,)), ((), ())), preferred_element_type=F32)


def _dot_tn(a, b):
    return lax.dot_general(a, b, (((0,), (0,)), ((), ())), preferred_element_type=F32)


def _split2(x):
    hi = x.astype(BF16)
    lo = (x - hi.astype(F32)).astype(BF16)
    return hi, lo


def _split3(x):
    hi = x.astype(BF16)
    r = x - hi.astype(F32)
    mid = r.astype(BF16)
    lo = (r - mid.astype(F32)).astype(BF16)
    return hi, mid, lo


def _layer_norm(x, g, b):
    mu = jnp.mean(x, axis=-1, keepdims=True)
    xc = x - mu
    var = jnp.mean(xc * xc, axis=-1, keepdims=True)
    return xc * lax.rsqrt(var + LN_EPS) * g + b


def _memproj_kernel(m_ref, wk_ref, wv_ref, k_ref, v_ref, kb_ref, vb_ref):
    m = m_ref[0].astype(BF16)
    k = _dot(m, wk_ref[...])
    v = _dot(m, wv_ref[...])
    k_ref[0] = k
    v_ref[0] = v
    kb_ref[0] = k.astype(BF16)
    vb_ref[0] = v.astype(BF16)


def _memproj(mem, wk, wv):
    b, n, d = mem.shape
    blk = pl.BlockSpec((1, n, d), lambda i: (i, 0, 0))
    wspec = pl.BlockSpec((d, d), lambda i: (0, 0))
    return pl.pallas_call(
        _memproj_kernel,
        grid=(b,),
        in_specs=[blk, wspec, wspec],
        out_specs=[blk, blk, blk, blk],
        out_shape=[jax.ShapeDtypeStruct((b, n, d), F32), jax.ShapeDtypeStruct((b, n, d), F32),
                   jax.ShapeDtypeStruct((b, n, d), BF16), jax.ShapeDtypeStruct((b, n, d), BF16)],
        compiler_params=_params(("arbitrary",)),
        name="memproj",
    )(mem, wk, wv)


def _inproj_kernel(x_ref, w_ref, pa_ref, q_ref, k_ref, v_ref, kb_ref, vb_ref, kb1_ref, *, attn_tile):
    bb, ts, d = x_ref.shape
    rows = bb * ts
    x = x_ref[...].reshape(rows, d).astype(BF16)
    pq = _dot(x, w_ref[:, C_RWKV:C_RWKV + D_DIFF])
    pk = _dot(x, w_ref[:, C_RWKV + D_DIFF:C_RWKV + 2 * D_DIFF])
    pv = _dot(x, w_ref[:, C_RWKV + 2 * D_DIFF:])
    pa_ref[...] = _dot(x, w_ref[:, :C_RWKV]).reshape(bb, ts, C_RWKV)
    if attn_tile:
        lane = lax.broadcasted_iota(jnp.int32, (rows, LANES), 1)
        kloc = lax.rem(pl.program_id(1) * ts + lax.broadcasted_iota(jnp.int32, (rows, LANES), 0), attn_tile)
        aug = jnp.where(lax.broadcasted_iota(jnp.int32, (VT_PAD, rows), 0) == 0, 1.0, 0.0).astype(BF16)
    for h in range(DIFF_HEADS):
        hs = slice(h * LANES, (h + 1) * LANES)
        qh = pq[:, hs]
        kh = pk[:, hs]
        k_ref[:, h] = kh.reshape(bb, ts, LANES)
        vh = pv[:, hs]
        v_ref[:, h] = vh.reshape(bb, ts, LANES)
        if attn_tile:
            q_ref[0, h] = (qh * (DIFF_QK ** -0.5 * LOG2E)).T.astype(BF16)
            vb_ref[0, h] = jnp.concatenate([vh.T.astype(BF16), aug], axis=0)
            bias = kloc.astype(F32) * (ALIBI_SLOPES[h] * LOG2E)
            b_hi = bias.astype(BF16).astype(F32)
            b_mid = (bias - b_hi).astype(BF16).astype(F32)
            b_lo = bias - b_hi - b_mid
            cols = jnp.where(lane == DIFF_QK, b_hi, jnp.where(lane == DIFF_QK + 1, b_mid,
                             jnp.where(lane == DIFF_QK + 2, b_lo, 0.0)))
            kb_ref[0, h] = jnp.where(lane < DIFF_QK, kh, cols).astype(BF16)
            kb1_ref[0, h] = jnp.where(lane < DIFF_QK, pltpu.roll(kh, DIFF_QK, axis=1), cols).astype(BF16)
        else:
            q_ref[:, h] = (qh * (DIFF_QK ** -0.5)).reshape(bb, ts, LANES).astype(BF16)
            vb_ref[:, h] = vh.reshape(bb, ts, LANES).astype(BF16)
            kb_ref[:, h] = kh.reshape(bb, ts, LANES).astype(BF16)
            kb1_ref[:, h] = kh.reshape(bb, ts, LANES).astype(BF16)


def _inproj(x, w_in_bf, bb, ts, attn_tile):
    b, s, d = x.shape
    hm = pl.BlockSpec((bb, DIFF_HEADS, ts, LANES), lambda i, j: (i, 0, j, 0))
    hshape = (b, DIFF_HEADS, s, LANES)
    if attn_tile:
        assert bb == 1
        qm = pl.BlockSpec((1, DIFF_HEADS, LANES, ts), lambda i, j: (i, 0, 0, j))
        qshape = (b, DIFF_HEADS, LANES, s)
        vm = pl.BlockSpec((1, DIFF_HEADS, LANES + VT_PAD, ts), lambda i, j: (i, 0, 0, j))
        vshape = (b, DIFF_HEADS, LANES + VT_PAD, s)
    else:
        qm, qshape, vm, vshape = hm, hshape, hm, hshape
    return pl.pallas_call(
        functools.partial(_inproj_kernel, attn_tile=attn_tile),
        grid=(b // bb, s // ts),
        in_specs=[pl.BlockSpec((bb, ts, d), lambda i, j: (i, j, 0)),
                  pl.BlockSpec((d, C_IN), lambda i, j: (0, 0))],
        out_specs=[pl.BlockSpec((bb, ts, C_RWKV), lambda i, j: (i, j, 0)), qm, hm, hm, hm, vm, hm],
        out_shape=[jax.ShapeDtypeStruct((b, s, C_RWKV), F32),
                   jax.ShapeDtypeStruct(qshape, BF16),
                   jax.ShapeDtypeStruct(hshape, F32), jax.ShapeDtypeStruct(hshape, F32),
                   jax.ShapeDtypeStruct(hshape, BF16), jax.ShapeDtypeStruct(vshape, BF16),
                   jax.ShapeDtypeStruct(hshape, BF16)],
        compiler_params=_params(("arbitrary", "arbitrary")),
        name="inproj",
    )(x, w_in_bf)


def _rwkv_kernel(pa_ref, shift_ref, s0_ref, mu_ref, w0_ref, w2_ref, a0_ref, a2_ref, g2_ref,
                 kk_ref, ka_ref, rk_ref, lnw_ref, lnb_ref, hsum_ref,
                 o_ref, sfin_ref,
                 carry_ref, al_s, be_s, kb_s, rb_s, v_s, gam_s, o_s, bonus_s, g_s, w1_s, n1_s, c0_s, n2_s,
                 *, chunk, group):
    L = chunk
    tb = pa_ref.shape[1]
    nchunk = tb // L
    t = pl.program_id(1)

    @pl.when(t == 0)
    def _():
        sfin_ref[...] = s0_ref[...]
        carry_ref[...] = shift_ref[0]

    p = pa_ref[0]
    prev = pltpu.roll(p, 1, axis=0)
    row = lax.broadcasted_iota(jnp.int32, (tb, 1), 0)
    prev = jnp.where(row == 0, carry_ref[...], prev)
    carry_ref[...] = p[tb - 1:tb, :]
    ps = p + (prev - p) * mu_ref[...]

    r = ps[:, :D_RWKV]
    k = ps[:, D_RWKV:2 * D_RWKV]
    v = ps[:, 2 * D_RWKV:3 * D_RWKV]
    o0 = 3 * D_RWKV
    w_lo = ps[:, o0:o0 + W_LORA]
    a_lo = ps[:, o0 + W_LORA:o0 + W_LORA + A_LORA]
    g_lo = ps[:, o0 + W_LORA + A_LORA:]

    hsum = hsum_ref[...]

    def headsum(x):
        return _dot(x.astype(BF16), hsum)

    z = w0_ref[...] + _dot(jnp.tanh(w_lo).astype(BF16), w2_ref[...])
    lw = -math.exp(-0.5) / (1.0 + jnp.exp(-z))
    a = 1.0 / (1.0 + jnp.exp(-(a0_ref[...] + _dot(a_lo.astype(BF16), a2_ref[...]))))
    g_s[...] = _dot((1.0 / (1.0 + jnp.exp(-g_lo))).astype(BF16), g2_ref[...])
    kk = k * kk_ref[...]
    kk = kk / jnp.maximum(jnp.sqrt(headsum(kk * kk)), 1e-12)
    kmod = k * (1.0 + (a - 1.0) * ka_ref[...])
    bonus_s[...] = headsum(r * kmod * rk_ref[...]) * v

    bi = lax.broadcasted_iota(jnp.int32, (tb, tb), 0)
    bj = lax.broadcasted_iota(jnp.int32, (tb, tb), 1)
    tri = (((bi // L) == (bj // L)) & (bi >= bj)).astype(BF16)
    h3 = _split3(lw)
    cum = _dot(tri, h3[0]) + _dot(tri, h3[1]) + _dot(tri, h3[2])
    gam = jnp.exp(cum)
    igam = jnp.exp(-cum)
    gam_s[...] = gam
    al_s[...] = kk * jnp.exp(cum - lw)
    be_s[...] = kk * a * igam
    kb_s[...] = kmod * igam
    rb_s[...] = r * gam
    v_s[...] = v

    l2 = 2 * L
    ri = lax.broadcasted_iota(jnp.int32, (l2, l2), 0)
    ci = lax.broadcasted_iota(jnp.int32, (l2, l2), 1)
    same = (ri // L) == (ci // L)
    strict = same & (ri > ci)
    incl = same & (ri >= ci)
    eye = (ri == ci).astype(F32)
    lane = lax.broadcasted_iota(jnp.int32, (1, LANES), 1)
    first = lane < RWKV_HEAD

    def stack(x):
        return jnp.concatenate([jnp.where(first, x, 0.0), jnp.where(first, 0.0, x)], axis=0)

    def phase_a(cg, carry):
        cs = []
        for gi in range(group):
            c = cg * group + gi
            sl = pl.ds(pl.multiple_of(c * L, L), L)
            gl = gam_s[pl.ds(c * L + (L - 1), 1), :]
            for j in range(RWKV_PAIRS):
                ls = slice(j * LANES, (j + 1) * LANES)
                cs.append(dict(c=c, j=j, gl=gl[:, ls], al2=stack(al_s[sl, ls]), rb2=stack(rb_s[sl, ls]),
                               be2=stack(be_s[sl, ls]), kb2=stack(kb_s[sl, ls]), v2=stack(v_s[sl, ls])))
        for d in cs:
            x = jnp.concatenate([d["al2"], d["rb2"]], axis=0).astype(BF16)
            y = jnp.concatenate([d["be2"], d["kb2"]], axis=0).astype(BF16)
            gmat = _dot_nt(x, y)
            ab = jnp.where(strict, gmat[:l2, :l2], 0.0)
            d["ak"] = jnp.where(strict, gmat[:l2, l2:], 0.0).astype(BF16)
            d["rbm"] = jnp.where(incl, gmat[l2:, :l2], 0.0).astype(BF16)
            d["rkm"] = jnp.where(incl, gmat[l2:, l2:], 0.0).astype(BF16)
            d["pw"] = ab
            d["tm"] = eye - ab
        n = 2
        while n < L:
            for d in cs:
                pb = d["pw"].astype(BF16)
                d["pw"] = _dot(pb, pb)
            for d in cs:
                d["tm"] = _dot(d["tm"].astype(BF16), (eye + d["pw"]).astype(BF16))
            n *= 2
        for d in cs:
            d["akv"] = _dot(d["ak"], d["v2"].astype(BF16))
        for d in cs:
            zz = jnp.concatenate([d["al2"], d["akv"]], axis=1).astype(BF16)
            d["m"] = _dot(d["tm"].astype(BF16), zz)
        for d in cs:
            rm = _dot(d["rbm"], d["m"].astype(BF16))
            rkv = _dot(d["rkm"], d["v2"].astype(BF16))
            n1_s[d["c"], d["j"]] = (d["rb2"] - rm[:, :LANES]).astype(BF16)
            n2_s[d["c"], d["j"]] = rkv - rm[:, LANES:]
        for d in cs:
            m1 = d["m"][:, :LANES].astype(BF16)
            m2 = d["m"][:, LANES:]
            w1 = _dot_tn(m1, d["be2"].astype(BF16)) * d["gl"]
            lhs = jnp.concatenate([d["v2"], -m2], axis=0).astype(BF16)
            rhs = jnp.concatenate([d["kb2"], d["be2"]], axis=0).astype(BF16)
            w1_s[d["c"], d["j"]] = w1.astype(BF16)
            c0_s[d["c"], d["j"]] = _dot_tn(lhs, rhs) * d["gl"]
        return carry

    lax.fori_loop(0, nchunk // group, phase_a, 0)

    def phase_b(c, carry):
        gl = gam_s[pl.ds(c * L + (L - 1), 1), :]
        outs = []
        for j in range(RWKV_PAIRS):
            sb = sfin_ref[0, j]
            sbb = sb.astype(BF16)
            o2 = _dot_nt(n1_s[c, j], sbb) + n2_s[c, j]
            outs.append(o2[:L] + o2[L:])
            sfin_ref[0, j] = sb * gl[:, j * LANES:(j + 1) * LANES] - _dot(sbb, w1_s[c, j]) + c0_s[c, j]
        o_s[pl.ds(pl.multiple_of(c * L, L), L), :] = jnp.concatenate(outs, axis=1)
        return carry

    lax.fori_loop(0, nchunk, phase_b, 0)

    o = o_s[...]
    inv_n = 1.0 / RWKV_HEAD
    mu_h = headsum(o) * inv_n
    oc = o - mu_h
    var_h = headsum(oc * oc) * inv_n
    on = oc * lax.rsqrt(var_h + RWKV_GN_EPS) * lnw_ref[...] + lnb_ref[...]
    o_ref[0] = ((on + bonus_s[...]) * g_s[...]).astype(o_ref.dtype)


def _rwkv(pa, shift_prev, s0_pairs, prm, tb, chunk, group):
    b, s, _ = pa.shape
    nchunk = tb // chunk
    row = lambda n: pl.BlockSpec((1, n), lambda i, j: (0, 0))
    mat = lambda m, n: pl.BlockSpec((m, n), lambda i, j: (0, 0))
    sspec = pl.BlockSpec((1, RWKV_PAIRS, LANES, LANES), lambda i, j: (i, 0, 0, 0))
    coef = lambda rows, dt: pltpu.VMEM((nchunk, RWKV_PAIRS, rows, LANES), dt)
    return pl.pallas_call(
        functools.partial(_rwkv_kernel, chunk=chunk, group=group),
        grid=(b, s // tb),
        in_specs=[pl.BlockSpec((1, tb, C_RWKV), lambda i, j: (i, j, 0)),
                  pl.BlockSpec((1, 1, C_RWKV), lambda i, j: (i, 0, 0)),
                  sspec,
                  row(C_RWKV), row(D_RWKV), mat(W_LORA, D_RWKV), row(D_RWKV), mat(A_LORA, D_RWKV),
                  mat(G_LORA, D_RWKV), row(D_RWKV), row(D_RWKV), row(D_RWKV), row(D_RWKV), row(D_RWKV),
                  mat(D_RWKV, D_RWKV)],
        out_specs=[pl.BlockSpec((1, tb, D_RWKV), lambda i, j: (i, j, 0)), sspec],
        out_shape=[jax.ShapeDtypeStruct((b, s, D_RWKV), BF16),
                   jax.ShapeDtypeStruct((b, RWKV_PAIRS, LANES, LANES), F32)],
        scratch_shapes=[pltpu.VMEM((1, C_RWKV), F32)] + [pltpu.VMEM((tb, D_RWKV), F32)] * 9
        + [coef(LANES, BF16), coef(2 * chunk, BF16), coef(LANES, F32), coef(2 * chunk, F32)],
        compiler_params=_params(("arbitrary", "arbitrary")),
        name="rwkv",
    )(pa, shift_prev, s0_pairs, prm["mu"], prm["w0"], prm["w2"], prm["a0"], prm["a2"], prm["g2"],
      prm["k_k"], prm["k_a"], prm["r_k"], prm["ln_w"], prm["ln_b"], prm["hsum"])


def _state_to_pairs(s):
    b = s.shape[0]
    s = s.reshape(b, RWKV_PAIRS, 2, RWKV_HEAD, RWKV_HEAD)
    z = jnp.zeros_like(s[:, :, 0])
    top = jnp.concatenate([s[:, :, 0], z], axis=-1)
    bot = jnp.concatenate([z, s[:, :, 1]], axis=-1)
    return jnp.concatenate([top, bot], axis=-2)


def _pairs_to_state(sp):
    b = sp.shape[0]
    h = RWKV_HEAD
    s = jnp.stack([sp[:, :, :h, :h], sp[:, :, h:, h:]], axis=2)
    return s.reshape(b, 2 * RWKV_PAIRS, h, h)


def _diff_finish(accs, ls, lam, sub, lam_init):
    o = accs[0] / ls[0] - lam * (accs[1] / ls[1])
    o = o * lax.rsqrt(jnp.mean(o * o, axis=-1, keepdims=True) + RMS_EPS)
    return o * sub * (1.0 - lam_init)


def _attn_kernel(lam_ref, qt_ref, k0_ref, k1_ref, vt_ref, tab_ref, sub_ref, o_ref, acc_ref, s_ref, m_ref, *, tq,
                 lam_init):
    h = pl.program_id(1)
    qi = pl.program_id(2)
    slope = lam_ref[1 + h] * LOG2E
    qt = qt_ref[0, 0]
    row = lax.broadcasted_iota(jnp.int32, (LANES, 1), 0)
    ones_rows = (row >= DIFF_QK) & (row < DIFF_QK + 3)
    fill = jnp.where(ones_rows, 1.0, 0.0).astype(BF16)

    def rhs(q_first):
        return jnp.where(row < DIFF_QK, q_first, fill)
    rs = (rhs(qt), rhs(jnp.concatenate([qt[DIFF_QK:], qt[:DIFF_QK]], axis=0)))
    k_refs = (k0_ref, k1_ref)
    acc_ref[...] = jnp.zeros_like(acc_ref)
    m_ref[...] = jnp.full_like(m_ref, NEG_INF)

    def scores(j, slot):
        ds = pl.ds(pl.multiple_of(j * tq, tq), tq)
        for c in range(2):
            s_ref[slot, c] = _dot(k_refs[c][0, 0, ds, :], rs[c])

    def softmax_pv(j, slot, diag):
        vt = vt_ref[0, 0, :, pl.ds(pl.multiple_of(j * tq, tq), tq)]
        off = slope * ((j - qi) * tq).astype(F32)
        for c in range(2):
            s = s_ref[slot, c]
            if diag:
                s = s + tab_ref[0]
            m = m_ref[c]
            m_new = jnp.maximum(m, jnp.max(s, axis=0, keepdims=True) + off)
            alpha = jnp.exp2(m - m_new)
            p = jnp.exp2(s - (m_new - off))
            m_ref[c] = m_new
            acc_ref[c] = alpha * acc_ref[c] + _dot(vt, p.astype(BF16))

    def pair(jj, carry):
        j = 2 * jj
        scores(j + 1, 1)
        softmax_pv(j, 0, False)
        scores(jnp.minimum(j + 2, qi), 0)
        softmax_pv(j + 1, 1, False)
        return carry

    scores(0, 0)
    lax.fori_loop(0, qi // 2, pair, 0)

    @pl.when(lax.rem(qi, 2) == 0)
    def _():
        softmax_pv(qi, 0, True)

    @pl.when(lax.rem(qi, 2) == 1)
    def _():
        scores(qi, 1)
        softmax_pv(qi - 1, 0, False)
        softmax_pv(qi, 1, True)

    a0 = acc_ref[0]
    a1 = acc_ref[1]
    o = a0[:DIFF_V] / a0[DIFF_V:DIFF_V + 1] - lam_ref[0] * (a1[:DIFF_V] / a1[DIFF_V:DIFF_V + 1])
    o = o * lax.rsqrt(jnp.mean(o * o, axis=0, keepdims=True) + RMS_EPS)
    o = o * (sub_ref[...] * (1.0 - lam_init))
    o_ref[0] = o.T.astype(o_ref.dtype)


def _attn_diag_table(tq):
    slopes = jnp.asarray(ALIBI_SLOPES, F32) * LOG2E
    pos = jnp.arange(tq, dtype=jnp.int32)
    kp, qp = pos[:, None], pos[None, :]
    rel = jnp.where(kp <= qp, 0, 2 * (qp - kp)).astype(F32)
    vis = (kp // CHUNK) <= (qp // CHUNK)
    return jnp.where(vis[None], slopes[:, None, None] * rel[None], NEG_INF)


def _attn_prompt(lam, qt, k0, k1, vt, subln_col, tq, lam_init):
    b, h, _, s = qt.shape
    kspec = pl.BlockSpec((1, 1, s, LANES), lambda i, j, t: (i, j, 0, 0))
    return pl.pallas_call(
        functools.partial(_attn_kernel, tq=tq, lam_init=lam_init),
        grid=(b, h, s // tq),
        in_specs=[pl.BlockSpec(memory_space=pltpu.SMEM),
                  pl.BlockSpec((1, 1, LANES, tq), lambda i, j, t: (i, j, 0, t)),
                  kspec, kspec,
                  pl.BlockSpec((1, 1, LANES + VT_PAD, s), lambda i, j, t: (i, j, 0, 0)),
                  pl.BlockSpec((1, tq, tq), lambda i, j, t: (j, 0, 0)),
                  pl.BlockSpec((DIFF_V, 1), lambda i, j, t: (0, 0))],
        out_specs=pl.BlockSpec((1, tq, LANES), lambda i, j, t: (i, t, j)),
        out_shape=jax.ShapeDtypeStruct((b, s, D_DIFF), BF16),
        scratch_shapes=[pltpu.VMEM((2, DIFF_V + VT_PAD, tq), F32), pltpu.VMEM((2, 2, tq, tq), F32),
                        pltpu.VMEM((2, 1, tq), F32)],
        compiler_params=_params(("arbitrary", "arbitrary", "arbitrary")),
        name="attn_prompt",
    )(lam, qt, k0, k1, vt, _attn_diag_table(tq), subln_col)


def _attn_cached_kernel(lam_ref, q_ref, kn_ref, vn_ref, kp_ref, vp_ref, sub_ref, o_ref, *, lam_init):
    h = pl.program_id(1)
    slope = lam_ref[1 + h]
    s_new = q_ref.shape[2]
    past = kp_ref.shape[2]
    q = q_ref[0, 0]
    kp = kp_ref[0, 0].astype(BF16)
    vp = vp_ref[0, 0].astype(BF16)
    kn = kn_ref[0, 0]
    vn = vn_ref[0, 0]
    q_pos = past + lax.broadcasted_iota(jnp.int32, (s_new, 1), 0)

    def bias(k_pos):
        dist = jnp.abs(q_pos - k_pos).astype(F32)
        vis = (k_pos // CHUNK) <= (q_pos // CHUNK)
        return jnp.where(vis, -slope * dist, NEG_INF), vis

    b_past, vis_past = bias(lax.broadcasted_iota(jnp.int32, (1, past), 1))
    b_new, vis_new = bias(past + lax.broadcasted_iota(jnp.int32, (1, s_new), 1))
    accs, ls = [], []
    for c in range(2):
        qc = q[:, c * DIFF_QK:(c + 1) * DIFF_QK]
        sp = jnp.where(vis_past, _dot_nt(qc, kp[:, c * DIFF_QK:(c + 1) * DIFF_QK]) + b_past, NEG_INF)
        sn = jnp.where(vis_new, _dot_nt(qc, kn[:, c * DIFF_QK:(c + 1) * DIFF_QK]) + b_new, NEG_INF)
        m = jnp.maximum(jnp.max(sp, axis=-1, keepdims=True), jnp.max(sn, axis=-1, keepdims=True))
        pp = jnp.exp(sp - m)
        pn = jnp.exp(sn - m)
        ls.append(jnp.sum(pp, axis=-1, keepdims=True) + jnp.sum(pn, axis=-1, keepdims=True))
        accs.append(_dot(pp.astype(BF16), vp) + _dot(pn.astype(BF16), vn))
    o = _diff_finish(accs, ls, lam_ref[0], sub_ref[...], lam_init)
    o_ref[0] = o.astype(o_ref.dtype)


def _attn_cached(lam, q, kb, vb, past_k, past_v, subln, lam_init):
    b, h, s, _ = q.shape
    past = past_k.shape[2]
    new = pl.BlockSpec((1, 1, s, LANES), lambda i, j: (i, j, 0, 0))
    old = pl.BlockSpec((1, 1, past, LANES), lambda i, j: (i, j, 0, 0))
    return pl.pallas_call(
        functools.partial(_attn_cached_kernel, lam_init=lam_init),
        grid=(b, h),
        in_specs=[pl.BlockSpec(memory_space=pltpu.SMEM), new, new, new, old, old,
                  pl.BlockSpec((1, DIFF_V), lambda i, j: (0, 0))],
        out_specs=pl.BlockSpec((1, s, LANES), lambda i, j: (i, 0, j)),
        out_shape=jax.ShapeDtypeStruct((b, s, D_DIFF), BF16),
        compiler_params=_params(("arbitrary", "arbitrary")),
        name="attn_cached",
    )(lam, q, kb, vb, past_k, past_v, subln)


def _route(lg):
    tm = lg.shape[0]
    lane = lax.broadcasted_iota(jnp.int32, (tm, LANES), 1)
    lane_f = lane.astype(F32)
    big = 1e9
    low = -3e38
    is_g = (lane >= N_EXPERTS) & (lane < N_EXPERTS + N_GROUPS)
    gl = jnp.where(is_g, lg, low)
    gmax = jnp.max(gl, axis=-1, keepdims=True)
    grp_lane = jnp.min(jnp.where(gl == gmax, lane_f, big), axis=-1, keepdims=True)
    gsum = jnp.sum(jnp.where(is_g, jnp.exp(gl - gmax), 0.0), axis=-1, keepdims=True)
    g_prob = 1.0 / gsum
    grp = grp_lane - float(N_EXPERTS)
    lane_grp = (lane // EXPERTS_PER_GROUP).astype(F32)
    el = jnp.where(lane_grp == grp, lg, low)
    v1 = jnp.max(el, axis=-1, keepdims=True)
    i1 = jnp.min(jnp.where(el == v1, lane_f, big), axis=-1, keepdims=True)
    el2 = jnp.where(lane_f == i1, low, el)
    v2 = jnp.max(el2, axis=-1, keepdims=True)
    i2 = jnp.min(jnp.where(el2 == v2, lane_f, big), axis=-1, keepdims=True)
    e21 = jnp.exp(v2 - v1)
    den = 1.0 + e21
    gate1 = g_prob / den
    gate2 = g_prob * e21 / den
    out = jnp.where(lane == 0, i1, jnp.where(lane == 1, i2, jnp.where(lane == 2, gate1,
                    jnp.where(lane == 3, gate2, 0.0))))
    return out


def _post_kernel(x_ref, oa_ref, ob_ref, mk_ref, mv_ref, wout_ref, wq_ref, wo_ref,
                 g1_ref, b1_ref, g2_ref, b2_ref, wrh_ref, wrl_ref, br_ref, hx_ref, *, nsplit):
    tm = x_ref.shape[1]
    rows = tm // nsplit
    parts = [slice(i * rows, (i + 1) * rows) for i in range(nsplit)]
    mix = [_dot(oa_ref[0, r, :], wout_ref[:D_RWKV, :]) + _dot(ob_ref[0, r, :], wout_ref[D_RWKV:, :]) for r in parts]
    h1 = [_layer_norm(DEEPNORM_ALPHA * x_ref[0, r, :] + m, g1_ref[...], b1_ref[...]) for r, m in zip(parts, mix)]
    q = [(_dot(h.astype(BF16), wq_ref[...]) * (MEM_HEAD ** -0.5)).astype(BF16) for h in h1]
    heads = [[] for _ in parts]
    for h in range(MEM_HEADS):
        hs = slice(h * MEM_HEAD, (h + 1) * MEM_HEAD)
        for i in range(nsplit):
            s = _dot_nt(q[i][:, hs], mk_ref[0, :, hs])
            m = jnp.max(s, axis=-1, keepdims=True)
            p = jnp.exp(s - m)
            p = p / jnp.sum(p, axis=-1, keepdims=True)
            heads[i].append(_dot(p.astype(BF16), mv_ref[0, :, hs]).astype(BF16))
    att = [_dot(jnp.concatenate(hd, axis=-1), wo_ref[...]) for hd in heads]
    h2 = [_layer_norm(DEEPNORM_ALPHA * h + a, g2_ref[...], b2_ref[...]) for h, a in zip(h1, att)]
    for r, h in zip(parts, h2):
        hx_ref[0, r, :D_MODEL] = h
        hi, lo = _split2(h)
        lg = _dot(hi, wrh_ref[...]) + _dot(hi, wrl_ref[...]) + _dot(lo, wrh_ref[...]) + br_ref[...]
        hx_ref[0, r, D_MODEL:] = _route(lg)


def _post(x, oa, ob, mk, mv, prm, tm, nsplit):
    b, s, d = x.shape
    n = mk.shape[1]
    tok = lambda w: pl.BlockSpec((1, tm, w), lambda i, j: (i, j, 0))
    mem = pl.BlockSpec((1, n, d), lambda i, j: (i, 0, 0))
    mat = lambda m_, n_: pl.BlockSpec((m_, n_), lambda i, j: (0, 0))
    return pl.pallas_call(
        functools.partial(_post_kernel, nsplit=nsplit),
        grid=(b, s // tm),
        in_specs=[tok(d), tok(D_RWKV), tok(D_DIFF), mem, mem, mat(d, d), mat(d, d), mat(d, d),
                  mat(1, d), mat(1, d), mat(1, d), mat(1, d), mat(d, LANES), mat(d, LANES), mat(1, LANES)],
        out_specs=tok(d + LANES),
        out_shape=jax.ShapeDtypeStruct((b, s, d + LANES), F32),
        compiler_params=_params(("arbitrary", "arbitrary")),
        name="post",
    )(x, oa, ob, mk, mv, prm["w_out"], prm["wq"], prm["wo"], prm["ln1_g"], prm["ln1_b"],
      prm["ln2_g"], prm["ln2_b"], prm["wr_hi"], prm["wr_lo"], prm["br"])


def _moe_kernel(bexp_ref, nused_ref, tok_ref, tokn_ref, h2_hbm, w1_ref, w3_ref, w2_ref, y_ref, xbuf, sem):
    i = pl.program_id(0)
    nblk = pl.num_programs(0)
    nused = nused_ref[0]
    ngrp = xbuf.shape[1]
    tb = ngrp * SUBLANES
    slot = lax.rem(i, 2)

    def issue_rows(idx_ref, sl):
        def group(g, c):
            for u in range(SUBLANES):
                r = idx_ref[0, 0, g * SUBLANES + u]
                pltpu.make_async_copy(h2_hbm.at[pl.ds(r, 1)], xbuf.at[sl, g, pl.ds(u, 1)], sem.at[sl]).start()
            return c
        lax.fori_loop(0, ngrp, group, 0)

    def wait_rows(sl):
        pltpu.make_async_copy(xbuf.at[1 - sl], xbuf.at[sl], sem.at[sl]).wait()

    @pl.when(i == 0)
    def _():
        issue_rows(tok_ref, 0)

    @pl.when(i < nused)
    def _():
        wait_rows(slot)
        issue_rows(tokn_ref, 1 - slot)
        x = xbuf[slot].reshape(tb, xbuf.shape[3]).astype(BF16)
        a = _dot(x, w1_ref[0])
        g = _dot(x, w3_ref[0])
        hmid = (a / (1.0 + jnp.exp(-a))) * g
        y_ref[...] = _dot(hmid.astype(BF16), w2_ref[0])

        @pl.when(i == nblk - 1)
        def _():
            wait_rows(1 - slot)

    @pl.when(i >= nused)
    def _():
        y_ref[...] = jnp.zeros_like(y_ref)

        @pl.when(i == nused)
        def _():
            wait_rows(slot)


def _moe_ffn(h2f, blk_exp, nused, tok, w1, w3, w2, tb):
    nblk = tok.shape[0]
    d = h2f.shape[1]
    grid_spec = pltpu.PrefetchScalarGridSpec(
        num_scalar_prefetch=2,
        grid=(nblk,),
        in_specs=[pl.BlockSpec((1, 1, tb), lambda i, be, nu: (i, 0, 0), memory_space=pltpu.SMEM),
                  pl.BlockSpec((1, 1, tb), lambda i, be, nu: (jnp.minimum(i + 1, nblk - 1), 0, 0),
                               memory_space=pltpu.SMEM),
                  pl.BlockSpec(memory_space=pl.ANY),
                  pl.BlockSpec((1, d, D_EXPERT), lambda i, be, nu: (be[i], 0, 0)),
                  pl.BlockSpec((1, d, D_EXPERT), lambda i, be, nu: (be[i], 0, 0)),
                  pl.BlockSpec((1, D_EXPERT, d), lambda i, be, nu: (be[i], 0, 0))],
        out_specs=pl.BlockSpec((tb, d), lambda i, be, nu: (i, 0)),
        scratch_shapes=[pltpu.VMEM((2, tb // SUBLANES, SUBLANES, d), F32), pltpu.SemaphoreType.DMA((2,))],
    )
    return pl.pallas_call(
        _moe_kernel,
        grid_spec=grid_spec,
        out_shape=jax.ShapeDtypeStruct((nblk * tb, d), F32),
        compiler_params=_params(("arbitrary",)),
        name="moe_ffn",
    )(blk_exp, nused, tok, tok, h2f, w1, w3, w2)


def _combine_kernel(pos_ref, posn_ref, ys_hbm, h2_ref, route_ref, g_ref, b_ref, y_ref, gbuf, sem):
    i = pl.program_id(0)
    n = pl.num_programs(0)
    tm = h2_ref.shape[0]
    slot = lax.rem(i, 2)

    ngrp = gbuf.shape[1]
    d = gbuf.shape[3]

    def issue_rows(idx_ref, sl):
        def group(g, c):
            for u in range(SUBLANES):
                r = idx_ref[0, 0, g * SUBLANES + u]
                pltpu.make_async_copy(ys_hbm.at[pl.ds(r, 1)], gbuf.at[sl, g, pl.ds(u, 1)], sem.at[sl]).start()
            return c
        lax.fori_loop(0, ngrp, group, 0)

    def wait_rows(sl):
        pltpu.make_async_copy(gbuf.at[1 - sl], gbuf.at[sl], sem.at[sl]).wait()

    @pl.when(i == 0)
    def _():
        issue_rows(pos_ref, 0)

    wait_rows(slot)
    issue_rows(posn_ref, 1 - slot)
    rt = route_ref[...]
    half = ngrp // 2
    moe = (rt[:, 2:3] * gbuf[slot, :half].reshape(tm, d) + rt[:, 3:4] * gbuf[slot, half:].reshape(tm, d))
    y_ref[...] = _layer_norm(DEEPNORM_ALPHA * h2_ref[...] + moe, g_ref[...], b_ref[...])

    @pl.when(i == n - 1)
    def _():
        wait_rows(1 - slot)


def _combine(pos, ysort, h2f, route, g, b, tm):
    t, d = h2f.shape
    n = t // tm
    return pl.pallas_call(
        _combine_kernel,
        grid=(n,),
        in_specs=[pl.BlockSpec((1, 1, 2 * tm), lambda i: (i, 0, 0), memory_space=pltpu.SMEM),
                  pl.BlockSpec((1, 1, 2 * tm), lambda i: (jnp.minimum(i + 1, n - 1), 0, 0),
                               memory_space=pltpu.SMEM),
                  pl.BlockSpec(memory_space=pl.ANY),
                  pl.BlockSpec((tm, d), lambda i: (i, 0)),
                  pl.BlockSpec((tm, LANES), lambda i: (i, 0)),
                  pl.BlockSpec((1, d), lambda i: (0, 0)),
                  pl.BlockSpec((1, d), lambda i: (0, 0))],
        out_specs=pl.BlockSpec((tm, d), lambda i: (i, 0)),
        out_shape=jax.ShapeDtypeStruct((t, d), F32),
        scratch_shapes=[pltpu.VMEM((2, 2 * tm // SUBLANES, SUBLANES, d), F32), pltpu.SemaphoreType.DMA((2,))],
        compiler_params=_params(("arbitrary",)),
        name="combine",
    )(pos, pos, ysort, h2f, route, g, b)


def _dispatch(expert, tb, tm):
    t = expert.shape[0]
    a = 2 * t
    ef = expert.reshape(a)
    order = jnp.argsort(ef, stable=True).astype(jnp.int32)
    es = ef[order]
    counts = jnp.sum((ef[:, None] == jnp.arange(N_EXPERTS, dtype=jnp.int32)[None, :]).astype(jnp.int32), axis=0)
    starts = jnp.cumsum(counts) - counts
    nb = (counts + tb - 1) // tb
    bend = jnp.cumsum(nb)
    bstart = bend - nb
    nblk = -(-a // tb) + N_EXPERTS
    blk = jnp.arange(nblk, dtype=jnp.int32)
    bexp = jnp.minimum(jnp.sum((blk[:, None] >= bend[None, :]).astype(jnp.int32), axis=1), N_EXPERTS - 1)
    row0 = starts[bexp] + (blk - bstart[bexp]) * tb
    idx = jnp.clip(row0[:, None] + jnp.arange(tb, dtype=jnp.int32)[None, :], 0, a - 1)
    tok = (order[idx] // 2).astype(jnp.int32).reshape(nblk, 1, tb)
    dest_sorted = bstart[es] * tb + (jnp.arange(a, dtype=jnp.int32) - starts[es])
    _, pos = lax.sort_key_val(order, dest_sorted.astype(jnp.int32))
    pos = pos.reshape(t, 2)
    pos = pos.reshape(t // tm, tm, 2).transpose(0, 2, 1).reshape(t // tm, 1, 2 * tm)
    return bexp, bend[-1:].astype(jnp.int32), tok, pos


def _moe(hx, prm, tb, tm):
    b, s, _ = hx.shape
    d = D_MODEL
    t = b * s
    h2f = hx[..., :d].reshape(t, d)
    rf = hx[..., d:].reshape(t, LANES)
    expert = rf[:, :2].astype(jnp.int32)
    bexp, nused, tok, pos = _dispatch(expert, tb, tm)
    ysort = _moe_ffn(h2f, bexp, nused, tok, prm["moe_w1"], prm["moe_w3"], prm["moe_w2"], tb)
    y = _combine(pos, ysort, h2f, rf, prm["ln3_g"], prm["ln3_b"], tm)
    return y.reshape(b, s, d)


N_PAIR_CLASSES = N_GROUPS * EXPERTS_PER_GROUP * EXPERTS_PER_GROUP
N_REAL_PAIRS = N_GROUPS * (EXPERTS_PER_GROUP * (EXPERTS_PER_GROUP - 1) // 2)


def _moe_pair_kernel(elo_ref, ehi_ref, nused_ref, tok_ref, tokn_ref, h2_hbm,
                     w1a_ref, w3a_ref, w2a_ref, w1b_ref, w3b_ref, w2b_ref, y_ref, xbuf, sem):
    i = pl.program_id(0)
    nblk = pl.num_programs(0)
    nused = nused_ref[0]
    ngrp = xbuf.shape[1]
    tb = ngrp * SUBLANES
    slot = lax.rem(i, 2)

    def issue_rows(idx_ref, sl):
        def group(g, c):
            for u in range(SUBLANES):
                r = idx_ref[0, 0, g * SUBLANES + u]
                pltpu.make_async_copy(h2_hbm.at[pl.ds(r, 1)], xbuf.at[sl, g, pl.ds(u, 1)], sem.at[sl]).start()
            return c
        lax.fori_loop(0, ngrp, group, 0)

    def wait_rows(sl):
        pltpu.make_async_copy(xbuf.at[1 - sl], xbuf.at[sl], sem.at[sl]).wait()

    def ffn(x, w1_ref, w3_ref, w2_ref):
        a = _dot(x, w1_ref[0])
        g = _dot(x, w3_ref[0])
        hmid = (a / (1.0 + jnp.exp(-a))) * g
        return _dot(hmid.astype(BF16), w2_ref[0])

    @pl.when(i == 0)
    def _():
        issue_rows(tok_ref, 0)

    @pl.when(i < nused)
    def _():
        wait_rows(slot)
        issue_rows(tokn_ref, 1 - slot)
        rows = xbuf[slot].reshape(tb, xbuf.shape[3])
        x = rows[:, :D_MODEL].astype(BF16)
        rt = rows[:, D_MODEL:]
        first_low = rt[:, 0:1] < rt[:, 1:2]
        g_lo = jnp.where(first_low, rt[:, 2:3], rt[:, 3:4])
        g_hi = jnp.where(first_low, rt[:, 3:4], rt[:, 2:3])
        y_ref[...] = g_lo * ffn(x, w1a_ref, w3a_ref, w2a_ref) + g_hi * ffn(x, w1b_ref, w3b_ref, w2b_ref)

        @pl.when(i == nblk - 1)
        def _():
            wait_rows(1 - slot)

    @pl.when(i >= nused)
    def _():
        y_ref[...] = jnp.zeros_like(y_ref)

        @pl.when(i == nused)
        def _():
            wait_rows(slot)


def _moe_pair_ffn(hxf, elo, ehi, nused, tok, w1, w3, w2, tb):
    nblk = tok.shape[0]
    d = D_MODEL
    wspec = lambda rows, cols, which: pl.BlockSpec(
        (1, rows, cols), (lambda i, lo, hi, nu: (lo[i], 0, 0)) if which == 0 else (lambda i, lo, hi, nu: (hi[i], 0, 0)))
    grid_spec = pltpu.PrefetchScalarGridSpec(
        num_scalar_prefetch=3,
        grid=(nblk,),
        in_specs=[pl.BlockSpec((1, 1, tb), lambda i, lo, hi, nu: (i, 0, 0), memory_space=pltpu.SMEM),
                  pl.BlockSpec((1, 1, tb), lambda i, lo, hi, nu: (jnp.minimum(i + 1, nblk - 1), 0, 0),
                               memory_space=pltpu.SMEM),
                  pl.BlockSpec(memory_space=pl.ANY),
                  wspec(d, D_EXPERT, 0), wspec(d, D_EXPERT, 0), wspec(D_EXPERT, d, 0),
                  wspec(d, D_EXPERT, 1), wspec(d, D_EXPERT, 1), wspec(D_EXPERT, d, 1)],
        out_specs=pl.BlockSpec((tb, d), lambda i, lo, hi, nu: (i, 0)),
        scratch_shapes=[pltpu.VMEM((2, tb // SUBLANES, SUBLANES, hxf.shape[1]), F32),
                        pltpu.SemaphoreType.DMA((2,))],
    )
    return pl.pallas_call(
        _moe_pair_kernel,
        grid_spec=grid_spec,
        out_shape=jax.ShapeDtypeStruct((nblk * tb, d), F32),
        compiler_params=_params(("arbitrary",)),
        name="moe_pair_ffn",
    )(elo, ehi, nused, tok, tok, hxf, w1, w3, w2, w1, w3, w2)


def _combine1_kernel(pos_ref, posn_ref, ys_hbm, hx_ref, g_ref, b_ref, y_ref, gbuf, sem):
    i = pl.program_id(0)
    n = pl.num_programs(0)
    tm = hx_ref.shape[0]
    slot = lax.rem(i, 2)
    ngrp = gbuf.shape[1]

    def issue_rows(idx_ref, sl):
        def group(g, c):
            for u in range(SUBLANES):
                r = idx_ref[0, 0, g * SUBLANES + u]
                pltpu.make_async_copy(ys_hbm.at[pl.ds(r, 1)], gbuf.at[sl, g, pl.ds(u, 1)], sem.at[sl]).start()
            return c
        lax.fori_loop(0, ngrp, group, 0)

    def wait_rows(sl):
        pltpu.make_async_copy(gbuf.at[1 - sl], gbuf.at[sl], sem.at[sl]).wait()

    @pl.when(i == 0)
    def _():
        issue_rows(pos_ref, 0)

    wait_rows(slot)
    issue_rows(posn_ref, 1 - slot)
    moe = gbuf[slot].reshape(tm, gbuf.shape[3])
    y_ref[...] = _layer_norm(DEEPNORM_ALPHA * hx_ref[:, :D_MODEL] + moe, g_ref[...], b_ref[...])

    @pl.when(i == n - 1)
    def _():
        wait_rows(1 - slot)


def _combine1(pos, ysort, hxf, g, b, tm):
    t, dx = hxf.shape
    d = D_MODEL
    n = t // tm
    return pl.pallas_call(
        _combine1_kernel,
        grid=(n,),
        in_specs=[pl.BlockSpec((1, 1, tm), lambda i: (i, 0, 0), memory_space=pltpu.SMEM),
                  pl.BlockSpec((1, 1, tm), lambda i: (jnp.minimum(i + 1, n - 1), 0, 0), memory_space=pltpu.SMEM),
                  pl.BlockSpec(memory_space=pl.ANY),
                  pl.BlockSpec((tm, dx), lambda i: (i, 0)),
                  pl.BlockSpec((1, d), lambda i: (0, 0)),
                  pl.BlockSpec((1, d), lambda i: (0, 0))],
        out_specs=pl.BlockSpec((tm, d), lambda i: (i, 0)),
        out_shape=jax.ShapeDtypeStruct((t, d), F32),
        scratch_shapes=[pltpu.VMEM((2, tm // SUBLANES, SUBLANES, d), F32), pltpu.SemaphoreType.DMA((2,))],
        compiler_params=_params(("arbitrary",)),
        name="combine1",
    )(pos, pos, ysort, hxf, g, b)


def _dispatch_pairs(e1, e2, tb, tm):
    t = e1.shape[0]
    lo = jnp.minimum(e1, e2)
    hi = jnp.maximum(e1, e2)
    epg = EXPERTS_PER_GROUP
    cls = (lo // epg) * (epg * epg) + (lo % epg) * epg + (hi % epg)
    cs, order = lax.sort((cls, jnp.arange(t, dtype=jnp.int32)), num_keys=1, is_stable=True)
    counts = jnp.sum((cls[:, None] == jnp.arange(N_PAIR_CLASSES, dtype=jnp.int32)[None, :]).astype(jnp.int32), axis=0)
    starts = jnp.cumsum(counts) - counts
    nb = (counts + tb - 1) // tb
    bend = jnp.cumsum(nb)
    bstart = bend - nb
    nblk = -(-t // tb) + N_REAL_PAIRS
    blk = jnp.arange(nblk, dtype=jnp.int32)
    bcls = jnp.minimum(jnp.sum((blk[:, None] >= bend[None, :]).astype(jnp.int32), axis=1), N_PAIR_CLASSES - 1)
    grp = bcls // (epg * epg)
    elo = (grp * epg + (bcls % (epg * epg)) // epg).astype(jnp.int32)
    ehi = (grp * epg + bcls % epg).astype(jnp.int32)
    row0 = starts[bcls] + (blk - bstart[bcls]) * tb
    idx = jnp.clip(row0[:, None] + jnp.arange(tb, dtype=jnp.int32)[None, :], 0, t - 1)
    tok = order[idx].astype(jnp.int32)
    delta = bstart * tb - starts
    onehot = cs[:, None] == jnp.arange(N_PAIR_CLASSES, dtype=jnp.int32)[None, :]
    dest_sorted = jnp.arange(t, dtype=jnp.int32) + jnp.sum(jnp.where(onehot, delta[None, :], 0), axis=1)
    _, pos = lax.sort_key_val(order, dest_sorted.astype(jnp.int32))
    return elo, ehi, bend[-1:].astype(jnp.int32), tok.reshape(nblk, 1, tb), pos.reshape(t // tm, 1, tm)


def _moe_pairs(hx, prm, tb, tm):
    b, s, dx = hx.shape
    t = b * s
    hxf = hx.reshape(t, dx)
    experts = hxf[:, D_MODEL:D_MODEL + 2].astype(jnp.int32)
    elo, ehi, nused, tok, pos = _dispatch_pairs(experts[:, 0], experts[:, 1], tb, tm)
    ysort = _moe_pair_ffn(hxf, elo, ehi, nused, tok, prm["moe_w1"], prm["moe_w3"], prm["moe_w2"], tb)
    y = _combine1(pos, ysort, hxf, prm["ln3_g"], prm["ln3_b"], tm)
    return y.reshape(b, s, D_MODEL)


def _trunk(x, shift_prev, state0, past_k, past_v, mk, mv, prm, lam, lam_init, cfg):
    b, s, _ = x.shape
    pa, q, k, v, kb, vb, kb1 = _inproj(x, prm["w_in"], cfg["in_bb"], cfg["in_ts"],
                                       cfg["attn_tq"] if past_k is None else 0)
    oa, sfin = _rwkv(pa, shift_prev, _state_to_pairs(state0), prm, cfg["rwkv_tb"], cfg["rwkv_chunk"],
                     cfg["rwkv_group"])
    if past_k is None:
        ob = _attn_prompt(lam, q, kb, kb1, vb, prm["subln"].reshape(DIFF_V, 1), cfg["attn_tq"], lam_init)
    else:
        ob = _attn_cached(lam, q, kb, vb, past_k, past_v, prm["subln"], lam_init)
    hx = _post(x, oa, ob, mk, mv, prm, cfg["post_tm"], cfg["post_split"])
    moe = _moe_pairs if cfg["moe_pairs"] else _moe
    y = moe(hx, prm, cfg["moe_tb"], cfg["comb_tm"])
    return y, k, v, _pairs_to_state(sfin), pa[:, s - 1:s, :]


def _prep(w, l, lam_scalar_inputs):
    row = lambda a: a[l].reshape(1, -1).astype(F32)
    idx = jnp.arange(D_RWKV, dtype=jnp.int32) // RWKV_HEAD
    wr = jnp.zeros((D_MODEL, LANES), F32)
    wr = wr.at[:, :N_EXPERTS].set(w["moe_w_expert"][l]).at[:, N_EXPERTS:N_EXPERTS + N_GROUPS].set(w["moe_w_group"][l])
    wr_hi = wr.astype(BF16)
    br = jnp.zeros((1, LANES), F32)
    br = br.at[0, :N_EXPERTS].set(w["moe_b_expert"][l]).at[0, N_EXPERTS:N_EXPERTS + N_GROUPS].set(w["moe_b_group"][l])
    return {
        "w_in": w["w_in"][l].astype(BF16),
        "mu": row(w["rwkv_mu"]), "w0": row(w["rwkv_w0"]), "w2": w["rwkv_w2"][l].astype(BF16),
        "a0": row(w["rwkv_a0"]), "a2": w["rwkv_a2"][l].astype(BF16), "g2": w["rwkv_g2"][l].astype(BF16),
        "k_k": row(w["rwkv_k_k"]), "k_a": row(w["rwkv_k_a"]), "r_k": row(w["rwkv_r_k"]),
        "ln_w": row(w["rwkv_ln_w"]), "ln_b": row(w["rwkv_ln_b"]),
        "hsum": (idx[:, None] == idx[None, :]).astype(BF16),
        "subln": row(w["diff_subln"]),
        "w_out": w["w_out"][l].astype(BF16), "wq": w["mem_wq"][l].astype(BF16), "wo": w["mem_wo"][l].astype(BF16),
        "ln1_g": row(w["ln1_g"]), "ln1_b": row(w["ln1_b"]), "ln2_g": row(w["ln2_g"]), "ln2_b": row(w["ln2_b"]),
        "ln3_g": row(w["ln3_g"]), "ln3_b": row(w["ln3_b"]),
        "wr_hi": wr_hi, "wr_lo": (wr - wr_hi.astype(F32)).astype(BF16), "br": br,
        "moe_w1": w["moe_w1"][l].astype(BF16), "moe_w3": w["moe_w3"][l].astype(BF16),
        "moe_w2": w["moe_w2"][l].astype(BF16),
    }


def _tile(n, pref):
    return pref if n % pref == 0 else n


def kernel(x_prompt, x_sample, mem_prompt, cache_diff_k, cache_diff_v, cache_mem_k, cache_mem_v, state_rwkv, state_shift, w_in, rwkv_mu, rwkv_w0, rwkv_w2, rwkv_a0, rwkv_a2, rwkv_g2, rwkv_k_k, rwkv_k_a, rwkv_r_k, rwkv_ln_w, rwkv_ln_b, diff_lq1, diff_lk1, diff_lq2, diff_lk2, diff_subln, w_out, ln1_g, ln1_b, mem_wq, mem_wk, mem_wv, mem_wo, ln2_g, ln2_b, moe_w_group, moe_b_group, moe_w_expert, moe_b_expert, moe_w1, moe_w3, moe_w2, ln3_g, ln3_b):
    w = dict(w_in=w_in, rwkv_mu=rwkv_mu, rwkv_w0=rwkv_w0, rwkv_w2=rwkv_w2, rwkv_a0=rwkv_a0, rwkv_a2=rwkv_a2,
             rwkv_g2=rwkv_g2, rwkv_k_k=rwkv_k_k, rwkv_k_a=rwkv_k_a, rwkv_r_k=rwkv_r_k, rwkv_ln_w=rwkv_ln_w,
             rwkv_ln_b=rwkv_ln_b, diff_subln=diff_subln, w_out=w_out, ln1_g=ln1_g, ln1_b=ln1_b, mem_wq=mem_wq,
             mem_wo=mem_wo, ln2_g=ln2_g, ln2_b=ln2_b, moe_w_group=moe_w_group, moe_b_group=moe_b_group,
             moe_w_expert=moe_w_expert, moe_b_expert=moe_b_expert, moe_w1=moe_w1, moe_w3=moe_w3, moe_w2=moe_w2,
             ln3_g=ln3_g, ln3_b=ln3_b)
    bp, sp, _ = x_prompt.shape
    bs, ss, _ = x_sample.shape
    depth = w_in.shape[0]
    cfg_p = dict(in_bb=1, in_ts=_tile(sp, 512), rwkv_tb=_tile(sp, 256), rwkv_chunk=CHUNK, rwkv_group=_tile(sp, 256) // CHUNK,
                 attn_tq=_tile(sp, 512), post_tm=_tile(sp, 512), post_split=2, moe_tb=256, moe_pairs=True, comb_tm=_tile(bp * sp, 256))
    cfg_s = dict(in_bb=bs, in_ts=ss, rwkv_tb=ss, rwkv_chunk=ss, rwkv_group=1, attn_tq=ss, post_tm=ss, post_split=1, moe_tb=64, moe_pairs=False,
                 comb_tm=_tile(bs * ss, 256))
    yp, ys = x_prompt, x_sample
    outs = [[] for _ in range(10)]
    for l in range(depth):
        prm = _prep(w, l, None)
        lam_init = 0.8 - 0.6 * math.exp(-0.3 * l)
        f = lambda z: z[l].astype(F32)
        lam = (jnp.exp(jnp.sum(f(diff_lq1) * f(diff_lk1))) - jnp.exp(jnp.sum(f(diff_lq2) * f(diff_lk2)))
               + lam_init).reshape(1)
        slopes = jnp.asarray(ALIBI_SLOPES, F32)
        lam = jnp.concatenate([lam, slopes, jnp.zeros((3,), F32)])
        mk_p, mv_p, mkb, mvb = _memproj(mem_prompt, mem_wk[l].astype(BF16), mem_wv[l].astype(BF16))
        yp, k_p, v_p, st_p, sh_p = _trunk(
            yp, jnp.zeros((bp, 1, C_RWKV), F32), jnp.zeros((bp, 2 * RWKV_PAIRS, RWKV_HEAD, RWKV_HEAD), F32),
            None, None, mkb, mvb, prm, lam, lam_init, cfg_p)
        ys, k_s, v_s, st_s, sh_s = _trunk(
            ys, state_shift[l], state_rwkv[l], cache_diff_k[l], cache_diff_v[l],
            cache_mem_k[l].astype(BF16), cache_mem_v[l].astype(BF16), prm, lam, lam_init, cfg_s)
        for lst, val in zip(outs, (k_p, v_p, mk_p, mv_p, st_p, sh_p, k_s, v_s, st_s, sh_s)):
            lst.append(val)
    return (yp, ys) + tuple(jnp.stack(o) for o in outs)
```

```python
import functools
import math

import jax
import jax.numpy as jnp
from jax import lax
from jax.experimental import pallas as pl
from jax.experimental.pallas import tpu as pltpu

F32 = jnp.float32
BF16 = jnp.bfloat16

D_MODEL = 1024
CHUNK = 64
D_RWKV = 512
RWKV_HEAD = 64
RWKV_PAIRS = D_RWKV // (2 * RWKV_HEAD)
W_LORA = 64
A_LORA = 64
G_LORA = 128
C_RWKV = 3 * D_RWKV + W_LORA + A_LORA + G_LORA
D_DIFF = 512
DIFF_HEADS = 4
DIFF_V = 128
DIFF_QK = 64
C_IN = C_RWKV + 3 * D_DIFF
MEM_HEADS = 4
MEM_HEAD = D_MODEL // MEM_HEADS
N_GROUPS = 4
EXPERTS_PER_GROUP = 8
N_EXPERTS = 32
D_EXPERT = 512
DEPTH = 1
DEEPNORM_ALPHA = (2.0 * DEPTH) ** 0.25
LN_EPS = 1e-5
RMS_EPS = 1e-5
RWKV_GN_EPS = 64e-5
NEG_INF = -1e30
LOG2E = math.log2(math.e)
ALIBI_SLOPES = tuple(2.0 ** (-8.0 * (h + 1) / DIFF_HEADS) for h in range(DIFF_HEADS))
VT_PAD = 16
LANES = 128
SUBLANES = 8
VMEM_LIMIT = 48 * 1024 * 1024


def _params(sem):
    return pltpu.CompilerParams(dimension_semantics=sem, vmem_limit_bytes=VMEM_LIMIT)


def _dot(a, b):
    return jnp.dot(a, b, preferred_element_type=F32)


def _dot_nt(a, b):
    return lax.dot_general(a, b, (((1,), (1,)), ((), ())), preferred_element_type=F32)


def _dot_tn(a, b):
    return lax.dot_general(a, b, (((0,), (0,)), ((), ())), preferred_element_type=F32)


def _split2(x):
    hi = x.astype(BF16)
    lo = (x - hi.astype(F32)).astype(BF16)
    return hi, lo


def _split3(x):
    hi = x.astype(BF16)
    r = x - hi.astype(F32)
    mid = r.astype(BF16)
    lo = (r - mid.astype(F32)).astype(BF16)
    return hi, mid, lo


def _layer_norm(x, g, b):
    mu = jnp.mean(x, axis=-1, keepdims=True)
    xc = x - mu
    var = jnp.mean(xc * xc, axis=-1, keepdims=True)
    return xc * lax.rsqrt(var + LN_EPS) * g + b


def _memproj_kernel(m_ref, wk_ref, wv_ref, k_ref, v_ref, kb_ref, vb_ref):
    m = m_ref[0].astype(BF16)
    k = _dot(m, wk_ref[...])
    v = _dot(m, wv_ref[...])
    k_ref[0] = k
    v_ref[0] = v
    kb_ref[0] = k.astype(BF16)
    vb_ref[0] = v.astype(BF16)


def _memproj(mem, wk, wv):
    b, n, d = mem.shape
    blk = pl.BlockSpec((1, n, d), lambda i: (i, 0, 0))
    wspec = pl.BlockSpec((d, d), lambda i: (0, 0))
    return pl.pallas_call(
        _memproj_kernel,
        grid=(b,),
        in_specs=[blk, wspec, wspec],
        out_specs=[blk, blk, blk, blk],
        out_shape=[jax.ShapeDtypeStruct((b, n, d), F32), jax.ShapeDtypeStruct((b, n, d), F32),
                   jax.ShapeDtypeStruct((b, n, d), BF16), jax.ShapeDtypeStruct((b, n, d), BF16)],
        compiler_params=_params(("arbitrary",)),
        name="memproj",
    )(mem, wk, wv)


def _inproj_kernel(x_ref, w_ref, pa_ref, q_ref, k_ref, v_ref, kb_ref, vb_ref, kb1_ref, *, attn_tile):
    bb, ts, d = x_ref.shape
    rows = bb * ts
    x = x_ref[...].reshape(rows, d).astype(BF16)
    pq = _dot(x, w_ref[:, C_RWKV:C_RWKV + D_DIFF])
    pk = _dot(x, w_ref[:, C_RWKV + D_DIFF:C_RWKV + 2 * D_DIFF])
    pv = _dot(x, w_ref[:, C_RWKV + 2 * D_DIFF:])
    pa_ref[...] = _dot(x, w_ref[:, :C_RWKV]).reshape(bb, ts, C_RWKV)
    if attn_tile:
        lane = lax.broadcasted_iota(jnp.int32, (rows, LANES), 1)
        kloc = lax.rem(pl.program_id(1) * ts + lax.broadcasted_iota(jnp.int32, (rows, LANES), 0), attn_tile)
        aug = jnp.where(lax.broadcasted_iota(jnp.int32, (VT_PAD, rows), 0) == 0, 1.0, 0.0).astype(BF16)
    for h in range(DIFF_HEADS):
        hs = slice(h * LANES, (h + 1) * LANES)
        qh = pq[:, hs]
        kh = pk[:, hs]
        k_ref[:, h] = kh.reshape(bb, ts, LANES)
        vh = pv[:, hs]
        v_ref[:, h] = vh.reshape(bb, ts, LANES)
        if attn_tile:
            q_ref[0, h] = (qh * (DIFF_QK ** -0.5 * LOG2E)).T.astype(BF16)
            vb_ref[0, h] = jnp.concatenate([vh.T.astype(BF16), aug], axis=0)
            bias = kloc.astype(F32) * (ALIBI_SLOPES[h] * LOG2E)
            b_hi = bias.astype(BF16).astype(F32)
            b_mid = (bias - b_hi).astype(BF16).astype(F32)
            b_lo = bias - b_hi - b_mid
            cols = jnp.where(lane == DIFF_QK, b_hi, jnp.where(lane == DIFF_QK + 1, b_mid,
                             jnp.where(lane == DIFF_QK + 2, b_lo, 0.0)))
            kb_ref[0, h] = jnp.where(lane < DIFF_QK, kh, cols).astype(BF16)
            kb1_ref[0, h] = jnp.where(lane < DIFF_QK, pltpu.roll(kh, DIFF_QK, axis=1), cols).astype(BF16)
        else:
            q_ref[:, h] = (qh * (DIFF_QK ** -0.5)).reshape(bb, ts, LANES).astype(BF16)
            vb_ref[:, h] = vh.reshape(bb, ts, LANES).astype(BF16)
            kb_ref[:, h] = kh.reshape(bb, ts, LANES).astype(BF16)
            kb1_ref[:, h] = kh.reshape(bb, ts, LANES).astype(BF16)


def _inproj(x, w_in_bf, bb, ts, attn_tile):
    b, s, d = x.shape
    hm = pl.BlockSpec((bb, DIFF_HEADS, ts, LANES), lambda i, j: (i, 0, j, 0))
    hshape = (b, DIFF_HEADS, s, LANES)
    if attn_tile:
        assert bb == 1
        qm = pl.BlockSpec((1, DIFF_HEADS, LANES, ts), lambda i, j: (i, 0, 0, j))
        qshape = (b, DIFF_HEADS, LANES, s)
        vm = pl.BlockSpec((1, DIFF_HEADS, LANES + VT_PAD, ts), lambda i, j: (i, 0, 0, j))
        vshape = (b, DIFF_HEADS, LANES + VT_PAD, s)
    else:
        qm, qshape, vm, vshape = hm, hshape, hm, hshape
    return pl.pallas_call(
        functools.partial(_inproj_kernel, attn_tile=attn_tile),
        grid=(b // bb, s // ts),
        in_specs=[pl.BlockSpec((bb, ts, d), lambda i, j: (i, j, 0)),
                  pl.BlockSpec((d, C_IN), lambda i, j: (0, 0))],
        out_specs=[pl.BlockSpec((bb, ts, C_RWKV), lambda i, j: (i, j, 0)), qm, hm, hm, hm, vm, hm],
        out_shape=[jax.ShapeDtypeStruct((b, s, C_RWKV), F32),
                   jax.ShapeDtypeStruct(qshape, BF16),
                   jax.ShapeDtypeStruct(hshape, F32), jax.ShapeDtypeStruct(hshape, F32),
                   jax.ShapeDtypeStruct(hshape, BF16), jax.ShapeDtypeStruct(vshape, BF16),
                   jax.ShapeDtypeStruct(hshape, BF16)],
        compiler_params=_params(("arbitrary", "arbitrary")),
        name="inproj",
    )(x, w_in_bf)


def _rwkv_kernel(pa_ref, shift_ref, s0_ref, mu_ref, w0_ref, w2_ref, a0_ref, a2_ref, g2_ref,
                 kk_ref, ka_ref, rk_ref, lnw_ref, lnb_ref, hsum_ref,
                 o_ref, sfin_ref,
                 carry_ref, al_s, be_s, kb_s, rb_s, v_s, gam_s, o_s, bonus_s, g_s, w1_s, n1_s, c0_s, n2_s,
                 *, chunk, group):
    L = chunk
    tb = pa_ref.shape[1]
    nchunk = tb // L
    t = pl.program_id(1)

    @pl.when(t == 0)
    def _():
        sfin_ref[...] = s0_ref[...]
        carry_ref[...] = shift_ref[0]

    p = pa_ref[0]
    prev = pltpu.roll(p, 1, axis=0)
    row = lax.broadcasted_iota(jnp.int32, (tb, 1), 0)
    prev = jnp.where(row == 0, carry_ref[...], prev)
    carry_ref[...] = p[tb - 1:tb, :]
    ps = p + (prev - p) * mu_ref[...]

    r = ps[:, :D_RWKV]
    k = ps[:, D_RWKV:2 * D_RWKV]
    v = ps[:, 2 * D_RWKV:3 * D_RWKV]
    o0 = 3 * D_RWKV
    w_lo = ps[:, o0:o0 + W_LORA]
    a_lo = ps[:, o0 + W_LORA:o0 + W_LORA + A_LORA]
    g_lo = ps[:, o0 + W_LORA + A_LORA:]

    hsum = hsum_ref[...]

    def headsum(x):
        return _dot(x.astype(BF16), hsum)

    z = w0_ref[...] + _dot(jnp.tanh(w_lo).astype(BF16), w2_ref[...])
    lw = -math.exp(-0.5) / (1.0 + jnp.exp(-z))
    a = 1.0 / (1.0 + jnp.exp(-(a0_ref[...] + _dot(a_lo.astype(BF16), a2_ref[...]))))
    g_s[...] = _dot((1.0 / (1.0 + jnp.exp(-g_lo))).astype(BF16), g2_ref[...])
    kk = k * kk_ref[...]
    kk = kk / jnp.maximum(jnp.sqrt(headsum(kk * kk)), 1e-12)
    kmod = k * (1.0 + (a - 1.0) * ka_ref[...])
    bonus_s[...] = headsum(r * kmod * rk_ref[...]) * v

    bi = lax.broadcasted_iota(jnp.int32, (tb, tb), 0)
    bj = lax.broadcasted_iota(jnp.int32, (tb, tb), 1)
    tri = (((bi // L) == (bj // L)) & (bi >= bj)).astype(BF16)
    h3 = _split3(lw)
    cum = _dot(tri, h3[0]) + _dot(tri, h3[1]) + _dot(tri, h3[2])
    gam = jnp.exp(cum)
    igam = jnp.exp(-cum)
    gam_s[...] = gam
    al_s[...] = kk * jnp.exp(cum - lw)
    be_s[...] = kk * a * igam
    kb_s[...] = kmod * igam
    rb_s[...] = r * gam
    v_s[...] = v

    l2 = 2 * L
    ri = lax.broadcasted_iota(jnp.int32, (l2, l2), 0)
    ci = lax.broadcasted_iota(jnp.int32, (l2, l2), 1)
    same = (ri // L) == (ci // L)
    strict = same & (ri > ci)
    incl = same & (ri >= ci)
    eye = (ri == ci).astype(F32)
    lane = lax.broadcasted_iota(jnp.int32, (1, LANES), 1)
    first = lane < RWKV_HEAD

    def stack(x):
        return jnp.concatenate([jnp.where(first, x, 0.0), jnp.where(first, 0.0, x)], axis=0)

    def phase_a(cg, carry):
        cs = []
        for gi in range(group):
            c = cg * group + gi
            sl = pl.ds(pl.multiple_of(c * L, L), L)
            gl = gam_s[pl.ds(c * L + (L - 1), 1), :]
            for j in range(RWKV_PAIRS):
                ls = slice(j * LANES, (j + 1) * LANES)
                cs.append(dict(c=c, j=j, gl=gl[:, ls], al2=stack(al_s[sl, ls]), rb2=stack(rb_s[sl, ls]),
                               be2=stack(be_s[sl, ls]), kb2=stack(kb_s[sl, ls]), v2=stack(v_s[sl, ls])))
        for d in cs:
            x = jnp.concatenate([d["al2"], d["rb2"]], axis=0).astype(BF16)
            y = jnp.concatenate([d["be2"], d["kb2"]], axis=0).astype(BF16)
            gmat = _dot_nt(x, y)
            ab = jnp.where(strict, gmat[:l2, :l2], 0.0)
            d["ak"] = jnp.where(strict, gmat[:l2, l2:], 0.0).astype(BF16)
            d["rbm"] = jnp.where(incl, gmat[l2:, :l2], 0.0).astype(BF16)
            d["rkm"] = jnp.where(incl, gmat[l2:, l2:], 0.0).astype(BF16)
            d["pw"] = ab
            d["tm"] = eye - ab
        n = 2
        while n < L:
            for d in cs:
                pb = d["pw"].astype(BF16)
                d["pw"] = _dot(pb, pb)
            for d in cs:
                d["tm"] = _dot(d["tm"].astype(BF16), (eye + d["pw"]).astype(BF16))
            n *= 2
        for d in cs:
            d["akv"] = _dot(d["ak"], d["v2"].astype(BF16))
        for d in cs:
            zz = jnp.concatenate([d["al2"], d["akv"]], axis=1).astype(BF16)
            d["m"] = _dot(d["tm"].astype(BF16), zz)
        for d in cs:
            rm = _dot(d["rbm"], d["m"].astype(BF16))
            rkv = _dot(d["rkm"], d["v2"].astype(BF16))
            n1_s[d["c"], d["j"]] = (d["rb2"] - rm[:, :LANES]).astype(BF16)
            n2_s[d["c"], d["j"]] = rkv - rm[:, LANES:]
        for d in cs:
            m1 = d["m"][:, :LANES].astype(BF16)
            m2 = d["m"][:, LANES:]
            w1 = _dot_tn(m1, d["be2"].astype(BF16)) * d["gl"]
            lhs = jnp.concatenate([d["v2"], -m2], axis=0).astype(BF16)
            rhs = jnp.concatenate([d["kb2"], d["be2"]], axis=0).astype(BF16)
            w1_s[d["c"], d["j"]] = w1.astype(BF16)
            c0_s[d["c"], d["j"]] = _dot_tn(lhs, rhs) * d["gl"]
        return carry

    lax.fori_loop(0, nchunk // group, phase_a, 0)

    def phase_b(c, carry):
        gl = gam_s[pl.ds(c * L + (L - 1), 1), :]
        outs = []
        for j in range(RWKV_PAIRS):
            sb = sfin_ref[0, j]
            sbb = sb.astype(BF16)
            o2 = _dot_nt(n1_s[c, j], sbb) + n2_s[c, j]
            outs.append(o2[:L] + o2[L:])
            sfin_ref[0, j] = sb * gl[:, j * LANES:(j + 1) * LANES] - _dot(sbb, w1_s[c, j]) + c0_s[c, j]
        o_s[pl.ds(pl.multiple_of(c * L, L), L), :] = jnp.concatenate(outs, axis=1)
        return carry

    lax.fori_loop(0, nchunk, phase_b, 0)

    o = o_s[...]
    inv_n = 1.0 / RWKV_HEAD
    mu_h = headsum(o) * inv_n
    oc = o - mu_h
    var_h = headsum(oc * oc) * inv_n
    on = oc * lax.rsqrt(var_h + RWKV_GN_EPS) * lnw_ref[...] + lnb_ref[...]
    o_ref[0] = ((on + bonus_s[...]) * g_s[...]).astype(o_ref.dtype)


def _rwkv(pa, shift_prev, s0_pairs, prm, tb, chunk, group):
    b, s, _ = pa.shape
    nchunk = tb // chunk
    row = lambda n: pl.BlockSpec((1, n), lambda i, j: (0, 0))
    mat = lambda m, n: pl.BlockSpec((m, n), lambda i, j: (0, 0))
    sspec = pl.BlockSpec((1, RWKV_PAIRS, LANES, LANES), lambda i, j: (i, 0, 0, 0))
    coef = lambda rows, dt: pltpu.VMEM((nchunk, RWKV_PAIRS, rows, LANES), dt)
    return pl.pallas_call(
        functools.partial(_rwkv_kernel, chunk=chunk, group=group),
        grid=(b, s // tb),
        in_specs=[pl.BlockSpec((1, tb, C_RWKV), lambda i, j: (i, j, 0)),
                  pl.BlockSpec((1, 1, C_RWKV), lambda i, j: (i, 0, 0)),
                  sspec,
                  row(C_RWKV), row(D_RWKV), mat(W_LORA, D_RWKV), row(D_RWKV), mat(A_LORA, D_RWKV),
                  mat(G_LORA, D_RWKV), row(D_RWKV), row(D_RWKV), row(D_RWKV), row(D_RWKV), row(D_RWKV),
                  mat(D_RWKV, D_RWKV)],
        out_specs=[pl.BlockSpec((1, tb, D_RWKV), lambda i, j: (i, j, 0)), sspec],
        out_shape=[jax.ShapeDtypeStruct((b, s, D_RWKV), BF16),
                   jax.ShapeDtypeStruct((b, RWKV_PAIRS, LANES, LANES), F32)],
        scratch_shapes=[pltpu.VMEM((1, C_RWKV), F32)] + [pltpu.VMEM((tb, D_RWKV), F32)] * 9
        + [coef(LANES, BF16), coef(2 * chunk, BF16), coef(LANES, F32), coef(2 * chunk, F32)],
        compiler_params=_params(("arbitrary", "arbitrary")),
        name="rwkv",
    )(pa, shift_prev, s0_pairs, prm["mu"], prm["w0"], prm["w2"], prm["a0"], prm["a2"], prm["g2"],
      prm["k_k"], prm["k_a"], prm["r_k"], prm["ln_w"], prm["ln_b"], prm["hsum"])


def _state_to_pairs(s):
    b = s.shape[0]
    s = s.reshape(b, RWKV_PAIRS, 2, RWKV_HEAD, RWKV_HEAD)
    z = jnp.zeros_like(s[:, :, 0])
    top = jnp.concatenate([s[:, :, 0], z], axis=-1)
    bot = jnp.concatenate([z, s[:, :, 1]], axis=-1)
    return jnp.concatenate([top, bot], axis=-2)


def _pairs_to_state(sp):
    b = sp.shape[0]
    h = RWKV_HEAD
    s = jnp.stack([sp[:, :, :h, :h], sp[:, :, h:, h:]], axis=2)
    return s.reshape(b, 2 * RWKV_PAIRS, h, h)


def _diff_finish(accs, ls, lam, sub, lam_init):
    o = accs[0] / ls[0] - lam * (accs[1] / ls[1])
    o = o * lax.rsqrt(jnp.mean(o * o, axis=-1, keepdims=True) + RMS_EPS)
    return o * sub * (1.0 - lam_init)


def _attn_kernel(lam_ref, qt_ref, k0_ref, k1_ref, vt_ref, tab_ref, sub_ref, o_ref, acc_ref, s_ref, m_ref, *, tq,
                 lam_init):
    h = pl.program_id(1)
    slope = lam_ref[1 + h] * LOG2E
    lam = lam_ref[0]
    row = lax.broadcasted_iota(jnp.int32, (LANES, 1), 0)
    ones_rows = (row >= DIFF_QK) & (row < DIFF_QK + 3)
    fill = jnp.where(ones_rows, 1.0, 0.0).astype(BF16)
    k_refs = (k0_ref, k1_ref)

    def rhs(q_first):
        return jnp.where(row < DIFF_QK, q_first, fill)

    def q_tile(qi, carry):
        qt = qt_ref[0, 0, :, pl.ds(pl.multiple_of(qi * tq, tq), tq)]
        rs = (rhs(qt), rhs(jnp.concatenate([qt[DIFF_QK:], qt[:DIFF_QK]], axis=0)))
        acc_ref[...] = jnp.zeros_like(acc_ref)
        m_ref[...] = jnp.full_like(m_ref, NEG_INF)

        def scores(j, slot):
            ds = pl.ds(pl.multiple_of(j * tq, tq), tq)
            for c in range(2):
                s_ref[slot, c] = _dot(k_refs[c][0, 0, ds, :], rs[c])

        def softmax_pv(j, slot, diag):
            vt = vt_ref[0, 0, :, pl.ds(pl.multiple_of(j * tq, tq), tq)]
            off = slope * jnp.asarray((j - qi) * tq, F32)
            for c in range(2):
                s = s_ref[slot, c]
                if diag:
                    s = s + tab_ref[0]
                m = m_ref[c]
                m_new = jnp.maximum(m, jnp.max(s, axis=0, keepdims=True) + off)
                alpha = jnp.exp2(m - m_new)
                p = jnp.exp2(s - (m_new - off))
                m_ref[c] = m_new
                acc_ref[c] = alpha * acc_ref[c] + _dot(vt, p.astype(BF16))

        def pair(jj, c):
            j = 2 * jj
            scores(j + 1, 0)
            softmax_pv(j, 1, False)
            scores(jnp.minimum(j + 2, jnp.maximum(qi - 1, 0)), 1)
            softmax_pv(j + 1, 0, False)
            return c

        scores(qi, 0)
        scores(0, 1)
        softmax_pv(qi, 0, True)
        lax.fori_loop(0, qi // 2, pair, 0)

        @pl.when(lax.rem(qi, 2) == 1)
        def _():
            softmax_pv(qi - 1, 1, False)

        a0 = acc_ref[0]
        a1 = acc_ref[1]
        o = a0[:DIFF_V] / a0[DIFF_V:DIFF_V + 1] - lam * (a1[:DIFF_V] / a1[DIFF_V:DIFF_V + 1])
        o = o * lax.rsqrt(jnp.mean(o * o, axis=0, keepdims=True) + RMS_EPS)
        o = o * (sub_ref[...] * (1.0 - lam_init))
        o_ref[0, pl.ds(pl.multiple_of(qi * tq, tq), tq), :] = o.T.astype(o_ref.dtype)
        return carry

    lax.fori_loop(0, qt_ref.shape[3] // tq, q_tile, 0)


def _attn_diag_table(tq):
    slopes = jnp.asarray(ALIBI_SLOPES, F32) * LOG2E
    pos = jnp.arange(tq, dtype=jnp.int32)
    kp, qp = pos[:, None], pos[None, :]
    rel = jnp.where(kp <= qp, 0, 2 * (qp - kp)).astype(F32)
    vis = (kp // CHUNK) <= (qp // CHUNK)
    return jnp.where(vis[None], slopes[:, None, None] * rel[None], NEG_INF)


def _attn_prompt(lam, qt, k0, k1, vt, subln_col, tq, lam_init):
    b, h, _, s = qt.shape
    kspec = pl.BlockSpec((1, 1, s, LANES), lambda i, j: (i, j, 0, 0))
    return pl.pallas_call(
        functools.partial(_attn_kernel, tq=tq, lam_init=lam_init),
        grid=(b, h),
        in_specs=[pl.BlockSpec(memory_space=pltpu.SMEM),
                  pl.BlockSpec((1, 1, LANES, s), lambda i, j: (i, j, 0, 0)),
                  kspec, kspec,
                  pl.BlockSpec((1, 1, LANES + VT_PAD, s), lambda i, j: (i, j, 0, 0)),
                  pl.BlockSpec((1, tq, tq), lambda i, j: (j, 0, 0)),
                  pl.BlockSpec((DIFF_V, 1), lambda i, j: (0, 0))],
        out_specs=pl.BlockSpec((1, s, LANES), lambda i, j: (i, 0, j)),
        out_shape=jax.ShapeDtypeStruct((b, s, D_DIFF), BF16),
        scratch_shapes=[pltpu.VMEM((2, DIFF_V + VT_PAD, tq), F32), pltpu.VMEM((2, 2, tq, tq), F32),
                        pltpu.VMEM((2, 1, tq), F32)],
        compiler_params=_params(("arbitrary", "arbitrary")),
        name="attn_prompt",
    )(lam, qt, k0, k1, vt, _attn_diag_table(tq), subln_col)


def _attn_cached_kernel(lam_ref, q_ref, kn_ref, vn_ref, kp_ref, vp_ref, sub_ref, o_ref, *, lam_init):
    h = pl.program_id(1)
    slope = lam_ref[1 + h]
    s_new = q_ref.shape[2]
    past = kp_ref.shape[2]
    q = q_ref[0, 0]
    kp = kp_ref[0, 0].astype(BF16)
    vp = vp_ref[0, 0].astype(BF16)
    kn = kn_ref[0, 0]
    vn = vn_ref[0, 0]
    q_pos = past + lax.broadcasted_iota(jnp.int32, (s_new, 1), 0)

    def bias(k_pos):
        dist = jnp.abs(q_pos - k_pos).astype(F32)
        vis = (k_pos // CHUNK) <= (q_pos // CHUNK)
        return jnp.where(vis, -slope * dist, NEG_INF), vis

    b_past, vis_past = bias(lax.broadcasted_iota(jnp.int32, (1, past), 1))
    b_new, vis_new = bias(past + lax.broadcasted_iota(jnp.int32, (1, s_new), 1))
    accs, ls = [], []
    for c in range(2):
        qc = q[:, c * DIFF_QK:(c + 1) * DIFF_QK]
        sp = jnp.where(vis_past, _dot_nt(qc, kp[:, c * DIFF_QK:(c + 1) * DIFF_QK]) + b_past, NEG_INF)
        sn = jnp.where(vis_new, _dot_nt(qc, kn[:, c * DIFF_QK:(c + 1) * DIFF_QK]) + b_new, NEG_INF)
        m = jnp.maximum(jnp.max(sp, axis=-1, keepdims=True), jnp.max(sn, axis=-1, keepdims=True))
        pp = jnp.exp(sp - m)
        pn = jnp.exp(sn - m)
        ls.append(jnp.sum(pp, axis=-1, keepdims=True) + jnp.sum(pn, axis=-1, keepdims=True))
        accs.append(_dot(pp.astype(BF16), vp) + _dot(pn.astype(BF16), vn))
    o = _diff_finish(accs, ls, lam_ref[0], sub_ref[...], lam_init)
    o_ref[0] = o.astype(o_ref.dtype)


def _attn_cached(lam, q, kb, vb, past_k, past_v, subln, lam_init):
    b, h, s, _ = q.shape
    past = past_k.shape[2]
    new = pl.BlockSpec((1, 1, s, LANES), lambda i, j: (i, j, 0, 0))
    old = pl.BlockSpec((1, 1, past, LANES), lambda i, j: (i, j, 0, 0))
    return pl.pallas_call(
        functools.partial(_attn_cached_kernel, lam_init=lam_init),
        grid=(b, h),
        in_specs=[pl.BlockSpec(memory_space=pltpu.SMEM), new, new, new, old, old,
                  pl.BlockSpec((1, DIFF_V), lambda i, j: (0, 0))],
        out_specs=pl.BlockSpec((1, s, LANES), lambda i, j: (i, 0, j)),
        out_shape=jax.ShapeDtypeStruct((b, s, D_DIFF), BF16),
        compiler_params=_params(("arbitrary", "arbitrary")),
        name="attn_cached",
    )(lam, q, kb, vb, past_k, past_v, subln)


def _route(lg):
    tm = lg.shape[0]
    lane = lax.broadcasted_iota(jnp.int32, (tm, LANES), 1)
    lane_f = lane.astype(F32)
    big = 1e9
    low = -3e38
    is_g = (lane >= N_EXPERTS) & (lane < N_EXPERTS + N_GROUPS)
    gl = jnp.where(is_g, lg, low)
    gmax = jnp.max(gl, axis=-1, keepdims=True)
    grp_lane = jnp.min(jnp.where(gl == gmax, lane_f, big), axis=-1, keepdims=True)
    gsum = jnp.sum(jnp.where(is_g, jnp.exp(gl - gmax), 0.0), axis=-1, keepdims=True)
    g_prob = 1.0 / gsum
    grp = grp_lane - float(N_EXPERTS)
    lane_grp = (lane // EXPERTS_PER_GROUP).astype(F32)
    el = jnp.where(lane_grp == grp, lg, low)
    v1 = jnp.max(el, axis=-1, keepdims=True)
    i1 = jnp.min(jnp.where(el == v1, lane_f, big), axis=-1, keepdims=True)
    el2 = jnp.where(lane_f == i1, low, el)
    v2 = jnp.max(el2, axis=-1, keepdims=True)
    i2 = jnp.min(jnp.where(el2 == v2, lane_f, big), axis=-1, keepdims=True)
    e21 = jnp.exp(v2 - v1)
    den = 1.0 + e21
    gate1 = g_prob / den
    gate2 = g_prob * e21 / den
    out = jnp.where(lane == 0, i1, jnp.where(lane == 1, i2, jnp.where(lane == 2, gate1,
                    jnp.where(lane == 3, gate2, 0.0))))
    return out


def _post_kernel(x_ref, oa_ref, ob_ref, mk_ref, mv_ref, wout_ref, wq_ref, wo_ref,
                 g1_ref, b1_ref, g2_ref, b2_ref, wrh_ref, wrl_ref, br_ref, hx_ref, *, nsplit):
    tm = x_ref.shape[1]
    rows = tm // nsplit
    parts = [slice(i * rows, (i + 1) * rows) for i in range(nsplit)]
    mix = [_dot(oa_ref[0, r, :], wout_ref[:D_RWKV, :]) + _dot(ob_ref[0, r, :], wout_ref[D_RWKV:, :]) for r in parts]
    h1 = [_layer_norm(DEEPNORM_ALPHA * x_ref[0, r, :] + m, g1_ref[...], b1_ref[...]) for r, m in zip(parts, mix)]
    q = [(_dot(h.astype(BF16), wq_ref[...]) * (MEM_HEAD ** -0.5)).astype(BF16) for h in h1]
    heads = [[] for _ in parts]
    for h in range(MEM_HEADS):
        hs = slice(h * MEM_HEAD, (h + 1) * MEM_HEAD)
        for i in range(nsplit):
            s = _dot_nt(q[i][:, hs], mk_ref[0, :, hs])
            m = jnp.max(s, axis=-1, keepdims=True)
            p = jnp.exp(s - m)
            p = p / jnp.sum(p, axis=-1, keepdims=True)
            heads[i].append(_dot(p.astype(BF16), mv_ref[0, :, hs]).astype(BF16))
    att = [_dot(jnp.concatenate(hd, axis=-1), wo_ref[...]) for hd in heads]
    h2 = [_layer_norm(DEEPNORM_ALPHA * h + a, g2_ref[...], b2_ref[...]) for h, a in zip(h1, att)]
    for r, h in zip(parts, h2):
        hx_ref[0, r, :D_MODEL] = h
        hi, lo = _split2(h)
        lg = _dot(hi, wrh_ref[...]) + _dot(hi, wrl_ref[...]) + _dot(lo, wrh_ref[...]) + br_ref[...]
        hx_ref[0, r, D_MODEL:] = _route(lg)


def _post(x, oa, ob, mk, mv, prm, tm, nsplit):
    b, s, d = x.shape
    n = mk.shape[1]
    tok = lambda w: pl.BlockSpec((1, tm, w), lambda i, j: (i, j, 0))
    mem = pl.BlockSpec((1, n, d), lambda i, j: (i, 0, 0))
    mat = lambda m_, n_: pl.BlockSpec((m_, n_), lambda i, j: (0, 0))
    return pl.pallas_call(
        functools.partial(_post_kernel, nsplit=nsplit),
        grid=(b, s // tm),
        in_specs=[tok(d), tok(D_RWKV), tok(D_DIFF), mem, mem, mat(d, d), mat(d, d), mat(d, d),
                  mat(1, d), mat(1, d), mat(1, d), mat(1, d), mat(d, LANES), mat(d, LANES), mat(1, LANES)],
        out_specs=tok(d + LANES),
        out_shape=jax.ShapeDtypeStruct((b, s, d + LANES), F32),
        compiler_params=_params(("arbitrary", "arbitrary")),
        name="post",
    )(x, oa, ob, mk, mv, prm["w_out"], prm["wq"], prm["wo"], prm["ln1_g"], prm["ln1_b"],
      prm["ln2_g"], prm["ln2_b"], prm["wr_hi"], prm["wr_lo"], prm["br"])


def _moe_kernel(bexp_ref, nused_ref, tok_ref, tokn_ref, h2_hbm, w1_ref, w3_ref, w2_ref, y_ref, xbuf, sem):
    i = pl.program_id(0)
    nblk = pl.num_programs(0)
    nused = nused_ref[0]
    ngrp = xbuf.shape[1]
    tb = ngrp * SUBLANES
    slot = lax.rem(i, 2)

    def issue_rows(idx_ref, sl):
        def group(g, c):
            for u in range(SUBLANES):
                r = idx_ref[0, 0, g * SUBLANES + u]
                pltpu.make_async_copy(h2_hbm.at[pl.ds(r, 1)], xbuf.at[sl, g, pl.ds(u, 1)], sem.at[sl]).start()
            return c
        lax.fori_loop(0, ngrp, group, 0)

    def wait_rows(sl):
        pltpu.make_async_copy(xbuf.at[1 - sl], xbuf.at[sl], sem.at[sl]).wait()

    @pl.when(i == 0)
    def _():
        issue_rows(tok_ref, 0)

    @pl.when(i < nused)
    def _():
        wait_rows(slot)
        issue_rows(tokn_ref, 1 - slot)
        x = xbuf[slot].reshape(tb, xbuf.shape[3]).astype(BF16)
        a = _dot(x, w1_ref[0])
        g = _dot(x, w3_ref[0])
        hmid = (a / (1.0 + jnp.exp(-a))) * g
        y_ref[...] = _dot(hmid.astype(BF16), w2_ref[0])

        @pl.when(i == nblk - 1)
        def _():
            wait_rows(1 - slot)

    @pl.when(i >= nused)
    def _():
        y_ref[...] = jnp.zeros_like(y_ref)

        @pl.when(i == nused)
        def _():
            wait_rows(slot)


def _moe_ffn(h2f, blk_exp, nused, tok, w1, w3, w2, tb):
    nblk = tok.shape[0]
    d = h2f.shape[1]
    grid_spec = pltpu.PrefetchScalarGridSpec(
        num_scalar_prefetch=2,
        grid=(nblk,),
        in_specs=[pl.BlockSpec((1, 1, tb), lambda i, be, nu: (i, 0, 0), memory_space=pltpu.SMEM),
                  pl.BlockSpec((1, 1, tb), lambda i, be, nu: (jnp.minimum(i + 1, nblk - 1), 0, 0),
                               memory_space=pltpu.SMEM),
                  pl.BlockSpec(memory_space=pl.ANY),
                  pl.BlockSpec((1, d, D_EXPERT), lambda i, be, nu: (be[i], 0, 0)),
                  pl.BlockSpec((1, d, D_EXPERT), lambda i, be, nu: (be[i], 0, 0)),
                  pl.BlockSpec((1, D_EXPERT, d), lambda i, be, nu: (be[i], 0, 0))],
        out_specs=pl.BlockSpec((tb, d), lambda i, be, nu: (i, 0)),
        scratch_shapes=[pltpu.VMEM((2, tb // SUBLANES, SUBLANES, d), F32), pltpu.SemaphoreType.DMA((2,))],
    )
    return pl.pallas_call(
        _moe_kernel,
        grid_spec=grid_spec,
        out_shape=jax.ShapeDtypeStruct((nblk * tb, d), F32),
        compiler_params=_params(("arbitrary",)),
        name="moe_ffn",
    )(blk_exp, nused, tok, tok, h2f, w1, w3, w2)


def _combine_kernel(pos_ref, posn_ref, ys_hbm, h2_ref, route_ref, g_ref, b_ref, y_ref, gbuf, sem):
    i = pl.program_id(0)
    n = pl.num_programs(0)
    tm = h2_ref.shape[0]
    slot = lax.rem(i, 2)

    ngrp = gbuf.shape[1]
    d = gbuf.shape[3]

    def issue_rows(idx_ref, sl):
        def group(g, c):
            for u in range(SUBLANES):
                r = idx_ref[0, 0, g * SUBLANES + u]
                pltpu.make_async_copy(ys_hbm.at[pl.ds(r, 1)], gbuf.at[sl, g, pl.ds(u, 1)], sem.at[sl]).start()
            return c
        lax.fori_loop(0, ngrp, group, 0)

    def wait_rows(sl):
        pltpu.make_async_copy(gbuf.at[1 - sl], gbuf.at[sl], sem.at[sl]).wait()

    @pl.when(i == 0)
    def _():
        issue_rows(pos_ref, 0)

    wait_rows(slot)
    issue_rows(posn_ref, 1 - slot)
    rt = route_ref[...]
    half = ngrp // 2
    moe = (rt[:, 2:3] * gbuf[slot, :half].reshape(tm, d) + rt[:, 3:4] * gbuf[slot, half:].reshape(tm, d))
    y_ref[...] = _layer_norm(DEEPNORM_ALPHA * h2_ref[...] + moe, g_ref[...], b_ref[...])

    @pl.when(i == n - 1)
    def _():
        wait_rows(1 - slot)


def _combine(pos, ysort, h2f, route, g, b, tm):
    t, d = h2f.shape
    n = t // tm
    return pl.pallas_call(
        _combine_kernel,
        grid=(n,),
        in_specs=[pl.BlockSpec((1, 1, 2 * tm), lambda i: (i, 0, 0), memory_space=pltpu.SMEM),
                  pl.BlockSpec((1, 1, 2 * tm), lambda i: (jnp.minimum(i + 1, n - 1), 0, 0),
                               memory_space=pltpu.SMEM),
                  pl.BlockSpec(memory_space=pl.ANY),
                  pl.BlockSpec((tm, d), lambda i: (i, 0)),
                  pl.BlockSpec((tm, LANES), lambda i: (i, 0)),
                  pl.BlockSpec((1, d), lambda i: (0, 0)),
                  pl.BlockSpec((1, d), lambda i: (0, 0))],
        out_specs=pl.BlockSpec((tm, d), lambda i: (i, 0)),
        out_shape=jax.ShapeDtypeStruct((t, d), F32),
        scratch_shapes=[pltpu.VMEM((2, 2 * tm // SUBLANES, SUBLANES, d), F32), pltpu.SemaphoreType.DMA((2,))],
        compiler_params=_params(("arbitrary",)),
        name="combine",
    )(pos, pos, ysort, h2f, route, g, b)


def _dispatch(expert, tb, tm):
    t = expert.shape[0]
    a = 2 * t
    ef = expert.reshape(a)
    order = jnp.argsort(ef, stable=True).astype(jnp.int32)
    es = ef[order]
    counts = jnp.sum((ef[:, None] == jnp.arange(N_EXPERTS, dtype=jnp.int32)[None, :]).astype(jnp.int32), axis=0)
    starts = jnp.cumsum(counts) - counts
    nb = (counts + tb - 1) // tb
    bend = jnp.cumsum(nb)
    bstart = bend - nb
    nblk = -(-a // tb) + N_EXPERTS
    blk = jnp.arange(nblk, dtype=jnp.int32)
    bexp = jnp.minimum(jnp.sum((blk[:, None] >= bend[None, :]).astype(jnp.int32), axis=1), N_EXPERTS - 1)
    row0 = starts[bexp] + (blk - bstart[bexp]) * tb
    idx = jnp.clip(row0[:, None] + jnp.arange(tb, dtype=jnp.int32)[None, :], 0, a - 1)
    tok = (order[idx] // 2).astype(jnp.int32).reshape(nblk, 1, tb)
    dest_sorted = bstart[es] * tb + (jnp.arange(a, dtype=jnp.int32) - starts[es])
    _, pos = lax.sort_key_val(order, dest_sorted.astype(jnp.int32))
    pos = pos.reshape(t, 2)
    pos = pos.reshape(t // tm, tm, 2).transpose(0, 2, 1).reshape(t // tm, 1, 2 * tm)
    return bexp, bend[-1:].astype(jnp.int32), tok, pos


def _moe(hx, prm, tb, tm):
    b, s, _ = hx.shape
    d = D_MODEL
    t = b * s
    h2f = hx[..., :d].reshape(t, d)
    rf = hx[..., d:].reshape(t, LANES)
    expert = rf[:, :2].astype(jnp.int32)
    bexp, nused, tok, pos = _dispatch(expert, tb, tm)
    ysort = _moe_ffn(h2f, bexp, nused, tok, prm["moe_w1"], prm["moe_w3"], prm["moe_w2"], tb)
    y = _combine(pos, ysort, h2f, rf, prm["ln3_g"], prm["ln3_b"], tm)
    return y.reshape(b, s, d)


N_PAIR_CLASSES = N_GROUPS * EXPERTS_PER_GROUP * EXPERTS_PER_GROUP
N_REAL_PAIRS = N_GROUPS * (EXPERTS_PER_GROUP * (EXPERTS_PER_GROUP - 1) // 2)


def _moe_pair_kernel(elo_ref, ehi_ref, nused_ref, tok_ref, tokn_ref, h2_hbm,
                     w1a_ref, w3a_ref, w2a_ref, w1b_ref, w3b_ref, w2b_ref, y_ref, xbuf, sem):
    i = pl.program_id(0)
    nblk = pl.num_programs(0)
    nused = nused_ref[0]
    ngrp = xbuf.shape[1]
    tb = ngrp * SUBLANES
    slot = lax.rem(i, 2)

    def issue_rows(idx_ref, sl):
        def group(g, c):
            for u in range(SUBLANES):
                r = idx_ref[0, 0, g * SUBLANES + u]
                pltpu.make_async_copy(h2_hbm.at[pl.ds(r, 1)], xbuf.at[sl, g, pl.ds(u, 1)], sem.at[sl]).start()
            return c
        lax.fori_loop(0, ngrp, group, 0)

    def wait_rows(sl):
        pltpu.make_async_copy(xbuf.at[1 - sl], xbuf.at[sl], sem.at[sl]).wait()

    def ffn(x, w1_ref, w3_ref, w2_ref):
        a = _dot(x, w1_ref[0])
        g = _dot(x, w3_ref[0])
        hmid = (a / (1.0 + jnp.exp(-a))) * g
        return _dot(hmid.astype(BF16), w2_ref[0])

    @pl.when(i == 0)
    def _():
        issue_rows(tok_ref, 0)

    @pl.when(i < nused)
    def _():
        wait_rows(slot)
        issue_rows(tokn_ref, 1 - slot)
        rows = xbuf[slot].reshape(tb, xbuf.shape[3])
        x = rows[:, :D_MODEL].astype(BF16)
        rt = rows[:, D_MODEL:]
        first_low = rt[:, 0:1] < rt[:, 1:2]
        g_lo = jnp.where(first_low, rt[:, 2:3], rt[:, 3:4])
        g_hi = jnp.where(first_low, rt[:, 3:4], rt[:, 2:3])
        y_ref[...] = g_lo * ffn(x, w1a_ref, w3a_ref, w2a_ref) + g_hi * ffn(x, w1b_ref, w3b_ref, w2b_ref)

        @pl.when(i == nblk - 1)
        def _():
            wait_rows(1 - slot)

    @pl.when(i >= nused)
    def _():
        y_ref[...] = jnp.zeros_like(y_ref)

        @pl.when(i == nused)
        def _():
            wait_rows(slot)


def _moe_pair_ffn(hxf, elo, ehi, nused, tok, w1, w3, w2, tb):
    nblk = tok.shape[0]
    d = D_MODEL
    wspec = lambda rows, cols, which: pl.BlockSpec(
        (1, rows, cols), (lambda i, lo, hi, nu: (lo[i], 0, 0)) if which == 0 else (lambda i, lo, hi, nu: (hi[i], 0, 0)))
    grid_spec = pltpu.PrefetchScalarGridSpec(
        num_scalar_prefetch=3,
        grid=(nblk,),
        in_specs=[pl.BlockSpec((1, 1, tb), lambda i, lo, hi, nu: (i, 0, 0), memory_space=pltpu.SMEM),
                  pl.BlockSpec((1, 1, tb), lambda i, lo, hi, nu: (jnp.minimum(i + 1, nblk - 1), 0, 0),
                               memory_space=pltpu.SMEM),
                  pl.BlockSpec(memory_space=pl.ANY),
                  wspec(d, D_EXPERT, 0), wspec(d, D_EXPERT, 0), wspec(D_EXPERT, d, 0),
                  wspec(d, D_EXPERT, 1), wspec(d, D_EXPERT, 1), wspec(D_EXPERT, d, 1)],
        out_specs=pl.BlockSpec((tb, d), lambda i, lo, hi, nu: (i, 0)),
        scratch_shapes=[pltpu.VMEM((2, tb // SUBLANES, SUBLANES, hxf.shape[1]), F32),
                        pltpu.SemaphoreType.DMA((2,))],
    )
    return pl.pallas_call(
        _moe_pair_kernel,
        grid_spec=grid_spec,
        out_shape=jax.ShapeDtypeStruct((nblk * tb, d), F32),
        compiler_params=_params(("arbitrary",)),
        name="moe_pair_ffn",
    )(elo, ehi, nused, tok, tok, hxf, w1, w3, w2, w1, w3, w2)


def _combine1_kernel(pos_ref, posn_ref, ys_hbm, hx_ref, g_ref, b_ref, y_ref, gbuf, sem):
    i = pl.program_id(0)
    n = pl.num_programs(0)
    tm = hx_ref.shape[0]
    slot = lax.rem(i, 2)
    ngrp = gbuf.shape[1]

    def issue_rows(idx_ref, sl):
        def group(g, c):
            for u in range(SUBLANES):
                r = idx_ref[0, 0, g * SUBLANES + u]
                pltpu.make_async_copy(ys_hbm.at[pl.ds(r, 1)], gbuf.at[sl, g, pl.ds(u, 1)], sem.at[sl]).start()
            return c
        lax.fori_loop(0, ngrp, group, 0)

    def wait_rows(sl):
        pltpu.make_async_copy(gbuf.at[1 - sl], gbuf.at[sl], sem.at[sl]).wait()

    @pl.when(i == 0)
    def _():
        issue_rows(pos_ref, 0)

    wait_rows(slot)
    issue_rows(posn_ref, 1 - slot)
    moe = gbuf[slot].reshape(tm, gbuf.shape[3])
    y_ref[...] = _layer_norm(DEEPNORM_ALPHA * hx_ref[:, :D_MODEL] + moe, g_ref[...], b_ref[...])

    @pl.when(i == n - 1)
    def _():
        wait_rows(1 - slot)


def _combine1(pos, ysort, hxf, g, b, tm):
    t, dx = hxf.shape
    d = D_MODEL
    n = t // tm
    return pl.pallas_call(
        _combine1_kernel,
        grid=(n,),
        in_specs=[pl.BlockSpec((1, 1, tm), lambda i: (i, 0, 0), memory_space=pltpu.SMEM),
                  pl.BlockSpec((1, 1, tm), lambda i: (jnp.minimum(i + 1, n - 1), 0, 0), memory_space=pltpu.SMEM),
                  pl.BlockSpec(memory_space=pl.ANY),
                  pl.BlockSpec((tm, dx), lambda i: (i, 0)),
                  pl.BlockSpec((1, d), lambda i: (0, 0)),
                  pl.BlockSpec((1, d), lambda i: (0, 0))],
        out_specs=pl.BlockSpec((tm, d), lambda i: (i, 0)),
        out_shape=jax.ShapeDtypeStruct((t, d), F32),
        scratch_shapes=[pltpu.VMEM((2, tm // SUBLANES, SUBLANES, d), F32), pltpu.SemaphoreType.DMA((2,))],
        compiler_params=_params(("arbitrary",)),
        name="combine1",
    )(pos, pos, ysort, hxf, g, b)


def _dispatch_pairs(e1, e2, tb, tm):
    t = e1.shape[0]
    lo = jnp.minimum(e1, e2)
    hi = jnp.maximum(e1, e2)
    epg = EXPERTS_PER_GROUP
    cls = (lo // epg) * (epg * epg) + (lo % epg) * epg + (hi % epg)
    cs, order = lax.sort((cls, jnp.arange(t, dtype=jnp.int32)), num_keys=1, is_stable=True)
    counts = jnp.sum((cls[:, None] == jnp.arange(N_PAIR_CLASSES, dtype=jnp.int32)[None, :]).astype(jnp.int32), axis=0)
    starts = jnp.cumsum(counts) - counts
    nb = (counts + tb - 1) // tb
    bend = jnp.cumsum(nb)
    bstart = bend - nb
    nblk = -(-t // tb) + N_REAL_PAIRS
    blk = jnp.arange(nblk, dtype=jnp.int32)
    bcls = jnp.minimum(jnp.sum((blk[:, None] >= bend[None, :]).astype(jnp.int32), axis=1), N_PAIR_CLASSES - 1)
    grp = bcls // (epg * epg)
    elo = (grp * epg + (bcls % (epg * epg)) // epg).astype(jnp.int32)
    ehi = (grp * epg + bcls % epg).astype(jnp.int32)
    row0 = starts[bcls] + (blk - bstart[bcls]) * tb
    idx = jnp.clip(row0[:, None] + jnp.arange(tb, dtype=jnp.int32)[None, :], 0, t - 1)
    tok = order[idx].astype(jnp.int32)
    delta = bstart * tb - starts
    onehot = cs[:, None] == jnp.arange(N_PAIR_CLASSES, dtype=jnp.int32)[None, :]
    dest_sorted = jnp.arange(t, dtype=jnp.int32) + jnp.sum(jnp.where(onehot, delta[None, :], 0), axis=1)
    _, pos = lax.sort_key_val(order, dest_sorted.astype(jnp.int32))
    return elo, ehi, bend[-1:].astype(jnp.int32), tok.reshape(nblk, 1, tb), pos.reshape(t // tm, 1, tm)


def _moe_pairs(hx, prm, tb, tm):
    b, s, dx = hx.shape
    t = b * s
    hxf = hx.reshape(t, dx)
    experts = hxf[:, D_MODEL:D_MODEL + 2].astype(jnp.int32)
    elo, ehi, nused, tok, pos = _dispatch_pairs(experts[:, 0], experts[:, 1], tb, tm)
    ysort = _moe_pair_ffn(hxf, elo, ehi, nused, tok, prm["moe_w1"], prm["moe_w3"], prm["moe_w2"], tb)
    y = _combine1(pos, ysort, hxf, prm["ln3_g"], prm["ln3_b"], tm)
    return y.reshape(b, s, D_MODEL)


def _trunk(x, shift_prev, state0, past_k, past_v, mk, mv, prm, lam, lam_init, cfg):
    b, s, _ = x.shape
    pa, q, k, v, kb, vb, kb1 = _inproj(x, prm["w_in"], cfg["in_bb"], cfg["in_ts"],
                                       cfg["attn_tq"] if past_k is None else 0)
    oa, sfin = _rwkv(pa, shift_prev, _state_to_pairs(state0), prm, cfg["rwkv_tb"], cfg["rwkv_chunk"],
                     cfg["rwkv_group"])
    if past_k is None:
        ob = _attn_prompt(lam, q, kb, kb1, vb, prm["subln"].reshape(DIFF_V, 1), cfg["attn_tq"], lam_init)
    else:
        ob = _attn_cached(lam, q, kb, vb, past_k, past_v, prm["subln"], lam_init)
    hx = _post(x, oa, ob, mk, mv, prm, cfg["post_tm"], cfg["post_split"])
    moe = _moe_pairs if cfg["moe_pairs"] else _moe
    y = moe(hx, prm, cfg["moe_tb"], cfg["comb_tm"])
    return y, k, v, _pairs_to_state(sfin), pa[:, s - 1:s, :]


def _prep(w, l):
    row = lambda a: a[l].reshape(1, -1).astype(F32)
    idx = jnp.arange(D_RWKV, dtype=jnp.int32) // RWKV_HEAD
    wr = jnp.zeros((D_MODEL, LANES), F32)
    wr = wr.at[:, :N_EXPERTS].set(w["moe_w_expert"][l]).at[:, N_EXPERTS:N_EXPERTS + N_GROUPS].set(w["moe_w_group"][l])
    wr_hi = wr.astype(BF16)
    br = jnp.zeros((1, LANES), F32)
    br = br.at[0, :N_EXPERTS].set(w["moe_b_expert"][l]).at[0, N_EXPERTS:N_EXPERTS + N_GROUPS].set(w["moe_b_group"][l])
    return {
        "w_in": w["w_in"][l].astype(BF16),
        "mu": row(w["rwkv_mu"]), "w0": row(w["rwkv_w0"]), "w2": w["rwkv_w2"][l].astype(BF16),
        "a0": row(w["rwkv_a0"]), "a2": w["rwkv_a2"][l].astype(BF16), "g2": w["rwkv_g2"][l].astype(BF16),
        "k_k": row(w["rwkv_k_k"]), "k_a": row(w["rwkv_k_a"]), "r_k": row(w["rwkv_r_k"]),
        "ln_w": row(w["rwkv_ln_w"]), "ln_b": row(w["rwkv_ln_b"]),
        "hsum": (idx[:, None] == idx[None, :]).astype(BF16),
        "subln": row(w["diff_subln"]),
        "w_out": w["w_out"][l].astype(BF16), "wq": w["mem_wq"][l].astype(BF16), "wo": w["mem_wo"][l].astype(BF16),
        "ln1_g": row(w["ln1_g"]), "ln1_b": row(w["ln1_b"]), "ln2_g": row(w["ln2_g"]), "ln2_b": row(w["ln2_b"]),
        "ln3_g": row(w["ln3_g"]), "ln3_b": row(w["ln3_b"]),
        "wr_hi": wr_hi, "wr_lo": (wr - wr_hi.astype(F32)).astype(BF16), "br": br,
        "moe_w1": w["moe_w1"][l].astype(BF16), "moe_w3": w["moe_w3"][l].astype(BF16),
        "moe_w2": w["moe_w2"][l].astype(BF16),
    }


def _tile(n, pref):
    return pref if n % pref == 0 else n


def kernel(x_prompt, x_sample, mem_prompt, cache_diff_k, cache_diff_v, cache_mem_k, cache_mem_v, state_rwkv, state_shift, w_in, rwkv_mu, rwkv_w0, rwkv_w2, rwkv_a0, rwkv_a2, rwkv_g2, rwkv_k_k, rwkv_k_a, rwkv_r_k, rwkv_ln_w, rwkv_ln_b, diff_lq1, diff_lk1, diff_lq2, diff_lk2, diff_subln, w_out, ln1_g, ln1_b, mem_wq, mem_wk, mem_wv, mem_wo, ln2_g, ln2_b, moe_w_group, moe_b_group, moe_w_expert, moe_b_expert, moe_w1, moe_w3, moe_w2, ln3_g, ln3_b):
    w = dict(w_in=w_in, rwkv_mu=rwkv_mu, rwkv_w0=rwkv_w0, rwkv_w2=rwkv_w2, rwkv_a0=rwkv_a0, rwkv_a2=rwkv_a2,
             rwkv_g2=rwkv_g2, rwkv_k_k=rwkv_k_k, rwkv_k_a=rwkv_k_a, rwkv_r_k=rwkv_r_k, rwkv_ln_w=rwkv_ln_w,
             rwkv_ln_b=rwkv_ln_b, diff_subln=diff_subln, w_out=w_out, ln1_g=ln1_g, ln1_b=ln1_b, mem_wq=mem_wq,
             mem_wo=mem_wo, ln2_g=ln2_g, ln2_b=ln2_b, moe_w_group=moe_w_group, moe_b_group=moe_b_group,
             moe_w_expert=moe_w_expert, moe_b_expert=moe_b_expert, moe_w1=moe_w1, moe_w3=moe_w3, moe_w2=moe_w2,
             ln3_g=ln3_g, ln3_b=ln3_b)
    bp, sp, _ = x_prompt.shape
    bs, ss, _ = x_sample.shape
    depth = w_in.shape[0]
    cfg_p = dict(in_bb=1, in_ts=_tile(sp, 512), rwkv_tb=_tile(sp, 256), rwkv_chunk=CHUNK, rwkv_group=_tile(sp, 256) // CHUNK,
                 attn_tq=_tile(sp, 512), post_tm=_tile(sp, 512), post_split=2, moe_tb=256, moe_pairs=True, comb_tm=_tile(bp * sp, 256))
    cfg_s = dict(in_bb=bs, in_ts=ss, rwkv_tb=ss, rwkv_chunk=ss, rwkv_group=1, attn_tq=ss, post_tm=ss, post_split=1, moe_tb=64, moe_pairs=False,
                 comb_tm=_tile(bs * ss, 256))
    yp, ys = x_prompt, x_sample
    outs = [[] for _ in range(10)]
    for l in range(depth):
        prm = _prep(w, l)
        lam_init = 0.8 - 0.6 * math.exp(-0.3 * l)
        f = lambda z: z[l].astype(F32)
        lam = (jnp.exp(jnp.sum(f(diff_lq1) * f(diff_lk1))) - jnp.exp(jnp.sum(f(diff_lq2) * f(diff_lk2)))
               + lam_init).reshape(1)
        slopes = jnp.asarray(ALIBI_SLOPES, F32)
        lam = jnp.concatenate([lam, slopes, jnp.zeros((3,), F32)])
        mk_p, mv_p, mkb, mvb = _memproj(mem_prompt, mem_wk[l].astype(BF16), mem_wv[l].astype(BF16))
        yp, k_p, v_p, st_p, sh_p = _trunk(
            yp, jnp.zeros((bp, 1, C_RWKV), F32), jnp.zeros((bp, 2 * RWKV_PAIRS, RWKV_HEAD, RWKV_HEAD), F32),
            None, None, mkb, mvb, prm, lam, lam_init, cfg_p)
        ys, k_s, v_s, st_s, sh_s = _trunk(
            ys, state_shift[l], state_rwkv[l], cache_diff_k[l], cache_diff_v[l],
            cache_mem_k[l].astype(BF16), cache_mem_v[l].astype(BF16), prm, lam, lam_init, cfg_s)
        for lst, val in zip(outs, (k_p, v_p, mk_p, mv_p, st_p, sh_p, k_s, v_s, st_s, sh_s)):
            lst.append(val)
    return (yp, ys) + tuple(jnp.stack(o) for o in outs)
```

```python
import functools
import math

import jax
import jax.numpy as jnp
from jax import lax
from jax.experimental import pallas as pl
from jax.experimental.pallas import tpu as pltpu

F32 = jnp.float32
BF16 = jnp.bfloat16

D_MODEL = 1024
CHUNK = 64
D_RWKV = 512
RWKV_HEAD = 64
RWKV_PAIRS = D_RWKV // (2 * RWKV_HEAD)
W_LORA = 64
A_LORA = 64
G_LORA = 128
C_RWKV = 3 * D_RWKV + W_LORA + A_LORA + G_LORA
D_DIFF = 512
DIFF_HEADS = 4
DIFF_V = 128
DIFF_QK = 64
C_IN = C_RWKV + 3 * D_DIFF
MEM_HEADS = 4
MEM_HEAD = D_MODEL // MEM_HEADS
N_GROUPS = 4
EXPERTS_PER_GROUP = 8
N_EXPERTS = 32
D_EXPERT = 512
DEPTH = 1
DEEPNORM_ALPHA = (2.0 * DEPTH) ** 0.25
LN_EPS = 1e-5
RMS_EPS = 1e-5
RWKV_GN_EPS = 64e-5
NEG_INF = -1e30
LOG2E = math.log2(math.e)
ALIBI_SLOPES = tuple(2.0 ** (-8.0 * (h + 1) / DIFF_HEADS) for h in range(DIFF_HEADS))
VT_PAD = 16
LANES = 128
SUBLANES = 8
VMEM_LIMIT = 48 * 1024 * 1024


def _params(sem):
    return pltpu.CompilerParams(dimension_semantics=sem, vmem_limit_bytes=VMEM_LIMIT)


def _dot(a, b):
    return jnp.dot(a, b, preferred_element_type=F32)


def _dot_nt(a, b):
    return lax.dot_general(a, b, (((1,), (1,)), ((), ())), preferred_element_type=F32)


def _dot_tn(a, b):
    return lax.dot_general(a, b, (((0,), (0,)), ((), ())), preferred_element_type=F32)


def _split2(x):
    hi = x.astype(BF16)
    lo = (x - hi.astype(F32)).astype(BF16)
    return hi, lo


def _split3(x):
    hi = x.astype(BF16)
    r = x - hi.astype(F32)
    mid = r.astype(BF16)
    lo = (r - mid.astype(F32)).astype(BF16)
    return hi, mid, lo


def _layer_norm(x, g, b):
    mu = jnp.mean(x, axis=-1, keepdims=True)
    xc = x - mu
    var = jnp.mean(xc * xc, axis=-1, keepdims=True)
    return xc * lax.rsqrt(var + LN_EPS) * g + b


def _memproj_kernel(m_ref, wk_ref, wv_ref, k_ref, v_ref, kb_ref, vb_ref):
    m = m_ref[0].astype(BF16)
    k = _dot(m, wk_ref[...])
    v = _dot(m, wv_ref[...])
    k_ref[0] = k
    v_ref[0] = v
    kb_ref[0] = k.astype(BF16)
    vb_ref[0] = v.astype(BF16)


def _memproj(mem, wk, wv):
    b, n, d = mem.shape
    blk = pl.BlockSpec((1, n, d), lambda i: (i, 0, 0))
    wspec = pl.BlockSpec((d, d), lambda i: (0, 0))
    return pl.pallas_call(
        _memproj_kernel,
        grid=(b,),
        in_specs=[blk, wspec, wspec],
        out_specs=[blk, blk, blk, blk],
        out_shape=[jax.ShapeDtypeStruct((b, n, d), F32), jax.ShapeDtypeStruct((b, n, d), F32),
                   jax.ShapeDtypeStruct((b, n, d), BF16), jax.ShapeDtypeStruct((b, n, d), BF16)],
        compiler_params=_params(("arbitrary",)),
        name="memproj",
    )(mem, wk, wv)


def _inproj_kernel(x_ref, w_ref, pa_ref, q_ref, k_ref, v_ref, kb_ref, vb_ref, kb1_ref, *, attn_tile):
    bb, ts, d = x_ref.shape
    rows = bb * ts
    x = x_ref[...].reshape(rows, d).astype(BF16)
    pq = _dot(x, w_ref[:, C_RWKV:C_RWKV + D_DIFF])
    pk = _dot(x, w_ref[:, C_RWKV + D_DIFF:C_RWKV + 2 * D_DIFF])
    pv = _dot(x, w_ref[:, C_RWKV + 2 * D_DIFF:])
    pa_ref[...] = _dot(x, w_ref[:, :C_RWKV]).reshape(bb, ts, C_RWKV)
    if attn_tile:
        lane = lax.broadcasted_iota(jnp.int32, (rows, LANES), 1)
        kloc = lax.rem(pl.program_id(1) * ts + lax.broadcasted_iota(jnp.int32, (rows, LANES), 0), attn_tile)
        aug = jnp.where(lax.broadcasted_iota(jnp.int32, (VT_PAD, rows), 0) == 0, 1.0, 0.0).astype(BF16)
    for h in range(DIFF_HEADS):
        hs = slice(h * LANES, (h + 1) * LANES)
        qh = pq[:, hs]
        kh = pk[:, hs]
        k_ref[:, h] = kh.reshape(bb, ts, LANES)
        vh = pv[:, hs]
        v_ref[:, h] = vh.reshape(bb, ts, LANES)
        if attn_tile:
            q_ref[0, h] = (qh * (DIFF_QK ** -0.5 * LOG2E)).T.astype(BF16)
            vb_ref[0, h] = jnp.concatenate([vh.T.astype(BF16), aug], axis=0)
            bias = kloc.astype(F32) * (ALIBI_SLOPES[h] * LOG2E)
            b_hi = bias.astype(BF16).astype(F32)
            b_mid = (bias - b_hi).astype(BF16).astype(F32)
            b_lo = bias - b_hi - b_mid
            cols = jnp.where(lane == DIFF_QK, b_hi, jnp.where(lane == DIFF_QK + 1, b_mid,
                             jnp.where(lane == DIFF_QK + 2, b_lo, 0.0)))
            kb_ref[0, h] = jnp.where(lane < DIFF_QK, kh, cols).astype(BF16)
            kb1_ref[0, h] = jnp.where(lane < DIFF_QK, pltpu.roll(kh, DIFF_QK, axis=1), cols).astype(BF16)
        else:
            q_ref[:, h] = (qh * (DIFF_QK ** -0.5)).reshape(bb, ts, LANES).astype(BF16)
            vb_ref[:, h] = vh.reshape(bb, ts, LANES).astype(BF16)
            kb_ref[:, h] = kh.reshape(bb, ts, LANES).astype(BF16)
            kb1_ref[:, h] = kh.reshape(bb, ts, LANES).astype(BF16)


def _inproj(x, w_in_bf, bb, ts, attn_tile):
    b, s, d = x.shape
    hm = pl.BlockSpec((bb, DIFF_HEADS, ts, LANES), lambda i, j: (i, 0, j, 0))
    hshape = (b, DIFF_HEADS, s, LANES)
    if attn_tile:
        assert bb == 1
        qm = pl.BlockSpec((1, DIFF_HEADS, LANES, ts), lambda i, j: (i, 0, 0, j))
        qshape = (b, DIFF_HEADS, LANES, s)
        vm = pl.BlockSpec((1, DIFF_HEADS, LANES + VT_PAD, ts), lambda i, j: (i, 0, 0, j))
        vshape = (b, DIFF_HEADS, LANES + VT_PAD, s)
    else:
        qm, qshape, vm, vshape = hm, hshape, hm, hshape
    return pl.pallas_call(
        functools.partial(_inproj_kernel, attn_tile=attn_tile),
        grid=(b // bb, s // ts),
        in_specs=[pl.BlockSpec((bb, ts, d), lambda i, j: (i, j, 0)),
                  pl.BlockSpec((d, C_IN), lambda i, j: (0, 0))],
        out_specs=[pl.BlockSpec((bb, ts, C_RWKV), lambda i, j: (i, j, 0)), qm, hm, hm, hm, vm, hm],
        out_shape=[jax.ShapeDtypeStruct((b, s, C_RWKV), F32),
                   jax.ShapeDtypeStruct(qshape, BF16),
                   jax.ShapeDtypeStruct(hshape, F32), jax.ShapeDtypeStruct(hshape, F32),
                   jax.ShapeDtypeStruct(hshape, BF16), jax.ShapeDtypeStruct(vshape, BF16),
                   jax.ShapeDtypeStruct(hshape, BF16)],
        compiler_params=_params(("arbitrary", "arbitrary")),
        name="inproj",
    )(x, w_in_bf)


def _rwkv_kernel(pa_ref, shift_ref, s0_ref, mu_ref, w0_ref, w2_ref, a0_ref, a2_ref, g2_ref,
                 kk_ref, ka_ref, rk_ref, lnw_ref, lnb_ref, hsum_ref,
                 o_ref, sfin_ref,
                 carry_ref, al_s, be_s, kb_s, rb_s, v_s, gam_s, o_s, bonus_s, g_s, w1_s, n1_s, c0_s, n2_s,
                 *, chunk, group):
    L = chunk
    tb = pa_ref.shape[1]
    nchunk = tb // L
    t = pl.program_id(1)

    @pl.when(t == 0)
    def _():
        sfin_ref[...] = s0_ref[...]
        carry_ref[...] = shift_ref[0]

    p = pa_ref[0]
    prev = pltpu.roll(p, 1, axis=0)
    row = lax.broadcasted_iota(jnp.int32, (tb, 1), 0)
    prev = jnp.where(row == 0, carry_ref[...], prev)
    carry_ref[...] = p[tb - 1:tb, :]
    ps = p + (prev - p) * mu_ref[...]

    r = ps[:, :D_RWKV]
    k = ps[:, D_RWKV:2 * D_RWKV]
    v = ps[:, 2 * D_RWKV:3 * D_RWKV]
    o0 = 3 * D_RWKV
    w_lo = ps[:, o0:o0 + W_LORA]
    a_lo = ps[:, o0 + W_LORA:o0 + W_LORA + A_LORA]
    g_lo = ps[:, o0 + W_LORA + A_LORA:]

    hsum = hsum_ref[...]

    def headsum(x):
        return _dot(x.astype(BF16), hsum)

    z = w0_ref[...] + _dot(jnp.tanh(w_lo).astype(BF16), w2_ref[...])
    lw = -math.exp(-0.5) / (1.0 + jnp.exp(-z))
    a = 1.0 / (1.0 + jnp.exp(-(a0_ref[...] + _dot(a_lo.astype(BF16), a2_ref[...]))))
    g_s[...] = _dot((1.0 / (1.0 + jnp.exp(-g_lo))).astype(BF16), g2_ref[...])
    kk = k * kk_ref[...]
    kk = kk / jnp.maximum(jnp.sqrt(headsum(kk * kk)), 1e-12)
    kmod = k * (1.0 + (a - 1.0) * ka_ref[...])
    bonus_s[...] = headsum(r * kmod * rk_ref[...]) * v

    bi = lax.broadcasted_iota(jnp.int32, (tb, tb), 0)
    bj = lax.broadcasted_iota(jnp.int32, (tb, tb), 1)
    tri = (((bi // L) == (bj // L)) & (bi >= bj)).astype(BF16)
    h3 = _split3(lw)
    cum = _dot(tri, h3[0]) + _dot(tri, h3[1]) + _dot(tri, h3[2])
    gam = jnp.exp(cum)
    igam = jnp.exp(-cum)
    gam_s[...] = gam
    al_s[...] = kk * jnp.exp(cum - lw)
    be_s[...] = kk * a * igam
    kb_s[...] = kmod * igam
    rb_s[...] = r * gam
    v_s[...] = v

    l2 = 2 * L
    ri = lax.broadcasted_iota(jnp.int32, (l2, l2), 0)
    ci = lax.broadcasted_iota(jnp.int32, (l2, l2), 1)
    same = (ri // L) == (ci // L)
    strict = same & (ri > ci)
    incl = same & (ri >= ci)
    eye = (ri == ci).astype(F32)
    lane = lax.broadcasted_iota(jnp.int32, (1, LANES), 1)
    first = lane < RWKV_HEAD

    def stack(x):
        return jnp.concatenate([jnp.where(first, x, 0.0), jnp.where(first, 0.0, x)], axis=0)

    def phase_a(cg, carry):
        cs = []
        for gi in range(group):
            c = cg * group + gi
            sl = pl.ds(pl.multiple_of(c * L, L), L)
            gl = gam_s[pl.ds(c * L + (L - 1), 1), :]
            for j in range(RWKV_PAIRS):
                ls = slice(j * LANES, (j + 1) * LANES)
                cs.append(dict(c=c, j=j, gl=gl[:, ls], al2=stack(al_s[sl, ls]), rb2=stack(rb_s[sl, ls]),
                               be2=stack(be_s[sl, ls]), kb2=stack(kb_s[sl, ls]), v2=stack(v_s[sl, ls])))
        for d in cs:
            x = jnp.concatenate([d["al2"], d["rb2"]], axis=0).astype(BF16)
            y = jnp.concatenate([d["be2"], d["kb2"]], axis=0).astype(BF16)
            gmat = _dot_nt(x, y)
            ab = jnp.where(strict, gmat[:l2, :l2], 0.0)
            d["ak"] = jnp.where(strict, gmat[:l2, l2:], 0.0).astype(BF16)
            d["rbm"] = jnp.where(incl, gmat[l2:, :l2], 0.0).astype(BF16)
            d["rkm"] = jnp.where(incl, gmat[l2:, l2:], 0.0).astype(BF16)
            d["pw"] = ab
            d["tm"] = eye - ab
        n = 2
        while n < L:
            for d in cs:
                pb = d["pw"].astype(BF16)
                d["pw"] = _dot(pb, pb)
            for d in cs:
                d["tm"] = _dot(d["tm"].astype(BF16), (eye + d["pw"]).astype(BF16))
            n *= 2
        for d in cs:
            d["akv"] = _dot(d["ak"], d["v2"].astype(BF16))
        for d in cs:
            zz = jnp.concatenate([d["al2"], d["akv"]], axis=1).astype(BF16)
            d["m"] = _dot(d["tm"].astype(BF16), zz)
        for d in cs:
            rm = _dot(d["rbm"], d["m"].astype(BF16))
            rkv = _dot(d["rkm"], d["v2"].astype(BF16))
            n1_s[d["c"], d["j"]] = (d["rb2"] - rm[:, :LANES]).astype(BF16)
            n2_s[d["c"], d["j"]] = rkv - rm[:, LANES:]
        for d in cs:
            m1 = d["m"][:, :LANES].astype(BF16)
            m2 = d["m"][:, LANES:]
            w1 = _dot_tn(m1, d["be2"].astype(BF16)) * d["gl"]
            lhs = jnp.concatenate([d["v2"], -m2], axis=0).astype(BF16)
            rhs = jnp.concatenate([d["kb2"], d["be2"]], axis=0).astype(BF16)
            w1_s[d["c"], d["j"]] = w1.astype(BF16)
            c0_s[d["c"], d["j"]] = _dot_tn(lhs, rhs) * d["gl"]
        return carry

    lax.fori_loop(0, nchunk // group, phase_a, 0)

    def phase_b(c, carry):
        gl = gam_s[pl.ds(c * L + (L - 1), 1), :]
        outs = []
        for j in range(RWKV_PAIRS):
            sb = sfin_ref[0, j]
            sbb = sb.astype(BF16)
            o2 = _dot_nt(n1_s[c, j], sbb) + n2_s[c, j]
            outs.append(o2[:L] + o2[L:])
            sfin_ref[0, j] = sb * gl[:, j * LANES:(j + 1) * LANES] - _dot(sbb, w1_s[c, j]) + c0_s[c, j]
        o_s[pl.ds(pl.multiple_of(c * L, L), L), :] = jnp.concatenate(outs, axis=1)
        return carry

    lax.fori_loop(0, nchunk, phase_b, 0)

    o = o_s[...]
    inv_n = 1.0 / RWKV_HEAD
    mu_h = headsum(o) * inv_n
    oc = o - mu_h
    var_h = headsum(oc * oc) * inv_n
    on = oc * lax.rsqrt(var_h + RWKV_GN_EPS) * lnw_ref[...] + lnb_ref[...]
    o_ref[0] = ((on + bonus_s[...]) * g_s[...]).astype(o_ref.dtype)


def _rwkv(pa, shift_prev, s0_pairs, prm, tb, chunk, group):
    b, s, _ = pa.shape
    nchunk = tb // chunk
    row = lambda n: pl.BlockSpec((1, n), lambda i, j: (0, 0))
    mat = lambda m, n: pl.BlockSpec((m, n), lambda i, j: (0, 0))
    sspec = pl.BlockSpec((1, RWKV_PAIRS, LANES, LANES), lambda i, j: (i, 0, 0, 0))
    coef = lambda rows, dt: pltpu.VMEM((nchunk, RWKV_PAIRS, rows, LANES), dt)
    return pl.pallas_call(
        functools.partial(_rwkv_kernel, chunk=chunk, group=group),
        grid=(b, s // tb),
        in_specs=[pl.BlockSpec((1, tb, C_RWKV), lambda i, j: (i, j, 0)),
                  pl.BlockSpec((1, 1, C_RWKV), lambda i, j: (i, 0, 0)),
                  sspec,
                  row(C_RWKV), row(D_RWKV), mat(W_LORA, D_RWKV), row(D_RWKV), mat(A_LORA, D_RWKV),
                  mat(G_LORA, D_RWKV), row(D_RWKV), row(D_RWKV), row(D_RWKV), row(D_RWKV), row(D_RWKV),
                  mat(D_RWKV, D_RWKV)],
        out_specs=[pl.BlockSpec((1, tb, D_RWKV), lambda i, j: (i, j, 0)), sspec],
        out_shape=[jax.ShapeDtypeStruct((b, s, D_RWKV), BF16),
                   jax.ShapeDtypeStruct((b, RWKV_PAIRS, LANES, LANES), F32)],
        scratch_shapes=[pltpu.VMEM((1, C_RWKV), F32)] + [pltpu.VMEM((tb, D_RWKV), F32)] * 9
        + [coef(LANES, BF16), coef(2 * chunk, BF16), coef(LANES, F32), coef(2 * chunk, F32)],
        compiler_params=_params(("arbitrary", "arbitrary")),
        name="rwkv",
    )(pa, shift_prev, s0_pairs, prm["mu"], prm["w0"], prm["w2"], prm["a0"], prm["a2"], prm["g2"],
      prm["k_k"], prm["k_a"], prm["r_k"], prm["ln_w"], prm["ln_b"], prm["hsum"])


def _state_to_pairs(s):
    b = s.shape[0]
    s = s.reshape(b, RWKV_PAIRS, 2, RWKV_HEAD, RWKV_HEAD)
    z = jnp.zeros_like(s[:, :, 0])
    top = jnp.concatenate([s[:, :, 0], z], axis=-1)
    bot = jnp.concatenate([z, s[:, :, 1]], axis=-1)
    return jnp.concatenate([top, bot], axis=-2)


def _pairs_to_state(sp):
    b = sp.shape[0]
    h = RWKV_HEAD
    s = jnp.stack([sp[:, :, :h, :h], sp[:, :, h:, h:]], axis=2)
    return s.reshape(b, 2 * RWKV_PAIRS, h, h)


def _diff_finish(accs, ls, lam, sub, lam_init):
    o = accs[0] / ls[0] - lam * (accs[1] / ls[1])
    o = o * lax.rsqrt(jnp.mean(o * o, axis=-1, keepdims=True) + RMS_EPS)
    return o * sub * (1.0 - lam_init)


def _attn_kernel(lam_ref, qt_ref, k0_ref, k1_ref, vt_ref, tab_ref, sub_ref, o_ref, acc_ref, s_ref, m_ref, *, tq,
                 lam_init):
    h = pl.program_id(1)
    slope = lam_ref[1 + h] * LOG2E
    lam = lam_ref[0]
    row = lax.broadcasted_iota(jnp.int32, (LANES, 1), 0)
    ones_rows = (row >= DIFF_QK) & (row < DIFF_QK + 3)
    fill = jnp.where(ones_rows, 1.0, 0.0).astype(BF16)
    k_refs = (k0_ref, k1_ref)

    def rhs(q_first):
        return jnp.where(row < DIFF_QK, q_first, fill)

    def q_tile(qi, carry):
        qt = qt_ref[0, 0, :, pl.ds(pl.multiple_of(qi * tq, tq), tq)]
        rs = (rhs(qt), rhs(jnp.concatenate([qt[DIFF_QK:], qt[:DIFF_QK]], axis=0)))
        acc_ref[...] = jnp.zeros_like(acc_ref)
        m_ref[...] = jnp.full_like(m_ref, NEG_INF)

        def scores(j, slot):
            ds = pl.ds(pl.multiple_of(j * tq, tq), tq)
            for c in range(2):
                s_ref[slot, c] = _dot(k_refs[c][0, 0, ds, :], rs[c])

        def softmax_pv(j, slot, diag):
            vt = vt_ref[0, 0, :, pl.ds(pl.multiple_of(j * tq, tq), tq)]
            off = slope * jnp.asarray((j - qi) * tq, F32)
            for c in range(2):
                s = s_ref[slot, c]
                if diag:
                    s = s + tab_ref[0]
                m = m_ref[c]
                m_new = jnp.maximum(m, jnp.max(s, axis=0, keepdims=True) + off)
                alpha = jnp.exp2(m - m_new)
                p = jnp.exp2(s - (m_new - off))
                m_ref[c] = m_new
                acc_ref[c] = alpha * acc_ref[c] + _dot(vt, p.astype(BF16))

        def pair(jj, c):
            j = 2 * jj
            scores(j + 1, 0)
            softmax_pv(j, 1, False)
            scores(jnp.minimum(j + 2, jnp.maximum(qi - 1, 0)), 1)
            softmax_pv(j + 1, 0, False)
            return c

        scores(qi, 0)
        scores(0, 1)
        softmax_pv(qi, 0, True)
        lax.fori_loop(0, qi // 2, pair, 0)

        @pl.when(lax.rem(qi, 2) == 1)
        def _():
            softmax_pv(qi - 1, 1, False)

        a0 = acc_ref[0]
        a1 = acc_ref[1]
        o = a0[:DIFF_V] / a0[DIFF_V:DIFF_V + 1] - lam * (a1[:DIFF_V] / a1[DIFF_V:DIFF_V + 1])
        o = o * lax.rsqrt(jnp.mean(o * o, axis=0, keepdims=True) + RMS_EPS)
        o = o * (sub_ref[...] * (1.0 - lam_init))
        o_ref[0, pl.ds(pl.multiple_of(qi * tq, tq), tq), :] = o.T.astype(o_ref.dtype)
        return carry

    lax.fori_loop(0, qt_ref.shape[3] // tq, q_tile, 0)


def _attn_diag_table(tq):
    slopes = jnp.asarray(ALIBI_SLOPES, F32) * LOG2E
    pos = jnp.arange(tq, dtype=jnp.int32)
    kp, qp = pos[:, None], pos[None, :]
    rel = jnp.where(kp <= qp, 0, 2 * (qp - kp)).astype(F32)
    vis = (kp // CHUNK) <= (qp // CHUNK)
    return jnp.where(vis[None], slopes[:, None, None] * rel[None], NEG_INF)


def _attn_prompt(lam, qt, k0, k1, vt, subln_col, tq, lam_init):
    b, h, _, s = qt.shape
    kspec = pl.BlockSpec((1, 1, s, LANES), lambda i, j: (i, j, 0, 0))
    return pl.pallas_call(
        functools.partial(_attn_kernel, tq=tq, lam_init=lam_init),
        grid=(b, h),
        in_specs=[pl.BlockSpec(memory_space=pltpu.SMEM),
                  pl.BlockSpec((1, 1, LANES, s), lambda i, j: (i, j, 0, 0)),
                  kspec, kspec,
                  pl.BlockSpec((1, 1, LANES + VT_PAD, s), lambda i, j: (i, j, 0, 0)),
                  pl.BlockSpec((1, tq, tq), lambda i, j: (j, 0, 0)),
                  pl.BlockSpec((DIFF_V, 1), lambda i, j: (0, 0))],
        out_specs=pl.BlockSpec((1, s, LANES), lambda i, j: (i, 0, j)),
        out_shape=jax.ShapeDtypeStruct((b, s, D_DIFF), BF16),
        scratch_shapes=[pltpu.VMEM((2, DIFF_V + VT_PAD, tq), F32), pltpu.VMEM((2, 2, tq, tq), F32),
                        pltpu.VMEM((2, 1, tq), F32)],
        compiler_params=_params(("arbitrary", "arbitrary")),
        name="attn_prompt",
    )(lam, qt, k0, k1, vt, _attn_diag_table(tq), subln_col)


def _attn_cached_kernel(lam_ref, q_ref, kn_ref, vn_ref, kp_ref, vp_ref, sub_ref, o_ref, *, lam_init):
    h = pl.program_id(1)
    slope = lam_ref[1 + h]
    s_new = q_ref.shape[2]
    past = kp_ref.shape[2]
    q = q_ref[0, 0]
    kp = kp_ref[0, 0].astype(BF16)
    vp = vp_ref[0, 0].astype(BF16)
    kn = kn_ref[0, 0]
    vn = vn_ref[0, 0]
    q_pos = past + lax.broadcasted_iota(jnp.int32, (s_new, 1), 0)

    def bias(k_pos):
        dist = jnp.abs(q_pos - k_pos).astype(F32)
        vis = (k_pos // CHUNK) <= (q_pos // CHUNK)
        return jnp.where(vis, -slope * dist, NEG_INF), vis

    b_past, vis_past = bias(lax.broadcasted_iota(jnp.int32, (1, past), 1))
    b_new, vis_new = bias(past + lax.broadcasted_iota(jnp.int32, (1, s_new), 1))
    accs, ls = [], []
    for c in range(2):
        qc = q[:, c * DIFF_QK:(c + 1) * DIFF_QK]
        sp = jnp.where(vis_past, _dot_nt(qc, kp[:, c * DIFF_QK:(c + 1) * DIFF_QK]) + b_past, NEG_INF)
        sn = jnp.where(vis_new, _dot_nt(qc, kn[:, c * DIFF_QK:(c + 1) * DIFF_QK]) + b_new, NEG_INF)
        m = jnp.maximum(jnp.max(sp, axis=-1, keepdims=True), jnp.max(sn, axis=-1, keepdims=True))
        pp = jnp.exp(sp - m)
        pn = jnp.exp(sn - m)
        ls.append(jnp.sum(pp, axis=-1, keepdims=True) + jnp.sum(pn, axis=-1, keepdims=True))
        accs.append(_dot(pp.astype(BF16), vp) + _dot(pn.astype(BF16), vn))
    o = _diff_finish(accs, ls, lam_ref[0], sub_ref[...], lam_init)
    o_ref[0] = o.astype(o_ref.dtype)


def _attn_cached(lam, q, kb, vb, past_k, past_v, subln, lam_init):
    b, h, s, _ = q.shape
    past = past_k.shape[2]
    new = pl.BlockSpec((1, 1, s, LANES), lambda i, j: (i, j, 0, 0))
    old = pl.BlockSpec((1, 1, past, LANES), lambda i, j: (i, j, 0, 0))
    return pl.pallas_call(
        functools.partial(_attn_cached_kernel, lam_init=lam_init),
        grid=(b, h),
        in_specs=[pl.BlockSpec(memory_space=pltpu.SMEM), new, new, new, old, old,
                  pl.BlockSpec((1, DIFF_V), lambda i, j: (0, 0))],
        out_specs=pl.BlockSpec((1, s, LANES), lambda i, j: (i, 0, j)),
        out_shape=jax.ShapeDtypeStruct((b, s, D_DIFF), BF16),
        compiler_params=_params(("arbitrary", "arbitrary")),
        name="attn_cached",
    )(lam, q, kb, vb, past_k, past_v, subln)


def _route(lg):
    tm = lg.shape[0]
    lane = lax.broadcasted_iota(jnp.int32, (tm, LANES), 1)
    lane_f = lane.astype(F32)
    big = 1e9
    low = -3e38
    is_g = (lane >= N_EXPERTS) & (lane < N_EXPERTS + N_GROUPS)
    gl = jnp.where(is_g, lg, low)
    gmax = jnp.max(gl, axis=-1, keepdims=True)
    grp_lane = jnp.min(jnp.where(gl == gmax, lane_f, big), axis=-1, keepdims=True)
    gsum = jnp.sum(jnp.where(is_g, jnp.exp(gl - gmax), 0.0), axis=-1, keepdims=True)
    g_prob = 1.0 / gsum
    grp = grp_lane - float(N_EXPERTS)
    lane_grp = (lane // EXPERTS_PER_GROUP).astype(F32)
    el = jnp.where(lane_grp == grp, lg, low)
    v1 = jnp.max(el, axis=-1, keepdims=True)
    i1 = jnp.min(jnp.where(el == v1, lane_f, big), axis=-1, keepdims=True)
    el2 = jnp.where(lane_f == i1, low, el)
    v2 = jnp.max(el2, axis=-1, keepdims=True)
    i2 = jnp.min(jnp.where(el2 == v2, lane_f, big), axis=-1, keepdims=True)
    e21 = jnp.exp(v2 - v1)
    den = 1.0 + e21
    gate1 = g_prob / den
    gate2 = g_prob * e21 / den
    out = jnp.where(lane == 0, i1, jnp.where(lane == 1, i2, jnp.where(lane == 2, gate1,
                    jnp.where(lane == 3, gate2, 0.0))))
    return out


def _post_kernel(x_ref, oa_ref, ob_ref, mk_ref, mv_ref, wout_ref, wq_ref, wo_ref,
                 g1_ref, b1_ref, g2_ref, b2_ref, wrh_ref, wrl_ref, br_ref, hx_ref, *, nsplit):
    tm = x_ref.shape[1]
    rows = tm // nsplit
    parts = [slice(i * rows, (i + 1) * rows) for i in range(nsplit)]
    mix = [_dot(oa_ref[0, r, :], wout_ref[:D_RWKV, :]) + _dot(ob_ref[0, r, :], wout_ref[D_RWKV:, :]) for r in parts]
    h1 = [_layer_norm(DEEPNORM_ALPHA * x_ref[0, r, :] + m, g1_ref[...], b1_ref[...]) for r, m in zip(parts, mix)]
    q = [(_dot(h.astype(BF16), wq_ref[...]) * (MEM_HEAD ** -0.5)).astype(BF16) for h in h1]
    heads = [[] for _ in parts]
    for h in range(MEM_HEADS):
        hs = slice(h * MEM_HEAD, (h + 1) * MEM_HEAD)
        for i in range(nsplit):
            s = _dot_nt(q[i][:, hs], mk_ref[0, :, hs])
            m = jnp.max(s, axis=-1, keepdims=True)
            p = jnp.exp(s - m)
            p = p / jnp.sum(p, axis=-1, keepdims=True)
            heads[i].append(_dot(p.astype(BF16), mv_ref[0, :, hs]).astype(BF16))
    att = [_dot(jnp.concatenate(hd, axis=-1), wo_ref[...]) for hd in heads]
    h2 = [_layer_norm(DEEPNORM_ALPHA * h + a, g2_ref[...], b2_ref[...]) for h, a in zip(h1, att)]
    for r, h in zip(parts, h2):
        hx_ref[0, r, :D_MODEL] = h
        hi, lo = _split2(h)
        lg = _dot(hi, wrh_ref[...]) + _dot(hi, wrl_ref[...]) + _dot(lo, wrh_ref[...]) + br_ref[...]
        hx_ref[0, r, D_MODEL:] = _route(lg)


def _post(x, oa, ob, mk, mv, prm, tm, nsplit):
    b, s, d = x.shape
    n = mk.shape[1]
    tok = lambda w: pl.BlockSpec((1, tm, w), lambda i, j: (i, j, 0))
    mem = pl.BlockSpec((1, n, d), lambda i, j: (i, 0, 0))
    mat = lambda m_, n_: pl.BlockSpec((m_, n_), lambda i, j: (0, 0))
    return pl.pallas_call(
        functools.partial(_post_kernel, nsplit=nsplit),
        grid=(b, s // tm),
        in_specs=[tok(d), tok(D_RWKV), tok(D_DIFF), mem, mem, mat(d, d), mat(d, d), mat(d, d),
                  mat(1, d), mat(1, d), mat(1, d), mat(1, d), mat(d, LANES), mat(d, LANES), mat(1, LANES)],
        out_specs=tok(d + LANES),
        out_shape=jax.ShapeDtypeStruct((b, s, d + LANES), F32),
        compiler_params=_params(("arbitrary", "arbitrary")),
        name="post",
    )(x, oa, ob, mk, mv, prm["w_out"], prm["wq"], prm["wo"], prm["ln1_g"], prm["ln1_b"],
      prm["ln2_g"], prm["ln2_b"], prm["wr_hi"], prm["wr_lo"], prm["br"])


def _moe_kernel(bexp_ref, nused_ref, tok_ref, tokn_ref, h2_hbm, w1_ref, w3_ref, w2_ref, y_ref, xbuf, sem):
    i = pl.program_id(0)
    nblk = pl.num_programs(0)
    nused = nused_ref[0]
    ngrp = xbuf.shape[1]
    tb = ngrp * SUBLANES
    slot = lax.rem(i, 2)

    def issue_rows(idx_ref, sl):
        def group(g, c):
            for u in range(SUBLANES):
                r = idx_ref[0, 0, g * SUBLANES + u]
                pltpu.make_async_copy(h2_hbm.at[pl.ds(r, 1)], xbuf.at[sl, g, pl.ds(u, 1)], sem.at[sl]).start()
            return c
        lax.fori_loop(0, ngrp, group, 0)

    def wait_rows(sl):
        pltpu.make_async_copy(xbuf.at[1 - sl], xbuf.at[sl], sem.at[sl]).wait()

    @pl.when(i == 0)
    def _():
        issue_rows(tok_ref, 0)

    @pl.when(i < nused)
    def _():
        wait_rows(slot)
        issue_rows(tokn_ref, 1 - slot)
        x = xbuf[slot].reshape(tb, xbuf.shape[3]).astype(BF16)
        a = _dot(x, w1_ref[0])
        g = _dot(x, w3_ref[0])
        hmid = (a / (1.0 + jnp.exp(-a))) * g
        y_ref[...] = _dot(hmid.astype(BF16), w2_ref[0])

        @pl.when(i == nblk - 1)
        def _():
            wait_rows(1 - slot)

    @pl.when(i >= nused)
    def _():
        y_ref[...] = jnp.zeros_like(y_ref)

        @pl.when(i == nused)
        def _():
            wait_rows(slot)


def _moe_ffn(h2f, blk_exp, nused, tok, w1, w3, w2, tb):
    nblk = tok.shape[0]
    d = h2f.shape[1]
    grid_spec = pltpu.PrefetchScalarGridSpec(
        num_scalar_prefetch=2,
        grid=(nblk,),
        in_specs=[pl.BlockSpec((1, 1, tb), lambda i, be, nu: (i, 0, 0), memory_space=pltpu.SMEM),
                  pl.BlockSpec((1, 1, tb), lambda i, be, nu: (jnp.minimum(i + 1, nblk - 1), 0, 0),
                               memory_space=pltpu.SMEM),
                  pl.BlockSpec(memory_space=pl.ANY),
                  pl.BlockSpec((1, d, D_EXPERT), lambda i, be, nu: (be[i], 0, 0)),
                  pl.BlockSpec((1, d, D_EXPERT), lambda i, be, nu: (be[i], 0, 0)),
                  pl.BlockSpec((1, D_EXPERT, d), lambda i, be, nu: (be[i], 0, 0))],
        out_specs=pl.BlockSpec((tb, d), lambda i, be, nu: (i, 0)),
        scratch_shapes=[pltpu.VMEM((2, tb // SUBLANES, SUBLANES, d), F32), pltpu.SemaphoreType.DMA((2,))],
    )
    return pl.pallas_call(
        _moe_kernel,
        grid_spec=grid_spec,
        out_shape=jax.ShapeDtypeStruct((nblk * tb, d), F32),
        compiler_params=_params(("arbitrary",)),
        name="moe_ffn",
    )(blk_exp, nused, tok, tok, h2f, w1, w3, w2)


def _combine_kernel(pos_ref, posn_ref, ys_hbm, h2_ref, route_ref, g_ref, b_ref, y_ref, gbuf, sem):
    i = pl.program_id(0)
    n = pl.num_programs(0)
    tm = h2_ref.shape[0]
    slot = lax.rem(i, 2)

    ngrp = gbuf.shape[1]
    d = gbuf.shape[3]

    def issue_rows(idx_ref, sl):
        def group(g, c):
            for u in range(SUBLANES):
                r = idx_ref[0, 0, g * SUBLANES + u]
                pltpu.make_async_copy(ys_hbm.at[pl.ds(r, 1)], gbuf.at[sl, g, pl.ds(u, 1)], sem.at[sl]).start()
            return c
        lax.fori_loop(0, ngrp, group, 0)

    def wait_rows(sl):
        pltpu.make_async_copy(gbuf.at[1 - sl], gbuf.at[sl], sem.at[sl]).wait()

    @pl.when(i == 0)
    def _():
        issue_rows(pos_ref, 0)

    wait_rows(slot)
    issue_rows(posn_ref, 1 - slot)
    rt = route_ref[...]
    half = ngrp // 2
    moe = (rt[:, 2:3] * gbuf[slot, :half].reshape(tm, d) + rt[:, 3:4] * gbuf[slot, half:].reshape(tm, d))
    y_ref[...] = _layer_norm(DEEPNORM_ALPHA * h2_ref[...] + moe, g_ref[...], b_ref[...])

    @pl.when(i == n - 1)
    def _():
        wait_rows(1 - slot)


def _combine(pos, ysort, h2f, route, g, b, tm):
    t, d = h2f.shape
    n = t // tm
    return pl.pallas_call(
        _combine_kernel,
        grid=(n,),
        in_specs=[pl.BlockSpec((1, 1, 2 * tm), lambda i: (i, 0, 0), memory_space=pltpu.SMEM),
                  pl.BlockSpec((1, 1, 2 * tm), lambda i: (jnp.minimum(i + 1, n - 1), 0, 0),
                               memory_space=pltpu.SMEM),
                  pl.BlockSpec(memory_space=pl.ANY),
                  pl.BlockSpec((tm, d), lambda i: (i, 0)),
                  pl.BlockSpec((tm, LANES), lambda i: (i, 0)),
                  pl.BlockSpec((1, d), lambda i: (0, 0)),
                  pl.BlockSpec((1, d), lambda i: (0, 0))],
        out_specs=pl.BlockSpec((tm, d), lambda i: (i, 0)),
        out_shape=jax.ShapeDtypeStruct((t, d), F32),
        scratch_shapes=[pltpu.VMEM((2, 2 * tm // SUBLANES, SUBLANES, d), F32), pltpu.SemaphoreType.DMA((2,))],
        compiler_params=_params(("arbitrary",)),
        name="combine",
    )(pos, pos, ysort, h2f, route, g, b)


def _dispatch(expert, tb, tm):
    t = expert.shape[0]
    a = 2 * t
    ef = expert.reshape(a)
    order = jnp.argsort(ef, stable=True).astype(jnp.int32)
    es = ef[order]
    counts = jnp.sum((ef[:, None] == jnp.arange(N_EXPERTS, dtype=jnp.int32)[None, :]).astype(jnp.int32), axis=0)
    starts = jnp.cumsum(counts) - counts
    nb = (counts + tb - 1) // tb
    bend = jnp.cumsum(nb)
    bstart = bend - nb
    nblk = -(-a // tb) + N_EXPERTS
    blk = jnp.arange(nblk, dtype=jnp.int32)
    bexp = jnp.minimum(jnp.sum((blk[:, None] >= bend[None, :]).astype(jnp.int32), axis=1), N_EXPERTS - 1)
    row0 = starts[bexp] + (blk - bstart[bexp]) * tb
    idx = jnp.clip(row0[:, None] + jnp.arange(tb, dtype=jnp.int32)[None, :], 0, a - 1)
    tok = (order[idx] // 2).astype(jnp.int32).reshape(nblk, 1, tb)
    dest_sorted = bstart[es] * tb + (jnp.arange(a, dtype=jnp.int32) - starts[es])
    _, pos = lax.sort_key_val(order, dest_sorted.astype(jnp.int32))
    pos = pos.reshape(t, 2)
    pos = pos.reshape(t // tm, tm, 2).transpose(0, 2, 1).reshape(t // tm, 1, 2 * tm)
    return bexp, bend[-1:].astype(jnp.int32), tok, pos


def _moe(hx, prm, tb, tm):
    b, s, _ = hx.shape
    d = D_MODEL
    t = b * s
    h2f = hx[..., :d].reshape(t, d)
    rf = hx[..., d:].reshape(t, LANES)
    expert = rf[:, :2].astype(jnp.int32)
    bexp, nused, tok, pos = _dispatch(expert, tb, tm)
    ysort = _moe_ffn(h2f, bexp, nused, tok, prm["moe_w1"], prm["moe_w3"], prm["moe_w2"], tb)
    y = _combine(pos, ysort, h2f, rf, prm["ln3_g"], prm["ln3_b"], tm)
    return y.reshape(b, s, d)


N_PAIR_CLASSES = N_GROUPS * EXPERTS_PER_GROUP * EXPERTS_PER_GROUP
N_REAL_PAIRS = N_GROUPS * (EXPERTS_PER_GROUP * (EXPERTS_PER_GROUP - 1) // 2)


def _moe_pair_kernel(elo_ref, ehi_ref, nused_ref, tok_ref, tokn_ref, h2_hbm,
                     w1a_ref, w3a_ref, w2a_ref, w1b_ref, w3b_ref, w2b_ref, y_ref, xbuf, sem):
    i = pl.program_id(0)
    nblk = pl.num_programs(0)
    nused = nused_ref[0]
    ngrp = xbuf.shape[1]
    tb = ngrp * SUBLANES
    slot = lax.rem(i, 2)

    def issue_rows(idx_ref, sl):
        def group(g, c):
            for u in range(SUBLANES):
                r = idx_ref[0, 0, g * SUBLANES + u]
                pltpu.make_async_copy(h2_hbm.at[pl.ds(r, 1)], xbuf.at[sl, g, pl.ds(u, 1)], sem.at[sl]).start()
            return c
        lax.fori_loop(0, ngrp, group, 0)

    def wait_rows(sl):
        pltpu.make_async_copy(xbuf.at[1 - sl], xbuf.at[sl], sem.at[sl]).wait()

    def ffn(x, w1_ref, w3_ref, w2_ref):
        a = _dot(x, w1_ref[0])
        g = _dot(x, w3_ref[0])
        hmid = (a / (1.0 + jnp.exp(-a))) * g
        return _dot(hmid.astype(BF16), w2_ref[0])

    @pl.when(i == 0)
    def _():
        issue_rows(tok_ref, 0)

    @pl.when(i < nused)
    def _():
        wait_rows(slot)
        issue_rows(tokn_ref, 1 - slot)
        rows = xbuf[slot].reshape(tb, xbuf.shape[3])
        x = rows[:, :D_MODEL].astype(BF16)
        rt = rows[:, D_MODEL:]
        first_low = rt[:, 0:1] < rt[:, 1:2]
        g_lo = jnp.where(first_low, rt[:, 2:3], rt[:, 3:4])
        g_hi = jnp.where(first_low, rt[:, 3:4], rt[:, 2:3])
        y_ref[...] = g_lo * ffn(x, w1a_ref, w3a_ref, w2a_ref) + g_hi * ffn(x, w1b_ref, w3b_ref, w2b_ref)

        @pl.when(i == nblk - 1)
        def _():
            wait_rows(1 - slot)

    @pl.when(i >= nused)
    def _():
        y_ref[...] = jnp.zeros_like(y_ref)

        @pl.when(i == nused)
        def _():
            wait_rows(slot)


def _moe_pair_ffn(hxf, elo, ehi, nused, tok, w1, w3, w2, tb):
    nblk = tok.shape[0]
    d = D_MODEL
    wspec = lambda rows, cols, which: pl.BlockSpec(
        (1, rows, cols), (lambda i, lo, hi, nu: (lo[i], 0, 0)) if which == 0 else (lambda i, lo, hi, nu: (hi[i], 0, 0)))
    grid_spec = pltpu.PrefetchScalarGridSpec(
        num_scalar_prefetch=3,
        grid=(nblk,),
        in_specs=[pl.BlockSpec((1, 1, tb), lambda i, lo, hi, nu: (i, 0, 0), memory_space=pltpu.SMEM),
                  pl.BlockSpec((1, 1, tb), lambda i, lo, hi, nu: (jnp.minimum(i + 1, nblk - 1), 0, 0),
                               memory_space=pltpu.SMEM),
                  pl.BlockSpec(memory_space=pl.ANY),
                  wspec(d, D_EXPERT, 0), wspec(d, D_EXPERT, 0), wspec(D_EXPERT, d, 0),
                  wspec(d, D_EXPERT, 1), wspec(d, D_EXPERT, 1), wspec(D_EXPERT, d, 1)],
        out_specs=pl.BlockSpec((tb, d), lambda i, lo, hi, nu: (i, 0)),
        scratch_shapes=[pltpu.VMEM((2, tb // SUBLANES, SUBLANES, hxf.shape[1]), F32),
                        pltpu.SemaphoreType.DMA((2,))],
    )
    return pl.pallas_call(
        _moe_pair_kernel,
        grid_spec=grid_spec,
        out_shape=jax.ShapeDtypeStruct((nblk * tb, d), F32),
        compiler_params=_params(("arbitrary",)),
        name="moe_pair_ffn",
    )(elo, ehi, nused, tok, tok, hxf, w1, w3, w2, w1, w3, w2)


def _combine1_kernel(pos_ref, posn_ref, ys_hbm, hx_ref, g_ref, b_ref, y_ref, gbuf, sem):
    i = pl.program_id(0)
    n = pl.num_programs(0)
    tm = hx_ref.shape[0]
    slot = lax.rem(i, 2)
    ngrp = gbuf.shape[1]

    def issue_rows(idx_ref, sl):
        def group(g, c):
            for u in range(SUBLANES):
                r = idx_ref[0, 0, g * SUBLANES + u]
                pltpu.make_async_copy(ys_hbm.at[pl.ds(r, 1)], gbuf.at[sl, g, pl.ds(u, 1)], sem.at[sl]).start()
            return c
        lax.fori_loop(0, ngrp, group, 0)

    def wait_rows(sl):
        pltpu.make_async_copy(gbuf.at[1 - sl], gbuf.at[sl], sem.at[sl]).wait()

    @pl.when(i == 0)
    def _():
        issue_rows(pos_ref, 0)

    wait_rows(slot)
    issue_rows(posn_ref, 1 - slot)
    moe = gbuf[slot].reshape(tm, gbuf.shape[3])
    y_ref[...] = _layer_norm(DEEPNORM_ALPHA * hx_ref[:, :D_MODEL] + moe, g_ref[...], b_ref[...])

    @pl.when(i == n - 1)
    def _():
        wait_rows(1 - slot)


def _combine1(pos, ysort, hxf, g, b, tm):
    t, dx = hxf.shape
    d = D_MODEL
    n = t // tm
    return pl.pallas_call(
        _combine1_kernel,
        grid=(n,),
        in_specs=[pl.BlockSpec((1, 1, tm), lambda i: (i, 0, 0), memory_space=pltpu.SMEM),
                  pl.BlockSpec((1, 1, tm), lambda i: (jnp.minimum(i + 1, n - 1), 0, 0), memory_space=pltpu.SMEM),
                  pl.BlockSpec(memory_space=pl.ANY),
                  pl.BlockSpec((tm, dx), lambda i: (i, 0)),
                  pl.BlockSpec((1, d), lambda i: (0, 0)),
                  pl.BlockSpec((1, d), lambda i: (0, 0))],
        out_specs=pl.BlockSpec((tm, d), lambda i: (i, 0)),
        out_shape=jax.ShapeDtypeStruct((t, d), F32),
        scratch_shapes=[pltpu.VMEM((2, tm // SUBLANES, SUBLANES, d), F32), pltpu.SemaphoreType.DMA((2,))],
        compiler_params=_params(("arbitrary",)),
        name="combine1",
    )(pos, pos, ysort, hxf, g, b)


def _dispatch_pairs(e1, e2, tb, tm):
    t = e1.shape[0]
    lo = jnp.minimum(e1, e2)
    hi = jnp.maximum(e1, e2)
    epg = EXPERTS_PER_GROUP
    cls = (lo // epg) * (epg * epg) + (lo % epg) * epg + (hi % epg)
    cs, order = lax.sort((cls, jnp.arange(t, dtype=jnp.int32)), num_keys=1, is_stable=True)
    counts = jnp.sum((cls[:, None] == jnp.arange(N_PAIR_CLASSES, dtype=jnp.int32)[None, :]).astype(jnp.int32), axis=0)
    starts = jnp.cumsum(counts) - counts
    nb = (counts + tb - 1) // tb
    bend = jnp.cumsum(nb)
    bstart = bend - nb
    nblk = -(-t // tb) + N_REAL_PAIRS
    blk = jnp.arange(nblk, dtype=jnp.int32)
    bcls = jnp.minimum(jnp.sum((blk[:, None] >= bend[None, :]).astype(jnp.int32), axis=1), N_PAIR_CLASSES - 1)
    grp = bcls // (epg * epg)
    elo = (grp * epg + (bcls % (epg * epg)) // epg).astype(jnp.int32)
    ehi = (grp * epg + bcls % epg).astype(jnp.int32)
    row0 = starts[bcls] + (blk - bstart[bcls]) * tb
    idx = jnp.clip(row0[:, None] + jnp.arange(tb, dtype=jnp.int32)[None, :], 0, t - 1)
    tok = order[idx].astype(jnp.int32)
    delta = bstart * tb - starts
    onehot = cs[:, None] == jnp.arange(N_PAIR_CLASSES, dtype=jnp.int32)[None, :]
    dest_sorted = jnp.arange(t, dtype=jnp.int32) + jnp.sum(jnp.where(onehot, delta[None, :], 0), axis=1)
    _, pos = lax.sort_key_val(order, dest_sorted.astype(jnp.int32))
    return elo, ehi, bend[-1:].astype(jnp.int32), tok.reshape(nblk, 1, tb), pos.reshape(t // tm, 1, tm)


def _moe_pairs(hx, prm, tb, tm):
    b, s, dx = hx.shape
    t = b * s
    hxf = hx.reshape(t, dx)
    experts = hxf[:, D_MODEL:D_MODEL + 2].astype(jnp.int32)
    elo, ehi, nused, tok, pos = _dispatch_pairs(experts[:, 0], experts[:, 1], tb, tm)
    ysort = _moe_pair_ffn(hxf, elo, ehi, nused, tok, prm["moe_w1"], prm["moe_w3"], prm["moe_w2"], tb)
    y = _combine1(pos, ysort, hxf, prm["ln3_g"], prm["ln3_b"], tm)
    return y.reshape(b, s, D_MODEL)


def _trunk(x, shift_prev, state0, past_k, past_v, mk, mv, prm, lam, lam_init, cfg):
    b, s, _ = x.shape
    pa, q, k, v, kb, vb, kb1 = _inproj(x, prm["w_in"], cfg["in_bb"], cfg["in_ts"],
                                       cfg["attn_tq"] if past_k is None else 0)
    oa, sfin = _rwkv(pa, shift_prev, _state_to_pairs(state0), prm, cfg["rwkv_tb"], cfg["rwkv_chunk"],
                     cfg["rwkv_group"])
    if past_k is None:
        ob = _attn_prompt(lam, q, kb, kb1, vb, prm["subln"].reshape(DIFF_V, 1), cfg["attn_tq"], lam_init)
    else:
        ob = _attn_cached(lam, q, kb, vb, past_k, past_v, prm["subln"], lam_init)
    hx = _post(x, oa, ob, mk, mv, prm, cfg["post_tm"], cfg["post_split"])
    moe = _moe_pairs if cfg["moe_pairs"] else _moe
    y = moe(hx, prm, cfg["moe_tb"], cfg["comb_tm"])
    return y, k, v, _pairs_to_state(sfin), pa[:, s - 1:s, :]


def _prep(w, l):
    row = lambda a: a[l].reshape(1, -1).astype(F32)
    idx = jnp.arange(D_RWKV, dtype=jnp.int32) // RWKV_HEAD
    wr = jnp.zeros((D_MODEL, LANES), F32)
    wr = wr.at[:, :N_EXPERTS].set(w["moe_w_expert"][l]).at[:, N_EXPERTS:N_EXPERTS + N_GROUPS].set(w["moe_w_group"][l])
    wr_hi = wr.astype(BF16)
    br = jnp.zeros((1, LANES), F32)
    br = br.at[0, :N_EXPERTS].set(w["moe_b_expert"][l]).at[0, N_EXPERTS:N_EXPERTS + N_GROUPS].set(w["moe_b_group"][l])
    return {
        "w_in": w["w_in"][l].astype(BF16),
        "mu": row(w["rwkv_mu"]), "w0": row(w["rwkv_w0"]), "w2": w["rwkv_w2"][l].astype(BF16),
        "a0": row(w["rwkv_a0"]), "a2": w["rwkv_a2"][l].astype(BF16), "g2": w["rwkv_g2"][l].astype(BF16),
        "k_k": row(w["rwkv_k_k"]), "k_a": row(w["rwkv_k_a"]), "r_k": row(w["rwkv_r_k"]),
        "ln_w": row(w["rwkv_ln_w"]), "ln_b": row(w["rwkv_ln_b"]),
        "hsum": (idx[:, None] == idx[None, :]).astype(BF16),
        "subln": row(w["diff_subln"]),
        "w_out": w["w_out"][l].astype(BF16), "wq": w["mem_wq"][l].astype(BF16), "wo": w["mem_wo"][l].astype(BF16),
        "ln1_g": row(w["ln1_g"]), "ln1_b": row(w["ln1_b"]), "ln2_g": row(w["ln2_g"]), "ln2_b": row(w["ln2_b"]),
        "ln3_g": row(w["ln3_g"]), "ln3_b": row(w["ln3_b"]),
        "wr_hi": wr_hi, "wr_lo": (wr - wr_hi.astype(F32)).astype(BF16), "br": br,
        "moe_w1": w["moe_w1"][l].astype(BF16), "moe_w3": w["moe_w3"][l].astype(BF16),
        "moe_w2": w["moe_w2"][l].astype(BF16),
    }


def _tile(n, pref):
    return pref if n % pref == 0 else n


def kernel(x_prompt, x_sample, mem_prompt, cache_diff_k, cache_diff_v, cache_mem_k, cache_mem_v, state_rwkv, state_shift, w_in, rwkv_mu, rwkv_w0, rwkv_w2, rwkv_a0, rwkv_a2, rwkv_g2, rwkv_k_k, rwkv_k_a, rwkv_r_k, rwkv_ln_w, rwkv_ln_b, diff_lq1, diff_lk1, diff_lq2, diff_lk2, diff_subln, w_out, ln1_g, ln1_b, mem_wq, mem_wk, mem_wv, mem_wo, ln2_g, ln2_b, moe_w_group, moe_b_group, moe_w_expert, moe_b_expert, moe_w1, moe_w3, moe_w2, ln3_g, ln3_b):
    w = dict(w_in=w_in, rwkv_mu=rwkv_mu, rwkv_w0=rwkv_w0, rwkv_w2=rwkv_w2, rwkv_a0=rwkv_a0, rwkv_a2=rwkv_a2,
             rwkv_g2=rwkv_g2, rwkv_k_k=rwkv_k_k, rwkv_k_a=rwkv_k_a, rwkv_r_k=rwkv_r_k, rwkv_ln_w=rwkv_ln_w,
             rwkv_ln_b=rwkv_ln_b, diff_subln=diff_subln, w_out=w_out, ln1_g=ln1_g, ln1_b=ln1_b, mem_wq=mem_wq,
             mem_wo=mem_wo, ln2_g=ln2_g, ln2_b=ln2_b, moe_w_group=moe_w_group, moe_b_group=moe_b_group,
             moe_w_expert=moe_w_expert, moe_b_expert=moe_b_expert, moe_w1=moe_w1, moe_w3=moe_w3, moe_w2=moe_w2,
             ln3_g=ln3_g, ln3_b=ln3_b)
    bp, sp, _ = x_prompt.shape
    bs, ss, _ = x_sample.shape
    depth = w_in.shape[0]
    cfg_p = dict(in_bb=1, in_ts=_tile(sp, 512), rwkv_tb=_tile(sp, 512), rwkv_chunk=CHUNK, rwkv_group=_tile(sp, 512) // CHUNK,
                 attn_tq=_tile(sp, 512), post_tm=_tile(sp, 512), post_split=2, moe_tb=256, moe_pairs=True, comb_tm=_tile(bp * sp, 256))
    cfg_s = dict(in_bb=bs, in_ts=ss, rwkv_tb=ss, rwkv_chunk=ss, rwkv_group=1, attn_tq=ss, post_tm=ss, post_split=1, moe_tb=64, moe_pairs=False,
                 comb_tm=_tile(bs * ss, 256))
    yp, ys = x_prompt, x_sample
    outs = [[] for _ in range(10)]
    for l in range(depth):
        prm = _prep(w, l)
        lam_init = 0.8 - 0.6 * math.exp(-0.3 * l)
        f = lambda z: z[l].astype(F32)
        lam = (jnp.exp(jnp.sum(f(diff_lq1) * f(diff_lk1))) - jnp.exp(jnp.sum(f(diff_lq2) * f(diff_lk2)))
               + lam_init).reshape(1)
        slopes = jnp.asarray(ALIBI_SLOPES, F32)
        lam = jnp.concatenate([lam, slopes, jnp.zeros((3,), F32)])
        mk_p, mv_p, mkb, mvb = _memproj(mem_prompt, mem_wk[l].astype(BF16), mem_wv[l].astype(BF16))
        yp, k_p, v_p, st_p, sh_p = _trunk(
            yp, jnp.zeros((bp, 1, C_RWKV), F32), jnp.zeros((bp, 2 * RWKV_PAIRS, RWKV_HEAD, RWKV_HEAD), F32),
            None, None, mkb, mvb, prm, lam, lam_init, cfg_p)
        ys, k_s, v_s, st_s, sh_s = _trunk(
            ys, state_shift[l], state_rwkv[l], cache_diff_k[l], cache_diff_v[l],
            cache_mem_k[l].astype(BF16), cache_mem_v[l].astype(BF16), prm, lam, lam_init, cfg_s)
        for lst, val in zip(outs, (k_p, v_p, mk_p, mv_p, st_p, sh_p, k_s, v_s, st_s, sh_s)):
            lst.append(val)
    return (yp, ys) + tuple(jnp.stack(o) for o in outs)
```

```python
import functools
import math

import jax
import jax.numpy as jnp
from jax import lax
from jax.experimental import pallas as pl
from jax.experimental.pallas import tpu as pltpu

F32 = jnp.float32
BF16 = jnp.bfloat16

D_MODEL = 1024
CHUNK = 64
D_RWKV = 512
RWKV_HEAD = 64
RWKV_PAIRS = D_RWKV // (2 * RWKV_HEAD)
W_LORA = 64
A_LORA = 64
G_LORA = 128
C_RWKV = 3 * D_RWKV + W_LORA + A_LORA + G_LORA
D_DIFF = 512
DIFF_HEADS = 4
DIFF_V = 128
DIFF_QK = 64
C_IN = C_RWKV + 3 * D_DIFF
MEM_HEADS = 4
MEM_HEAD = D_MODEL // MEM_HEADS
N_GROUPS = 4
EXPERTS_PER_GROUP = 8
N_EXPERTS = 32
D_EXPERT = 512
DEPTH = 1
DEEPNORM_ALPHA = (2.0 * DEPTH) ** 0.25
LN_EPS = 1e-5
RMS_EPS = 1e-5
RWKV_GN_EPS = 64e-5
NEG_INF = -1e30
LOG2E = math.log2(math.e)
ALIBI_SLOPES = tuple(2.0 ** (-8.0 * (h + 1) / DIFF_HEADS) for h in range(DIFF_HEADS))
VT_PAD = 16
LANES = 128
SUBLANES = 8
VMEM_LIMIT = 48 * 1024 * 1024


def _params(sem):
    return pltpu.CompilerParams(dimension_semantics=sem, vmem_limit_bytes=VMEM_LIMIT)


def _dot(a, b):
    return jnp.dot(a, b, preferred_element_type=F32)


def _dot_nt(a, b):
    return lax.dot_general(a, b, (((1,), (1,)), ((), ())), preferred_element_type=F32)


def _dot_tn(a, b):
    return lax.dot_general(a, b, (((0,), (0,)), ((), ())), preferred_element_type=F32)


def _split2(x):
    hi = x.astype(BF16)
    lo = (x - hi.astype(F32)).astype(BF16)
    return hi, lo


def _split3(x):
    hi = x.astype(BF16)
    r = x - hi.astype(F32)
    mid = r.astype(BF16)
    lo = (r - mid.astype(F32)).astype(BF16)
    return hi, mid, lo


def _layer_norm(x, g, b):
    mu = jnp.mean(x, axis=-1, keepdims=True)
    xc = x - mu
    var = jnp.mean(xc * xc, axis=-1, keepdims=True)
    return xc * lax.rsqrt(var + LN_EPS) * g + b


def _memproj_kernel(m_ref, wk_ref, wv_ref, k_ref, v_ref, kb_ref, vb_ref):
    m = m_ref[0].astype(BF16)
    k = _dot(m, wk_ref[...])
    v = _dot(m, wv_ref[...])
    k_ref[0] = k
    v_ref[0] = v
    kb_ref[0] = k.astype(BF16)
    vb_ref[0] = v.astype(BF16)


def _memproj(mem, wk, wv):
    b, n, d = mem.shape
    blk = pl.BlockSpec((1, n, d), lambda i: (i, 0, 0))
    wspec = pl.BlockSpec((d, d), lambda i: (0, 0))
    return pl.pallas_call(
        _memproj_kernel,
        grid=(b,),
        in_specs=[blk, wspec, wspec],
        out_specs=[blk, blk, blk, blk],
        out_shape=[jax.ShapeDtypeStruct((b, n, d), F32), jax.ShapeDtypeStruct((b, n, d), F32),
                   jax.ShapeDtypeStruct((b, n, d), BF16), jax.ShapeDtypeStruct((b, n, d), BF16)],
        compiler_params=_params(("arbitrary",)),
        name="memproj",
    )(mem, wk, wv)


def _inproj_kernel(x_ref, w_ref, pa_ref, q_ref, k_ref, v_ref, kb_ref, vb_ref, kb1_ref, *, attn_tile):
    bb, ts, d = x_ref.shape
    rows = bb * ts
    x = x_ref[...].reshape(rows, d).astype(BF16)
    pq = _dot(x, w_ref[:, C_RWKV:C_RWKV + D_DIFF])
    pk = _dot(x, w_ref[:, C_RWKV + D_DIFF:C_RWKV + 2 * D_DIFF])
    pv = _dot(x, w_ref[:, C_RWKV + 2 * D_DIFF:])
    pa_ref[...] = _dot(x, w_ref[:, :C_RWKV]).reshape(bb, ts, C_RWKV)
    if attn_tile:
        lane = lax.broadcasted_iota(jnp.int32, (rows, LANES), 1)
        kloc = lax.rem(pl.program_id(1) * ts + lax.broadcasted_iota(jnp.int32, (rows, LANES), 0), attn_tile)
        aug = jnp.where(lax.broadcasted_iota(jnp.int32, (VT_PAD, rows), 0) == 0, 1.0, 0.0).astype(BF16)
    for h in range(DIFF_HEADS):
        hs = slice(h * LANES, (h + 1) * LANES)
        qh = pq[:, hs]
        kh = pk[:, hs]
        k_ref[:, h] = kh.reshape(bb, ts, LANES)
        vh = pv[:, hs]
        v_ref[:, h] = vh.reshape(bb, ts, LANES)
        if attn_tile:
            q_ref[0, h] = (qh * (DIFF_QK ** -0.5 * LOG2E)).T.astype(BF16)
            vb_ref[0, h] = jnp.concatenate([vh.T.astype(BF16), aug], axis=0)
            bias = kloc.astype(F32) * (ALIBI_SLOPES[h] * LOG2E)
            b_hi = bias.astype(BF16).astype(F32)
            b_mid = (bias - b_hi).astype(BF16).astype(F32)
            b_lo = bias - b_hi - b_mid
            cols = jnp.where(lane == DIFF_QK, b_hi, jnp.where(lane == DIFF_QK + 1, b_mid,
                             jnp.where(lane == DIFF_QK + 2, b_lo, 0.0)))
            kb_ref[0, h] = jnp.where(lane < DIFF_QK, kh, cols).astype(BF16)
            kb1_ref[0, h] = jnp.where(lane < DIFF_QK, pltpu.roll(kh, DIFF_QK, axis=1), cols).astype(BF16)
        else:
            q_ref[:, h] = (qh * (DIFF_QK ** -0.5)).reshape(bb, ts, LANES).astype(BF16)
            vb_ref[:, h] = vh.reshape(bb, ts, LANES).astype(BF16)
            kb_ref[:, h] = kh.reshape(bb, ts, LANES).astype(BF16)
            kb1_ref[:, h] = kh.reshape(bb, ts, LANES).astype(BF16)


def _inproj(x, w_in_bf, bb, ts, attn_tile):
    b, s, d = x.shape
    hm = pl.BlockSpec((bb, DIFF_HEADS, ts, LANES), lambda i, j: (i, 0, j, 0))
    hshape = (b, DIFF_HEADS, s, LANES)
    if attn_tile:
        assert bb == 1
        qm = pl.BlockSpec((1, DIFF_HEADS, LANES, ts), lambda i, j: (i, 0, 0, j))
        qshape = (b, DIFF_HEADS, LANES, s)
        vm = pl.BlockSpec((1, DIFF_HEADS, LANES + VT_PAD, ts), lambda i, j: (i, 0, 0, j))
        vshape = (b, DIFF_HEADS, LANES + VT_PAD, s)
    else:
        qm, qshape, vm, vshape = hm, hshape, hm, hshape
    return pl.pallas_call(
        functools.partial(_inproj_kernel, attn_tile=attn_tile),
        grid=(b // bb, s // ts),
        in_specs=[pl.BlockSpec((bb, ts, d), lambda i, j: (i, j, 0)),
                  pl.BlockSpec((d, C_IN), lambda i, j: (0, 0))],
        out_specs=[pl.BlockSpec((bb, ts, C_RWKV), lambda i, j: (i, j, 0)), qm, hm, hm, hm, vm, hm],
        out_shape=[jax.ShapeDtypeStruct((b, s, C_RWKV), F32),
                   jax.ShapeDtypeStruct(qshape, BF16),
                   jax.ShapeDtypeStruct(hshape, F32), jax.ShapeDtypeStruct(hshape, F32),
                   jax.ShapeDtypeStruct(hshape, BF16), jax.ShapeDtypeStruct(vshape, BF16),
                   jax.ShapeDtypeStruct(hshape, BF16)],
        compiler_params=_params(("arbitrary", "arbitrary")),
        name="inproj",
    )(x, w_in_bf)


def _rwkv_kernel(pa_ref, shift_ref, s0_ref, mu_ref, w0_ref, w2_ref, a0_ref, a2_ref, g2_ref,
                 kk_ref, ka_ref, rk_ref, lnw_ref, lnb_ref, hsum_ref,
                 o_ref, sfin_ref,
                 carry_ref, al_s, be_s, kb_s, rb_s, v_s, gam_s, o_s, bonus_s, g_s, w1_s, n1_s, c0_s, n2_s,
                 *, chunk, group):
    L = chunk
    tb = pa_ref.shape[1]
    nchunk = tb // L
    t = pl.program_id(1)

    @pl.when(t == 0)
    def _():
        sfin_ref[...] = s0_ref[...]
        carry_ref[...] = shift_ref[0]

    p = pa_ref[0]
    prev = pltpu.roll(p, 1, axis=0)
    row = lax.broadcasted_iota(jnp.int32, (tb, 1), 0)
    prev = jnp.where(row == 0, carry_ref[...], prev)
    carry_ref[...] = p[tb - 1:tb, :]
    ps = p + (prev - p) * mu_ref[...]

    r = ps[:, :D_RWKV]
    k = ps[:, D_RWKV:2 * D_RWKV]
    v = ps[:, 2 * D_RWKV:3 * D_RWKV]
    o0 = 3 * D_RWKV
    w_lo = ps[:, o0:o0 + W_LORA]
    a_lo = ps[:, o0 + W_LORA:o0 + W_LORA + A_LORA]
    g_lo = ps[:, o0 + W_LORA + A_LORA:]

    hsum = hsum_ref[...]

    def headsum(x):
        return _dot(x.astype(BF16), hsum)

    z = w0_ref[...] + _dot(jnp.tanh(w_lo).astype(BF16), w2_ref[...])
    lw = -math.exp(-0.5) / (1.0 + jnp.exp(-z))
    a = 1.0 / (1.0 + jnp.exp(-(a0_ref[...] + _dot(a_lo.astype(BF16), a2_ref[...]))))
    g_s[...] = _dot((1.0 / (1.0 + jnp.exp(-g_lo))).astype(BF16), g2_ref[...])
    kk = k * kk_ref[...]
    kk = kk / jnp.maximum(jnp.sqrt(headsum(kk * kk)), 1e-12)
    kmod = k * (1.0 + (a - 1.0) * ka_ref[...])
    bonus_s[...] = headsum(r * kmod * rk_ref[...]) * v

    bi = lax.broadcasted_iota(jnp.int32, (tb, tb), 0)
    bj = lax.broadcasted_iota(jnp.int32, (tb, tb), 1)
    tri = (((bi // L) == (bj // L)) & (bi >= bj)).astype(BF16)
    h3 = _split3(lw)
    cum = _dot(tri, h3[0]) + _dot(tri, h3[1]) + _dot(tri, h3[2])
    gam = jnp.exp(cum)
    igam = jnp.exp(-cum)
    gam_s[...] = gam
    al_s[...] = kk * jnp.exp(cum - lw)
    be_s[...] = kk * a * igam
    kb_s[...] = kmod * igam
    rb_s[...] = r * gam
    v_s[...] = v

    l2 = 2 * L
    ri = lax.broadcasted_iota(jnp.int32, (l2, l2), 0)
    ci = lax.broadcasted_iota(jnp.int32, (l2, l2), 1)
    same = (ri // L) == (ci // L)
    strict = same & (ri > ci)
    incl = same & (ri >= ci)
    eye = (ri == ci).astype(F32)
    lane = lax.broadcasted_iota(jnp.int32, (1, LANES), 1)
    first = lane < RWKV_HEAD

    def stack(x):
        return jnp.concatenate([jnp.where(first, x, 0.0), jnp.where(first, 0.0, x)], axis=0)

    def phase_a(cg, carry):
        cs = []
        for gi in range(group):
            c = cg * group + gi
            sl = pl.ds(pl.multiple_of(c * L, L), L)
            gl = gam_s[pl.ds(c * L + (L - 1), 1), :]
            for j in range(RWKV_PAIRS):
                ls = slice(j * LANES, (j + 1) * LANES)
                cs.append(dict(c=c, j=j, gl=gl[:, ls], al2=stack(al_s[sl, ls]), rb2=stack(rb_s[sl, ls]),
                               be2=stack(be_s[sl, ls]), kb2=stack(kb_s[sl, ls]), v2=stack(v_s[sl, ls])))
        for d in cs:
            x = jnp.concatenate([d["al2"], d["rb2"]], axis=0).astype(BF16)
            y = jnp.concatenate([d["be2"], d["kb2"]], axis=0).astype(BF16)
            gmat = _dot_nt(x, y)
            ab = jnp.where(strict, gmat[:l2, :l2], 0.0)
            d["ak"] = jnp.where(strict, gmat[:l2, l2:], 0.0).astype(BF16)
            d["rbm"] = jnp.where(incl, gmat[l2:, :l2], 0.0).astype(BF16)
            d["rkm"] = jnp.where(incl, gmat[l2:, l2:], 0.0).astype(BF16)
            d["pw"] = ab
            d["tm"] = eye - ab
        n = 2
        while n < L:
            for d in cs:
                pb = d["pw"].astype(BF16)
                d["pw"] = _dot(pb, pb)
            for d in cs:
                d["tm"] = _dot(d["tm"].astype(BF16), (eye + d["pw"]).astype(BF16))
            n *= 2
        for d in cs:
            d["akv"] = _dot(d["ak"], d["v2"].astype(BF16))
        for d in cs:
            zz = jnp.concatenate([d["al2"], d["akv"]], axis=1).astype(BF16)
            d["m"] = _dot(d["tm"].astype(BF16), zz)
        for d in cs:
            rm = _dot(d["rbm"], d["m"].astype(BF16))
            rkv = _dot(d["rkm"], d["v2"].astype(BF16))
            n1_s[d["c"], d["j"]] = (d["rb2"] - rm[:, :LANES]).astype(BF16)
            n2_s[d["c"], d["j"]] = rkv - rm[:, LANES:]
        for d in cs:
            m1 = d["m"][:, :LANES].astype(BF16)
            m2 = d["m"][:, LANES:]
            w1 = _dot_tn(m1, d["be2"].astype(BF16)) * d["gl"]
            lhs = jnp.concatenate([d["v2"], -m2], axis=0).astype(BF16)
            rhs = jnp.concatenate([d["kb2"], d["be2"]], axis=0).astype(BF16)
            w1_s[d["c"], d["j"]] = w1.astype(BF16)
            c0_s[d["c"], d["j"]] = _dot_tn(lhs, rhs) * d["gl"]
        return carry

    lax.fori_loop(0, nchunk // group, phase_a, 0)

    def phase_b(c, carry):
        gl = gam_s[pl.ds(c * L + (L - 1), 1), :]
        outs = []
        for j in range(RWKV_PAIRS):
            sb = sfin_ref[0, j]
            sbb = sb.astype(BF16)
            o2 = _dot_nt(n1_s[c, j], sbb) + n2_s[c, j]
            outs.append(o2[:L] + o2[L:])
            sfin_ref[0, j] = sb * gl[:, j * LANES:(j + 1) * LANES] - _dot(sbb, w1_s[c, j]) + c0_s[c, j]
        o_s[pl.ds(pl.multiple_of(c * L, L), L), :] = jnp.concatenate(outs, axis=1)
        return carry

    lax.fori_loop(0, nchunk, phase_b, 0)

    o = o_s[...]
    inv_n = 1.0 / RWKV_HEAD
    mu_h = headsum(o) * inv_n
    oc = o - mu_h
    var_h = headsum(oc * oc) * inv_n
    on = oc * lax.rsqrt(var_h + RWKV_GN_EPS) * lnw_ref[...] + lnb_ref[...]
    o_ref[0] = ((on + bonus_s[...]) * g_s[...]).astype(o_ref.dtype)


def _rwkv(pa, shift_prev, s0_pairs, prm, tb, chunk, group):
    b, s, _ = pa.shape
    nchunk = tb // chunk
    row = lambda n: pl.BlockSpec((1, n), lambda i, j: (0, 0))
    mat = lambda m, n: pl.BlockSpec((m, n), lambda i, j: (0, 0))
    sspec = pl.BlockSpec((1, RWKV_PAIRS, LANES, LANES), lambda i, j: (i, 0, 0, 0))
    coef = lambda rows, dt: pltpu.VMEM((nchunk, RWKV_PAIRS, rows, LANES), dt)
    return pl.pallas_call(
        functools.partial(_rwkv_kernel, chunk=chunk, group=group),
        grid=(b, s // tb),
        in_specs=[pl.BlockSpec((1, tb, C_RWKV), lambda i, j: (i, j, 0)),
                  pl.BlockSpec((1, 1, C_RWKV), lambda i, j: (i, 0, 0)),
                  sspec,
                  row(C_RWKV), row(D_RWKV), mat(W_LORA, D_RWKV), row(D_RWKV), mat(A_LORA, D_RWKV),
                  mat(G_LORA, D_RWKV), row(D_RWKV), row(D_RWKV), row(D_RWKV), row(D_RWKV), row(D_RWKV),
                  mat(D_RWKV, D_RWKV)],
        out_specs=[pl.BlockSpec((1, tb, D_RWKV), lambda i, j: (i, j, 0)), sspec],
        out_shape=[jax.ShapeDtypeStruct((b, s, D_RWKV), BF16),
                   jax.ShapeDtypeStruct((b, RWKV_PAIRS, LANES, LANES), F32)],
        scratch_shapes=[pltpu.VMEM((1, C_RWKV), F32)] + [pltpu.VMEM((tb, D_RWKV), F32)] * 9
        + [coef(LANES, BF16), coef(2 * chunk, BF16), coef(LANES, F32), coef(2 * chunk, F32)],
        compiler_params=_params(("arbitrary", "arbitrary")),
        name="rwkv",
    )(pa, shift_prev, s0_pairs, prm["mu"], prm["w0"], prm["w2"], prm["a0"], prm["a2"], prm["g2"],
      prm["k_k"], prm["k_a"], prm["r_k"], prm["ln_w"], prm["ln_b"], prm["hsum"])


def _state_to_pairs(s):
    b = s.shape[0]
    s = s.reshape(b, RWKV_PAIRS, 2, RWKV_HEAD, RWKV_HEAD)
    z = jnp.zeros_like(s[:, :, 0])
    top = jnp.concatenate([s[:, :, 0], z], axis=-1)
    bot = jnp.concatenate([z, s[:, :, 1]], axis=-1)
    return jnp.concatenate([top, bot], axis=-2)


def _pairs_to_state(sp):
    b = sp.shape[0]
    h = RWKV_HEAD
    s = jnp.stack([sp[:, :, :h, :h], sp[:, :, h:, h:]], axis=2)
    return s.reshape(b, 2 * RWKV_PAIRS, h, h)


def _diff_finish(accs, ls, lam, sub, lam_init):
    o = accs[0] / ls[0] - lam * (accs[1] / ls[1])
    o = o * lax.rsqrt(jnp.mean(o * o, axis=-1, keepdims=True) + RMS_EPS)
    return o * sub * (1.0 - lam_init)


def _attn_kernel(lam_ref, qt_ref, k0_ref, k1_ref, vt_ref, tab_ref, sub_ref, o_ref, acc_ref, s_ref, m_ref, *, tq,
                 lam_init):
    h = pl.program_id(1)
    slope = lam_ref[1 + h] * LOG2E
    lam = lam_ref[0]
    row = lax.broadcasted_iota(jnp.int32, (LANES, 1), 0)
    ones_rows = (row >= DIFF_QK) & (row < DIFF_QK + 3)
    fill = jnp.where(ones_rows, 1.0, 0.0).astype(BF16)
    k_refs = (k0_ref, k1_ref)

    def rhs(q_first):
        return jnp.where(row < DIFF_QK, q_first, fill)

    def q_tile(qi, carry):
        qt = qt_ref[0, 0, :, pl.ds(pl.multiple_of(qi * tq, tq), tq)]
        rs = (rhs(qt), rhs(jnp.concatenate([qt[DIFF_QK:], qt[:DIFF_QK]], axis=0)))
        acc_ref[...] = jnp.zeros_like(acc_ref)
        m_ref[...] = jnp.full_like(m_ref, NEG_INF)

        def scores(j, slot):
            ds = pl.ds(pl.multiple_of(j * tq, tq), tq)
            for c in range(2):
                s_ref[slot, c] = _dot(k_refs[c][0, 0, ds, :], rs[c])

        def softmax_pv(j, slot, diag):
            vt = vt_ref[0, 0, :, pl.ds(pl.multiple_of(j * tq, tq), tq)]
            off = slope * jnp.asarray((j - qi) * tq, F32)
            for c in range(2):
                s = s_ref[slot, c]
                if diag:
                    s = s + tab_ref[0]
                m = m_ref[c]
                m_new = jnp.maximum(m, jnp.max(s, axis=0, keepdims=True) + off)
                alpha = jnp.exp2(m - m_new)
                p = jnp.exp2(s - (m_new - off))
                m_ref[c] = m_new
                acc_ref[c] = alpha * acc_ref[c] + _dot(vt, p.astype(BF16))

        def pair(jj, c):
            j = 2 * jj
            scores(j + 1, 0)
            softmax_pv(j, 1, False)
            scores(jnp.minimum(j + 2, jnp.maximum(qi - 1, 0)), 1)
            softmax_pv(j + 1, 0, False)
            return c

        scores(qi, 0)
        scores(0, 1)
        softmax_pv(qi, 0, True)
        lax.fori_loop(0, qi // 2, pair, 0)

        @pl.when(lax.rem(qi, 2) == 1)
        def _():
            softmax_pv(qi - 1, 1, False)

        a0 = acc_ref[0]
        a1 = acc_ref[1]
        o = a0[:DIFF_V] / a0[DIFF_V:DIFF_V + 1] - lam * (a1[:DIFF_V] / a1[DIFF_V:DIFF_V + 1])
        o = o * lax.rsqrt(jnp.mean(o * o, axis=0, keepdims=True) + RMS_EPS)
        o = o * (sub_ref[...] * (1.0 - lam_init))
        o_ref[0, pl.ds(pl.multiple_of(qi * tq, tq), tq), :] = o.T.astype(o_ref.dtype)
        return carry

    lax.fori_loop(0, qt_ref.shape[3] // tq, q_tile, 0)


def _attn_diag_table(tq):
    slopes = jnp.asarray(ALIBI_SLOPES, F32) * LOG2E
    pos = jnp.arange(tq, dtype=jnp.int32)
    kp, qp = pos[:, None], pos[None, :]
    rel = jnp.where(kp <= qp, 0, 2 * (qp - kp)).astype(F32)
    vis = (kp // CHUNK) <= (qp // CHUNK)
    return jnp.where(vis[None], slopes[:, None, None] * rel[None], NEG_INF)


def _attn_prompt(lam, qt, k0, k1, vt, subln_col, tq, lam_init):
    b, h, _, s = qt.shape
    kspec = pl.BlockSpec((1, 1, s, LANES), lambda i, j: (i, j, 0, 0))
    return pl.pallas_call(
        functools.partial(_attn_kernel, tq=tq, lam_init=lam_init),
        grid=(b, h),
        in_specs=[pl.BlockSpec(memory_space=pltpu.SMEM),
                  pl.BlockSpec((1, 1, LANES, s), lambda i, j: (i, j, 0, 0)),
                  kspec, kspec,
                  pl.BlockSpec((1, 1, LANES + VT_PAD, s), lambda i, j: (i, j, 0, 0)),
                  pl.BlockSpec((1, tq, tq), lambda i, j: (j, 0, 0)),
                  pl.BlockSpec((DIFF_V, 1), lambda i, j: (0, 0))],
        out_specs=pl.BlockSpec((1, s, LANES), lambda i, j: (i, 0, j)),
        out_shape=jax.ShapeDtypeStruct((b, s, D_DIFF), BF16),
        scratch_shapes=[pltpu.VMEM((2, DIFF_V + VT_PAD, tq), F32), pltpu.VMEM((2, 2, tq, tq), F32),
                        pltpu.VMEM((2, 1, tq), F32)],
        compiler_params=_params(("arbitrary", "arbitrary")),
        name="attn_prompt",
    )(lam, qt, k0, k1, vt, _attn_diag_table(tq), subln_col)


def _attn_cached_kernel(lam_ref, q_ref, kn_ref, vn_ref, kp_ref, vp_ref, sub_ref, o_ref, *, lam_init):
    h = pl.program_id(1)
    slope = lam_ref[1 + h]
    s_new = q_ref.shape[2]
    past = kp_ref.shape[2]
    q = q_ref[0, 0]
    kp = kp_ref[0, 0].astype(BF16)
    vp = vp_ref[0, 0].astype(BF16)
    kn = kn_ref[0, 0]
    vn = vn_ref[0, 0]
    q_pos = past + lax.broadcasted_iota(jnp.int32, (s_new, 1), 0)

    def bias(k_pos):
        dist = jnp.abs(q_pos - k_pos).astype(F32)
        vis = (k_pos // CHUNK) <= (q_pos // CHUNK)
        return jnp.where(vis, -slope * dist, NEG_INF), vis

    b_past, vis_past = bias(lax.broadcasted_iota(jnp.int32, (1, past), 1))
    b_new, vis_new = bias(past + lax.broadcasted_iota(jnp.int32, (1, s_new), 1))
    accs, ls = [], []
    for c in range(2):
        qc = q[:, c * DIFF_QK:(c + 1) * DIFF_QK]
        sp = jnp.where(vis_past, _dot_nt(qc, kp[:, c * DIFF_QK:(c + 1) * DIFF_QK]) + b_past, NEG_INF)
        sn = jnp.where(vis_new, _dot_nt(qc, kn[:, c * DIFF_QK:(c + 1) * DIFF_QK]) + b_new, NEG_INF)
        m = jnp.maximum(jnp.max(sp, axis=-1, keepdims=True), jnp.max(sn, axis=-1, keepdims=True))
        pp = jnp.exp(sp - m)
        pn = jnp.exp(sn - m)
        ls.append(jnp.sum(pp, axis=-1, keepdims=True) + jnp.sum(pn, axis=-1, keepdims=True))
        accs.append(_dot(pp.astype(BF16), vp) + _dot(pn.astype(BF16), vn))
    o = _diff_finish(accs, ls, lam_ref[0], sub_ref[...], lam_init)
    o_ref[0] = o.astype(o_ref.dtype)


def _attn_cached(lam, q, kb, vb, past_k, past_v, subln, lam_init):
    b, h, s, _ = q.shape
    past = past_k.shape[2]
    new = pl.BlockSpec((1, 1, s, LANES), lambda i, j: (i, j, 0, 0))
    old = pl.BlockSpec((1, 1, past, LANES), lambda i, j: (i, j, 0, 0))
    return pl.pallas_call(
        functools.partial(_attn_cached_kernel, lam_init=lam_init),
        grid=(b, h),
        in_specs=[pl.BlockSpec(memory_space=pltpu.SMEM), new, new, new, old, old,
                  pl.BlockSpec((1, DIFF_V), lambda i, j: (0, 0))],
        out_specs=pl.BlockSpec((1, s, LANES), lambda i, j: (i, 0, j)),
        out_shape=jax.ShapeDtypeStruct((b, s, D_DIFF), BF16),
        compiler_params=_params(("arbitrary", "arbitrary")),
        name="attn_cached",
    )(lam, q, kb, vb, past_k, past_v, subln)


def _route(lg):
    tm = lg.shape[0]
    lane = lax.broadcasted_iota(jnp.int32, (tm, LANES), 1)
    lane_f = lane.astype(F32)
    big = 1e9
    low = -3e38
    is_g = (lane >= N_EXPERTS) & (lane < N_EXPERTS + N_GROUPS)
    gl = jnp.where(is_g, lg, low)
    gmax = jnp.max(gl, axis=-1, keepdims=True)
    grp_lane = jnp.min(jnp.where(gl == gmax, lane_f, big), axis=-1, keepdims=True)
    gsum = jnp.sum(jnp.where(is_g, jnp.exp(gl - gmax), 0.0), axis=-1, keepdims=True)
    g_prob = 1.0 / gsum
    grp = grp_lane - float(N_EXPERTS)
    lane_grp = (lane // EXPERTS_PER_GROUP).astype(F32)
    el = jnp.where(lane_grp == grp, lg, low)
    v1 = jnp.max(el, axis=-1, keepdims=True)
    i1 = jnp.min(jnp.where(el == v1, lane_f, big), axis=-1, keepdims=True)
    el2 = jnp.where(lane_f == i1, low, el)
    v2 = jnp.max(el2, axis=-1, keepdims=True)
    i2 = jnp.min(jnp.where(el2 == v2, lane_f, big), axis=-1, keepdims=True)
    e21 = jnp.exp(v2 - v1)
    den = 1.0 + e21
    gate1 = g_prob / den
    gate2 = g_prob * e21 / den
    out = jnp.where(lane == 0, i1, jnp.where(lane == 1, i2, jnp.where(lane == 2, gate1,
                    jnp.where(lane == 3, gate2, 0.0))))
    return out


def _post_kernel(x_ref, oa_ref, ob_ref, mk_ref, mv_ref, wout_ref, wq_ref, wo_ref,
                 g1_ref, b1_ref, g2_ref, b2_ref, wrh_ref, wrl_ref, br_ref, hx_ref, *, nsplit):
    tm = x_ref.shape[1]
    rows = tm // nsplit
    parts = [slice(i * rows, (i + 1) * rows) for i in range(nsplit)]
    mix = [_dot(oa_ref[0, r, :], wout_ref[:D_RWKV, :]) + _dot(ob_ref[0, r, :], wout_ref[D_RWKV:, :]) for r in parts]
    h1 = [_layer_norm(DEEPNORM_ALPHA * x_ref[0, r, :] + m, g1_ref[...], b1_ref[...]) for r, m in zip(parts, mix)]
    q = [(_dot(h.astype(BF16), wq_ref[...]) * (MEM_HEAD ** -0.5)).astype(BF16) for h in h1]
    heads = [[] for _ in parts]
    for h in range(MEM_HEADS):
        hs = slice(h * MEM_HEAD, (h + 1) * MEM_HEAD)
        for i in range(nsplit):
            s = _dot_nt(q[i][:, hs], mk_ref[0, :, hs])
            m = jnp.max(s, axis=-1, keepdims=True)
            p = jnp.exp(s - m)
            p = p / jnp.sum(p, axis=-1, keepdims=True)
            heads[i].append(_dot(p.astype(BF16), mv_ref[0, :, hs]).astype(BF16))
    att = [_dot(jnp.concatenate(hd, axis=-1), wo_ref[...]) for hd in heads]
    h2 = [_layer_norm(DEEPNORM_ALPHA * h + a, g2_ref[...], b2_ref[...]) for h, a in zip(h1, att)]
    for r, h in zip(parts, h2):
        hx_ref[0, r, :D_MODEL] = h
        hi, lo = _split2(h)
        lg = _dot(hi, wrh_ref[...]) + _dot(hi, wrl_ref[...]) + _dot(lo, wrh_ref[...]) + br_ref[...]
        hx_ref[0, r, D_MODEL:] = _route(lg)


def _post(x, oa, ob, mk, mv, prm, tm, nsplit):
    b, s, d = x.shape
    n = mk.shape[1]
    tok = lambda w: pl.BlockSpec((1, tm, w), lambda i, j: (i, j, 0))
    mem = pl.BlockSpec((1, n, d), lambda i, j: (i, 0, 0))
    mat = lambda m_, n_: pl.BlockSpec((m_, n_), lambda i, j: (0, 0))
    return pl.pallas_call(
        functools.partial(_post_kernel, nsplit=nsplit),
        grid=(b, s // tm),
        in_specs=[tok(d), tok(D_RWKV), tok(D_DIFF), mem, mem, mat(d, d), mat(d, d), mat(d, d),
                  mat(1, d), mat(1, d), mat(1, d), mat(1, d), mat(d, LANES), mat(d, LANES), mat(1, LANES)],
        out_specs=tok(d + LANES),
        out_shape=jax.ShapeDtypeStruct((b, s, d + LANES), F32),
        compiler_params=_params(("arbitrary", "arbitrary")),
        name="post",
    )(x, oa, ob, mk, mv, prm["w_out"], prm["wq"], prm["wo"], prm["ln1_g"], prm["ln1_b"],
      prm["ln2_g"], prm["ln2_b"], prm["wr_hi"], prm["wr_lo"], prm["br"])


def _moe_kernel(bexp_ref, nused_ref, tok_ref, tokn_ref, h2_hbm, w1_ref, w3_ref, w2_ref, y_ref, xbuf, sem):
    i = pl.program_id(0)
    nblk = pl.num_programs(0)
    nused = nused_ref[0]
    ngrp = xbuf.shape[1]
    tb = ngrp * SUBLANES
    slot = lax.rem(i, 2)

    def issue_rows(idx_ref, sl):
        def group(g, c):
            for u in range(SUBLANES):
                r = idx_ref[0, 0, g * SUBLANES + u]
                pltpu.make_async_copy(h2_hbm.at[pl.ds(r, 1)], xbuf.at[sl, g, pl.ds(u, 1)], sem.at[sl]).start()
            return c
        lax.fori_loop(0, ngrp, group, 0)

    def wait_rows(sl):
        pltpu.make_async_copy(xbuf.at[1 - sl], xbuf.at[sl], sem.at[sl]).wait()

    @pl.when(i == 0)
    def _():
        issue_rows(tok_ref, 0)

    @pl.when(i < nused)
    def _():
        wait_rows(slot)
        issue_rows(tokn_ref, 1 - slot)
        x = xbuf[slot].reshape(tb, xbuf.shape[3]).astype(BF16)
        a = _dot(x, w1_ref[0])
        g = _dot(x, w3_ref[0])
        hmid = (a / (1.0 + jnp.exp(-a))) * g
        y_ref[...] = _dot(hmid.astype(BF16), w2_ref[0])

        @pl.when(i == nblk - 1)
        def _():
            wait_rows(1 - slot)

    @pl.when(i >= nused)
    def _():
        y_ref[...] = jnp.zeros_like(y_ref)

        @pl.when(i == nused)
        def _():
            wait_rows(slot)


def _moe_ffn(h2f, blk_exp, nused, tok, w1, w3, w2, tb):
    nblk = tok.shape[0]
    d = h2f.shape[1]
    grid_spec = pltpu.PrefetchScalarGridSpec(
        num_scalar_prefetch=2,
        grid=(nblk,),
        in_specs=[pl.BlockSpec((1, 1, tb), lambda i, be, nu: (i, 0, 0), memory_space=pltpu.SMEM),
                  pl.BlockSpec((1, 1, tb), lambda i, be, nu: (jnp.minimum(i + 1, nblk - 1), 0, 0),
                               memory_space=pltpu.SMEM),
                  pl.BlockSpec(memory_space=pl.ANY),
                  pl.BlockSpec((1, d, D_EXPERT), lambda i, be, nu: (be[i], 0, 0)),
                  pl.BlockSpec((1, d, D_EXPERT), lambda i, be, nu: (be[i], 0, 0)),
                  pl.BlockSpec((1, D_EXPERT, d), lambda i, be, nu: (be[i], 0, 0))],
        out_specs=pl.BlockSpec((tb, d), lambda i, be, nu: (i, 0)),
        scratch_shapes=[pltpu.VMEM((2, tb // SUBLANES, SUBLANES, d), F32), pltpu.SemaphoreType.DMA((2,))],
    )
    return pl.pallas_call(
        _moe_kernel,
        grid_spec=grid_spec,
        out_shape=jax.ShapeDtypeStruct((nblk * tb, d), F32),
        compiler_params=_params(("arbitrary",)),
        name="moe_ffn",
    )(blk_exp, nused, tok, tok, h2f, w1, w3, w2)


def _combine_kernel(pos_ref, posn_ref, ys_hbm, h2_ref, route_ref, g_ref, b_ref, y_ref, gbuf, sem):
    i = pl.program_id(0)
    n = pl.num_programs(0)
    tm = h2_ref.shape[0]
    slot = lax.rem(i, 2)

    ngrp = gbuf.shape[1]
    d = gbuf.shape[3]

    def issue_rows(idx_ref, sl):
        def group(g, c):
            for u in range(SUBLANES):
                r = idx_ref[0, 0, g * SUBLANES + u]
                pltpu.make_async_copy(ys_hbm.at[pl.ds(r, 1)], gbuf.at[sl, g, pl.ds(u, 1)], sem.at[sl]).start()
            return c
        lax.fori_loop(0, ngrp, group, 0)

    def wait_rows(sl):
        pltpu.make_async_copy(gbuf.at[1 - sl], gbuf.at[sl], sem.at[sl]).wait()

    @pl.when(i == 0)
    def _():
        issue_rows(pos_ref, 0)

    wait_rows(slot)
    issue_rows(posn_ref, 1 - slot)
    rt = route_ref[...]
    half = ngrp // 2
    moe = (rt[:, 2:3] * gbuf[slot, :half].reshape(tm, d) + rt[:, 3:4] * gbuf[slot, half:].reshape(tm, d))
    y_ref[...] = _layer_norm(DEEPNORM_ALPHA * h2_ref[...] + moe, g_ref[...], b_ref[...])

    @pl.when(i == n - 1)
    def _():
        wait_rows(1 - slot)


def _combine(pos, ysort, h2f, route, g, b, tm):
    t, d = h2f.shape
    n = t // tm
    return pl.pallas_call(
        _combine_kernel,
        grid=(n,),
        in_specs=[pl.BlockSpec((1, 1, 2 * tm), lambda i: (i, 0, 0), memory_space=pltpu.SMEM),
                  pl.BlockSpec((1, 1, 2 * tm), lambda i: (jnp.minimum(i + 1, n - 1), 0, 0),
                               memory_space=pltpu.SMEM),
                  pl.BlockSpec(memory_space=pl.ANY),
                  pl.BlockSpec((tm, d), lambda i: (i, 0)),
                  pl.BlockSpec((tm, LANES), lambda i: (i, 0)),
                  pl.BlockSpec((1, d), lambda i: (0, 0)),
                  pl.BlockSpec((1, d), lambda i: (0, 0))],
        out_specs=pl.BlockSpec((tm, d), lambda i: (i, 0)),
        out_shape=jax.ShapeDtypeStruct((t, d), F32),
        scratch_shapes=[pltpu.VMEM((2, 2 * tm // SUBLANES, SUBLANES, d), F32), pltpu.SemaphoreType.DMA((2,))],
        compiler_params=_params(("arbitrary",)),
        name="combine",
    )(pos, pos, ysort, h2f, route, g, b)


def _dispatch(expert, tb, tm):
    t = expert.shape[0]
    a = 2 * t
    ef = expert.reshape(a)
    order = jnp.argsort(ef, stable=True).astype(jnp.int32)
    es = ef[order]
    counts = jnp.sum((ef[:, None] == jnp.arange(N_EXPERTS, dtype=jnp.int32)[None, :]).astype(jnp.int32), axis=0)
    starts = jnp.cumsum(counts) - counts
    nb = (counts + tb - 1) // tb
    bend = jnp.cumsum(nb)
    bstart = bend - nb
    nblk = -(-a // tb) + N_EXPERTS
    blk = jnp.arange(nblk, dtype=jnp.int32)
    bexp = jnp.minimum(jnp.sum((blk[:, None] >= bend[None, :]).astype(jnp.int32), axis=1), N_EXPERTS - 1)
    row0 = starts[bexp] + (blk - bstart[bexp]) * tb
    idx = jnp.clip(row0[:, None] + jnp.arange(tb, dtype=jnp.int32)[None, :], 0, a - 1)
    tok = (order[idx] // 2).astype(jnp.int32).reshape(nblk, 1, tb)
    dest_sorted = bstart[es] * tb + (jnp.arange(a, dtype=jnp.int32) - starts[es])
    _, pos = lax.sort_key_val(order, dest_sorted.astype(jnp.int32))
    pos = pos.reshape(t, 2)
    pos = pos.reshape(t // tm, tm, 2).transpose(0, 2, 1).reshape(t // tm, 1, 2 * tm)
    return bexp, bend[-1:].astype(jnp.int32), tok, pos


def _moe(hx, prm, tb, tm):
    b, s, _ = hx.shape
    d = D_MODEL
    t = b * s
    h2f = hx[..., :d].reshape(t, d)
    rf = hx[..., d:].reshape(t, LANES)
    expert = rf[:, :2].astype(jnp.int32)
    bexp, nused, tok, pos = _dispatch(expert, tb, tm)
    ysort = _moe_ffn(h2f, bexp, nused, tok, prm["moe_w1"], prm["moe_w3"], prm["moe_w2"], tb)
    y = _combine(pos, ysort, h2f, rf, prm["ln3_g"], prm["ln3_b"], tm)
    return y.reshape(b, s, d)


N_PAIR_CLASSES = N_GROUPS * EXPERTS_PER_GROUP * EXPERTS_PER_GROUP
N_REAL_PAIRS = N_GROUPS * (EXPERTS_PER_GROUP * (EXPERTS_PER_GROUP - 1) // 2)


def _moe_pair_kernel(elo_ref, ehi_ref, nused_ref, tok_ref, tokn_ref, h2_hbm,
                     w1a_ref, w3a_ref, w2a_ref, w1b_ref, w3b_ref, w2b_ref, y_ref, xbuf, sem):
    i = pl.program_id(0)
    nblk = pl.num_programs(0)
    nused = nused_ref[0]
    ngrp = xbuf.shape[1]
    tb = ngrp * SUBLANES
    slot = lax.rem(i, 2)

    def issue_rows(idx_ref, sl):
        def group(g, c):
            for u in range(SUBLANES):
                r = idx_ref[0, 0, g * SUBLANES + u]
                pltpu.make_async_copy(h2_hbm.at[pl.ds(r, 1)], xbuf.at[sl, g, pl.ds(u, 1)], sem.at[sl]).start()
            return c
        lax.fori_loop(0, ngrp, group, 0)

    def wait_rows(sl):
        pltpu.make_async_copy(xbuf.at[1 - sl], xbuf.at[sl], sem.at[sl]).wait()

    def ffn(x, w1_ref, w3_ref, w2_ref):
        a = _dot(x, w1_ref[0])
        g = _dot(x, w3_ref[0])
        hmid = (a / (1.0 + jnp.exp(-a))) * g
        return _dot(hmid.astype(BF16), w2_ref[0])

    @pl.when(i == 0)
    def _():
        issue_rows(tok_ref, 0)

    @pl.when(i < nused)
    def _():
        wait_rows(slot)
        issue_rows(tokn_ref, 1 - slot)
        rows = xbuf[slot].reshape(tb, xbuf.shape[3])
        x = rows[:, :D_MODEL].astype(BF16)
        rt = rows[:, D_MODEL:]
        first_low = rt[:, 0:1] < rt[:, 1:2]
        g_lo = jnp.where(first_low, rt[:, 2:3], rt[:, 3:4])
        g_hi = jnp.where(first_low, rt[:, 3:4], rt[:, 2:3])
        y_ref[...] = g_lo * ffn(x, w1a_ref, w3a_ref, w2a_ref) + g_hi * ffn(x, w1b_ref, w3b_ref, w2b_ref)

        @pl.when(i == nblk - 1)
        def _():
            wait_rows(1 - slot)

    @pl.when(i >= nused)
    def _():
        y_ref[...] = jnp.zeros_like(y_ref)

        @pl.when(i == nused)
        def _():
            wait_rows(slot)


def _moe_pair_ffn(hxf, elo, ehi, nused, tok, w1, w3, w2, tb):
    nblk = tok.shape[0]
    d = D_MODEL
    wspec = lambda rows, cols, which: pl.BlockSpec(
        (1, rows, cols), (lambda i, lo, hi, nu: (lo[i], 0, 0)) if which == 0 else (lambda i, lo, hi, nu: (hi[i], 0, 0)))
    grid_spec = pltpu.PrefetchScalarGridSpec(
        num_scalar_prefetch=3,
        grid=(nblk,),
        in_specs=[pl.BlockSpec((1, 1, tb), lambda i, lo, hi, nu: (i, 0, 0), memory_space=pltpu.SMEM),
                  pl.BlockSpec((1, 1, tb), lambda i, lo, hi, nu: (jnp.minimum(i + 1, nblk - 1), 0, 0),
                               memory_space=pltpu.SMEM),
                  pl.BlockSpec(memory_space=pl.ANY),
                  wspec(d, D_EXPERT, 0), wspec(d, D_EXPERT, 0), wspec(D_EXPERT, d, 0),
                  wspec(d, D_EXPERT, 1), wspec(d, D_EXPERT, 1), wspec(D_EXPERT, d, 1)],
        out_specs=pl.BlockSpec((tb, d), lambda i, lo, hi, nu: (i, 0)),
        scratch_shapes=[pltpu.VMEM((2, tb // SUBLANES, SUBLANES, hxf.shape[1]), F32),
                        pltpu.SemaphoreType.DMA((2,))],
    )
    return pl.pallas_call(
        _moe_pair_kernel,
        grid_spec=grid_spec,
        out_shape=jax.ShapeDtypeStruct((nblk * tb, d), F32),
        compiler_params=_params(("arbitrary",)),
        name="moe_pair_ffn",
    )(elo, ehi, nused, tok, tok, hxf, w1, w3, w2, w1, w3, w2)


def _combine1_kernel(pos_ref, posn_ref, ys_hbm, hx_ref, g_ref, b_ref, y_ref, gbuf, sem):
    i = pl.program_id(0)
    n = pl.num_programs(0)
    tm = hx_ref.shape[0]
    slot = lax.rem(i, 2)
    ngrp = gbuf.shape[1]

    def issue_rows(idx_ref, sl):
        def group(g, c):
            for u in range(SUBLANES):
                r = idx_ref[0, 0, g * SUBLANES + u]
                pltpu.make_async_copy(ys_hbm.at[pl.ds(r, 1)], gbuf.at[sl, g, pl.ds(u, 1)], sem.at[sl]).start()
            return c
        lax.fori_loop(0, ngrp, group, 0)

    def wait_rows(sl):
        pltpu.make_async_copy(gbuf.at[1 - sl], gbuf.at[sl], sem.at[sl]).wait()

    @pl.when(i == 0)
    def _():
        issue_rows(pos_ref, 0)

    wait_rows(slot)
    issue_rows(posn_ref, 1 - slot)
    moe = gbuf[slot].reshape(tm, gbuf.shape[3])
    y_ref[...] = _layer_norm(DEEPNORM_ALPHA * hx_ref[:, :D_MODEL] + moe, g_ref[...], b_ref[...])

    @pl.when(i == n - 1)
    def _():
        wait_rows(1 - slot)


def _combine1(pos, ysort, hxf, g, b, tm):
    t, dx = hxf.shape
    d = D_MODEL
    n = t // tm
    return pl.pallas_call(
        _combine1_kernel,
        grid=(n,),
        in_specs=[pl.BlockSpec((1, 1, tm), lambda i: (i, 0, 0), memory_space=pltpu.SMEM),
                  pl.BlockSpec((1, 1, tm), lambda i: (jnp.minimum(i + 1, n - 1), 0, 0), memory_space=pltpu.SMEM),
                  pl.BlockSpec(memory_space=pl.ANY),
                  pl.BlockSpec((tm, dx), lambda i: (i, 0)),
                  pl.BlockSpec((1, d), lambda i: (0, 0)),
                  pl.BlockSpec((1, d), lambda i: (0, 0))],
        out_specs=pl.BlockSpec((tm, d), lambda i: (i, 0)),
        out_shape=jax.ShapeDtypeStruct((t, d), F32),
        scratch_shapes=[pltpu.VMEM((2, tm // SUBLANES, SUBLANES, d), F32), pltpu.SemaphoreType.DMA((2,))],
        compiler_params=_params(("arbitrary",)),
        name="combine1",
    )(pos, pos, ysort, hxf, g, b)


def _dispatch_pairs(e1, e2, tb, tm):
    t = e1.shape[0]
    lo = jnp.minimum(e1, e2)
    hi = jnp.maximum(e1, e2)
    epg = EXPERTS_PER_GROUP
    cls = (lo // epg) * (epg * epg) + (lo % epg) * epg + (hi % epg)
    cs, order = lax.sort((cls, jnp.arange(t, dtype=jnp.int32)), num_keys=1, is_stable=True)
    counts = jnp.sum((cls[:, None] == jnp.arange(N_PAIR_CLASSES, dtype=jnp.int32)[None, :]).astype(jnp.int32), axis=0)
    starts = jnp.cumsum(counts) - counts
    nb = (counts + tb - 1) // tb
    bend = jnp.cumsum(nb)
    bstart = bend - nb
    nblk = -(-t // tb) + N_REAL_PAIRS
    blk = jnp.arange(nblk, dtype=jnp.int32)
    bcls = jnp.minimum(jnp.sum((blk[:, None] >= bend[None, :]).astype(jnp.int32), axis=1), N_PAIR_CLASSES - 1)
    grp = bcls // (epg * epg)
    elo = (grp * epg + (bcls % (epg * epg)) // epg).astype(jnp.int32)
    ehi = (grp * epg + bcls % epg).astype(jnp.int32)
    row0 = starts[bcls] + (blk - bstart[bcls]) * tb
    idx = jnp.clip(row0[:, None] + jnp.arange(tb, dtype=jnp.int32)[None, :], 0, t - 1)
    tok = order[idx].astype(jnp.int32)
    delta = bstart * tb - starts
    onehot = cs[:, None] == jnp.arange(N_PAIR_CLASSES, dtype=jnp.int32)[None, :]
    dest_sorted = jnp.arange(t, dtype=jnp.int32) + jnp.sum(jnp.where(onehot, delta[None, :], 0), axis=1)
    _, pos = lax.sort_key_val(order, dest_sorted.astype(jnp.int32))
    return elo, ehi, bend[-1:].astype(jnp.int32), tok.reshape(nblk, 1, tb), pos.reshape(t // tm, 1, tm)


def _moe_pairs(hx, prm, tb, tm):
    b, s, dx = hx.shape
    t = b * s
    hxf = hx.reshape(t, dx)
    experts = hxf[:, D_MODEL:D_MODEL + 2].astype(jnp.int32)
    elo, ehi, nused, tok, pos = _dispatch_pairs(experts[:, 0], experts[:, 1], tb, tm)
    ysort = _moe_pair_ffn(hxf, elo, ehi, nused, tok, prm["moe_w1"], prm["moe_w3"], prm["moe_w2"], tb)
    y = _combine1(pos, ysort, hxf, prm["ln3_g"], prm["ln3_b"], tm)
    return y.reshape(b, s, D_MODEL)


def _trunk(x, shift_prev, state0, past_k, past_v, mk, mv, prm, lam, lam_init, cfg):
    b, s, _ = x.shape
    pa, q, k, v, kb, vb, kb1 = _inproj(x, prm["w_in"], cfg["in_bb"], cfg["in_ts"],
                                       cfg["attn_tq"] if past_k is None else 0)
    oa, sfin = _rwkv(pa, shift_prev, _state_to_pairs(state0), prm, cfg["rwkv_tb"], cfg["rwkv_chunk"],
                     cfg["rwkv_group"])
    if past_k is None:
        ob = _attn_prompt(lam, q, kb, kb1, vb, prm["subln"].reshape(DIFF_V, 1), cfg["attn_tq"], lam_init)
    else:
        ob = _attn_cached(lam, q, kb, vb, past_k, past_v, prm["subln"], lam_init)
    hx = _post(x, oa, ob, mk, mv, prm, cfg["post_tm"], cfg["post_split"])
    moe = _moe_pairs if cfg["moe_pairs"] else _moe
    y = moe(hx, prm, cfg["moe_tb"], cfg["comb_tm"])
    return y, k, v, _pairs_to_state(sfin), pa[:, s - 1:s, :]


def _prep(w, l):
    row = lambda a: a[l].reshape(1, -1).astype(F32)
    idx = jnp.arange(D_RWKV, dtype=jnp.int32) // RWKV_HEAD
    wr = jnp.zeros((D_MODEL, LANES), F32)
    wr = wr.at[:, :N_EXPERTS].set(w["moe_w_expert"][l]).at[:, N_EXPERTS:N_EXPERTS + N_GROUPS].set(w["moe_w_group"][l])
    wr_hi = wr.astype(BF16)
    br = jnp.zeros((1, LANES), F32)
    br = br.at[0, :N_EXPERTS].set(w["moe_b_expert"][l]).at[0, N_EXPERTS:N_EXPERTS + N_GROUPS].set(w["moe_b_group"][l])
    return {
        "w_in": w["w_in"][l].astype(BF16),
        "mu": row(w["rwkv_mu"]), "w0": row(w["rwkv_w0"]), "w2": w["rwkv_w2"][l].astype(BF16),
        "a0": row(w["rwkv_a0"]), "a2": w["rwkv_a2"][l].astype(BF16), "g2": w["rwkv_g2"][l].astype(BF16),
        "k_k": row(w["rwkv_k_k"]), "k_a": row(w["rwkv_k_a"]), "r_k": row(w["rwkv_r_k"]),
        "ln_w": row(w["rwkv_ln_w"]), "ln_b": row(w["rwkv_ln_b"]),
        "hsum": (idx[:, None] == idx[None, :]).astype(BF16),
        "subln": row(w["diff_subln"]),
        "w_out": w["w_out"][l].astype(BF16), "wq": w["mem_wq"][l].astype(BF16), "wo": w["mem_wo"][l].astype(BF16),
        "ln1_g": row(w["ln1_g"]), "ln1_b": row(w["ln1_b"]), "ln2_g": row(w["ln2_g"]), "ln2_b": row(w["ln2_b"]),
        "ln3_g": row(w["ln3_g"]), "ln3_b": row(w["ln3_b"]),
        "wr_hi": wr_hi, "wr_lo": (wr - wr_hi.astype(F32)).astype(BF16), "br": br,
        "moe_w1": w["moe_w1"][l].astype(BF16), "moe_w3": w["moe_w3"][l].astype(BF16),
        "moe_w2": w["moe_w2"][l].astype(BF16),
    }


def _tile(n, pref):
    return pref if n % pref == 0 else n


def kernel(x_prompt, x_sample, mem_prompt, cache_diff_k, cache_diff_v, cache_mem_k, cache_mem_v, state_rwkv, state_shift, w_in, rwkv_mu, rwkv_w0, rwkv_w2, rwkv_a0, rwkv_a2, rwkv_g2, rwkv_k_k, rwkv_k_a, rwkv_r_k, rwkv_ln_w, rwkv_ln_b, diff_lq1, diff_lk1, diff_lq2, diff_lk2, diff_subln, w_out, ln1_g, ln1_b, mem_wq, mem_wk, mem_wv, mem_wo, ln2_g, ln2_b, moe_w_group, moe_b_group, moe_w_expert, moe_b_expert, moe_w1, moe_w3, moe_w2, ln3_g, ln3_b):
    w = dict(w_in=w_in, rwkv_mu=rwkv_mu, rwkv_w0=rwkv_w0, rwkv_w2=rwkv_w2, rwkv_a0=rwkv_a0, rwkv_a2=rwkv_a2,
             rwkv_g2=rwkv_g2, rwkv_k_k=rwkv_k_k, rwkv_k_a=rwkv_k_a, rwkv_r_k=rwkv_r_k, rwkv_ln_w=rwkv_ln_w,
             rwkv_ln_b=rwkv_ln_b, diff_subln=diff_subln, w_out=w_out, ln1_g=ln1_g, ln1_b=ln1_b, mem_wq=mem_wq,
             mem_wo=mem_wo, ln2_g=ln2_g, ln2_b=ln2_b, moe_w_group=moe_w_group, moe_b_group=moe_b_group,
             moe_w_expert=moe_w_expert, moe_b_expert=moe_b_expert, moe_w1=moe_w1, moe_w3=moe_w3, moe_w2=moe_w2,
             ln3_g=ln3_g, ln3_b=ln3_b)
    bp, sp, _ = x_prompt.shape
    bs, ss, _ = x_sample.shape
    depth = w_in.shape[0]
    cfg_p = dict(in_bb=1, in_ts=_tile(sp, 512), rwkv_tb=_tile(sp, 512), rwkv_chunk=CHUNK, rwkv_group=_tile(sp, 512) // CHUNK,
                 attn_tq=_tile(sp, 512), post_tm=_tile(sp, 1024), post_split=2, moe_tb=256, moe_pairs=True,
                 comb_tm=_tile(bp * sp, 512))
    cfg_s = dict(in_bb=bs, in_ts=ss, rwkv_tb=ss, rwkv_chunk=ss, rwkv_group=1, attn_tq=ss, post_tm=ss, post_split=1, moe_tb=64, moe_pairs=False,
                 comb_tm=_tile(bs * ss, 256))
    yp, ys = x_prompt, x_sample
    outs = [[] for _ in range(10)]
    for l in range(depth):
        prm = _prep(w, l)
        lam_init = 0.8 - 0.6 * math.exp(-0.3 * l)
        f = lambda z: z[l].astype(F32)
        lam = (jnp.exp(jnp.sum(f(diff_lq1) * f(diff_lk1))) - jnp.exp(jnp.sum(f(diff_lq2) * f(diff_lk2)))
               + lam_init).reshape(1)
        slopes = jnp.asarray(ALIBI_SLOPES, F32)
        lam = jnp.concatenate([lam, slopes, jnp.zeros((3,), F32)])
        mk_p, mv_p, mkb, mvb = _memproj(mem_prompt, mem_wk[l].astype(BF16), mem_wv[l].astype(BF16))
        yp, k_p, v_p, st_p, sh_p = _trunk(
            yp, jnp.zeros((bp, 1, C_RWKV), F32), jnp.zeros((bp, 2 * RWKV_PAIRS, RWKV_HEAD, RWKV_HEAD), F32),
            None, None, mkb, mvb, prm, lam, lam_init, cfg_p)
        ys, k_s, v_s, st_s, sh_s = _trunk(
            ys, state_shift[l], state_rwkv[l], cache_diff_k[l], cache_diff_v[l],
            cache_mem_k[l].astype(BF16), cache_mem_v[l].astype(BF16), prm, lam, lam_init, cfg_s)
        for lst, val in zip(outs, (k_p, v_p, mk_p, mv_p, st_p, sh_p, k_s, v_s, st_s, sh_s)):
            lst.append(val)
    return (yp, ys) + tuple(jnp.stack(o) for o in outs)
```
